```python
import math
import jax, jax.numpy as jnp
from jax import lax
import numpy as np

D_MODEL = 1024
BATCH = 2
SEQ = 8192
DEPTH = 1
DEC_BATCH = 128
DEC_SEQ = 1
PAST_LEN = 8192
PAGE_SIZE = 128

D_MIX = D_MODEL
D_RNN = D_MIX // 2
N_RNN_BLOCKS = 8
RNN_BLOCK = D_RNN // N_RNN_BLOCKS
CONV_W = 4
LRU_C = 8.0
D_ATT = D_MIX - D_RNN
HEAD_DIM = 64
N_HEADS = D_ATT // HEAD_DIM
N_KV_HEADS = 2
GROUP = N_HEADS // N_KV_HEADS
KV_DIM = N_KV_HEADS * HEAD_DIM
WINDOW = 128
N_BUCKETS = 32
MAX_DISTANCE = 128
EPS = 1e-6
NEG_INF = -1e30
SPLITS = (D_RNN, 2 * D_RNN, 2 * D_RNN + D_ATT, 2 * D_RNN + D_ATT + KV_DIM, 2 * D_RNN + D_ATT + 2 * KV_DIM)
D_IN = 2 * D_RNN + 2 * D_ATT + 2 * KV_DIM

kernel_name = 'hymba_rglru_swa_sink_decode_step'


def rms_norm(x, g):
    xf = x.astype(jnp.float32)
    y = xf * lax.rsqrt(jnp.mean(xf * xf, axis=-1, keepdims=True) + EPS)
    return (y * g.astype(jnp.float32)).astype(x.dtype)


def t5_bucket(dist):
    dist = jnp.maximum(dist, 0)
    max_exact = N_BUCKETS // 2
    d = jnp.maximum(dist, 1).astype(jnp.float32)
    large = max_exact + (jnp.log(d / max_exact) / math.log(MAX_DISTANCE / max_exact)
                         * (N_BUCKETS - max_exact)).astype(jnp.int32)
    large = jnp.minimum(large, N_BUCKETS - 1)
    return jnp.where(dist < max_exact, dist, large)


def rel_bias_from_dist(dist, rel_bias):
    b = rel_bias.astype(jnp.float32)[t5_bucket(dist)]
    return jnp.moveaxis(b, -1, 0).reshape(N_KV_HEADS, GROUP, *dist.shape)


def softmax_with_sink(s, sinks):
    sk = jnp.broadcast_to(sinks.astype(jnp.float32).reshape(N_KV_HEADS, GROUP, 1, 1), s.shape[:-1] + (1,))
    p = jax.nn.softmax(jnp.concatenate([s, sk], axis=-1), axis=-1)
    return p[..., :-1]


def window_attention_prompt(q, k, v, sinks, rel_bias):
    B, S = q.shape[:2]
    nb = S // WINDOW
    qb = q.astype(jnp.float32).reshape(B, nb, WINDOW, N_KV_HEADS, GROUP, HEAD_DIM) * (HEAD_DIM ** -0.5)
    pad = jnp.zeros((B, WINDOW, N_KV_HEADS, HEAD_DIM), jnp.float32)
    kp = jnp.concatenate([pad, k.astype(jnp.float32)], axis=1).reshape(B, nb + 1, WINDOW, N_KV_HEADS, HEAD_DIM)
    vp = jnp.concatenate([pad, v.astype(jnp.float32)], axis=1).reshape(B, nb + 1, WINDOW, N_KV_HEADS, HEAD_DIM)
    kb = jnp.concatenate([kp[:, :-1], kp[:, 1:]], axis=2)
    vb = jnp.concatenate([vp[:, :-1], vp[:, 1:]], axis=2)
    qi = jnp.arange(WINDOW)[:, None]
    kj = jnp.arange(2 * WINDOW)[None, :]
    dist = qi + WINDOW - kj
    band = (dist >= 0) & (dist < WINDOW)
    has_prev = (jnp.arange(nb)[:, None, None] > 0) | (kj >= WINDOW)[None]
    mask = band[None] & has_prev
    bias = rel_bias_from_dist(dist, rel_bias)
    s = jnp.einsum('bnqhgd,bnkhd->bnhgqk', qb, kb) + bias[None, None]
    s = jnp.where(mask[None, :, None, None], s, NEG_INF)
    p = softmax_with_sink(s, sinks)
    o = jnp.einsum('bnhgqk,bnkhd->bnqhgd', p, vb)
    return o.reshape(B, S, D_ATT)


def window_attention_sample(q, k, v, k_buf, v_buf, sinks, rel_bias):
    B, T = q.shape[:2]
    wb = k_buf.shape[1]
    kc = jnp.concatenate([k_buf.astype(k.dtype), k], axis=1)
    vc = jnp.concatenate([v_buf.astype(v.dtype), v], axis=1)
    qf = q.astype(jnp.float32).reshape(B, T, N_KV_HEADS, GROUP, HEAD_DIM) * (HEAD_DIM ** -0.5)
    qi = jnp.arange(T)[:, None]
    kj = jnp.arange(wb + T)[None, :]
    dist = qi + wb - kj
    mask = (dist >= 0) & (dist < WINDOW)
    bias = rel_bias_from_dist(dist, rel_bias)
    s = jnp.einsum('bqhgd,bkhd->bhgqk', qf, kc.astype(jnp.float32)) + bias[None]
    s = jnp.where(mask[None, None, None], s, NEG_INF)
    p = softmax_with_sink(s, sinks)
    o = jnp.einsum('bhgqk,bkhd->bqhgd', p, vc.astype(jnp.float32))
    return o.reshape(B, T, D_ATT), kc[:, T:], vc[:, T:]


def causal_conv(x, conv_state, w, b):
    T = x.shape[1]
    xp = jnp.concatenate([conv_state.astype(x.dtype), x], axis=1)
    y = b.astype(x.dtype) + w[0].astype(x.dtype) * xp[:, 0:T]
    for tap in range(1, CONV_W):
        y = y + w[tap].astype(x.dtype) * xp[:, tap:tap + T]
    return y, xp[:, T:]


def rg_lru(x, h0, w_a, b_a, w_x, b_x, lam):
    B, T, _ = x.shape
    xf = x.astype(jnp.float32)
    xb = xf.reshape(B, T, N_RNN_BLOCKS, RNN_BLOCK)
    r = jax.nn.sigmoid(jnp.einsum('btni,nij->btnj', xb, w_a.astype(jnp.float32)).reshape(B, T, D_RNN)
                       + b_a.astype(jnp.float32))
    i = jax.nn.sigmoid(jnp.einsum('btni,nij->btnj', xb, w_x.astype(jnp.float32)).reshape(B, T, D_RNN)
                       + b_x.astype(jnp.float32))
    log_a = -LRU_C * r * jax.nn.softplus(-lam.astype(jnp.float32))
    a = jnp.exp(log_a)
    bx = jnp.sqrt(-jnp.expm1(2.0 * log_a)) * (i * xf)

    def combine(left, right):
        a1, b1 = left
        a2, b2 = right
        return a1 * a2, a2 * b1 + b2

    a_cum, b_cum = lax.associative_scan(combine, (a, bx), axis=1)
    h = a_cum * h0.astype(jnp.float32)[:, None] + b_cum
    return h, h[:, -1]


def decoder_layer(x, conv_state, h0, k_buf, v_buf, win_buf, rel_bias,
                  norm_pre, norm_post, w_in, conv_w, conv_b, w_gate_a, b_gate_a,
                  w_gate_x, b_gate_x, lru_lambda, attn_sinks, w_out):
    B, T, _ = x.shape
    xn = rms_norm(x, norm_pre)
    proj = jnp.einsum('btd,de->bte', xn, w_in.astype(xn.dtype))
    x_rnn, g_rnn, q, k, v, g_att = jnp.split(proj, SPLITS, axis=-1)
    xc, new_conv = causal_conv(x_rnn, conv_state, conv_w, conv_b)
    h, h_last = rg_lru(xc, h0, w_gate_a, b_gate_a, w_gate_x, b_gate_x, lru_lambda)
    rnn_out = h * jax.nn.silu(g_rnn.astype(jnp.float32))
    q = q.reshape(B, T, N_HEADS, HEAD_DIM)
    k = k.reshape(B, T, N_KV_HEADS, HEAD_DIM)
    v = v.reshape(B, T, N_KV_HEADS, HEAD_DIM)
    if k_buf is None:
        att = window_attention_prompt(q, k, v, attn_sinks, rel_bias)
        new_k, new_v = k[:, T - win_buf:], v[:, T - win_buf:]
    else:
        att, new_k, new_v = window_attention_sample(q, k, v, k_buf, v_buf, attn_sinks, rel_bias)
    att_out = att * jax.nn.silu(g_att.astype(jnp.float32))
    mix = jnp.concatenate([rnn_out, att_out], axis=-1).astype(x.dtype)
    out = jnp.einsum('bte,ed->btd', mix, w_out.astype(x.dtype))
    y = x + rms_norm(out, norm_post)
    return y, new_conv, h_last, new_k, new_v


def setup_inputs(seed: int = 0) -> dict:
    key = jax.random.key(seed)
    ks = jax.random.split(key, 20)
    win_buf = min(WINDOW, PAST_LEN)
    nrm = jax.random.normal
    x_prompt = nrm(ks[0], (BATCH, SEQ, D_MODEL), jnp.float32)
    x_sample = nrm(ks[1], (DEC_BATCH, DEC_SEQ, D_MODEL), jnp.float32)
    state_conv = nrm(ks[2], (DEPTH, DEC_BATCH, CONV_W - 1, D_RNN), jnp.float32)
    state_rnn = 0.5 * nrm(ks[3], (DEPTH, DEC_BATCH, D_RNN), jnp.float32)
    cache_k_win = nrm(ks[4], (DEPTH, DEC_BATCH, win_buf, N_KV_HEADS, HEAD_DIM), jnp.float32)
    cache_v_win = nrm(ks[5], (DEPTH, DEC_BATCH, win_buf, N_KV_HEADS, HEAD_DIM), jnp.float32)
    norm_pre = 1.0 + 0.05 * nrm(ks[6], (DEPTH, D_MODEL), jnp.float32)
    norm_post = 1.0 + 0.05 * nrm(ks[7], (DEPTH, D_MODEL), jnp.float32)
    w_in = nrm(ks[8], (DEPTH, D_MODEL, D_IN), jnp.float32) * D_MODEL ** -0.5
    conv_w = nrm(ks[9], (DEPTH, CONV_W, D_RNN), jnp.float32) * CONV_W ** -0.5
    conv_b = 0.01 * nrm(ks[10], (DEPTH, D_RNN), jnp.float32)
    w_gate_a = nrm(ks[11], (DEPTH, N_RNN_BLOCKS, RNN_BLOCK, RNN_BLOCK), jnp.float32) * RNN_BLOCK ** -0.5
    b_gate_a = 0.01 * nrm(ks[12], (DEPTH, D_RNN), jnp.float32)
    w_gate_x = nrm(ks[13], (DEPTH, N_RNN_BLOCKS, RNN_BLOCK, RNN_BLOCK), jnp.float32) * RNN_BLOCK ** -0.5
    b_gate_x = 0.01 * nrm(ks[14], (DEPTH, D_RNN), jnp.float32)
    a0 = jax.random.uniform(ks[15], (DEPTH, D_RNN), jnp.float32, minval=0.9, maxval=0.999)
    s = a0 ** (1.0 / LRU_C)
    lru_lambda = jnp.log(s) - jnp.log1p(-s)
    attn_sinks = 0.5 * nrm(ks[16], (DEPTH, N_HEADS), jnp.float32)
    rel_bias = 0.1 * nrm(ks[17], (N_BUCKETS, N_HEADS), jnp.float32)
    w_out = nrm(ks[18], (DEPTH, D_MIX, D_MODEL), jnp.float32) * D_MIX ** -0.5
    return {'x_prompt': x_prompt, 'x_sample': x_sample, 'state_conv': state_conv, 'state_rnn': state_rnn,
            'cache_k_win': cache_k_win, 'cache_v_win': cache_v_win,
            'norm_pre': norm_pre, 'norm_post': norm_post, 'w_in': w_in, 'conv_w': conv_w, 'conv_b': conv_b,
            'w_gate_a': w_gate_a, 'b_gate_a': b_gate_a, 'w_gate_x': w_gate_x, 'b_gate_x': b_gate_x,
            'lru_lambda': lru_lambda, 'attn_sinks': attn_sinks, 'rel_bias': rel_bias, 'w_out': w_out}


def reference(x_prompt, x_sample, state_conv, state_rnn, cache_k_win, cache_v_win,
              norm_pre, norm_post, w_in, conv_w, conv_b, w_gate_a, b_gate_a, w_gate_x, b_gate_x,
              lru_lambda, attn_sinks, rel_bias, w_out):
    win_buf = cache_k_win.shape[2]
    y_prompt, y_sample = x_prompt, x_sample
    conv_p, rnn_p, kw_p, vw_p = [], [], [], []
    conv_s, rnn_s, kw_s, vw_s = [], [], [], []
    for l in range(DEPTH):
        lw = (norm_pre[l], norm_post[l], w_in[l], conv_w[l], conv_b[l], w_gate_a[l], b_gate_a[l],
              w_gate_x[l], b_gate_x[l], lru_lambda[l], attn_sinks[l], w_out[l])
        zc = jnp.zeros((y_prompt.shape[0], CONV_W - 1, D_RNN), y_prompt.dtype)
        zh = jnp.zeros((y_prompt.shape[0], D_RNN), jnp.float32)
        y_prompt, c, h, kw, vw = decoder_layer(y_prompt, zc, zh, None, None, win_buf, rel_bias, *lw)
        conv_p.append(c)
        rnn_p.append(h)
        kw_p.append(kw)
        vw_p.append(vw)
        y_sample, c, h, kw, vw = decoder_layer(y_sample, state_conv[l], state_rnn[l], cache_k_win[l],
                                               cache_v_win[l], win_buf, rel_bias, *lw)
        conv_s.append(c)
        rnn_s.append(h)
        kw_s.append(kw)
        vw_s.append(vw)
    new_conv_prompt = jnp.stack(conv_p)
    new_rnn_prompt = jnp.stack(rnn_p)
    new_k_win_prompt = jnp.stack(kw_p)
    new_v_win_prompt = jnp.stack(vw_p)
    new_conv_sample = jnp.stack(conv_s)
    new_rnn_sample = jnp.stack(rnn_s)
    new_k_win_sample = jnp.stack(kw_s)
    new_v_win_sample = jnp.stack(vw_s)
    return (y_prompt, y_sample, new_conv_prompt, new_rnn_prompt, new_k_win_prompt, new_v_win_prompt,
            new_conv_sample, new_rnn_sample, new_k_win_sample, new_v_win_sample)
```

```python
import functools
import math

import numpy as np
import jax
import jax.numpy as jnp
from jax import lax
from jax.experimental import pallas as pl
from jax.experimental.pallas import tpu as pltpu

D_MODEL = 1024
D_RNN = 512
D_ATT = 512
HEAD_DIM = 64
N_HEADS = 8
N_KV_HEADS = 2
GROUP = N_HEADS // N_KV_HEADS
KV_DIM = N_KV_HEADS * HEAD_DIM
N_RNN_BLOCKS = 8
RNN_BLOCK = D_RNN // N_RNN_BLOCKS
CONV_W = 4
LRU_C = 8.0
WINDOW = 128
N_BUCKETS = 32
MAX_DISTANCE = 128
EPS = 1e-6
NEG_INF = -1e30
D_IN = 2 * D_RNN + 2 * D_ATT + 2 * KV_DIM

C_XRNN = 0
C_GRNN = D_RNN
C_Q = 2 * D_RNN
C_KV = 2 * D_RNN + D_ATT
C_GATT = 2 * D_RNN + D_ATT + 2 * KV_DIM

SUBLANES = 8
LANES = 128
GATE_GROUP = 256
PROMPT_BLOCK = 512
SAMPLE_BLOCK = 16
VMEM_LIMIT = 56 * 1024 * 1024

F32 = jnp.float32
BF16 = jnp.bfloat16


def _t5_bucket_np(dist):
    dist = np.maximum(dist, 0)
    max_exact = N_BUCKETS // 2
    d = np.maximum(dist, 1).astype(np.float32)
    ratio = np.log(d / np.float32(max_exact)) / np.float32(math.log(MAX_DISTANCE / max_exact))
    large = max_exact + (ratio * np.float32(N_BUCKETS - max_exact)).astype(np.int32)
    large = np.minimum(large, N_BUCKETS - 1)
    return np.where(dist < max_exact, dist, large).astype(np.int32)


def _folded_bucket_table():
    i = np.arange(WINDOW)[:, None]
    j = np.arange(WINDOW)[None, :]
    dist = np.where(j <= i, i - j, i + WINDOW - j)
    return _t5_bucket_np(dist)


def _sigmoid(x):
    return 1.0 / (1.0 + jnp.exp(-x))


def _silu(x):
    return x * _sigmoid(x)


def _softplus(x):
    return jnp.maximum(x, 0.0) + jnp.log1p(jnp.exp(-jnp.abs(x)))


def _rms_scale(x):
    return lax.rsqrt(jnp.mean(x * x, axis=-1, keepdims=True) + EPS)


def _dot(a, b):
    return jnp.dot(a, b, preferred_element_type=F32)


def _dot_nt(a, b):
    return lax.dot_general(a, b, (((1,), (1,)), ((), ())), preferred_element_type=F32)


def _rglru_gates(xc, wg_ref, rp):
    xcb = xc.astype(BF16)
    zs = [_dot(xcb[:, g * GATE_GROUP:(g + 1) * GATE_GROUP], wg_ref[g])
          for g in range(D_RNN // GATE_GROUP)]
    za = jnp.concatenate([z[:, :GATE_GROUP] for z in zs], axis=1)
    zx = jnp.concatenate([z[:, GATE_GROUP:] for z in zs], axis=1)
    r = _sigmoid(za + rp[5:6])
    i = _sigmoid(zx + rp[6:7])
    log_a = (-LRU_C * r) * _softplus(-rp[7:8])
    a = jnp.exp(log_a)
    bx = jnp.sqrt(1.0 - a * a) * (i * xc)
    return a, bx


def _build_bias_table(bkt, rb_ref, head):
    tbl = jnp.zeros(bkt.shape, F32)
    for k in range(N_BUCKETS):
        tbl = jnp.where(bkt == k, rb_ref[k, head], tbl)
    return tbl


def _prompt_kernel(rb_ref, sinks_ref, x_ref, bkt_ref, gpre_ref, gpost_ref, rp_ref, win_ref,
                   wg_ref, wout_ref,
                   y_ref, conv_ref, rnn_ref, kw_ref, vw_ref,
                   bias_ref, xpad_ref, hc_ref, kd_ref, vd_ref, att_ref, *, tb):
    b = pl.program_id(0)
    t = pl.program_id(1)
    last_t = pl.num_programs(1) - 1
    nq = tb // WINDOW

    @pl.when((b == 0) & (t == 0))
    def _():
        bkt = bkt_ref[...]
        for h in range(N_HEADS):
            bias_ref[h] = _build_bias_table(bkt, rb_ref, h)

    @pl.when(t == 0)
    def _():
        xpad_ref[0:SUBLANES, :] = jnp.zeros((SUBLANES, D_RNN), F32)
        hc_ref[...] = jnp.zeros((1, D_RNN), F32)
        kd_ref[:, 0:WINDOW, :] = jnp.zeros((N_KV_HEADS, WINDOW, LANES), BF16)
        vd_ref[:, 0:WINDOW, :] = jnp.zeros((N_KV_HEADS, WINDOW, LANES), BF16)

    x = x_ref[0]
    xn = (x * _rms_scale(x) * gpre_ref[...]).astype(BF16)
    rp = rp_ref[...]

    x_rnn = _dot(xn, win_ref[:, C_XRNN:C_XRNN + D_RNN])
    xpad_ref[SUBLANES:SUBLANES + tb, :] = x_rnn
    xc = rp[4:5] + rp[0:1] * xpad_ref[SUBLANES - 3:SUBLANES - 3 + tb, :]
    xc = xc + rp[1:2] * xpad_ref[SUBLANES - 2:SUBLANES - 2 + tb, :]
    xc = xc + rp[2:3] * xpad_ref[SUBLANES - 1:SUBLANES - 1 + tb, :]
    xc = xc + rp[3:4] * x_rnn

    @pl.when(t == last_t)
    def _():
        conv_ref[0] = xpad_ref[SUBLANES + tb - (CONV_W - 1):SUBLANES + tb, :]

    xpad_ref[0:SUBLANES, :] = xpad_ref[tb:tb + SUBLANES, :]

    a, bx = _rglru_gates(xc, wg_ref, rp)

    ng = tb // SUBLANES
    a3 = a.reshape(ng, SUBLANES, D_RNN)
    b3 = bx.reshape(ng, SUBLANES, D_RNN)
    row = lax.broadcasted_iota(jnp.int32, (ng, SUBLANES, D_RNN), 1)
    d = 1
    while d < SUBLANES:
        keep = row >= d
        a_sh = pltpu.roll(a3, d, axis=1)
        b_sh = pltpu.roll(b3, d, axis=1)
        b3 = jnp.where(keep, a3 * b_sh, 0.0) + b3
        a3 = jnp.where(keep, a3 * a_sh, a3)
        d *= 2
    carry = hc_ref[...]
    hs = []
    for g in range(ng):
        hg = a3[g] * carry + b3[g]
        carry = hg[SUBLANES - 1:SUBLANES, :]
        hs.append(hg)
    h = jnp.concatenate(hs, axis=0)
    hc_ref[...] = carry

    @pl.when(t == last_t)
    def _():
        rnn_ref[0] = carry

    g_rnn = _dot(xn, win_ref[:, C_GRNN:C_GRNN + D_RNN])
    rnn_out = (h * _silu(g_rnn)).astype(BF16)

    kv = _dot(xn, win_ref[:, C_KV:C_KV + 2 * KV_DIM])
    k01 = kv[:, :KV_DIM]
    v01 = kv[:, KV_DIM:]
    lo = lax.broadcasted_iota(jnp.int32, (tb, LANES), 1) < HEAD_DIM
    k10 = pltpu.roll(k01, HEAD_DIM, axis=1)
    v10 = pltpu.roll(v01, HEAD_DIM, axis=1)
    kd_ref[0, WINDOW:WINDOW + tb, :] = jnp.where(lo, k01, k10).astype(BF16)
    kd_ref[1, WINDOW:WINDOW + tb, :] = jnp.where(lo, k10, k01).astype(BF16)
    vd_ref[0, WINDOW:WINDOW + tb, :] = jnp.where(lo, v01, v10).astype(BF16)
    vd_ref[1, WINDOW:WINDOW + tb, :] = jnp.where(lo, v10, v01).astype(BF16)

    @pl.when(t == last_t)
    def _():
        kw_ref[0] = k01[tb - WINDOW:, :]
        vw_ref[0] = v01[tb - WINDOW:, :]

    qb = _dot(xn, win_ref[:, C_Q:C_Q + D_ATT]).astype(BF16)
    qi_idx = lax.broadcasted_iota(jnp.int32, (WINDOW, WINDOW), 0)
    kj_idx = lax.broadcasted_iota(jnp.int32, (WINDOW, WINDOW), 1)
    own = kj_idx <= qi_idx
    lane_lo = kj_idx < HEAD_DIM
    has_prev0 = (jnp.zeros((WINDOW, WINDOW), jnp.int32) + t) > 0
    zero_b = jnp.zeros((WINDOW, WINDOW), BF16)
    for qi in range(nq):
        rows = slice(qi * WINDOW, (qi + 1) * WINDOW)
        win = slice(qi * WINDOW, qi * WINDOW + 2 * WINDOW)
        for c in range(D_ATT // LANES):
            kvh = (2 * c) // GROUP
            qc = qb[rows, c * LANES:(c + 1) * LANES]
            kwin = kd_ref[kvh, win, :]
            vwin = vd_ref[kvh, win, :]
            outs = []
            for half in range(2):
                head = 2 * c + half
                qh = jnp.where(lane_lo, qc, zero_b) if half == 0 else jnp.where(lane_lo, zero_b, qc)
                s = _dot_nt(qh, kwin)
                sf = jnp.where(own, s[:, WINDOW:], s[:, :WINDOW]) + bias_ref[head]
                if qi == 0:
                    sf = jnp.where(own | has_prev0, sf, NEG_INF)
                sink = sinks_ref[head]
                m = jnp.maximum(jnp.max(sf, axis=1, keepdims=True), sink)
                e = jnp.exp(sf - m)
                den = jnp.sum(e, axis=1, keepdims=True) + jnp.exp(sink - m)
                pb = (e * (1.0 / den)).astype(BF16)
                pcat = jnp.concatenate([jnp.where(own, zero_b, pb), jnp.where(own, pb, zero_b)], axis=1)
                outs.append(_dot(pcat, vwin))
            att_ref[rows, c * LANES:(c + 1) * LANES] = jnp.where(lane_lo, outs[0], outs[1])

    kd_ref[:, 0:WINDOW, :] = kd_ref[:, tb:tb + WINDOW, :]
    vd_ref[:, 0:WINDOW, :] = vd_ref[:, tb:tb + WINDOW, :]

    g_att = _dot(xn, win_ref[:, C_GATT:C_GATT + D_ATT])
    att_out = (att_ref[...] * _silu(g_att)).astype(BF16)

    out = _dot(rnn_out, wout_ref[0:D_RNN, :]) + _dot(att_out, wout_ref[D_RNN:, :])
    y_ref[0] = x + out * _rms_scale(out) * gpost_ref[...]


def _prompt_call(x, rb, sinks, bkt, gpre, gpost, rp, win, wg, wout):
    bsz, seq, _ = x.shape
    tb = PROMPT_BLOCK
    nt = seq // tb
    const = lambda *shape: pl.BlockSpec(shape, lambda b, t: (0,) * len(shape))
    smem = pl.BlockSpec(memory_space=pltpu.SMEM)
    out_shapes = (
        jax.ShapeDtypeStruct((bsz, seq, D_MODEL), F32),
        jax.ShapeDtypeStruct((bsz, CONV_W - 1, D_RNN), F32),
        jax.ShapeDtypeStruct((bsz, 1, D_RNN), F32),
        jax.ShapeDtypeStruct((bsz, WINDOW, KV_DIM), F32),
        jax.ShapeDtypeStruct((bsz, WINDOW, KV_DIM), F32),
    )
    return pl.pallas_call(
        functools.partial(_prompt_kernel, tb=tb),
        grid=(bsz, nt),
        in_specs=[
            smem, smem,
            pl.BlockSpec((1, tb, D_MODEL), lambda b, t: (b, t, 0)),
            const(WINDOW, WINDOW),
            const(1, D_MODEL), const(1, D_MODEL),
            const(SUBLANES, D_RNN),
            const(D_MODEL, D_IN),
            const(D_RNN // GATE_GROUP, GATE_GROUP, 2 * GATE_GROUP),
            const(D_MODEL, D_MODEL),
        ],
        out_specs=(
            pl.BlockSpec((1, tb, D_MODEL), lambda b, t: (b, t, 0)),
            pl.BlockSpec((1, CONV_W - 1, D_RNN), lambda b, t: (b, 0, 0)),
            pl.BlockSpec((1, 1, D_RNN), lambda b, t: (b, 0, 0)),
            pl.BlockSpec((1, WINDOW, KV_DIM), lambda b, t: (b, 0, 0)),
            pl.BlockSpec((1, WINDOW, KV_DIM), lambda b, t: (b, 0, 0)),
        ),
        out_shape=out_shapes,
        scratch_shapes=[
            pltpu.VMEM((N_HEADS, WINDOW, WINDOW), F32),
            pltpu.VMEM((tb + SUBLANES, D_RNN), F32),
            pltpu.VMEM((1, D_RNN), F32),
            pltpu.VMEM((N_KV_HEADS, tb + WINDOW, LANES), BF16),
            pltpu.VMEM((N_KV_HEADS, tb + WINDOW, LANES), BF16),
            pltpu.VMEM((tb, D_ATT), F32),
        ],
        compiler_params=pltpu.CompilerParams(
            dimension_semantics=("arbitrary", "arbitrary"),
            vmem_limit_bytes=VMEM_LIMIT),
        name="prompt_layer",
    )(rb, sinks, x, bkt, gpre, gpost, rp, win, wg, wout)


def _sample_kernel(rb_ref, sinks_ref, x_ref, bktrow_ref, gpre_ref, gpost_ref, rp_ref, win_ref,
                   wg_ref, wout_ref, sc_ref, h0_ref, kc_ref, vc_ref,
                   y_ref, conv_ref, rnn_ref, ko_ref, vo_ref,
                   qz_ref, o_ref, knew_ref, vnew_ref, rnnout_ref, gatt_ref, bias_ref, sink_ref,
                   *, nb, bb):
    g = pl.program_id(0)
    last_g = pl.num_programs(0) - 1
    sub = lax.broadcasted_iota(jnp.int32, (SUBLANES, LANES), 0)
    lane_lo = lax.broadcasted_iota(jnp.int32, (nb, LANES), 1) < HEAD_DIM

    @pl.when(g == 0)
    def _():
        bkt = jnp.broadcast_to(bktrow_ref[...], (SUBLANES, LANES))
        bias = jnp.zeros((SUBLANES, LANES), F32)
        sinkm = jnp.zeros((SUBLANES, LANES), F32)
        for h in range(N_HEADS):
            bias = jnp.where(sub == h, _build_bias_table(bkt, rb_ref, h), bias)
            sinkm = jnp.where(sub == h, sinks_ref[h], sinkm)
        bias_ref[...] = bias
        sink_ref[...] = sinkm

        x = x_ref[...]
        xn = (x * _rms_scale(x) * gpre_ref[...]).astype(BF16)
        rp = rp_ref[...]
        x_rnn = _dot(xn, win_ref[:, C_XRNN:C_XRNN + D_RNN])
        xc = rp[4:5] + rp[0:1] * sc_ref[0]
        xc = xc + rp[1:2] * sc_ref[1]
        xc = xc + rp[2:3] * sc_ref[2]
        xc = xc + rp[3:4] * x_rnn
        conv_ref[0] = sc_ref[1]
        conv_ref[1] = sc_ref[2]
        conv_ref[2] = x_rnn
        a, bx = _rglru_gates(xc, wg_ref, rp)
        h = a * h0_ref[...] + bx
        rnn_ref[...] = h
        g_rnn = _dot(xn, win_ref[:, C_GRNN:C_GRNN + D_RNN])
        rnnout_ref[...] = h * _silu(g_rnn)
        gatt_ref[...] = _dot(xn, win_ref[:, C_GATT:C_GATT + D_ATT])
        kv = _dot(xn, win_ref[:, C_KV:C_KV + 2 * KV_DIM])
        knew_ref[...] = kv[:, :KV_DIM]
        vnew_ref[...] = kv[:, KV_DIM:]
        q = _dot(xn, win_ref[:, C_Q:C_Q + D_ATT])
        for head in range(N_HEADS):
            c = head // 2
            kvh = head // GROUP
            qc = q[:, c * LANES:(c + 1) * LANES]
            if (head % 2) != kvh:
                qc = pltpu.roll(qc, HEAD_DIM, axis=1)
            qz = jnp.where(lane_lo, qc, 0.0) if kvh == 0 else jnp.where(lane_lo, 0.0, qc)
            qz_ref[pl.ds(head, nb, stride=N_HEADS), :] = qz

    rowi = lax.broadcasted_iota(jnp.int32, (WINDOW, LANES), 0)
    newest = rowi == WINDOW - 1

    def body(bi, carry):
        bidx = g * bb + bi
        newk = jnp.where(newest, knew_ref[pl.ds(bidx, 1), :], pltpu.roll(kc_ref[bi], WINDOW - 1, axis=0))
        newv = jnp.where(newest, vnew_ref[pl.ds(bidx, 1), :], pltpu.roll(vc_ref[bi], WINDOW - 1, axis=0))
        ko_ref[bi] = newk
        vo_ref[bi] = newv
        r0 = pl.multiple_of(bidx * N_HEADS, N_HEADS)
        qz = qz_ref[pl.ds(r0, N_HEADS), :].astype(BF16)
        s = _dot_nt(qz, newk.astype(BF16)) + bias_ref[...]
        sinkm = sink_ref[...]
        m = jnp.maximum(jnp.max(s, axis=1, keepdims=True), sinkm)
        e = jnp.exp(s - m)
        den = jnp.sum(e, axis=1, keepdims=True) + jnp.exp(sinkm - m)
        p = (e * (1.0 / den)).astype(BF16)
        o_ref[pl.ds(r0, N_HEADS), :] = _dot(p, newv.astype(BF16))
        return carry

    lax.fori_loop(0, bb, body, 0)

    @pl.when(g == last_g)
    def _():
        cols = []
        for c in range(D_ATT // LANES):
            kvh = (2 * c) // GROUP
            halves = []
            for half in range(2):
                oh = o_ref[pl.ds(2 * c + half, nb, stride=N_HEADS), :]
                if half != kvh:
                    oh = pltpu.roll(oh, HEAD_DIM, axis=1)
                halves.append(oh)
            cols.append(jnp.where(lane_lo, halves[0], halves[1]))
        att = jnp.concatenate(cols, axis=1)
        att_out = (att * _silu(gatt_ref[...])).astype(BF16)
        rnn_out = rnnout_ref[...].astype(BF16)
        out = _dot(rnn_out, wout_ref[0:D_RNN, :]) + _dot(att_out, wout_ref[D_RNN:, :])
        y_ref[...] = x_ref[...] + out * _rms_scale(out) * gpost_ref[...]


def _sample_call(x, rb, sinks, bktrow, gpre, gpost, rp, win, wg, wout, sc, h0, kc, vc):
    nb = x.shape[0]
    bb = SAMPLE_BLOCK
    const = lambda *shape: pl.BlockSpec(shape, lambda g: (0,) * len(shape))
    smem = pl.BlockSpec(memory_space=pltpu.SMEM)
    win_spec = pl.BlockSpec((bb, WINDOW, KV_DIM), lambda g: (g, 0, 0))
    out_shapes = (
        jax.ShapeDtypeStruct((nb, D_MODEL), F32),
        jax.ShapeDtypeStruct((CONV_W - 1, nb, D_RNN), F32),
        jax.ShapeDtypeStruct((nb, D_RNN), F32),
        jax.ShapeDtypeStruct((nb, WINDOW, KV_DIM), F32),
        jax.ShapeDtypeStruct((nb, WINDOW, KV_DIM), F32),
    )
    return pl.pallas_call(
        functools.partial(_sample_kernel, nb=nb, bb=bb),
        grid=(nb // bb,),
        in_specs=[
            smem, smem,
            const(nb, D_MODEL),
            const(1, LANES),
            const(1, D_MODEL), const(1, D_MODEL),
            const(SUBLANES, D_RNN),
            const(D_MODEL, D_IN),
            const(D_RNN // GATE_GROUP, GATE_GROUP, 2 * GATE_GROUP),
            const(D_MODEL, D_MODEL),
            const(CONV_W - 1, nb, D_RNN),
            const(nb, D_RNN),
            win_spec, win_spec,
        ],
        out_specs=(
            const(nb, D_MODEL),
            const(CONV_W - 1, nb, D_RNN),
            const(nb, D_RNN),
            win_spec, win_spec,
        ),
        out_shape=out_shapes,
        scratch_shapes=[
            pltpu.VMEM((nb * N_HEADS, LANES), F32),
            pltpu.VMEM((nb * N_HEADS, LANES), F32),
            pltpu.VMEM((nb, KV_DIM), F32),
            pltpu.VMEM((nb, KV_DIM), F32),
            pltpu.VMEM((nb, D_RNN), F32),
            pltpu.VMEM((nb, D_ATT), F32),
            pltpu.VMEM((SUBLANES, LANES), F32),
            pltpu.VMEM((SUBLANES, LANES), F32),
        ],
        compiler_params=pltpu.CompilerParams(
            dimension_semantics=("arbitrary",),
            vmem_limit_bytes=VMEM_LIMIT),
        name="sample_layer",
    )(rb, sinks, x, bktrow, gpre, gpost, rp, win, wg, wout, sc, h0, kc, vc)


def _block_diag_gate(w):
    per = GATE_GROUP // RNN_BLOCK
    w4 = w.reshape(D_RNN // GATE_GROUP, per, RNN_BLOCK, RNN_BLOCK)
    eye = jnp.eye(per, dtype=w.dtype)
    bd = w4[:, :, :, None, :] * eye[None, :, None, :, None]
    return bd.reshape(D_RNN // GATE_GROUP, GATE_GROUP, GATE_GROUP)


def kernel(x_prompt, x_sample, state_conv, state_rnn, cache_k_win, cache_v_win, norm_pre, norm_post,
           w_in, conv_w, conv_b, w_gate_a, b_gate_a, w_gate_x, b_gate_x, lru_lambda, attn_sinks,
           rel_bias, w_out):
    assert w_in.shape[0] == 1, "single-layer trunk"
    bsz, seq, _ = x_prompt.shape
    nb = x_sample.shape[0]
    wb = cache_k_win.shape[2]
    assert wb == WINDOW and x_sample.shape[1] == 1 and seq % PROMPT_BLOCK == 0 and nb % SAMPLE_BLOCK == 0

    qscale = jnp.ones((D_IN,), F32).at[C_Q:C_Q + D_ATT].set(HEAD_DIM ** -0.5)
    win = (w_in[0] * qscale).astype(BF16)
    wout = w_out[0].astype(BF16)
    wg = jnp.concatenate([_block_diag_gate(w_gate_a[0]), _block_diag_gate(w_gate_x[0])], axis=-1).astype(BF16)
    rp = jnp.concatenate([conv_w[0], conv_b, b_gate_a, b_gate_x, lru_lambda], axis=0)
    gpre = norm_pre.reshape(1, D_MODEL)
    gpost = norm_post.reshape(1, D_MODEL)
    sinks = attn_sinks[0]
    bkt_np = _folded_bucket_table()
    bkt = jnp.asarray(bkt_np)
    bktrow = jnp.asarray(bkt_np[WINDOW - 1:WINDOW, :])

    y_p, conv_p, rnn_p, kw_p, vw_p = _prompt_call(x_prompt, rel_bias, sinks, bkt, gpre, gpost, rp, win, wg, wout)

    y_s, conv_s, rnn_s, kw_s, vw_s = _sample_call(
        x_sample.reshape(nb, D_MODEL), rel_bias, sinks, bktrow, gpre, gpost, rp, win, wg, wout,
        jnp.transpose(state_conv[0], (1, 0, 2)), state_rnn[0],
        cache_k_win[0].reshape(nb, wb, KV_DIM), cache_v_win[0].reshape(nb, wb, KV_DIM))

    kv5 = lambda z, n: z.reshape(1, n, wb, N_KV_HEADS, HEAD_DIM)
    return (y_p, y_s.reshape(nb, 1, D_MODEL),
            conv_p[None], rnn_p.reshape(1, bsz, D_RNN), kv5(kw_p, bsz), kv5(vw_p, bsz),
            jnp.transpose(conv_s, (1, 0, 2))[None], rnn_s[None], kv5(kw_s, nb), kv5(vw_s, nb))
```

```python
import functools
import math

import numpy as np
import jax
import jax.numpy as jnp
from jax import lax
from jax.experimental import pallas as pl
from jax.experimental.pallas import tpu as pltpu

D_MODEL = 1024
D_RNN = 512
D_ATT = 512
HEAD_DIM = 64
N_HEADS = 8
N_KV_HEADS = 2
GROUP = N_HEADS // N_KV_HEADS
KV_DIM = N_KV_HEADS * HEAD_DIM
N_RNN_BLOCKS = 8
RNN_BLOCK = D_RNN // N_RNN_BLOCKS
CONV_W = 4
LRU_C = 8.0
WINDOW = 128
N_BUCKETS = 32
MAX_DISTANCE = 128
EPS = 1e-6
NEG_INF = -1e30
D_IN = 2 * D_RNN + 2 * D_ATT + 2 * KV_DIM

C_XRNN = 0
C_GRNN = D_RNN
C_Q = 2 * D_RNN
C_KV = 2 * D_RNN + D_ATT
C_GATT = 2 * D_RNN + D_ATT + 2 * KV_DIM

SUBLANES = 8
LANES = 128
GATE_GROUP = 256
PROMPT_BLOCK = 512
SAMPLE_BLOCK = 16
VMEM_LIMIT = 56 * 1024 * 1024

LOG2E = 1.4426950408889634
TINY = 1e-37
F32 = jnp.float32
BF16 = jnp.bfloat16


def _t5_bucket_np(dist):
    dist = np.maximum(dist, 0)
    max_exact = N_BUCKETS // 2
    d = np.maximum(dist, 1).astype(np.float32)
    ratio = np.log(d / np.float32(max_exact)) / np.float32(math.log(MAX_DISTANCE / max_exact))
    large = max_exact + (ratio * np.float32(N_BUCKETS - max_exact)).astype(np.int32)
    large = np.minimum(large, N_BUCKETS - 1)
    return np.where(dist < max_exact, dist, large).astype(np.int32)


def _folded_bucket_table():
    i = np.arange(WINDOW)[:, None]
    j = np.arange(WINDOW)[None, :]
    dist = np.where(j <= i, i - j, i + WINDOW - j)
    return _t5_bucket_np(dist)


def _sigmoid_of_neg(xn):
    return 1.0 / (1.0 + jnp.exp(xn))


def _silu(x):
    return x * (1.0 / (1.0 + jnp.exp2(x * -LOG2E)))


def _softplus(x):
    return jnp.maximum(x, 0.0) + jnp.log1p(jnp.exp(-jnp.abs(x)))


def _rms_scale(x):
    return lax.rsqrt(jnp.mean(x * x, axis=-1, keepdims=True) + EPS)


def _dot(a, b):
    return jnp.dot(a, b, preferred_element_type=F32)


def _dot_nt(a, b):
    return lax.dot_general(a, b, (((1,), (1,)), ((), ())), preferred_element_type=F32)


def _rglru_gates(xc, wg_ref, rp):
    xcb = xc.astype(BF16)
    zs = [_dot(xcb[:, g * GATE_GROUP:(g + 1) * GATE_GROUP], wg_ref[g])
          for g in range(D_RNN // GATE_GROUP)]
    za = jnp.concatenate([z[:, :GATE_GROUP] for z in zs], axis=1)
    zx = jnp.concatenate([z[:, GATE_GROUP:] for z in zs], axis=1)
    r = _sigmoid_of_neg(za + rp[5:6])
    i = _sigmoid_of_neg(zx + rp[6:7])
    log_a = (-LRU_C * r) * _softplus(-rp[7:8])
    a = jnp.exp(log_a)
    u = 1.0 - a * a
    bx = (u * lax.rsqrt(jnp.maximum(u, TINY))) * (i * xc)
    return a, bx


def _build_bias_table(bkt, rb_ref, head):
    tbl = jnp.zeros(bkt.shape, F32)
    for k in range(N_BUCKETS):
        tbl = jnp.where(bkt == k, rb_ref[k, head], tbl)
    return tbl


def _prompt_kernel(rb_ref, sinks_ref, x_ref, bkt_ref, gpre_ref, gpost_ref, rp_ref, win_ref,
                   wg_ref, wout_ref,
                   y_ref, conv_ref, rnn_ref, kw_ref, vw_ref,
                   bias_ref, xpad_ref, hc_ref, kd_ref, vd_ref, att_ref, *, tb):
    b = pl.program_id(0)
    t = pl.program_id(1)
    last_t = pl.num_programs(1) - 1
    nq = tb // WINDOW

    @pl.when((b == 0) & (t == 0))
    def _():
        bkt = bkt_ref[...]
        for h in range(N_HEADS):
            bias_ref[h] = _build_bias_table(bkt, rb_ref, h)

    @pl.when(t == 0)
    def _():
        xpad_ref[...] = jnp.zeros((SUBLANES, D_RNN), F32)
        hc_ref[...] = jnp.zeros((1, D_RNN), F32)
        kd_ref[:, 0:WINDOW, :] = jnp.zeros((N_KV_HEADS, WINDOW, LANES), BF16)
        vd_ref[:, 0:WINDOW, :] = jnp.zeros((N_KV_HEADS, WINDOW, LANES), BF16)

    x = x_ref[0]
    xn = (x * _rms_scale(x) * gpre_ref[...]).astype(BF16)
    rp = rp_ref[...]

    x_rnn = _dot(xn, win_ref[:, C_XRNN:C_XRNN + D_RNN])
    hist = xpad_ref[...]
    sub8 = lax.broadcasted_iota(jnp.int32, (SUBLANES, D_RNN), 0)
    xc = rp[4:5]
    for tap in range(CONV_W - 1):
        shift = CONV_W - 1 - tap
        rolled = pltpu.roll(x_rnn, shift, axis=0)
        head = jnp.where(sub8 < shift, pltpu.roll(hist, shift, axis=0), rolled[0:SUBLANES])
        xc = xc + rp[tap:tap + 1] * jnp.concatenate([head, rolled[SUBLANES:]], axis=0)
    xc = xc + rp[3:4] * x_rnn
    xpad_ref[...] = x_rnn[tb - SUBLANES:, :]
    conv_ref[0] = xpad_ref[SUBLANES - (CONV_W - 1):SUBLANES, :]

    a, bx = _rglru_gates(xc, wg_ref, rp)

    ng = tb // SUBLANES
    a3 = a.reshape(ng, SUBLANES, D_RNN)
    b3 = bx.reshape(ng, SUBLANES, D_RNN)
    row = lax.broadcasted_iota(jnp.int32, (ng, SUBLANES, D_RNN), 1)
    d = 1
    while d < SUBLANES:
        keep = row >= d
        a_sh = pltpu.roll(a3, d, axis=1)
        b_sh = pltpu.roll(b3, d, axis=1)
        b3 = jnp.where(keep, a3 * b_sh, 0.0) + b3
        a3 = jnp.where(keep, a3 * a_sh, a3)
        d *= 2
    carry = hc_ref[...]
    hs = []
    for g in range(ng):
        hg = a3[g] * carry + b3[g]
        carry = hg[SUBLANES - 1:SUBLANES, :]
        hs.append(hg)
    h = jnp.concatenate(hs, axis=0)
    hc_ref[...] = carry
    rnn_ref[0] = carry

    g_rnn = _dot(xn, win_ref[:, C_GRNN:C_GRNN + D_RNN])
    rnn_out = (h * _silu(g_rnn)).astype(BF16)

    kv = _dot(xn, win_ref[:, C_KV:C_KV + 2 * KV_DIM])
    k01 = kv[:, :KV_DIM]
    v01 = kv[:, KV_DIM:]
    lo = lax.broadcasted_iota(jnp.int32, (tb, LANES), 1) < HEAD_DIM
    k10 = pltpu.roll(k01, HEAD_DIM, axis=1)
    v10 = pltpu.roll(v01, HEAD_DIM, axis=1)
    kd_ref[0, WINDOW:WINDOW + tb, :] = jnp.where(lo, k01, k10).astype(BF16)
    kd_ref[1, WINDOW:WINDOW + tb, :] = jnp.where(lo, k10, k01).astype(BF16)
    vd_ref[0, WINDOW:WINDOW + tb, :] = jnp.where(lo, v01, v10).astype(BF16)
    vd_ref[1, WINDOW:WINDOW + tb, :] = jnp.where(lo, v10, v01).astype(BF16)

    kw_ref[0] = k01[tb - WINDOW:, :]
    vw_ref[0] = v01[tb - WINDOW:, :]

    qb = _dot(xn, win_ref[:, C_Q:C_Q + D_ATT]).astype(BF16)
    qi_idx = lax.broadcasted_iota(jnp.int32, (WINDOW, WINDOW), 0)
    kj_idx = lax.broadcasted_iota(jnp.int32, (WINDOW, WINDOW), 1)
    own = kj_idx <= qi_idx
    lane_lo = kj_idx < HEAD_DIM
    has_prev0 = (jnp.zeros((WINDOW, WINDOW), jnp.int32) + t) > 0
    zero_b = jnp.zeros((WINDOW, WINDOW), BF16)
    for qi in range(nq):
        rows = slice(qi * WINDOW, (qi + 1) * WINDOW)
        win = slice(qi * WINDOW, qi * WINDOW + 2 * WINDOW)
        for c in range(D_ATT // LANES):
            kvh = (2 * c) // GROUP
            qc = qb[rows, c * LANES:(c + 1) * LANES]
            kwin = kd_ref[kvh, win, :]
            vwin = vd_ref[kvh, win, :]
            outs = []
            for half in range(2):
                head = 2 * c + half
                qh = jnp.where(lane_lo, qc, zero_b) if half == 0 else jnp.where(lane_lo, zero_b, qc)
                s = _dot_nt(qh, kwin)
                sf = jnp.where(own, s[:, WINDOW:], s[:, :WINDOW]) + bias_ref[head]
                if qi == 0:
                    sf = jnp.where(own | has_prev0, sf, NEG_INF)
                m = jnp.max(sf, axis=1, keepdims=True)
                e = jnp.exp(sf - m)
                den = jnp.sum(e, axis=1, keepdims=True) + jnp.exp(sinks_ref[head] - m)
                pb = (e * (1.0 / den)).astype(BF16)
                pcat = jnp.concatenate([jnp.where(own, zero_b, pb), jnp.where(own, pb, zero_b)], axis=1)
                outs.append(_dot(pcat, vwin))
            att_ref[rows, c * LANES:(c + 1) * LANES] = jnp.where(lane_lo, outs[0], outs[1])

    kd_ref[:, 0:WINDOW, :] = kd_ref[:, tb:tb + WINDOW, :]
    vd_ref[:, 0:WINDOW, :] = vd_ref[:, tb:tb + WINDOW, :]

    g_att = _dot(xn, win_ref[:, C_GATT:C_GATT + D_ATT])
    att_out = (att_ref[...] * _silu(g_att)).astype(BF16)

    out = _dot(rnn_out, wout_ref[0:D_RNN, :]) + _dot(att_out, wout_ref[D_RNN:, :])
    y_ref[0] = x + out * _rms_scale(out) * gpost_ref[...]


def _prompt_call(x, rb, sinks, bkt, gpre, gpost, rp, win, wg, wout):
    bsz, seq, _ = x.shape
    tb = PROMPT_BLOCK
    nt = seq // tb
    const = lambda *shape: pl.BlockSpec(shape, lambda b, t: (0,) * len(shape))
    smem = pl.BlockSpec(memory_space=pltpu.SMEM)
    out_shapes = (
        jax.ShapeDtypeStruct((bsz, seq, D_MODEL), F32),
        jax.ShapeDtypeStruct((bsz, CONV_W - 1, D_RNN), F32),
        jax.ShapeDtypeStruct((bsz, 1, D_RNN), F32),
        jax.ShapeDtypeStruct((bsz, WINDOW, KV_DIM), F32),
        jax.ShapeDtypeStruct((bsz, WINDOW, KV_DIM), F32),
    )
    return pl.pallas_call(
        functools.partial(_prompt_kernel, tb=tb),
        grid=(bsz, nt),
        in_specs=[
            smem, smem,
            pl.BlockSpec((1, tb, D_MODEL), lambda b, t: (b, t, 0)),
            const(WINDOW, WINDOW),
            const(1, D_MODEL), const(1, D_MODEL),
            const(SUBLANES, D_RNN),
            const(D_MODEL, D_IN),
            const(D_RNN // GATE_GROUP, GATE_GROUP, 2 * GATE_GROUP),
            const(D_MODEL, D_MODEL),
        ],
        out_specs=(
            pl.BlockSpec((1, tb, D_MODEL), lambda b, t: (b, t, 0)),
            pl.BlockSpec((1, CONV_W - 1, D_RNN), lambda b, t: (b, 0, 0)),
            pl.BlockSpec((1, 1, D_RNN), lambda b, t: (b, 0, 0)),
            pl.BlockSpec((1, WINDOW, KV_DIM), lambda b, t: (b, 0, 0)),
            pl.BlockSpec((1, WINDOW, KV_DIM), lambda b, t: (b, 0, 0)),
        ),
        out_shape=out_shapes,
        scratch_shapes=[
            pltpu.VMEM((N_HEADS, WINDOW, WINDOW), F32),
            pltpu.VMEM((SUBLANES, D_RNN), F32),
            pltpu.VMEM((1, D_RNN), F32),
            pltpu.VMEM((N_KV_HEADS, tb + WINDOW, LANES), BF16),
            pltpu.VMEM((N_KV_HEADS, tb + WINDOW, LANES), BF16),
            pltpu.VMEM((tb, D_ATT), F32),
        ],
        compiler_params=pltpu.CompilerParams(
            dimension_semantics=("arbitrary", "arbitrary"),
            vmem_limit_bytes=VMEM_LIMIT),
        name="prompt_layer",
    )(rb, sinks, x, bkt, gpre, gpost, rp, win, wg, wout)


def _sample_kernel(rb_ref, sinks_ref, x_ref, bktrow_ref, gpre_ref, gpost_ref, rp_ref, win_ref,
                   wg_ref, wout_ref, sc_ref, h0_ref, kc_ref, vc_ref,
                   y_ref, conv_ref, rnn_ref, ko_ref, vo_ref,
                   qz_ref, o_ref, knew_ref, vnew_ref, rnnout_ref, gatt_ref, bias_ref, sink_ref,
                   *, nb, bb):
    g = pl.program_id(0)
    last_g = pl.num_programs(0) - 1
    sub = lax.broadcasted_iota(jnp.int32, (SUBLANES, LANES), 0)
    lane_lo = lax.broadcasted_iota(jnp.int32, (nb, LANES), 1) < HEAD_DIM

    @pl.when(g == 0)
    def _():
        bkt = jnp.broadcast_to(bktrow_ref[...], (SUBLANES, LANES))
        bias = jnp.zeros((SUBLANES, LANES), F32)
        sinkm = jnp.zeros((SUBLANES, LANES), F32)
        for h in range(N_HEADS):
            bias = jnp.where(sub == h, _build_bias_table(bkt, rb_ref, h), bias)
            sinkm = jnp.where(sub == h, sinks_ref[h], sinkm)
        bias_ref[...] = bias
        sink_ref[...] = sinkm

        x = x_ref[...]
        xn = (x * _rms_scale(x) * gpre_ref[...]).astype(BF16)
        rp = rp_ref[...]
        x_rnn = _dot(xn, win_ref[:, C_XRNN:C_XRNN + D_RNN])
        xc = rp[4:5] + rp[0:1] * sc_ref[0]
        xc = xc + rp[1:2] * sc_ref[1]
        xc = xc + rp[2:3] * sc_ref[2]
        xc = xc + rp[3:4] * x_rnn
        conv_ref[0] = sc_ref[1]
        conv_ref[1] = sc_ref[2]
        conv_ref[2] = x_rnn
        a, bx = _rglru_gates(xc, wg_ref, rp)
        h = a * h0_ref[...] + bx
        rnn_ref[...] = h
        g_rnn = _dot(xn, win_ref[:, C_GRNN:C_GRNN + D_RNN])
        rnnout_ref[...] = h * _silu(g_rnn)
        gatt_ref[...] = _dot(xn, win_ref[:, C_GATT:C_GATT + D_ATT])
        kv = _dot(xn, win_ref[:, C_KV:C_KV + 2 * KV_DIM])
        knew_ref[...] = kv[:, :KV_DIM]
        vnew_ref[...] = kv[:, KV_DIM:]
        q = _dot(xn, win_ref[:, C_Q:C_Q + D_ATT])
        for head in range(N_HEADS):
            c = head // 2
            kvh = head // GROUP
            qc = q[:, c * LANES:(c + 1) * LANES]
            if (head % 2) != kvh:
                qc = pltpu.roll(qc, HEAD_DIM, axis=1)
            qz = jnp.where(lane_lo, qc, 0.0) if kvh == 0 else jnp.where(lane_lo, 0.0, qc)
            qz_ref[pl.ds(head, nb, stride=N_HEADS), :] = qz

    rowi = lax.broadcasted_iota(jnp.int32, (WINDOW, LANES), 0)
    newest = rowi == WINDOW - 1

    def body(bi, carry):
        bidx = g * bb + bi
        newk = jnp.where(newest, knew_ref[pl.ds(bidx, 1), :], pltpu.roll(kc_ref[bi], WINDOW - 1, axis=0))
        newv = jnp.where(newest, vnew_ref[pl.ds(bidx, 1), :], pltpu.roll(vc_ref[bi], WINDOW - 1, axis=0))
        ko_ref[bi] = newk
        vo_ref[bi] = newv
        r0 = pl.multiple_of(bidx * N_HEADS, N_HEADS)
        qz = qz_ref[pl.ds(r0, N_HEADS), :].astype(BF16)
        s = _dot_nt(qz, newk.astype(BF16)) + bias_ref[...]
        sinkm = sink_ref[...]
        m = jnp.maximum(jnp.max(s, axis=1, keepdims=True), sinkm)
        e = jnp.exp(s - m)
        den = jnp.sum(e, axis=1, keepdims=True) + jnp.exp(sinkm - m)
        p = (e * (1.0 / den)).astype(BF16)
        o_ref[pl.ds(r0, N_HEADS), :] = _dot(p, newv.astype(BF16))
        return carry

    lax.fori_loop(0, bb, body, 0)

    @pl.when(g == last_g)
    def _():
        cols = []
        for c in range(D_ATT // LANES):
            kvh = (2 * c) // GROUP
            halves = []
            for half in range(2):
                oh = o_ref[pl.ds(2 * c + half, nb, stride=N_HEADS), :]
                if half != kvh:
                    oh = pltpu.roll(oh, HEAD_DIM, axis=1)
                halves.append(oh)
            cols.append(jnp.where(lane_lo, halves[0], halves[1]))
        att = jnp.concatenate(cols, axis=1)
        att_out = (att * _silu(gatt_ref[...])).astype(BF16)
        rnn_out = rnnout_ref[...].astype(BF16)
        out = _dot(rnn_out, wout_ref[0:D_RNN, :]) + _dot(att_out, wout_ref[D_RNN:, :])
        y_ref[...] = x_ref[...] + out * _rms_scale(out) * gpost_ref[...]


def _sample_call(x, rb, sinks, bktrow, gpre, gpost, rp, win, wg, wout, sc, h0, kc, vc):
    nb = x.shape[0]
    bb = SAMPLE_BLOCK
    const = lambda *shape: pl.BlockSpec(shape, lambda g: (0,) * len(shape))
    smem = pl.BlockSpec(memory_space=pltpu.SMEM)
    win_spec = pl.BlockSpec((bb, WINDOW, KV_DIM), lambda g: (g, 0, 0))
    out_shapes = (
        jax.ShapeDtypeStruct((nb, D_MODEL), F32),
        jax.ShapeDtypeStruct((CONV_W - 1, nb, D_RNN), F32),
        jax.ShapeDtypeStruct((nb, D_RNN), F32),
        jax.ShapeDtypeStruct((nb, WINDOW, KV_DIM), F32),
        jax.ShapeDtypeStruct((nb, WINDOW, KV_DIM), F32),
    )
    return pl.pallas_call(
        functools.partial(_sample_kernel, nb=nb, bb=bb),
        grid=(nb // bb,),
        in_specs=[
            smem, smem,
            const(nb, D_MODEL),
            const(1, LANES),
            const(1, D_MODEL), const(1, D_MODEL),
            const(SUBLANES, D_RNN),
            const(D_MODEL, D_IN),
            const(D_RNN // GATE_GROUP, GATE_GROUP, 2 * GATE_GROUP),
            const(D_MODEL, D_MODEL),
            const(CONV_W - 1, nb, D_RNN),
            const(nb, D_RNN),
            win_spec, win_spec,
        ],
        out_specs=(
            const(nb, D_MODEL),
            const(CONV_W - 1, nb, D_RNN),
            const(nb, D_RNN),
            win_spec, win_spec,
        ),
        out_shape=out_shapes,
        scratch_shapes=[
            pltpu.VMEM((nb * N_HEADS, LANES), F32),
            pltpu.VMEM((nb * N_HEADS, LANES), F32),
            pltpu.VMEM((nb, KV_DIM), F32),
            pltpu.VMEM((nb, KV_DIM), F32),
            pltpu.VMEM((nb, D_RNN), F32),
            pltpu.VMEM((nb, D_ATT), F32),
            pltpu.VMEM((SUBLANES, LANES), F32),
            pltpu.VMEM((SUBLANES, LANES), F32),
        ],
        compiler_params=pltpu.CompilerParams(
            dimension_semantics=("arbitrary",),
            vmem_limit_bytes=VMEM_LIMIT),
        name="sample_layer",
    )(rb, sinks, x, bktrow, gpre, gpost, rp, win, wg, wout, sc, h0, kc, vc)


def _block_diag_gate(w):
    per = GATE_GROUP // RNN_BLOCK
    w4 = w.reshape(D_RNN // GATE_GROUP, per, RNN_BLOCK, RNN_BLOCK)
    eye = jnp.eye(per, dtype=w.dtype)
    bd = w4[:, :, :, None, :] * eye[None, :, None, :, None]
    return bd.reshape(D_RNN // GATE_GROUP, GATE_GROUP, GATE_GROUP)


def kernel(x_prompt, x_sample, state_conv, state_rnn, cache_k_win, cache_v_win, norm_pre, norm_post,
           w_in, conv_w, conv_b, w_gate_a, b_gate_a, w_gate_x, b_gate_x, lru_lambda, attn_sinks,
           rel_bias, w_out):
    assert w_in.shape[0] == 1, "single-layer trunk"
    bsz, seq, _ = x_prompt.shape
    nb = x_sample.shape[0]
    wb = cache_k_win.shape[2]
    assert wb == WINDOW and x_sample.shape[1] == 1 and seq % PROMPT_BLOCK == 0 and nb % SAMPLE_BLOCK == 0

    qscale = jnp.ones((D_IN,), F32).at[C_Q:C_Q + D_ATT].set(HEAD_DIM ** -0.5)
    win = (w_in[0] * qscale).astype(BF16)
    wout = w_out[0].astype(BF16)
    wg = jnp.concatenate([_block_diag_gate(-w_gate_a[0]), _block_diag_gate(-w_gate_x[0])], axis=-1).astype(BF16)
    rp = jnp.concatenate([conv_w[0], conv_b, -b_gate_a, -b_gate_x, lru_lambda], axis=0)
    gpre = norm_pre.reshape(1, D_MODEL)
    gpost = norm_post.reshape(1, D_MODEL)
    sinks = attn_sinks[0]
    bkt_np = _folded_bucket_table()
    bkt = jnp.asarray(bkt_np)
    bktrow = jnp.asarray(bkt_np[WINDOW - 1:WINDOW, :])

    y_p, conv_p, rnn_p, kw_p, vw_p = _prompt_call(x_prompt, rel_bias, sinks, bkt, gpre, gpost, rp, win, wg, wout)

    y_s, conv_s, rnn_s, kw_s, vw_s = _sample_call(
        x_sample.reshape(nb, D_MODEL), rel_bias, sinks, bktrow, gpre, gpost, rp, win, wg, wout,
        jnp.transpose(state_conv[0], (1, 0, 2)), state_rnn[0],
        cache_k_win[0].reshape(nb, wb, KV_DIM), cache_v_win[0].reshape(nb, wb, KV_DIM))

    kv5 = lambda z, n: z.reshape(1, n, wb, N_KV_HEADS, HEAD_DIM)
    return (y_p, y_s.reshape(nb, 1, D_MODEL),
            conv_p[None], rnn_p.reshape(1, bsz, D_RNN), kv5(kw_p, bsz), kv5(vw_p, bsz),
            jnp.transpose(conv_s, (1, 0, 2))[None], rnn_s[None], kv5(kw_s, nb), kv5(vw_s, nb))
```

```python
import functools
import math

import numpy as np
import jax
import jax.numpy as jnp
from jax import lax
from jax.experimental import pallas as pl
from jax.experimental.pallas import tpu as pltpu

D_MODEL = 1024
D_RNN = 512
D_ATT = 512
HEAD_DIM = 64
N_HEADS = 8
N_KV_HEADS = 2
GROUP = N_HEADS // N_KV_HEADS
KV_DIM = N_KV_HEADS * HEAD_DIM
N_RNN_BLOCKS = 8
RNN_BLOCK = D_RNN // N_RNN_BLOCKS
CONV_W = 4
LRU_C = 8.0
WINDOW = 128
N_BUCKETS = 32
MAX_DISTANCE = 128
EPS = 1e-6
NEG_INF = -1e30
D_IN = 2 * D_RNN + 2 * D_ATT + 2 * KV_DIM

C_XRNN = 0
C_GRNN = D_RNN
C_Q = 2 * D_RNN
C_KV = 2 * D_RNN + D_ATT
C_GATT = 2 * D_RNN + D_ATT + 2 * KV_DIM

SUBLANES = 8
LANES = 128
GATE_GROUP = 256
ONES_ROWS = 16
SCORE_LOOKAHEAD = 2
PROMPT_SEQS = 2
PROMPT_BLOCK = 512
SAMPLE_BLOCK = 16
VMEM_LIMIT = 56 * 1024 * 1024

LOG2E = 1.4426950408889634
TINY = 1e-37
F32 = jnp.float32
BF16 = jnp.bfloat16


def _spread(groups, bulk):
    out, done = [], 0
    for i, group in enumerate(groups):
        out += group
        want = (i + 1) * len(bulk) // len(groups)
        out += bulk[done:want]
        done = want
    return out


def _step_program(npairs):
    a, b = range(PROMPT_SEQS)
    n_out = D_MODEL // GATE_GROUP

    def attention(s):
        groups = [[("score", s, n) for n in range(SCORE_LOOKAHEAD)]]
        for n in range(npairs):
            ahead = [("score", s, n + SCORE_LOOKAHEAD)] if n + SCORE_LOOKAHEAD < npairs else []
            groups.append(ahead + [("value", s, n)])
        return groups

    project = lambda s: [("project", s, name) for name in ("x_rnn", "kv", "q")]
    gates = lambda s: [("gate_piece", s, i) for i in range(4)]
    outs = lambda s: [("out_piece", s, j) for j in range(n_out)]
    pa, pb = project(a), project(b)
    prog = [("norm", a), pa[0], ("norm", b), pa[1], pa[2], ("conv_gates", a), ("store_kv", a)]
    prog += _spread(attention(a), [("gate_math", a)] + pb + gates(a) + [("scan", a)])
    prog += [("att_finish", a), ("conv_gates", b), ("store_kv", b)]
    prog += _spread(attention(b), [("gate_math", b)] + gates(b) + outs(a) + [("scan", b)])
    prog += [("finish", a), ("att_finish", b)] + outs(b) + [("finish", b)]
    return prog


def _t5_bucket_np(dist):
    dist = np.maximum(dist, 0)
    max_exact = N_BUCKETS // 2
    d = np.maximum(dist, 1).astype(np.float32)
    ratio = np.log(d / np.float32(max_exact)) / np.float32(math.log(MAX_DISTANCE / max_exact))
    large = max_exact + (ratio * np.float32(N_BUCKETS - max_exact)).astype(np.int32)
    large = np.minimum(large, N_BUCKETS - 1)
    return np.where(dist < max_exact, dist, large).astype(np.int32)


def _folded_bucket_table():
    i = np.arange(WINDOW)[:, None]
    j = np.arange(WINDOW)[None, :]
    dist = np.where(j <= i, i - j, i + WINDOW - j)
    return _t5_bucket_np(dist)


def _sigmoid_of_neg(xn):
    return 1.0 / (1.0 + jnp.exp(xn))


def _silu(x):
    return x * (1.0 / (1.0 + jnp.exp2(x * -LOG2E)))


def _softplus(x):
    return jnp.maximum(x, 0.0) + jnp.log1p(jnp.exp(-jnp.abs(x)))


def _rms_scale(x):
    return lax.rsqrt(jnp.mean(x * x, axis=-1, keepdims=True) + EPS)


def _dot(a, b):
    return jnp.dot(a, b, preferred_element_type=F32)


def _dot_nt(a, b):
    return lax.dot_general(a, b, (((1,), (1,)), ((), ())), preferred_element_type=F32)


def _rglru_gate_dots(xc, wg_ref):
    xcb = xc.astype(BF16)
    zs = [_dot(xcb[:, g * GATE_GROUP:(g + 1) * GATE_GROUP], wg_ref[g])
          for g in range(D_RNN // GATE_GROUP)]
    za = jnp.concatenate([z[:, :GATE_GROUP] for z in zs], axis=1)
    zx = jnp.concatenate([z[:, GATE_GROUP:] for z in zs], axis=1)
    return za, zx


def _rglru_gate_math(xc, za, zx, rp):
    r = _sigmoid_of_neg(za + rp[5:6])
    i = _sigmoid_of_neg(zx + rp[6:7])
    log_a = (-LRU_C * r) * _softplus(-rp[7:8])
    a = jnp.exp(log_a)
    u = 1.0 - a * a
    bx = (u * lax.rsqrt(jnp.maximum(u, TINY))) * (i * xc)
    return a, bx


def _prepare_params(raw, rp_ref, win_ref, wg_ref, wout_ref):
    cw_raw, cb_raw, ba_raw, bx_raw, lam_raw, win_raw, wga_raw, wgx_raw, wout_raw = raw
    rp_ref[0:CONV_W, :] = cw_raw[...]
    rp_ref[4:5, :] = cb_raw[...]
    rp_ref[5:6, :] = -ba_raw[...]
    rp_ref[6:7, :] = -bx_raw[...]
    rp_ref[7:8, :] = lam_raw[...]
    for c0 in range(0, D_IN, GATE_GROUP):
        w = win_raw[:, c0:c0 + GATE_GROUP]
        if C_Q <= c0 < C_Q + D_ATT:
            w = w * HEAD_DIM ** -0.5
        win_ref[:, c0:c0 + GATE_GROUP] = w.astype(BF16)
    for c0 in range(0, D_MODEL, GATE_GROUP):
        wout_ref[:, c0:c0 + GATE_GROUP] = wout_raw[:, c0:c0 + GATE_GROUP].astype(BF16)
    per = GATE_GROUP // RNN_BLOCK
    zero = jnp.zeros((RNN_BLOCK, RNN_BLOCK), F32)
    for g in range(D_RNN // GATE_GROUP):
        for half, w_raw in enumerate((wga_raw, wgx_raw)):
            rows = [jnp.concatenate([-w_raw[g * per + m] if k == m else zero for k in range(per)], axis=1)
                    for m in range(per)]
            wg_ref[g, :, half * GATE_GROUP:(half + 1) * GATE_GROUP] = jnp.concatenate(rows, axis=0).astype(BF16)


def _build_bias_table(bkt, rb_ref, head):
    tbl = jnp.zeros(bkt.shape, F32)
    for k in range(N_BUCKETS):
        tbl = jnp.where(bkt == k, rb_ref[k, head], tbl)
    return tbl


def _prompt_kernel(rb_ref, sinks_ref, x_ref, bkt_ref, gpre_ref, gpost_ref,
                   cw_raw, cb_raw, ba_raw, bx_raw, lam_raw, win_raw, wga_raw, wgx_raw, wout_raw,
                   y_ref, conv_ref, rnn_ref, kw_ref, vw_ref,
                   bias_ref, xpad_ref, hc_ref, kd_ref, vt_ref, att_ref, xr_ref, hn_ref,
                   rp_ref, win_ref, wg_ref, wout_ref, *, tb, nseq):
    t = pl.program_id(0)
    nq = tb // WINDOW

    @pl.when(t == 0)
    def _():
        _prepare_params((cw_raw, cb_raw, ba_raw, bx_raw, lam_raw, win_raw, wga_raw, wgx_raw, wout_raw),
                        rp_ref, win_ref, wg_ref, wout_ref)
        bkt = bkt_ref[...]
        for h in range(N_HEADS):
            bias_ref[h] = _build_bias_table(bkt, rb_ref, h)
        xpad_ref[...] = jnp.zeros((nseq, (CONV_W - 1) * SUBLANES, D_RNN), F32)
        hc_ref[...] = jnp.zeros((nseq, 1, D_RNN), F32)
        kd_ref[:, :, 0:WINDOW, :] = jnp.zeros((nseq, N_KV_HEADS, WINDOW, LANES), BF16)
        vt_ref[:, :, 0:HEAD_DIM, 0:WINDOW] = jnp.zeros((nseq, N_KV_HEADS, HEAD_DIM, WINDOW), BF16)
        vt_ref[:, :, HEAD_DIM:, :] = jnp.ones((nseq, N_KV_HEADS, ONES_ROWS, tb + WINDOW), BF16)

    rp = rp_ref[...]
    sub8 = lax.broadcasted_iota(jnp.int32, (SUBLANES, D_RNN), 0)
    lo = lax.broadcasted_iota(jnp.int32, (tb, LANES), 1) < HEAD_DIM
    key_idx = lax.broadcasted_iota(jnp.int32, (WINDOW, WINDOW), 0)
    qry_idx = lax.broadcasted_iota(jnp.int32, (WINDOW, WINDOW), 1)
    own = key_idx <= qry_idx
    lane_lo = qry_idx < HEAD_DIM
    has_prev0 = (jnp.zeros((WINDOW, WINDOW), jnp.int32) + t) > 0
    zero_b = jnp.zeros((WINDOW, WINDOW), BF16)
    pair_lo = lax.broadcasted_iota(jnp.int32, (1, 2 * WINDOW), 1) < WINDOW
    pairs = [(qi, c) for qi in range(nq) for c in range(D_ATT // LANES)]
    gate_cols = [C_GRNN, C_GRNN + GATE_GROUP, C_GATT, C_GATT + GATE_GROUP]

    def norm(s, st):
        x = x_ref[s]
        st.update(x=x, xn=(x * _rms_scale(x) * gpre_ref[...]).astype(BF16))

    def project(s, st, name):
        c0, width = dict(x_rnn=(C_XRNN, D_RNN), kv=(C_KV, 2 * KV_DIM), q=(C_Q, D_ATT))[name]
        st[name] = _dot(st["xn"], win_ref[:, c0:c0 + width])

    clen = tb // SUBLANES
    pitch = clen + SUBLANES
    ncol = D_RNN // LANES
    ntail = CONV_W - 1

    def conv_gates(s, st):
        x_rnn = st["x_rnn"]
        for c in range(ncol):
            for k in range(SUBLANES):
                xr_ref[s, c, k * pitch:k * pitch + clen, :] = x_rnn[k * clen:(k + 1) * clen, c * LANES:(c + 1) * LANES]
        xp = jnp.concatenate(
            [jnp.concatenate([xr_ref[s, c, pl.ds(v, SUBLANES, stride=pitch), :] for v in range(clen)], axis=0)
             for c in range(ncol)], axis=1)
        tail = xp[(clen - ntail) * SUBLANES:, :]
        prev_tail = xpad_ref[s]
        heads = [jnp.where(sub8 == 0,
                           pltpu.roll(prev_tail[j * SUBLANES:(j + 1) * SUBLANES], 1, axis=0),
                           pltpu.roll(tail[j * SUBLANES:(j + 1) * SUBLANES], 1, axis=0)) for j in range(ntail)]
        xc = rp[4:5]
        for tap in range(ntail):
            shift = ntail - tap
            xc = xc + rp[tap:tap + 1] * jnp.concatenate(heads[ntail - shift:] + [xp[:(clen - shift) * SUBLANES]], axis=0)
        xc = xc + rp[3:4] * xp
        xpad_ref[s] = tail
        for j in range(ntail):
            conv_ref[j, s:s + 1, :] = tail[j * SUBLANES + SUBLANES - 1:(j + 1) * SUBLANES]
        st["xc"] = xc
        st["za"], st["zx"] = _rglru_gate_dots(xc, wg_ref)

    def gate_math(s, st):
        st["a"], st["bx"] = _rglru_gate_math(st.pop("xc"), st.pop("za"), st.pop("zx"), rp)

    def store_kv(s, st):
        st["qb"] = st.pop("q").astype(BF16)
        k01 = st["kv"][:, :KV_DIM]
        v01 = st["kv"][:, KV_DIM:]
        k10 = pltpu.roll(k01, HEAD_DIM, axis=1)
        kd_ref[s, 0, WINDOW:WINDOW + tb, :] = jnp.where(lo, k01, k10).astype(BF16)
        kd_ref[s, 1, WINDOW:WINDOW + tb, :] = jnp.where(lo, k10, k01).astype(BF16)
        v_t = v01.T.astype(BF16)
        vt_ref[s, 0, 0:HEAD_DIM, WINDOW:WINDOW + tb] = v_t[0:HEAD_DIM]
        vt_ref[s, 1, 0:HEAD_DIM, WINDOW:WINDOW + tb] = v_t[HEAD_DIM:]
        kw_ref[s] = k01[tb - WINDOW:, :].T
        vw_ref[s] = v01[tb - WINDOW:, :].T

    def gate_piece(s, st, i):
        st[("gate", i)] = _dot(st["xn"], win_ref[:, gate_cols[i]:gate_cols[i] + GATE_GROUP])

    def scores(s, st, qi, c):
        rows = slice(qi * WINDOW, (qi + 1) * WINDOW)
        win = slice(qi * WINDOW, qi * WINDOW + 2 * WINDOW)
        kvh = (2 * c) // GROUP
        qc = st["qb"][rows, c * LANES:(c + 1) * LANES]
        qpair = jnp.concatenate([jnp.where(lane_lo, qc, zero_b), jnp.where(lane_lo, zero_b, qc)], axis=0)
        sc = _dot_nt(kd_ref[s, kvh, win, :], qpair)
        p_prev, p_own, ms = [], [], []
        for half in range(2):
            head = 2 * c + half
            cols = slice(half * WINDOW, (half + 1) * WINDOW)
            sf = jnp.where(own, sc[WINDOW:, cols], sc[:WINDOW, cols]) + bias_ref[head]
            if qi == 0:
                sf = jnp.where(own | has_prev0, sf, NEG_INF)
            m = jnp.max(sf, axis=0, keepdims=True)
            e = jnp.exp(sf - m).astype(BF16)
            p_prev.append(jnp.where(own, zero_b, e))
            p_own.append(jnp.where(own, e, zero_b))
            ms.append(m)
        p = jnp.concatenate([jnp.concatenate(p_prev, axis=1), jnp.concatenate(p_own, axis=1)], axis=0)
        return p, jnp.concatenate(ms, axis=1)

    def values(s, qi, c, p, m):
        rows = slice(qi * WINDOW, (qi + 1) * WINDOW)
        win = slice(qi * WINDOW, qi * WINDOW + 2 * WINDOW)
        kvh = (2 * c) // GROUP
        oa = _dot(vt_ref[s, kvh, :, win], p)
        sink = jnp.where(pair_lo, sinks_ref[2 * c], sinks_ref[2 * c + 1])
        den = oa[HEAD_DIM:HEAD_DIM + 1, :] + jnp.exp(sink - m)
        o = oa[0:HEAD_DIM, :] * (1.0 / den)
        o_t = jnp.concatenate([o[:, :WINDOW], o[:, WINDOW:]], axis=0)
        att_ref[s, rows, c * LANES:(c + 1) * LANES] = o_t.T

    def score(s, st, n):
        st[("p", n)] = scores(s, st, *pairs[n])

    def value(s, st, n):
        values(s, *pairs[n], *st.pop(("p", n)))

    def att_finish(s, st):
        kd_ref[s, :, 0:WINDOW, :] = kd_ref[s, :, tb:tb + WINDOW, :]
        vt_ref[s, :, :, 0:WINDOW] = vt_ref[s, :, :, tb:tb + WINDOW]
        g_att = jnp.concatenate([st[("gate", 2)], st[("gate", 3)]], axis=1)
        st["att_out"] = (att_ref[s] * _silu(g_att)).astype(BF16)

    def scan(s, st):
        a3 = st["a"].reshape(clen, SUBLANES, D_RNN)
        b3 = st["bx"].reshape(clen, SUBLANES, D_RNN)
        hl, pr = [b3[0]], [a3[0]]
        for v in range(1, clen):
            hl.append(a3[v] * hl[-1] + b3[v])
            pr.append(a3[v] * pr[-1])
        h_end, p_end = hl[-1], pr[-1]
        h_in = jnp.broadcast_to(hc_ref[s], (SUBLANES, D_RNN))
        entry = h_in
        for _ in range(SUBLANES - 1):
            entry = jnp.where(sub8 == 0, h_in, pltpu.roll(h_end + p_end * entry, 1, axis=0))
        carry = (h_end + p_end * entry)[SUBLANES - 1:SUBLANES, :]
        hc_ref[s] = carry
        rnn_ref[0, s:s + 1, :] = carry
        for v in range(clen):
            hv = hl[v] + pr[v] * entry
            for c in range(ncol):
                hn_ref[s, c, pl.ds(v, SUBLANES, stride=pitch), :] = hv[:, c * LANES:(c + 1) * LANES]
        h = jnp.concatenate(
            [jnp.concatenate([hn_ref[s, c, k * pitch:k * pitch + clen, :] for k in range(SUBLANES)], axis=0)
             for c in range(ncol)], axis=1)
        g_rnn = jnp.concatenate([st[("gate", 0)], st[("gate", 1)]], axis=1)
        st["rnn_out"] = (h * _silu(g_rnn)).astype(BF16)

    def out_piece(s, st, j):
        cols = slice(j * GATE_GROUP, (j + 1) * GATE_GROUP)
        st[("out", j)] = _dot(st["rnn_out"], wout_ref[0:D_RNN, cols]) + _dot(st["att_out"], wout_ref[D_RNN:, cols])

    def finish(s, st):
        out = jnp.concatenate([st[("out", j)] for j in range(D_MODEL // GATE_GROUP)], axis=1)
        y_ref[s] = st["x"] + out * _rms_scale(out) * gpost_ref[...]

    tasks = dict(norm=norm, project=project, conv_gates=conv_gates, gate_math=gate_math, store_kv=store_kv, gate_piece=gate_piece,
                 score=score, value=value, att_finish=att_finish, scan=scan, out_piece=out_piece, finish=finish)
    sts = [dict() for _ in range(nseq)]
    for name, s, *arg in _step_program(len(pairs)):
        tasks[name](s, sts[s], *arg)


def _raw_param_specs(const):
    return [
        const(CONV_W, D_RNN), const(1, D_RNN), const(1, D_RNN), const(1, D_RNN), const(1, D_RNN),
        const(D_MODEL, D_IN),
        const(N_RNN_BLOCKS, RNN_BLOCK, RNN_BLOCK), const(N_RNN_BLOCKS, RNN_BLOCK, RNN_BLOCK),
        const(D_MODEL, D_MODEL),
    ]


def _param_scratch():
    return [
        pltpu.VMEM((SUBLANES, D_RNN), F32),
        pltpu.VMEM((D_MODEL, D_IN), BF16),
        pltpu.VMEM((D_RNN // GATE_GROUP, GATE_GROUP, 2 * GATE_GROUP), BF16),
        pltpu.VMEM((D_MODEL, D_MODEL), BF16),
    ]


def _prompt_call(x, rb, sinks, bkt, gpre, gpost, raw):
    bsz, seq, _ = x.shape
    tb = PROMPT_BLOCK
    nt = seq // tb
    const = lambda *shape: pl.BlockSpec(shape, lambda t: (0,) * len(shape), pipeline_mode=pl.Buffered(1))
    whole = lambda *shape: pl.BlockSpec(shape, lambda t: (0,) * len(shape))
    smem = pl.BlockSpec(memory_space=pltpu.SMEM)
    out_shapes = (
        jax.ShapeDtypeStruct((bsz, seq, D_MODEL), F32),
        jax.ShapeDtypeStruct((CONV_W - 1, bsz, D_RNN), F32),
        jax.ShapeDtypeStruct((1, bsz, D_RNN), F32),
        jax.ShapeDtypeStruct((bsz, KV_DIM, WINDOW), F32),
        jax.ShapeDtypeStruct((bsz, KV_DIM, WINDOW), F32),
    )
    return pl.pallas_call(
        functools.partial(_prompt_kernel, tb=tb, nseq=bsz),
        grid=(nt,),
        in_specs=[
            smem, smem,
            pl.BlockSpec((bsz, tb, D_MODEL), lambda t: (0, t, 0)),
            const(WINDOW, WINDOW),
            const(1, D_MODEL), const(1, D_MODEL),
            *_raw_param_specs(const),
        ],
        out_specs=(
            pl.BlockSpec((bsz, tb, D_MODEL), lambda t: (0, t, 0)),
            whole(CONV_W - 1, bsz, D_RNN),
            whole(1, bsz, D_RNN),
            whole(bsz, KV_DIM, WINDOW),
            whole(bsz, KV_DIM, WINDOW),
        ),
        out_shape=out_shapes,
        scratch_shapes=[
            pltpu.VMEM((N_HEADS, WINDOW, WINDOW), F32),
            pltpu.VMEM((bsz, (CONV_W - 1) * SUBLANES, D_RNN), F32),
            pltpu.VMEM((bsz, 1, D_RNN), F32),
            pltpu.VMEM((bsz, N_KV_HEADS, tb + WINDOW, LANES), BF16),
            pltpu.VMEM((bsz, N_KV_HEADS, HEAD_DIM + ONES_ROWS, tb + WINDOW), BF16),
            pltpu.VMEM((bsz, tb, D_ATT), F32),
            pltpu.VMEM((bsz, D_RNN // LANES, tb + SUBLANES * SUBLANES, LANES), F32),
            pltpu.VMEM((bsz, D_RNN // LANES, tb + SUBLANES * SUBLANES, LANES), F32),
            *_param_scratch(),
        ],
        compiler_params=pltpu.CompilerParams(
            dimension_semantics=("arbitrary",),
            vmem_limit_bytes=VMEM_LIMIT),
        name="prompt_layer",
    )(rb, sinks, x, bkt, gpre, gpost, *raw)


def _sample_kernel(rb_ref, sinks_ref, x_ref, bktrow_ref, gpre_ref, gpost_ref,
                   cw_raw, cb_raw, ba_raw, bx_raw, lam_raw, win_raw, wga_raw, wgx_raw, wout_raw,
                   sc_ref, h0_ref, kc_ref, vc_ref,
                   y_ref, conv_ref, rnn_ref, ko_ref, vo_ref,
                   qz_ref, o_ref, knew_ref, vnew_ref, rnnout_ref, gatt_ref, bias_ref, sink_ref,
                   rp_ref, win_ref, wg_ref, wout_ref, *, nb, bb):
    g = pl.program_id(0)
    last_g = pl.num_programs(0) - 1
    sub = lax.broadcasted_iota(jnp.int32, (SUBLANES, LANES), 0)
    lane_lo = lax.broadcasted_iota(jnp.int32, (nb, LANES), 1) < HEAD_DIM

    @pl.when(g == 0)
    def _():
        _prepare_params((cw_raw, cb_raw, ba_raw, bx_raw, lam_raw, win_raw, wga_raw, wgx_raw, wout_raw),
                        rp_ref, win_ref, wg_ref, wout_ref)
        bkt = jnp.broadcast_to(bktrow_ref[...], (SUBLANES, LANES))
        bias = jnp.zeros((SUBLANES, LANES), F32)
        sinkm = jnp.zeros((SUBLANES, LANES), F32)
        for h in range(N_HEADS):
            bias = jnp.where(sub == h, _build_bias_table(bkt, rb_ref, h), bias)
            sinkm = jnp.where(sub == h, sinks_ref[h], sinkm)
        bias_ref[...] = bias
        sink_ref[...] = sinkm

        x = x_ref[...]
        xn = (x * _rms_scale(x) * gpre_ref[...]).astype(BF16)
        rp = rp_ref[...]
        x_rnn = _dot(xn, win_ref[:, C_XRNN:C_XRNN + D_RNN])
        xc = rp[4:5] + rp[0:1] * sc_ref[0]
        xc = xc + rp[1:2] * sc_ref[1]
        xc = xc + rp[2:3] * sc_ref[2]
        xc = xc + rp[3:4] * x_rnn
        conv_ref[0] = sc_ref[1]
        conv_ref[1] = sc_ref[2]
        conv_ref[2] = x_rnn
        a, bx = _rglru_gate_math(xc, *_rglru_gate_dots(xc, wg_ref), rp)
        h = a * h0_ref[...] + bx
        rnn_ref[...] = h
        g_rnn = _dot(xn, win_ref[:, C_GRNN:C_GRNN + D_RNN])
        rnnout_ref[...] = h * _silu(g_rnn)
        gatt_ref[...] = _dot(xn, win_ref[:, C_GATT:C_GATT + D_ATT])
        kv = _dot(xn, win_ref[:, C_KV:C_KV + 2 * KV_DIM])
        knew_ref[...] = kv[:, :KV_DIM]
        vnew_ref[...] = kv[:, KV_DIM:]
        q = _dot(xn, win_ref[:, C_Q:C_Q + D_ATT])
        for head in range(N_HEADS):
            c = head // 2
            kvh = head // GROUP
            qc = q[:, c * LANES:(c + 1) * LANES]
            if (head % 2) != kvh:
                qc = pltpu.roll(qc, HEAD_DIM, axis=1)
            qz = jnp.where(lane_lo, qc, 0.0) if kvh == 0 else jnp.where(lane_lo, 0.0, qc)
            qz_ref[pl.ds(head, nb, stride=N_HEADS), :] = qz

    rowi = lax.broadcasted_iota(jnp.int32, (WINDOW, LANES), 0)
    newest = rowi == WINDOW - 1

    bias = bias_ref[...]
    sinkm = sink_ref[...]
    row0 = [pl.multiple_of((g * bb + bi) * N_HEADS, N_HEADS) for bi in range(bb)]
    scores = []
    for bi in range(bb):
        newk = jnp.where(newest, knew_ref[pl.ds(g * bb + bi, 1), :], pltpu.roll(kc_ref[bi].T, WINDOW - 1, axis=0))
        ko_ref[bi] = newk.T
        qz = qz_ref[pl.ds(row0[bi], N_HEADS), :].astype(BF16)
        scores.append(_dot_nt(qz, newk.astype(BF16)) + bias)
    probs = []
    for s in scores:
        m = jnp.maximum(jnp.max(s, axis=1, keepdims=True), sinkm)
        e = jnp.exp(s - m)
        den = jnp.sum(e, axis=1, keepdims=True) + jnp.exp(sinkm - m)
        probs.append((e * (1.0 / den)).astype(BF16))
    for bi in range(bb):
        newv = jnp.where(newest, vnew_ref[pl.ds(g * bb + bi, 1), :], pltpu.roll(vc_ref[bi].T, WINDOW - 1, axis=0))
        vo_ref[bi] = newv.T
        o_ref[pl.ds(row0[bi], N_HEADS), :] = _dot(probs[bi], newv.astype(BF16))

    @pl.when(g == last_g)
    def _():
        cols = []
        for c in range(D_ATT // LANES):
            kvh = (2 * c) // GROUP
            halves = []
            for half in range(2):
                oh = o_ref[pl.ds(2 * c + half, nb, stride=N_HEADS), :]
                if half != kvh:
                    oh = pltpu.roll(oh, HEAD_DIM, axis=1)
                halves.append(oh)
            cols.append(jnp.where(lane_lo, halves[0], halves[1]))
        att = jnp.concatenate(cols, axis=1)
        att_out = (att * _silu(gatt_ref[...])).astype(BF16)
        rnn_out = rnnout_ref[...].astype(BF16)
        out = _dot(rnn_out, wout_ref[0:D_RNN, :]) + _dot(att_out, wout_ref[D_RNN:, :])
        y_ref[...] = x_ref[...] + out * _rms_scale(out) * gpost_ref[...]


def _sample_call(x, rb, sinks, bktrow, gpre, gpost, raw, sc, h0, kc_t, vc_t):
    nb = x.shape[0]
    bb = SAMPLE_BLOCK
    const = lambda *shape: pl.BlockSpec(shape, lambda g: (0,) * len(shape), pipeline_mode=pl.Buffered(1))
    smem = pl.BlockSpec(memory_space=pltpu.SMEM)
    win_t_spec = pl.BlockSpec((bb, KV_DIM, WINDOW), lambda g: (g, 0, 0))
    out_shapes = (
        jax.ShapeDtypeStruct((nb, D_MODEL), F32),
        jax.ShapeDtypeStruct((CONV_W - 1, nb, D_RNN), F32),
        jax.ShapeDtypeStruct((nb, D_RNN), F32),
        jax.ShapeDtypeStruct((nb, KV_DIM, WINDOW), F32),
        jax.ShapeDtypeStruct((nb, KV_DIM, WINDOW), F32),
    )
    return pl.pallas_call(
        functools.partial(_sample_kernel, nb=nb, bb=bb),
        grid=(nb // bb,),
        in_specs=[
            smem, smem,
            const(nb, D_MODEL),
            const(1, LANES),
            const(1, D_MODEL), const(1, D_MODEL),
            *_raw_param_specs(const),
            const(CONV_W - 1, nb, D_RNN),
            const(nb, D_RNN),
            win_t_spec, win_t_spec,
        ],
        out_specs=(
            pl.BlockSpec((nb, D_MODEL), lambda g: (0, 0)),
            pl.BlockSpec((CONV_W - 1, nb, D_RNN), lambda g: (0, 0, 0)),
            pl.BlockSpec((nb, D_RNN), lambda g: (0, 0)),
            win_t_spec, win_t_spec,
        ),
        out_shape=out_shapes,
        scratch_shapes=[
            pltpu.VMEM((nb * N_HEADS, LANES), F32),
            pltpu.VMEM((nb * N_HEADS, LANES), F32),
            pltpu.VMEM((nb, KV_DIM), F32),
            pltpu.VMEM((nb, KV_DIM), F32),
            pltpu.VMEM((nb, D_RNN), F32),
            pltpu.VMEM((nb, D_ATT), F32),
            pltpu.VMEM((SUBLANES, LANES), F32),
            pltpu.VMEM((SUBLANES, LANES), F32),
            *_param_scratch(),
        ],
        compiler_params=pltpu.CompilerParams(
            dimension_semantics=("arbitrary",),
            vmem_limit_bytes=VMEM_LIMIT),
        name="sample_layer",
    )(rb, sinks, x, bktrow, gpre, gpost, *raw, sc, h0, kc_t, vc_t)


def kernel(x_prompt, x_sample, state_conv, state_rnn, cache_k_win, cache_v_win, norm_pre, norm_post,
           w_in, conv_w, conv_b, w_gate_a, b_gate_a, w_gate_x, b_gate_x, lru_lambda, attn_sinks,
           rel_bias, w_out):
    assert w_in.shape[0] == 1, "single-layer trunk"
    bsz, seq, _ = x_prompt.shape
    nb = x_sample.shape[0]
    wb = cache_k_win.shape[2]
    assert wb == WINDOW and x_sample.shape[1] == 1 and seq % PROMPT_BLOCK == 0 and nb % SAMPLE_BLOCK == 0
    assert bsz == PROMPT_SEQS, "the step program is written for this many prompt sequences"

    raw = (conv_w[0], conv_b, b_gate_a, b_gate_x, lru_lambda, w_in[0], w_gate_a[0], w_gate_x[0], w_out[0])
    gpre = norm_pre.reshape(1, D_MODEL)
    gpost = norm_post.reshape(1, D_MODEL)
    sinks = attn_sinks[0]
    bkt_np = _folded_bucket_table()
    bkt = jnp.asarray(np.ascontiguousarray(bkt_np.T))
    bktrow = jnp.asarray(bkt_np[WINDOW - 1:WINDOW, :])

    y_p, conv_p, rnn_p, kw_p, vw_p = _prompt_call(x_prompt, rel_bias, sinks, bkt, gpre, gpost, raw)

    feature_major = lambda z: jnp.transpose(z[0], (0, 2, 3, 1)).reshape(nb, KV_DIM, wb)
    y_s, conv_s, rnn_s, kw_s, vw_s = _sample_call(
        x_sample.reshape(nb, D_MODEL), rel_bias, sinks, bktrow, gpre, gpost, raw,
        jnp.transpose(state_conv[0], (1, 0, 2)), state_rnn[0],
        feature_major(cache_k_win), feature_major(cache_v_win))

    kv5 = lambda z, n: jnp.transpose(z.reshape(n, N_KV_HEADS, HEAD_DIM, wb), (0, 3, 1, 2))[None]
    return (y_p, y_s.reshape(nb, 1, D_MODEL),
            jnp.transpose(conv_p, (1, 0, 2))[None], rnn_p, kv5(kw_p, bsz), kv5(vw_p, bsz),
            jnp.transpose(conv_s, (1, 0, 2))[None], rnn_s[None], kv5(kw_s, nb), kv5(vw_s, nb))
```

```python
import functools
import math

import numpy as np
import jax
import jax.numpy as jnp
from jax import lax
from jax.experimental import pallas as pl
from jax.experimental.pallas import tpu as pltpu

D_MODEL = 1024
D_RNN = 512
D_ATT = 512
HEAD_DIM = 64
N_HEADS = 8
N_KV_HEADS = 2
GROUP = N_HEADS // N_KV_HEADS
KV_DIM = N_KV_HEADS * HEAD_DIM
N_RNN_BLOCKS = 8
RNN_BLOCK = D_RNN // N_RNN_BLOCKS
CONV_W = 4
LRU_C = 8.0
WINDOW = 128
N_BUCKETS = 32
MAX_DISTANCE = 128
EPS = 1e-6
NEG_INF = -1e30
D_IN = 2 * D_RNN + 2 * D_ATT + 2 * KV_DIM

C_XRNN = 0
C_GRNN = D_RNN
C_Q = 2 * D_RNN
C_KV = 2 * D_RNN + D_ATT
C_GATT = 2 * D_RNN + D_ATT + 2 * KV_DIM

SUBLANES = 8
LANES = 128
GATE_GROUP = 256
ONES_ROWS = 16
SCORE_LOOKAHEAD = 2
PROMPT_SEQS = 2
PROMPT_BLOCK = 512
SAMPLE_BLOCK = 16
VMEM_LIMIT = 56 * 1024 * 1024

LOG2E = 1.4426950408889634
TINY = 1e-37
F32 = jnp.float32
BF16 = jnp.bfloat16


def _spread(groups, bulk):
    out, done = [], 0
    for i, group in enumerate(groups):
        out += group
        want = (i + 1) * len(bulk) // len(groups)
        out += bulk[done:want]
        done = want
    return out


def _step_program(npairs):
    a, b = range(PROMPT_SEQS)
    n_out = D_MODEL // GATE_GROUP

    def attention(s):
        groups = [[("score", s, n) for n in range(SCORE_LOOKAHEAD)]]
        for n in range(npairs):
            ahead = [("score", s, n + SCORE_LOOKAHEAD)] if n + SCORE_LOOKAHEAD < npairs else []
            groups.append(ahead + [("value", s, n)])
        return groups

    project = lambda s: [("project", s, name) for name in ("x_rnn", "kv", "q")]
    gates = lambda s: [("gate_piece", s, i) for i in range(4)]
    outs = lambda s: [("out_piece", s, j) for j in range(n_out)]
    pa, pb = project(a), project(b)
    prog = [("norm", a), pa[0], ("norm", b), pa[1], pa[2], ("conv_gates", a), ("store_kv", a)]
    prog += _spread(attention(a), [("gate_math", a)] + pb + gates(a) + [("scan", a)])
    prog += [("att_finish", a), ("conv_gates", b), ("store_kv", b)]
    prog += _spread(attention(b), [("gate_math", b)] + gates(b) + outs(a) + [("scan", b)])
    prog += [("finish", a), ("att_finish", b)] + outs(b) + [("finish", b)]
    return prog


def _t5_bucket_np(dist):
    dist = np.maximum(dist, 0)
    max_exact = N_BUCKETS // 2
    d = np.maximum(dist, 1).astype(np.float32)
    ratio = np.log(d / np.float32(max_exact)) / np.float32(math.log(MAX_DISTANCE / max_exact))
    large = max_exact + (ratio * np.float32(N_BUCKETS - max_exact)).astype(np.int32)
    large = np.minimum(large, N_BUCKETS - 1)
    return np.where(dist < max_exact, dist, large).astype(np.int32)


def _folded_bucket_table():
    i = np.arange(WINDOW)[:, None]
    j = np.arange(WINDOW)[None, :]
    dist = np.where(j <= i, i - j, i + WINDOW - j)
    return _t5_bucket_np(dist)


def _sigmoid_of_neg(xn):
    return 1.0 / (1.0 + jnp.exp(xn))


def _silu(x):
    return x * (1.0 / (1.0 + jnp.exp2(x * -LOG2E)))


def _softplus(x):
    return jnp.maximum(x, 0.0) + jnp.log1p(jnp.exp(-jnp.abs(x)))


def _rms_scale(x):
    return lax.rsqrt(jnp.mean(x * x, axis=-1, keepdims=True) + EPS)


def _dot(a, b):
    return jnp.dot(a, b, preferred_element_type=F32)


def _dot_nt(a, b):
    return lax.dot_general(a, b, (((1,), (1,)), ((), ())), preferred_element_type=F32)


def _rglru_gate_dots(xc, wg_ref):
    xcb = xc.astype(BF16)
    zs = [_dot(xcb[:, g * GATE_GROUP:(g + 1) * GATE_GROUP], wg_ref[g])
          for g in range(D_RNN // GATE_GROUP)]
    za = jnp.concatenate([z[:, :GATE_GROUP] for z in zs], axis=1)
    zx = jnp.concatenate([z[:, GATE_GROUP:] for z in zs], axis=1)
    return za, zx


def _rglru_gate_math(xc, za, zx, rp):
    r = _sigmoid_of_neg(za + rp[5:6])
    i = _sigmoid_of_neg(zx + rp[6:7])
    log_a = (-LRU_C * r) * _softplus(-rp[7:8])
    a = jnp.exp(log_a)
    u = 1.0 - a * a
    bx = (u * lax.rsqrt(jnp.maximum(u, TINY))) * (i * xc)
    return a, bx


def _prepare_params(raw, rp_ref, win_ref, wg_ref, wout_ref):
    cw_raw, cb_raw, ba_raw, bx_raw, lam_raw, win_raw, wga_raw, wgx_raw, wout_raw = raw
    rp_ref[0:CONV_W, :] = cw_raw[...]
    rp_ref[4:5, :] = cb_raw[...]
    rp_ref[5:6, :] = -ba_raw[...]
    rp_ref[6:7, :] = -bx_raw[...]
    rp_ref[7:8, :] = lam_raw[...]
    chunks = [(win_raw, win_ref, c0) for c0 in range(0, D_IN, GATE_GROUP)]
    chunks += [(wout_raw, wout_ref, c0) for c0 in range(0, D_MODEL, GATE_GROUP)]

    def stream(buf, sem):
        def chunk_copy(i):
            src, _, c0 = chunks[i]
            return pltpu.make_async_copy(src.at[:, pl.ds(c0, GATE_GROUP)], buf.at[i % 2], sem.at[i % 2])

        chunk_copy(0).start()
        for i, (src, dst, c0) in enumerate(chunks):
            if i + 1 < len(chunks):
                chunk_copy(i + 1).start()
            chunk_copy(i).wait()
            w = buf[i % 2]
            if src is win_raw and C_Q <= c0 < C_Q + D_ATT:
                w = w * HEAD_DIM ** -0.5
            dst[:, c0:c0 + GATE_GROUP] = w.astype(BF16)

    pl.run_scoped(stream, pltpu.VMEM((2, D_MODEL, GATE_GROUP), F32), pltpu.SemaphoreType.DMA((2,)))
    per = GATE_GROUP // RNN_BLOCK
    zero = jnp.zeros((RNN_BLOCK, RNN_BLOCK), F32)
    for g in range(D_RNN // GATE_GROUP):
        for half, w_raw in enumerate((wga_raw, wgx_raw)):
            rows = [jnp.concatenate([-w_raw[g * per + m] if k == m else zero for k in range(per)], axis=1)
                    for m in range(per)]
            wg_ref[g, :, half * GATE_GROUP:(half + 1) * GATE_GROUP] = jnp.concatenate(rows, axis=0).astype(BF16)


def _build_bias_table(bkt, rb_ref, head):
    tbl = jnp.zeros(bkt.shape, F32)
    for k in range(N_BUCKETS):
        tbl = jnp.where(bkt == k, rb_ref[k, head], tbl)
    return tbl


def _prompt_kernel(rb_ref, sinks_ref, x_ref, bkt_ref, gpre_ref, gpost_ref,
                   cw_raw, cb_raw, ba_raw, bx_raw, lam_raw, win_raw, wga_raw, wgx_raw, wout_raw,
                   y_ref, conv_ref, rnn_ref, kw_ref, vw_ref,
                   bias_ref, xpad_ref, hc_ref, kd_ref, vt_ref, att_ref, xr_ref, hn_ref,
                   rp_ref, win_ref, wg_ref, wout_ref, *, tb, nseq):
    t = pl.program_id(0)
    nq = tb // WINDOW

    @pl.when(t == 0)
    def _():
        _prepare_params((cw_raw, cb_raw, ba_raw, bx_raw, lam_raw, win_raw, wga_raw, wgx_raw, wout_raw),
                        rp_ref, win_ref, wg_ref, wout_ref)
        bkt = bkt_ref[...]
        for h in range(N_HEADS):
            bias_ref[h] = _build_bias_table(bkt, rb_ref, h)
        xpad_ref[...] = jnp.zeros((nseq, (CONV_W - 1) * SUBLANES, D_RNN), F32)
        hc_ref[...] = jnp.zeros((nseq, 1, D_RNN), F32)
        kd_ref[:, :, 0:WINDOW, :] = jnp.zeros((nseq, N_KV_HEADS, WINDOW, LANES), BF16)
        vt_ref[:, :, 0:HEAD_DIM, 0:WINDOW] = jnp.zeros((nseq, N_KV_HEADS, HEAD_DIM, WINDOW), BF16)
        vt_ref[:, :, HEAD_DIM:, :] = jnp.ones((nseq, N_KV_HEADS, ONES_ROWS, tb + WINDOW), BF16)

    rp = rp_ref[...]
    sub8 = lax.broadcasted_iota(jnp.int32, (SUBLANES, D_RNN), 0)
    lo = lax.broadcasted_iota(jnp.int32, (tb, LANES), 1) < HEAD_DIM
    key_idx = lax.broadcasted_iota(jnp.int32, (WINDOW, WINDOW), 0)
    qry_idx = lax.broadcasted_iota(jnp.int32, (WINDOW, WINDOW), 1)
    own = key_idx <= qry_idx
    lane_lo = qry_idx < HEAD_DIM
    has_prev0 = (jnp.zeros((WINDOW, WINDOW), jnp.int32) + t) > 0
    zero_b = jnp.zeros((WINDOW, WINDOW), BF16)
    pair_lo = lax.broadcasted_iota(jnp.int32, (1, 2 * WINDOW), 1) < WINDOW
    pairs = [(qi, c) for qi in range(nq) for c in range(D_ATT // LANES)]
    gate_cols = [C_GRNN, C_GRNN + GATE_GROUP, C_GATT, C_GATT + GATE_GROUP]

    def norm(s, st):
        x = x_ref[s]
        st.update(x=x, xn=(x * _rms_scale(x) * gpre_ref[...]).astype(BF16))

    def project(s, st, name):
        c0, width = dict(x_rnn=(C_XRNN, D_RNN), kv=(C_KV, 2 * KV_DIM), q=(C_Q, D_ATT))[name]
        st[name] = _dot(st["xn"], win_ref[:, c0:c0 + width])

    clen = tb // SUBLANES
    pitch = clen + SUBLANES
    ncol = D_RNN // LANES
    ntail = CONV_W - 1

    def conv_gates(s, st):
        x_rnn = st["x_rnn"]
        for c in range(ncol):
            for k in range(SUBLANES):
                xr_ref[s, c, k * pitch:k * pitch + clen, :] = x_rnn[k * clen:(k + 1) * clen, c * LANES:(c + 1) * LANES]
        xp = jnp.concatenate(
            [jnp.concatenate([xr_ref[s, c, pl.ds(v, SUBLANES, stride=pitch), :] for v in range(clen)], axis=0)
             for c in range(ncol)], axis=1)
        tail = xp[(clen - ntail) * SUBLANES:, :]
        prev_tail = xpad_ref[s]
        heads = [jnp.where(sub8 == 0,
                           pltpu.roll(prev_tail[j * SUBLANES:(j + 1) * SUBLANES], 1, axis=0),
                           pltpu.roll(tail[j * SUBLANES:(j + 1) * SUBLANES], 1, axis=0)) for j in range(ntail)]
        xc = rp[4:5]
        for tap in range(ntail):
            shift = ntail - tap
            xc = xc + rp[tap:tap + 1] * jnp.concatenate(heads[ntail - shift:] + [xp[:(clen - shift) * SUBLANES]], axis=0)
        xc = xc + rp[3:4] * xp
        xpad_ref[s] = tail
        for j in range(ntail):
            conv_ref[j, s:s + 1, :] = tail[j * SUBLANES + SUBLANES - 1:(j + 1) * SUBLANES]
        st["xc"] = xc
        st["za"], st["zx"] = _rglru_gate_dots(xc, wg_ref)

    def gate_math(s, st):
        st["a"], st["bx"] = _rglru_gate_math(st.pop("xc"), st.pop("za"), st.pop("zx"), rp)

    def store_kv(s, st):
        st["qb"] = st.pop("q").astype(BF16)
        k01 = st["kv"][:, :KV_DIM]
        v01 = st["kv"][:, KV_DIM:]
        k10 = pltpu.roll(k01, HEAD_DIM, axis=1)
        kd_ref[s, 0, WINDOW:WINDOW + tb, :] = jnp.where(lo, k01, k10).astype(BF16)
        kd_ref[s, 1, WINDOW:WINDOW + tb, :] = jnp.where(lo, k10, k01).astype(BF16)
        v_t = v01.T.astype(BF16)
        vt_ref[s, 0, 0:HEAD_DIM, WINDOW:WINDOW + tb] = v_t[0:HEAD_DIM]
        vt_ref[s, 1, 0:HEAD_DIM, WINDOW:WINDOW + tb] = v_t[HEAD_DIM:]
        kw_ref[s] = k01[tb - WINDOW:, :].T
        vw_ref[s] = v01[tb - WINDOW:, :].T

    def gate_piece(s, st, i):
        st[("gate", i)] = _dot(st["xn"], win_ref[:, gate_cols[i]:gate_cols[i] + GATE_GROUP])

    def scores(s, st, qi, c):
        rows = slice(qi * WINDOW, (qi + 1) * WINDOW)
        win = slice(qi * WINDOW, qi * WINDOW + 2 * WINDOW)
        kvh = (2 * c) // GROUP
        qc = st["qb"][rows, c * LANES:(c + 1) * LANES]
        qpair = jnp.concatenate([jnp.where(lane_lo, qc, zero_b), jnp.where(lane_lo, zero_b, qc)], axis=0)
        sc = _dot_nt(kd_ref[s, kvh, win, :], qpair)
        p_prev, p_own, ms = [], [], []
        for half in range(2):
            head = 2 * c + half
            cols = slice(half * WINDOW, (half + 1) * WINDOW)
            sf = jnp.where(own, sc[WINDOW:, cols], sc[:WINDOW, cols]) + bias_ref[head]
            if qi == 0:
                sf = jnp.where(own | has_prev0, sf, NEG_INF)
            m = jnp.max(sf, axis=0, keepdims=True)
            e = jnp.exp(sf - m).astype(BF16)
            p_prev.append(jnp.where(own, zero_b, e))
            p_own.append(jnp.where(own, e, zero_b))
            ms.append(m)
        p = jnp.concatenate([jnp.concatenate(p_prev, axis=1), jnp.concatenate(p_own, axis=1)], axis=0)
        return p, jnp.concatenate(ms, axis=1)

    def values(s, qi, c, p, m):
        rows = slice(qi * WINDOW, (qi + 1) * WINDOW)
        win = slice(qi * WINDOW, qi * WINDOW + 2 * WINDOW)
        kvh = (2 * c) // GROUP
        oa = _dot(vt_ref[s, kvh, :, win], p)
        sink = jnp.where(pair_lo, sinks_ref[2 * c], sinks_ref[2 * c + 1])
        den = oa[HEAD_DIM:HEAD_DIM + 1, :] + jnp.exp(sink - m)
        o = oa[0:HEAD_DIM, :] * (1.0 / den)
        o_t = jnp.concatenate([o[:, :WINDOW], o[:, WINDOW:]], axis=0)
        att_ref[s, rows, c * LANES:(c + 1) * LANES] = o_t.T

    def score(s, st, n):
        st[("p", n)] = scores(s, st, *pairs[n])

    def value(s, st, n):
        values(s, *pairs[n], *st.pop(("p", n)))

    def att_finish(s, st):
        kd_ref[s, :, 0:WINDOW, :] = kd_ref[s, :, tb:tb + WINDOW, :]
        vt_ref[s, :, :, 0:WINDOW] = vt_ref[s, :, :, tb:tb + WINDOW]
        g_att = jnp.concatenate([st[("gate", 2)], st[("gate", 3)]], axis=1)
        st["att_out"] = (att_ref[s] * _silu(g_att)).astype(BF16)

    def scan(s, st):
        a3 = st["a"].reshape(clen, SUBLANES, D_RNN)
        b3 = st["bx"].reshape(clen, SUBLANES, D_RNN)
        hl, pr = [b3[0]], [a3[0]]
        for v in range(1, clen):
            hl.append(a3[v] * hl[-1] + b3[v])
            pr.append(a3[v] * pr[-1])
        h_end, p_end = hl[-1], pr[-1]
        h_in = jnp.broadcast_to(hc_ref[s], (SUBLANES, D_RNN))
        entry = h_in
        for _ in range(SUBLANES - 1):
            entry = jnp.where(sub8 == 0, h_in, pltpu.roll(h_end + p_end * entry, 1, axis=0))
        carry = (h_end + p_end * entry)[SUBLANES - 1:SUBLANES, :]
        hc_ref[s] = carry
        rnn_ref[0, s:s + 1, :] = carry
        for v in range(clen):
            hv = hl[v] + pr[v] * entry
            for c in range(ncol):
                hn_ref[s, c, pl.ds(v, SUBLANES, stride=pitch), :] = hv[:, c * LANES:(c + 1) * LANES]
        h = jnp.concatenate(
            [jnp.concatenate([hn_ref[s, c, k * pitch:k * pitch + clen, :] for k in range(SUBLANES)], axis=0)
             for c in range(ncol)], axis=1)
        g_rnn = jnp.concatenate([st[("gate", 0)], st[("gate", 1)]], axis=1)
        st["rnn_out"] = (h * _silu(g_rnn)).astype(BF16)

    def out_piece(s, st, j):
        cols = slice(j * GATE_GROUP, (j + 1) * GATE_GROUP)
        st[("out", j)] = _dot(st["rnn_out"], wout_ref[0:D_RNN, cols]) + _dot(st["att_out"], wout_ref[D_RNN:, cols])

    def finish(s, st):
        out = jnp.concatenate([st[("out", j)] for j in range(D_MODEL // GATE_GROUP)], axis=1)
        y_ref[s] = st["x"] + out * _rms_scale(out) * gpost_ref[...]

    tasks = dict(norm=norm, project=project, conv_gates=conv_gates, gate_math=gate_math, store_kv=store_kv, gate_piece=gate_piece,
                 score=score, value=value, att_finish=att_finish, scan=scan, out_piece=out_piece, finish=finish)
    sts = [dict() for _ in range(nseq)]
    for name, s, *arg in _step_program(len(pairs)):
        tasks[name](s, sts[s], *arg)


def _raw_param_specs(const):
    hbm = pl.BlockSpec(memory_space=pl.ANY)
    return [
        const(CONV_W, D_RNN), const(1, D_RNN), const(1, D_RNN), const(1, D_RNN), const(1, D_RNN),
        hbm,
        const(N_RNN_BLOCKS, RNN_BLOCK, RNN_BLOCK), const(N_RNN_BLOCKS, RNN_BLOCK, RNN_BLOCK),
        hbm,
    ]


def _param_scratch():
    return [
        pltpu.VMEM((SUBLANES, D_RNN), F32),
        pltpu.VMEM((D_MODEL, D_IN), BF16),
        pltpu.VMEM((D_RNN // GATE_GROUP, GATE_GROUP, 2 * GATE_GROUP), BF16),
        pltpu.VMEM((D_MODEL, D_MODEL), BF16),
    ]


def _prompt_call(x, rb, sinks, bkt, gpre, gpost, raw):
    bsz, seq, _ = x.shape
    tb = PROMPT_BLOCK
    nt = seq // tb
    const = lambda *shape: pl.BlockSpec(shape, lambda t: (0,) * len(shape), pipeline_mode=pl.Buffered(1))
    whole = lambda *shape: pl.BlockSpec(shape, lambda t: (0,) * len(shape))
    smem = pl.BlockSpec(memory_space=pltpu.SMEM)
    out_shapes = (
        jax.ShapeDtypeStruct((bsz, seq, D_MODEL), F32),
        jax.ShapeDtypeStruct((CONV_W - 1, bsz, D_RNN), F32),
        jax.ShapeDtypeStruct((1, bsz, D_RNN), F32),
        jax.ShapeDtypeStruct((bsz, KV_DIM, WINDOW), F32),
        jax.ShapeDtypeStruct((bsz, KV_DIM, WINDOW), F32),
    )
    return pl.pallas_call(
        functools.partial(_prompt_kernel, tb=tb, nseq=bsz),
        grid=(nt,),
        in_specs=[
            smem, smem,
            pl.BlockSpec((bsz, tb, D_MODEL), lambda t: (0, t, 0)),
            const(WINDOW, WINDOW),
            const(1, D_MODEL), const(1, D_MODEL),
            *_raw_param_specs(const),
        ],
        out_specs=(
            pl.BlockSpec((bsz, tb, D_MODEL), lambda t: (0, t, 0)),
            whole(CONV_W - 1, bsz, D_RNN),
            whole(1, bsz, D_RNN),
            whole(bsz, KV_DIM, WINDOW),
            whole(bsz, KV_DIM, WINDOW),
        ),
        out_shape=out_shapes,
        scratch_shapes=[
            pltpu.VMEM((N_HEADS, WINDOW, WINDOW), F32),
            pltpu.VMEM((bsz, (CONV_W - 1) * SUBLANES, D_RNN), F32),
            pltpu.VMEM((bsz, 1, D_RNN), F32),
            pltpu.VMEM((bsz, N_KV_HEADS, tb + WINDOW, LANES), BF16),
            pltpu.VMEM((bsz, N_KV_HEADS, HEAD_DIM + ONES_ROWS, tb + WINDOW), BF16),
            pltpu.VMEM((bsz, tb, D_ATT), F32),
            pltpu.VMEM((bsz, D_RNN // LANES, tb + SUBLANES * SUBLANES, LANES), F32),
            pltpu.VMEM((bsz, D_RNN // LANES, tb + SUBLANES * SUBLANES, LANES), F32),
            *_param_scratch(),
        ],
        compiler_params=pltpu.CompilerParams(
            dimension_semantics=("arbitrary",),
            vmem_limit_bytes=VMEM_LIMIT),
        name="prompt_layer",
    )(rb, sinks, x, bkt, gpre, gpost, *raw)


def _sample_kernel(rb_ref, sinks_ref, x_ref, bktrow_ref, gpre_ref, gpost_ref,
                   cw_raw, cb_raw, ba_raw, bx_raw, lam_raw, win_raw, wga_raw, wgx_raw, wout_raw,
                   sc_ref, h0_ref, kc_ref, vc_ref,
                   y_ref, conv_ref, rnn_ref, ko_ref, vo_ref,
                   qz_ref, o_ref, knew_ref, vnew_ref, rnnout_ref, gatt_ref, bias_ref, sink_ref,
                   rp_ref, win_ref, wg_ref, wout_ref, *, nb, bb):
    g = pl.program_id(0)
    last_g = pl.num_programs(0) - 1
    sub = lax.broadcasted_iota(jnp.int32, (SUBLANES, LANES), 0)
    lane_lo = lax.broadcasted_iota(jnp.int32, (nb, LANES), 1) < HEAD_DIM

    @pl.when(g == 0)
    def _():
        _prepare_params((cw_raw, cb_raw, ba_raw, bx_raw, lam_raw, win_raw, wga_raw, wgx_raw, wout_raw),
                        rp_ref, win_ref, wg_ref, wout_ref)
        bkt = jnp.broadcast_to(bktrow_ref[...], (SUBLANES, LANES))
        bias = jnp.zeros((SUBLANES, LANES), F32)
        sinkm = jnp.zeros((SUBLANES, LANES), F32)
        for h in range(N_HEADS):
            bias = jnp.where(sub == h, _build_bias_table(bkt, rb_ref, h), bias)
            sinkm = jnp.where(sub == h, sinks_ref[h], sinkm)
        bias_ref[...] = bias
        sink_ref[...] = sinkm

        x = x_ref[...]
        xn = (x * _rms_scale(x) * gpre_ref[...]).astype(BF16)
        rp = rp_ref[...]
        x_rnn = _dot(xn, win_ref[:, C_XRNN:C_XRNN + D_RNN])
        xc = rp[4:5] + rp[0:1] * sc_ref[0]
        xc = xc + rp[1:2] * sc_ref[1]
        xc = xc + rp[2:3] * sc_ref[2]
        xc = xc + rp[3:4] * x_rnn
        conv_ref[0] = sc_ref[1]
        conv_ref[1] = sc_ref[2]
        conv_ref[2] = x_rnn
        a, bx = _rglru_gate_math(xc, *_rglru_gate_dots(xc, wg_ref), rp)
        h = a * h0_ref[...] + bx
        rnn_ref[...] = h
        g_rnn = _dot(xn, win_ref[:, C_GRNN:C_GRNN + D_RNN])
        rnnout_ref[...] = h * _silu(g_rnn)
        gatt_ref[...] = _dot(xn, win_ref[:, C_GATT:C_GATT + D_ATT])
        kv = _dot(xn, win_ref[:, C_KV:C_KV + 2 * KV_DIM])
        knew_ref[...] = kv[:, :KV_DIM]
        vnew_ref[...] = kv[:, KV_DIM:]
        q = _dot(xn, win_ref[:, C_Q:C_Q + D_ATT])
        for head in range(N_HEADS):
            c = head // 2
            kvh = head // GROUP
            qc = q[:, c * LANES:(c + 1) * LANES]
            if (head % 2) != kvh:
                qc = pltpu.roll(qc, HEAD_DIM, axis=1)
            qz = jnp.where(lane_lo, qc, 0.0) if kvh == 0 else jnp.where(lane_lo, 0.0, qc)
            qz_ref[pl.ds(head, nb, stride=N_HEADS), :] = qz

    rowi = lax.broadcasted_iota(jnp.int32, (WINDOW, LANES), 0)
    newest = rowi == WINDOW - 1

    bias = bias_ref[...]
    sinkm = sink_ref[...]
    row0 = [pl.multiple_of((g * bb + bi) * N_HEADS, N_HEADS) for bi in range(bb)]
    scores = []
    for bi in range(bb):
        newk = jnp.where(newest, knew_ref[pl.ds(g * bb + bi, 1), :], pltpu.roll(kc_ref[bi].T, WINDOW - 1, axis=0))
        ko_ref[bi] = newk.T
        qz = qz_ref[pl.ds(row0[bi], N_HEADS), :].astype(BF16)
        scores.append(_dot_nt(qz, newk.astype(BF16)) + bias)
    probs = []
    for s in scores:
        m = jnp.maximum(jnp.max(s, axis=1, keepdims=True), sinkm)
        e = jnp.exp(s - m)
        den = jnp.sum(e, axis=1, keepdims=True) + jnp.exp(sinkm - m)
        probs.append((e * (1.0 / den)).astype(BF16))
    for bi in range(bb):
        newv = jnp.where(newest, vnew_ref[pl.ds(g * bb + bi, 1), :], pltpu.roll(vc_ref[bi].T, WINDOW - 1, axis=0))
        vo_ref[bi] = newv.T
        o_ref[pl.ds(row0[bi], N_HEADS), :] = _dot(probs[bi], newv.astype(BF16))

    @pl.when(g == last_g)
    def _():
        cols = []
        for c in range(D_ATT // LANES):
            kvh = (2 * c) // GROUP
            halves = []
            for half in range(2):
                oh = o_ref[pl.ds(2 * c + half, nb, stride=N_HEADS), :]
                if half != kvh:
                    oh = pltpu.roll(oh, HEAD_DIM, axis=1)
                halves.append(oh)
            cols.append(jnp.where(lane_lo, halves[0], halves[1]))
        att = jnp.concatenate(cols, axis=1)
        att_out = (att * _silu(gatt_ref[...])).astype(BF16)
        rnn_out = rnnout_ref[...].astype(BF16)
        out = _dot(rnn_out, wout_ref[0:D_RNN, :]) + _dot(att_out, wout_ref[D_RNN:, :])
        y_ref[...] = x_ref[...] + out * _rms_scale(out) * gpost_ref[...]


def _sample_call(x, rb, sinks, bktrow, gpre, gpost, raw, sc, h0, kc_t, vc_t):
    nb = x.shape[0]
    bb = SAMPLE_BLOCK
    const = lambda *shape: pl.BlockSpec(shape, lambda g: (0,) * len(shape), pipeline_mode=pl.Buffered(1))
    smem = pl.BlockSpec(memory_space=pltpu.SMEM)
    win_t_spec = pl.BlockSpec((bb, KV_DIM, WINDOW), lambda g: (g, 0, 0))
    out_shapes = (
        jax.ShapeDtypeStruct((nb, D_MODEL), F32),
        jax.ShapeDtypeStruct((CONV_W - 1, nb, D_RNN), F32),
        jax.ShapeDtypeStruct((nb, D_RNN), F32),
        jax.ShapeDtypeStruct((nb, KV_DIM, WINDOW), F32),
        jax.ShapeDtypeStruct((nb, KV_DIM, WINDOW), F32),
    )
    return pl.pallas_call(
        functools.partial(_sample_kernel, nb=nb, bb=bb),
        grid=(nb // bb,),
        in_specs=[
            smem, smem,
            const(nb, D_MODEL),
            const(1, LANES),
            const(1, D_MODEL), const(1, D_MODEL),
            *_raw_param_specs(const),
            const(CONV_W - 1, nb, D_RNN),
            const(nb, D_RNN),
            win_t_spec, win_t_spec,
        ],
        out_specs=(
            pl.BlockSpec((nb, D_MODEL), lambda g: (0, 0)),
            pl.BlockSpec((CONV_W - 1, nb, D_RNN), lambda g: (0, 0, 0)),
            pl.BlockSpec((nb, D_RNN), lambda g: (0, 0)),
            win_t_spec, win_t_spec,
        ),
        out_shape=out_shapes,
        scratch_shapes=[
            pltpu.VMEM((nb * N_HEADS, LANES), F32),
            pltpu.VMEM((nb * N_HEADS, LANES), F32),
            pltpu.VMEM((nb, KV_DIM), F32),
            pltpu.VMEM((nb, KV_DIM), F32),
            pltpu.VMEM((nb, D_RNN), F32),
            pltpu.VMEM((nb, D_ATT), F32),
            pltpu.VMEM((SUBLANES, LANES), F32),
            pltpu.VMEM((SUBLANES, LANES), F32),
            *_param_scratch(),
        ],
        compiler_params=pltpu.CompilerParams(
            dimension_semantics=("arbitrary",),
            vmem_limit_bytes=VMEM_LIMIT),
        name="sample_layer",
    )(rb, sinks, x, bktrow, gpre, gpost, *raw, sc, h0, kc_t, vc_t)


def kernel(x_prompt, x_sample, state_conv, state_rnn, cache_k_win, cache_v_win, norm_pre, norm_post,
           w_in, conv_w, conv_b, w_gate_a, b_gate_a, w_gate_x, b_gate_x, lru_lambda, attn_sinks,
           rel_bias, w_out):
    assert w_in.shape[0] == 1, "single-layer trunk"
    bsz, seq, _ = x_prompt.shape
    nb = x_sample.shape[0]
    wb = cache_k_win.shape[2]
    assert wb == WINDOW and x_sample.shape[1] == 1 and seq % PROMPT_BLOCK == 0 and nb % SAMPLE_BLOCK == 0
    assert bsz == PROMPT_SEQS, "the step program is written for this many prompt sequences"

    raw = (conv_w[0], conv_b, b_gate_a, b_gate_x, lru_lambda, w_in[0], w_gate_a[0], w_gate_x[0], w_out[0])
    gpre = norm_pre.reshape(1, D_MODEL)
    gpost = norm_post.reshape(1, D_MODEL)
    sinks = attn_sinks[0]
    bkt_np = _folded_bucket_table()
    bkt = jnp.asarray(np.ascontiguousarray(bkt_np.T))
    bktrow = jnp.asarray(bkt_np[WINDOW - 1:WINDOW, :])

    y_p, conv_p, rnn_p, kw_p, vw_p = _prompt_call(x_prompt, rel_bias, sinks, bkt, gpre, gpost, raw)

    feature_major = lambda z: jnp.transpose(z[0], (0, 2, 3, 1)).reshape(nb, KV_DIM, wb)
    y_s, conv_s, rnn_s, kw_s, vw_s = _sample_call(
        x_sample.reshape(nb, D_MODEL), rel_bias, sinks, bktrow, gpre, gpost, raw,
        jnp.transpose(state_conv[0], (1, 0, 2)), state_rnn[0],
        feature_major(cache_k_win), feature_major(cache_v_win))

    kv5 = lambda z, n: jnp.transpose(z.reshape(n, N_KV_HEADS, HEAD_DIM, wb), (0, 3, 1, 2))[None]
    return (y_p, y_s.reshape(nb, 1, D_MODEL),
            jnp.transpose(conv_p, (1, 0, 2))[None], rnn_p, kv5(kw_p, bsz), kv5(vw_p, bsz),
            jnp.transpose(conv_s, (1, 0, 2))[None], rnn_s[None], kv5(kw_s, nb), kv5(vw_s, nb))
```

```python
import functools
import math

import numpy as np
import jax
import jax.numpy as jnp
from jax import lax
from jax.experimental import pallas as pl
from jax.experimental.pallas import tpu as pltpu

D_MODEL = 1024
D_RNN = 512
D_ATT = 512
HEAD_DIM = 64
N_HEADS = 8
N_KV_HEADS = 2
GROUP = N_HEADS // N_KV_HEADS
KV_DIM = N_KV_HEADS * HEAD_DIM
N_RNN_BLOCKS = 8
RNN_BLOCK = D_RNN // N_RNN_BLOCKS
CONV_W = 4
LRU_C = 8.0
WINDOW = 128
N_BUCKETS = 32
MAX_DISTANCE = 128
EPS = 1e-6
NEG_INF = -1e30
D_IN = 2 * D_RNN + 2 * D_ATT + 2 * KV_DIM

C_XRNN = 0
C_GRNN = D_RNN
C_Q = 2 * D_RNN
C_KV = 2 * D_RNN + D_ATT
C_GATT = 2 * D_RNN + D_ATT + 2 * KV_DIM

SUBLANES = 8
LANES = 128
GATE_GROUP = 256
ONES_ROWS = 16
SCORE_LOOKAHEAD = 2
PROMPT_SEQS = 2
PROMPT_BLOCK = 512
SAMPLE_BLOCK = 16
PREP_ROWS = 256
VMEM_LIMIT = 56 * 1024 * 1024

LOG2E = 1.4426950408889634
TINY = 1e-37
F32 = jnp.float32
BF16 = jnp.bfloat16


def _spread(groups, bulk):
    out, done = [], 0
    for i, group in enumerate(groups):
        out += group
        want = (i + 1) * len(bulk) // len(groups)
        out += bulk[done:want]
        done = want
    return out


def _step_program(npairs):
    a, b = range(PROMPT_SEQS)
    n_out = D_MODEL // GATE_GROUP

    def attention(s):
        groups = [[("score", s, n) for n in range(SCORE_LOOKAHEAD)]]
        for n in range(npairs):
            ahead = [("score", s, n + SCORE_LOOKAHEAD)] if n + SCORE_LOOKAHEAD < npairs else []
            groups.append(ahead + [("value", s, n)])
        return groups

    project = lambda s: [("project", s, name) for name in ("x_rnn", "kv", "q")]
    gates = lambda s: [("gate_piece", s, i) for i in range(4)]
    outs = lambda s: [("out_piece", s, j) for j in range(n_out)]
    pa, pb = project(a), project(b)
    prog = [("norm", a), pa[0], ("norm", b), pa[1], pa[2], ("conv_gates", a), ("store_kv", a)]
    prog += _spread(attention(a), [("gate_math", a)] + pb + gates(a) + [("scan", a)])
    prog += [("att_finish", a), ("conv_gates", b), ("store_kv", b)]
    prog += _spread(attention(b), [("gate_math", b)] + gates(b) + outs(a) + [("scan", b)])
    prog += [("finish", a), ("att_finish", b)] + outs(b) + [("finish", b)]
    return prog


def _t5_bucket_np(dist):
    dist = np.maximum(dist, 0)
    max_exact = N_BUCKETS // 2
    d = np.maximum(dist, 1).astype(np.float32)
    ratio = np.log(d / np.float32(max_exact)) / np.float32(math.log(MAX_DISTANCE / max_exact))
    large = max_exact + (ratio * np.float32(N_BUCKETS - max_exact)).astype(np.int32)
    large = np.minimum(large, N_BUCKETS - 1)
    return np.where(dist < max_exact, dist, large).astype(np.int32)


def _folded_bucket_table():
    i = np.arange(WINDOW)[:, None]
    j = np.arange(WINDOW)[None, :]
    dist = np.where(j <= i, i - j, i + WINDOW - j)
    return _t5_bucket_np(dist)


def _sigmoid_of_neg(xn):
    return 1.0 / (1.0 + jnp.exp(xn))


def _silu(x):
    return x * (1.0 / (1.0 + jnp.exp2(x * -LOG2E)))


def _softplus(x):
    return jnp.maximum(x, 0.0) + jnp.log1p(jnp.exp(-jnp.abs(x)))


def _rms_scale(x):
    return lax.rsqrt(jnp.mean(x * x, axis=-1, keepdims=True) + EPS)


def _dot(a, b):
    return jnp.dot(a, b, preferred_element_type=F32)


def _dot_nt(a, b):
    return lax.dot_general(a, b, (((1,), (1,)), ((), ())), preferred_element_type=F32)


def _rglru_gate_dots(xc, wg_ref):
    xcb = xc.astype(BF16)
    zs = [_dot(xcb[:, g * GATE_GROUP:(g + 1) * GATE_GROUP], wg_ref[g])
          for g in range(D_RNN // GATE_GROUP)]
    za = jnp.concatenate([z[:, :GATE_GROUP] for z in zs], axis=1)
    zx = jnp.concatenate([z[:, GATE_GROUP:] for z in zs], axis=1)
    return za, zx


def _rglru_gate_math(xc, za, zx, rp):
    r = _sigmoid_of_neg(za + rp[5:6])
    i = _sigmoid_of_neg(zx + rp[6:7])
    log_a = (-LRU_C * r) * _softplus(-rp[7:8])
    a = jnp.exp(log_a)
    u = 1.0 - a * a
    bx = (u * lax.rsqrt(jnp.maximum(u, TINY))) * (i * xc)
    return a, bx


def _prep_kernel(cw_raw, cb_raw, ba_raw, bx_raw, lam_raw, win_raw, wga_raw, wgx_raw, wout_raw,
                 rp_ref, win_ref, wg_ref, wout_ref):
    for c0 in range(0, D_IN, GATE_GROUP):
        w = win_raw[:, c0:c0 + GATE_GROUP]
        if C_Q <= c0 < C_Q + D_ATT:
            w = w * HEAD_DIM ** -0.5
        win_ref[:, c0:c0 + GATE_GROUP] = w.astype(BF16)
    wout_ref[...] = wout_raw[...].astype(BF16)

    @pl.when(pl.program_id(0) == 0)
    def _():
        rp_ref[0:CONV_W, :] = cw_raw[...]
        rp_ref[4:5, :] = cb_raw[...]
        rp_ref[5:6, :] = -ba_raw[...]
        rp_ref[6:7, :] = -bx_raw[...]
        rp_ref[7:8, :] = lam_raw[...]
        per = GATE_GROUP // RNN_BLOCK
        zero = jnp.zeros((RNN_BLOCK, RNN_BLOCK), F32)
        for g in range(D_RNN // GATE_GROUP):
            for half, w_raw in enumerate((wga_raw, wgx_raw)):
                rows = [jnp.concatenate([-w_raw[g * per + m] if k == m else zero for k in range(per)], axis=1)
                        for m in range(per)]
                wg_ref[g, :, half * GATE_GROUP:(half + 1) * GATE_GROUP] = jnp.concatenate(rows, axis=0).astype(BF16)


def _prep_call(conv_w, conv_b, b_gate_a, b_gate_x, lru_lambda, w_in, w_gate_a, w_gate_x, w_out):
    rows = PREP_ROWS
    whole = lambda *shape: pl.BlockSpec(shape, lambda i: (0,) * len(shape))
    return pl.pallas_call(
        _prep_kernel,
        grid=(D_MODEL // rows,),
        in_specs=[
            whole(CONV_W, D_RNN), whole(1, D_RNN), whole(1, D_RNN), whole(1, D_RNN), whole(1, D_RNN),
            pl.BlockSpec((rows, D_IN), lambda i: (i, 0)),
            whole(N_RNN_BLOCKS, RNN_BLOCK, RNN_BLOCK), whole(N_RNN_BLOCKS, RNN_BLOCK, RNN_BLOCK),
            pl.BlockSpec((rows, D_MODEL), lambda i: (i, 0)),
        ],
        out_specs=(
            whole(SUBLANES, D_RNN),
            pl.BlockSpec((rows, D_IN), lambda i: (i, 0)),
            whole(D_RNN // GATE_GROUP, GATE_GROUP, 2 * GATE_GROUP),
            pl.BlockSpec((rows, D_MODEL), lambda i: (i, 0)),
        ),
        out_shape=(
            jax.ShapeDtypeStruct((SUBLANES, D_RNN), F32),
            jax.ShapeDtypeStruct((D_MODEL, D_IN), BF16),
            jax.ShapeDtypeStruct((D_RNN // GATE_GROUP, GATE_GROUP, 2 * GATE_GROUP), BF16),
            jax.ShapeDtypeStruct((D_MODEL, D_MODEL), BF16),
        ),
        compiler_params=pltpu.CompilerParams(dimension_semantics=("arbitrary",)),
        name="param_layout",
    )(conv_w, conv_b, b_gate_a, b_gate_x, lru_lambda, w_in, w_gate_a, w_gate_x, w_out)


def _build_bias_table(bkt, rb_ref, head):
    tbl = jnp.zeros(bkt.shape, F32)
    for k in range(N_BUCKETS):
        tbl = jnp.where(bkt == k, rb_ref[k, head], tbl)
    return tbl


def _prompt_kernel(rb_ref, sinks_ref, x_ref, bkt_ref, gpre_ref, gpost_ref, rp_ref, win_ref, wg_ref, wout_ref,
                   y_ref, conv_ref, rnn_ref, kw_ref, vw_ref,
                   bias_ref, xpad_ref, hc_ref, kd_ref, vt_ref, att_ref, xr_ref, hn_ref, *, tb, nseq):
    t = pl.program_id(0)
    nq = tb // WINDOW

    @pl.when(t == 0)
    def _():
        bkt = bkt_ref[...]
        for h in range(N_HEADS):
            bias_ref[h] = _build_bias_table(bkt, rb_ref, h)
        xpad_ref[...] = jnp.zeros((nseq, (CONV_W - 1) * SUBLANES, D_RNN), F32)
        hc_ref[...] = jnp.zeros((nseq, 1, D_RNN), F32)
        kd_ref[:, :, 0:WINDOW, :] = jnp.zeros((nseq, N_KV_HEADS, WINDOW, LANES), BF16)
        vt_ref[:, :, 0:HEAD_DIM, 0:WINDOW] = jnp.zeros((nseq, N_KV_HEADS, HEAD_DIM, WINDOW), BF16)
        vt_ref[:, :, HEAD_DIM:, :] = jnp.ones((nseq, N_KV_HEADS, ONES_ROWS, tb + WINDOW), BF16)

    rp = rp_ref[...]
    sub8 = lax.broadcasted_iota(jnp.int32, (SUBLANES, D_RNN), 0)
    lo = lax.broadcasted_iota(jnp.int32, (tb, LANES), 1) < HEAD_DIM
    key_idx = lax.broadcasted_iota(jnp.int32, (WINDOW, WINDOW), 0)
    qry_idx = lax.broadcasted_iota(jnp.int32, (WINDOW, WINDOW), 1)
    own = key_idx <= qry_idx
    lane_lo = qry_idx < HEAD_DIM
    has_prev0 = (jnp.zeros((WINDOW, WINDOW), jnp.int32) + t) > 0
    zero_b = jnp.zeros((WINDOW, WINDOW), BF16)
    pair_lo = lax.broadcasted_iota(jnp.int32, (1, 2 * WINDOW), 1) < WINDOW
    pairs = [(qi, c) for qi in range(nq) for c in range(D_ATT // LANES)]
    gate_cols = [C_GRNN, C_GRNN + GATE_GROUP, C_GATT, C_GATT + GATE_GROUP]

    def norm(s, st):
        x = x_ref[s]
        st.update(x=x, xn=(x * _rms_scale(x) * gpre_ref[...]).astype(BF16))

    def project(s, st, name):
        c0, width = dict(x_rnn=(C_XRNN, D_RNN), kv=(C_KV, 2 * KV_DIM), q=(C_Q, D_ATT))[name]
        st[name] = _dot(st["xn"], win_ref[:, c0:c0 + width])

    clen = tb // SUBLANES
    pitch = clen + SUBLANES
    ncol = D_RNN // LANES
    ntail = CONV_W - 1

    def conv_gates(s, st):
        x_rnn = st["x_rnn"]
        for c in range(ncol):
            for k in range(SUBLANES):
                xr_ref[s, c, k * pitch:k * pitch + clen, :] = x_rnn[k * clen:(k + 1) * clen, c * LANES:(c + 1) * LANES]
        xp = jnp.concatenate(
            [jnp.concatenate([xr_ref[s, c, pl.ds(v, SUBLANES, stride=pitch), :] for v in range(clen)], axis=0)
             for c in range(ncol)], axis=1)
        tail = xp[(clen - ntail) * SUBLANES:, :]
        prev_tail = xpad_ref[s]
        heads = [jnp.where(sub8 == 0,
                           pltpu.roll(prev_tail[j * SUBLANES:(j + 1) * SUBLANES], 1, axis=0),
                           pltpu.roll(tail[j * SUBLANES:(j + 1) * SUBLANES], 1, axis=0)) for j in range(ntail)]
        xc = rp[4:5]
        for tap in range(ntail):
            shift = ntail - tap
            xc = xc + rp[tap:tap + 1] * jnp.concatenate(heads[ntail - shift:] + [xp[:(clen - shift) * SUBLANES]], axis=0)
        xc = xc + rp[3:4] * xp
        xpad_ref[s] = tail
        for j in range(ntail):
            conv_ref[j, s:s + 1, :] = tail[j * SUBLANES + SUBLANES - 1:(j + 1) * SUBLANES]
        st["xc"] = xc
        st["za"], st["zx"] = _rglru_gate_dots(xc, wg_ref)

    def gate_math(s, st):
        st["a"], st["bx"] = _rglru_gate_math(st.pop("xc"), st.pop("za"), st.pop("zx"), rp)

    def store_kv(s, st):
        st["qb"] = st.pop("q").astype(BF16)
        k01 = st["kv"][:, :KV_DIM]
        v01 = st["kv"][:, KV_DIM:]
        k10 = pltpu.roll(k01, HEAD_DIM, axis=1)
        kd_ref[s, 0, WINDOW:WINDOW + tb, :] = jnp.where(lo, k01, k10).astype(BF16)
        kd_ref[s, 1, WINDOW:WINDOW + tb, :] = jnp.where(lo, k10, k01).astype(BF16)
        v_t = v01.T.astype(BF16)
        vt_ref[s, 0, 0:HEAD_DIM, WINDOW:WINDOW + tb] = v_t[0:HEAD_DIM]
        vt_ref[s, 1, 0:HEAD_DIM, WINDOW:WINDOW + tb] = v_t[HEAD_DIM:]
        kw_ref[s] = k01[tb - WINDOW:, :].T
        vw_ref[s] = v01[tb - WINDOW:, :].T

    def gate_piece(s, st, i):
        st[("gate", i)] = _dot(st["xn"], win_ref[:, gate_cols[i]:gate_cols[i] + GATE_GROUP])

    def scores(s, st, qi, c):
        rows = slice(qi * WINDOW, (qi + 1) * WINDOW)
        win = slice(qi * WINDOW, qi * WINDOW + 2 * WINDOW)
        kvh = (2 * c) // GROUP
        qc = st["qb"][rows, c * LANES:(c + 1) * LANES]
        qpair = jnp.concatenate([jnp.where(lane_lo, qc, zero_b), jnp.where(lane_lo, zero_b, qc)], axis=0)
        sc = _dot_nt(kd_ref[s, kvh, win, :], qpair)
        p_prev, p_own, ms = [], [], []
        for half in range(2):
            head = 2 * c + half
            cols = slice(half * WINDOW, (half + 1) * WINDOW)
            sf = jnp.where(own, sc[WINDOW:, cols], sc[:WINDOW, cols]) + bias_ref[head]
            if qi == 0:
                sf = jnp.where(own | has_prev0, sf, NEG_INF)
            m = jnp.max(sf, axis=0, keepdims=True)
            e = jnp.exp(sf - m).astype(BF16)
            p_prev.append(jnp.where(own, zero_b, e))
            p_own.append(jnp.where(own, e, zero_b))
            ms.append(m)
        p = jnp.concatenate([jnp.concatenate(p_prev, axis=1), jnp.concatenate(p_own, axis=1)], axis=0)
        return p, jnp.concatenate(ms, axis=1)

    def values(s, qi, c, p, m):
        rows = slice(qi * WINDOW, (qi + 1) * WINDOW)
        win = slice(qi * WINDOW, qi * WINDOW + 2 * WINDOW)
        kvh = (2 * c) // GROUP
        oa = _dot(vt_ref[s, kvh, :, win], p)
        sink = jnp.where(pair_lo, sinks_ref[2 * c], sinks_ref[2 * c + 1])
        den = oa[HEAD_DIM:HEAD_DIM + 1, :] + jnp.exp(sink - m)
        o = oa[0:HEAD_DIM, :] * (1.0 / den)
        o_t = jnp.concatenate([o[:, :WINDOW], o[:, WINDOW:]], axis=0)
        att_ref[s, rows, c * LANES:(c + 1) * LANES] = o_t.T

    def score(s, st, n):
        st[("p", n)] = scores(s, st, *pairs[n])

    def value(s, st, n):
        values(s, *pairs[n], *st.pop(("p", n)))

    def att_finish(s, st):
        kd_ref[s, :, 0:WINDOW, :] = kd_ref[s, :, tb:tb + WINDOW, :]
        vt_ref[s, :, :, 0:WINDOW] = vt_ref[s, :, :, tb:tb + WINDOW]
        g_att = jnp.concatenate([st[("gate", 2)], st[("gate", 3)]], axis=1)
        st["att_out"] = (att_ref[s] * _silu(g_att)).astype(BF16)

    def scan(s, st):
        a3 = st["a"].reshape(clen, SUBLANES, D_RNN)
        b3 = st["bx"].reshape(clen, SUBLANES, D_RNN)
        hl, pr = [b3[0]], [a3[0]]
        for v in range(1, clen):
            hl.append(a3[v] * hl[-1] + b3[v])
            pr.append(a3[v] * pr[-1])
        h_end, p_end = hl[-1], pr[-1]
        h_in = jnp.broadcast_to(hc_ref[s], (SUBLANES, D_RNN))
        entry = h_in
        for _ in range(SUBLANES - 1):
            entry = jnp.where(sub8 == 0, h_in, pltpu.roll(h_end + p_end * entry, 1, axis=0))
        carry = (h_end + p_end * entry)[SUBLANES - 1:SUBLANES, :]
        hc_ref[s] = carry
        rnn_ref[0, s:s + 1, :] = carry
        for v in range(clen):
            hv = hl[v] + pr[v] * entry
            for c in range(ncol):
                hn_ref[s, c, pl.ds(v, SUBLANES, stride=pitch), :] = hv[:, c * LANES:(c + 1) * LANES]
        h = jnp.concatenate(
            [jnp.concatenate([hn_ref[s, c, k * pitch:k * pitch + clen, :] for k in range(SUBLANES)], axis=0)
             for c in range(ncol)], axis=1)
        g_rnn = jnp.concatenate([st[("gate", 0)], st[("gate", 1)]], axis=1)
        st["rnn_out"] = (h * _silu(g_rnn)).astype(BF16)

    def out_piece(s, st, j):
        cols = slice(j * GATE_GROUP, (j + 1) * GATE_GROUP)
        st[("out", j)] = _dot(st["rnn_out"], wout_ref[0:D_RNN, cols]) + _dot(st["att_out"], wout_ref[D_RNN:, cols])

    def finish(s, st):
        out = jnp.concatenate([st[("out", j)] for j in range(D_MODEL // GATE_GROUP)], axis=1)
        y_ref[s] = st["x"] + out * _rms_scale(out) * gpost_ref[...]

    tasks = dict(norm=norm, project=project, conv_gates=conv_gates, gate_math=gate_math, store_kv=store_kv, gate_piece=gate_piece,
                 score=score, value=value, att_finish=att_finish, scan=scan, out_piece=out_piece, finish=finish)
    sts = [dict() for _ in range(nseq)]
    for name, s, *arg in _step_program(len(pairs)):
        tasks[name](s, sts[s], *arg)


def _param_specs(const):
    return [
        const(SUBLANES, D_RNN),
        const(D_MODEL, D_IN),
        const(D_RNN // GATE_GROUP, GATE_GROUP, 2 * GATE_GROUP),
        const(D_MODEL, D_MODEL),
    ]


def _prompt_call(x, rb, sinks, bkt, gpre, gpost, params):
    bsz, seq, _ = x.shape
    tb = PROMPT_BLOCK
    nt = seq // tb
    const = lambda *shape: pl.BlockSpec(shape, lambda t: (0,) * len(shape), pipeline_mode=pl.Buffered(1))
    whole = lambda *shape: pl.BlockSpec(shape, lambda t: (0,) * len(shape))
    smem = pl.BlockSpec(memory_space=pltpu.SMEM)
    out_shapes = (
        jax.ShapeDtypeStruct((bsz, seq, D_MODEL), F32),
        jax.ShapeDtypeStruct((CONV_W - 1, bsz, D_RNN), F32),
        jax.ShapeDtypeStruct((1, bsz, D_RNN), F32),
        jax.ShapeDtypeStruct((bsz, KV_DIM, WINDOW), F32),
        jax.ShapeDtypeStruct((bsz, KV_DIM, WINDOW), F32),
    )
    return pl.pallas_call(
        functools.partial(_prompt_kernel, tb=tb, nseq=bsz),
        grid=(nt,),
        in_specs=[
            smem, smem,
            pl.BlockSpec((bsz, tb, D_MODEL), lambda t: (0, t, 0)),
            const(WINDOW, WINDOW),
            const(1, D_MODEL), const(1, D_MODEL),
            *_param_specs(const),
        ],
        out_specs=(
            pl.BlockSpec((bsz, tb, D_MODEL), lambda t: (0, t, 0)),
            whole(CONV_W - 1, bsz, D_RNN),
            whole(1, bsz, D_RNN),
            whole(bsz, KV_DIM, WINDOW),
            whole(bsz, KV_DIM, WINDOW),
        ),
        out_shape=out_shapes,
        scratch_shapes=[
            pltpu.VMEM((N_HEADS, WINDOW, WINDOW), F32),
            pltpu.VMEM((bsz, (CONV_W - 1) * SUBLANES, D_RNN), F32),
            pltpu.VMEM((bsz, 1, D_RNN), F32),
            pltpu.VMEM((bsz, N_KV_HEADS, tb + WINDOW, LANES), BF16),
            pltpu.VMEM((bsz, N_KV_HEADS, HEAD_DIM + ONES_ROWS, tb + WINDOW), BF16),
            pltpu.VMEM((bsz, tb, D_ATT), F32),
            pltpu.VMEM((bsz, D_RNN // LANES, tb + SUBLANES * SUBLANES, LANES), F32),
            pltpu.VMEM((bsz, D_RNN // LANES, tb + SUBLANES * SUBLANES, LANES), F32),
        ],
        compiler_params=pltpu.CompilerParams(
            dimension_semantics=("arbitrary",),
            vmem_limit_bytes=VMEM_LIMIT),
        name="prompt_layer",
    )(rb, sinks, x, bkt, gpre, gpost, *params)


def _sample_kernel(rb_ref, sinks_ref, x_ref, bktrow_ref, gpre_ref, gpost_ref, rp_ref, win_ref, wg_ref, wout_ref,
                   sc_ref, h0_ref, kc_ref, vc_ref,
                   y_ref, conv_ref, rnn_ref, ko_ref, vo_ref,
                   qz_ref, o_ref, knew_ref, vnew_ref, rnnout_ref, gatt_ref, bias_ref, sink_ref, *, nb, bb):
    g = pl.program_id(0)
    last_g = pl.num_programs(0) - 1
    sub = lax.broadcasted_iota(jnp.int32, (SUBLANES, LANES), 0)
    lane_lo = lax.broadcasted_iota(jnp.int32, (nb, LANES), 1) < HEAD_DIM

    @pl.when(g == 0)
    def _():
        bkt = jnp.broadcast_to(bktrow_ref[...], (SUBLANES, LANES))
        bias = jnp.zeros((SUBLANES, LANES), F32)
        sinkm = jnp.zeros((SUBLANES, LANES), F32)
        for h in range(N_HEADS):
            bias = jnp.where(sub == h, _build_bias_table(bkt, rb_ref, h), bias)
            sinkm = jnp.where(sub == h, sinks_ref[h], sinkm)
        bias_ref[...] = bias
        sink_ref[...] = sinkm

        x = x_ref[...]
        xn = (x * _rms_scale(x) * gpre_ref[...]).astype(BF16)
        rp = rp_ref[...]
        x_rnn = _dot(xn, win_ref[:, C_XRNN:C_XRNN + D_RNN])
        xc = rp[4:5] + rp[0:1] * sc_ref[0]
        xc = xc + rp[1:2] * sc_ref[1]
        xc = xc + rp[2:3] * sc_ref[2]
        xc = xc + rp[3:4] * x_rnn
        conv_ref[0] = sc_ref[1]
        conv_ref[1] = sc_ref[2]
        conv_ref[2] = x_rnn
        a, bx = _rglru_gate_math(xc, *_rglru_gate_dots(xc, wg_ref), rp)
        h = a * h0_ref[...] + bx
        rnn_ref[...] = h
        g_rnn = _dot(xn, win_ref[:, C_GRNN:C_GRNN + D_RNN])
        rnnout_ref[...] = h * _silu(g_rnn)
        gatt_ref[...] = _dot(xn, win_ref[:, C_GATT:C_GATT + D_ATT])
        kv = _dot(xn, win_ref[:, C_KV:C_KV + 2 * KV_DIM])
        knew_ref[...] = kv[:, :KV_DIM]
        vnew_ref[...] = kv[:, KV_DIM:]
        q = _dot(xn, win_ref[:, C_Q:C_Q + D_ATT])
        for head in range(N_HEADS):
            c = head // 2
            kvh = head // GROUP
            qc = q[:, c * LANES:(c + 1) * LANES]
            if (head % 2) != kvh:
                qc = pltpu.roll(qc, HEAD_DIM, axis=1)
            qz = jnp.where(lane_lo, qc, 0.0) if kvh == 0 else jnp.where(lane_lo, 0.0, qc)
            qz_ref[pl.ds(head, nb, stride=N_HEADS), :] = qz

    rowi = lax.broadcasted_iota(jnp.int32, (WINDOW, LANES), 0)
    newest = rowi == WINDOW - 1

    bias = bias_ref[...]
    sinkm = sink_ref[...]
    row0 = [pl.multiple_of((g * bb + bi) * N_HEADS, N_HEADS) for bi in range(bb)]
    scores = []
    for bi in range(bb):
        newk = jnp.where(newest, knew_ref[pl.ds(g * bb + bi, 1), :], pltpu.roll(kc_ref[bi].T, WINDOW - 1, axis=0))
        ko_ref[bi] = newk.T
        qz = qz_ref[pl.ds(row0[bi], N_HEADS), :].astype(BF16)
        scores.append(_dot_nt(qz, newk.astype(BF16)) + bias)
    probs = []
    for s in scores:
        m = jnp.maximum(jnp.max(s, axis=1, keepdims=True), sinkm)
        e = jnp.exp(s - m)
        den = jnp.sum(e, axis=1, keepdims=True) + jnp.exp(sinkm - m)
        probs.append((e * (1.0 / den)).astype(BF16))
    for bi in range(bb):
        newv = jnp.where(newest, vnew_ref[pl.ds(g * bb + bi, 1), :], pltpu.roll(vc_ref[bi].T, WINDOW - 1, axis=0))
        vo_ref[bi] = newv.T
        o_ref[pl.ds(row0[bi], N_HEADS), :] = _dot(probs[bi], newv.astype(BF16))

    @pl.when(g == last_g)
    def _():
        cols = []
        for c in range(D_ATT // LANES):
            kvh = (2 * c) // GROUP
            halves = []
            for half in range(2):
                oh = o_ref[pl.ds(2 * c + half, nb, stride=N_HEADS), :]
                if half != kvh:
                    oh = pltpu.roll(oh, HEAD_DIM, axis=1)
                halves.append(oh)
            cols.append(jnp.where(lane_lo, halves[0], halves[1]))
        att = jnp.concatenate(cols, axis=1)
        att_out = (att * _silu(gatt_ref[...])).astype(BF16)
        rnn_out = rnnout_ref[...].astype(BF16)
        out = _dot(rnn_out, wout_ref[0:D_RNN, :]) + _dot(att_out, wout_ref[D_RNN:, :])
        y_ref[...] = x_ref[...] + out * _rms_scale(out) * gpost_ref[...]


def _sample_call(x, rb, sinks, bktrow, gpre, gpost, params, sc, h0, kc_t, vc_t):
    nb = x.shape[0]
    bb = SAMPLE_BLOCK
    const = lambda *shape: pl.BlockSpec(shape, lambda g: (0,) * len(shape), pipeline_mode=pl.Buffered(1))
    smem = pl.BlockSpec(memory_space=pltpu.SMEM)
    win_t_spec = pl.BlockSpec((bb, KV_DIM, WINDOW), lambda g: (g, 0, 0))
    out_shapes = (
        jax.ShapeDtypeStruct((nb, D_MODEL), F32),
        jax.ShapeDtypeStruct((CONV_W - 1, nb, D_RNN), F32),
        jax.ShapeDtypeStruct((nb, D_RNN), F32),
        jax.ShapeDtypeStruct((nb, KV_DIM, WINDOW), F32),
        jax.ShapeDtypeStruct((nb, KV_DIM, WINDOW), F32),
    )
    return pl.pallas_call(
        functools.partial(_sample_kernel, nb=nb, bb=bb),
        grid=(nb // bb,),
        in_specs=[
            smem, smem,
            const(nb, D_MODEL),
            const(1, LANES),
            const(1, D_MODEL), const(1, D_MODEL),
            *_param_specs(const),
            const(CONV_W - 1, nb, D_RNN),
            const(nb, D_RNN),
            win_t_spec, win_t_spec,
        ],
        out_specs=(
            pl.BlockSpec((nb, D_MODEL), lambda g: (0, 0)),
            pl.BlockSpec((CONV_W - 1, nb, D_RNN), lambda g: (0, 0, 0)),
            pl.BlockSpec((nb, D_RNN), lambda g: (0, 0)),
            win_t_spec, win_t_spec,
        ),
        out_shape=out_shapes,
        scratch_shapes=[
            pltpu.VMEM((nb * N_HEADS, LANES), F32),
            pltpu.VMEM((nb * N_HEADS, LANES), F32),
            pltpu.VMEM((nb, KV_DIM), F32),
            pltpu.VMEM((nb, KV_DIM), F32),
            pltpu.VMEM((nb, D_RNN), F32),
            pltpu.VMEM((nb, D_ATT), F32),
            pltpu.VMEM((SUBLANES, LANES), F32),
            pltpu.VMEM((SUBLANES, LANES), F32),
        ],
        compiler_params=pltpu.CompilerParams(
            dimension_semantics=("arbitrary",),
            vmem_limit_bytes=VMEM_LIMIT),
        name="sample_layer",
    )(rb, sinks, x, bktrow, gpre, gpost, *params, sc, h0, kc_t, vc_t)


def kernel(x_prompt, x_sample, state_conv, state_rnn, cache_k_win, cache_v_win, norm_pre, norm_post,
           w_in, conv_w, conv_b, w_gate_a, b_gate_a, w_gate_x, b_gate_x, lru_lambda, attn_sinks,
           rel_bias, w_out):
    assert w_in.shape[0] == 1, "single-layer trunk"
    bsz, seq, _ = x_prompt.shape
    nb = x_sample.shape[0]
    wb = cache_k_win.shape[2]
    assert wb == WINDOW and x_sample.shape[1] == 1 and seq % PROMPT_BLOCK == 0 and nb % SAMPLE_BLOCK == 0
    assert bsz == PROMPT_SEQS, "the step program is written for this many prompt sequences"

    params = _prep_call(conv_w[0], conv_b, b_gate_a, b_gate_x, lru_lambda, w_in[0], w_gate_a[0], w_gate_x[0], w_out[0])
    gpre = norm_pre.reshape(1, D_MODEL)
    gpost = norm_post.reshape(1, D_MODEL)
    sinks = attn_sinks[0]
    bkt_np = _folded_bucket_table()
    bkt = jnp.asarray(np.ascontiguousarray(bkt_np.T))
    bktrow = jnp.asarray(bkt_np[WINDOW - 1:WINDOW, :])

    y_p, conv_p, rnn_p, kw_p, vw_p = _prompt_call(x_prompt, rel_bias, sinks, bkt, gpre, gpost, params)

    feature_major = lambda z: jnp.transpose(z[0], (0, 2, 3, 1)).reshape(nb, KV_DIM, wb)
    y_s, conv_s, rnn_s, kw_s, vw_s = _sample_call(
        x_sample.reshape(nb, D_MODEL), rel_bias, sinks, bktrow, gpre, gpost, params,
        jnp.transpose(state_conv[0], (1, 0, 2)), state_rnn[0],
        feature_major(cache_k_win), feature_major(cache_v_win))

    kv5 = lambda z, n: jnp.transpose(z.reshape(n, N_KV_HEADS, HEAD_DIM, wb), (0, 3, 1, 2))[None]
    return (y_p, y_s.reshape(nb, 1, D_MODEL),
            jnp.transpose(conv_p, (1, 0, 2))[None], rnn_p, kv5(kw_p, bsz), kv5(vw_p, bsz),
            jnp.transpose(conv_s, (1, 0, 2))[None], rnn_s[None], kv5(kw_s, nb), kv5(vw_s, nb))
```

```python
import collections
import functools
import math

import numpy as np
import jax
import jax.numpy as jnp
from jax import lax
from jax.experimental import pallas as pl
from jax.experimental.pallas import tpu as pltpu

D_MODEL = 1024
D_RNN = 512
D_ATT = 512
HEAD_DIM = 64
N_HEADS = 8
N_KV_HEADS = 2
GROUP = N_HEADS // N_KV_HEADS
KV_DIM = N_KV_HEADS * HEAD_DIM
N_RNN_BLOCKS = 8
RNN_BLOCK = D_RNN // N_RNN_BLOCKS
CONV_W = 4
LRU_C = 8.0
WINDOW = 128
N_BUCKETS = 32
MAX_DISTANCE = 128
EPS = 1e-6
NEG_INF = -1e30
D_IN = 2 * D_RNN + 2 * D_ATT + 2 * KV_DIM

C_XRNN = 0
C_GRNN = D_RNN
C_Q = 2 * D_RNN
C_KV = 2 * D_RNN + D_ATT
C_GATT = 2 * D_RNN + D_ATT + 2 * KV_DIM

SUBLANES = 8
LANES = 128
GATE_GROUP = 256
ONES_ROWS = 16
SCORE_LOOKAHEAD = 2
PROMPT_SEQS = 2
PROMPT_BLOCK = 512
PREP_ROWS = 256
VMEM_LIMIT = 56 * 1024 * 1024

LOG2E = 1.4426950408889634
TINY = 1e-37
F32 = jnp.float32
BF16 = jnp.bfloat16


def _spread(groups, bulk):
    out, done = [], 0
    for i, group in enumerate(groups):
        out += group
        want = (i + 1) * len(bulk) // len(groups)
        out += bulk[done:want]
        done = want
    return out


def _step_program(npairs):
    a, b = range(PROMPT_SEQS)
    n_out = D_MODEL // GATE_GROUP

    def attention(s):
        groups = [[("score", s, n) for n in range(SCORE_LOOKAHEAD)]]
        for n in range(npairs):
            ahead = [("score", s, n + SCORE_LOOKAHEAD)] if n + SCORE_LOOKAHEAD < npairs else []
            groups.append(ahead + [("value", s, n)])
        return groups

    project = lambda s: [("project", s, name) for name in ("x_rnn", "kv", "q")]
    gates = lambda s: [("gate_piece", s, i) for i in range(4)]
    outs = lambda s: [("out_piece", s, j) for j in range(n_out)]
    pa, pb = project(a), project(b)
    prog = [("sample_scores", a), ("norm", a), pa[0], ("norm", b), pa[1], pa[2], ("sample_values", a)]
    prog += [("conv_gates", a), ("store_kv", a)]
    prog += _spread(attention(a), [("gate_math", a)] + pb + gates(a) + [("scan", a)])
    prog += [("att_finish", a), ("conv_gates", b), ("store_kv", b)]
    prog += _spread(attention(b), [("gate_math", b)] + gates(b) + outs(a) + [("scan", b)])
    prog += [("finish", a), ("att_finish", b)] + outs(b) + [("finish", b)]
    return prog


def _t5_bucket_np(dist):
    dist = np.maximum(dist, 0)
    max_exact = N_BUCKETS // 2
    d = np.maximum(dist, 1).astype(np.float32)
    ratio = np.log(d / np.float32(max_exact)) / np.float32(math.log(MAX_DISTANCE / max_exact))
    large = max_exact + (ratio * np.float32(N_BUCKETS - max_exact)).astype(np.int32)
    large = np.minimum(large, N_BUCKETS - 1)
    return np.where(dist < max_exact, dist, large).astype(np.int32)


def _folded_bucket_table():
    i = np.arange(WINDOW)[:, None]
    j = np.arange(WINDOW)[None, :]
    dist = np.where(j <= i, i - j, i + WINDOW - j)
    return _t5_bucket_np(dist)


def _sigmoid_of_neg(xn):
    return 1.0 / (1.0 + jnp.exp(xn))


def _silu(x):
    return x * (1.0 / (1.0 + jnp.exp2(x * -LOG2E)))


def _softplus(x):
    return jnp.maximum(x, 0.0) + jnp.log1p(jnp.exp(-jnp.abs(x)))


def _rms_scale(x):
    return lax.rsqrt(jnp.mean(x * x, axis=-1, keepdims=True) + EPS)


def _dot(a, b):
    return jnp.dot(a, b, preferred_element_type=F32)


def _dot_nt(a, b):
    return lax.dot_general(a, b, (((1,), (1,)), ((), ())), preferred_element_type=F32)


def _rglru_gate_dots(xc, wg_ref):
    xcb = xc.astype(BF16)
    zs = [_dot(xcb[:, g * GATE_GROUP:(g + 1) * GATE_GROUP], wg_ref[g])
          for g in range(D_RNN // GATE_GROUP)]
    za = jnp.concatenate([z[:, :GATE_GROUP] for z in zs], axis=1)
    zx = jnp.concatenate([z[:, GATE_GROUP:] for z in zs], axis=1)
    return za, zx


def _rglru_gate_math(xc, za, zx, rp):
    r = _sigmoid_of_neg(za + rp[5:6])
    i = _sigmoid_of_neg(zx + rp[6:7])
    log_a = (-LRU_C * r) * _softplus(-rp[7:8])
    a = jnp.exp(log_a)
    u = 1.0 - a * a
    bx = (u * lax.rsqrt(jnp.maximum(u, TINY))) * (i * xc)
    return a, bx


def _prep_kernel(cw_raw, cb_raw, ba_raw, bx_raw, lam_raw, win_raw, wga_raw, wgx_raw, wout_raw,
                 rp_ref, win_ref, wg_ref, wout_ref):
    for c0 in range(0, D_IN, GATE_GROUP):
        w = win_raw[:, c0:c0 + GATE_GROUP]
        if C_Q <= c0 < C_Q + D_ATT:
            w = w * HEAD_DIM ** -0.5
        win_ref[:, c0:c0 + GATE_GROUP] = w.astype(BF16)
    wout_ref[...] = wout_raw[...].astype(BF16)

    @pl.when(pl.program_id(0) == 0)
    def _():
        rp_ref[0:CONV_W, :] = cw_raw[...]
        rp_ref[4:5, :] = cb_raw[...]
        rp_ref[5:6, :] = -ba_raw[...]
        rp_ref[6:7, :] = -bx_raw[...]
        rp_ref[7:8, :] = lam_raw[...]
        per = GATE_GROUP // RNN_BLOCK
        zero = jnp.zeros((RNN_BLOCK, RNN_BLOCK), F32)
        for g in range(D_RNN // GATE_GROUP):
            for half, w_raw in enumerate((wga_raw, wgx_raw)):
                rows = [jnp.concatenate([-w_raw[g * per + m] if k == m else zero for k in range(per)], axis=1)
                        for m in range(per)]
                wg_ref[g, :, half * GATE_GROUP:(half + 1) * GATE_GROUP] = jnp.concatenate(rows, axis=0).astype(BF16)


def _prep_call(conv_w, conv_b, b_gate_a, b_gate_x, lru_lambda, w_in, w_gate_a, w_gate_x, w_out):
    rows = PREP_ROWS
    whole = lambda *shape: pl.BlockSpec(shape, lambda i: (0,) * len(shape))
    return pl.pallas_call(
        _prep_kernel,
        grid=(D_MODEL // rows,),
        in_specs=[
            whole(CONV_W, D_RNN), whole(1, D_RNN), whole(1, D_RNN), whole(1, D_RNN), whole(1, D_RNN),
            pl.BlockSpec((rows, D_IN), lambda i: (i, 0)),
            whole(N_RNN_BLOCKS, RNN_BLOCK, RNN_BLOCK), whole(N_RNN_BLOCKS, RNN_BLOCK, RNN_BLOCK),
            pl.BlockSpec((rows, D_MODEL), lambda i: (i, 0)),
        ],
        out_specs=(
            whole(SUBLANES, D_RNN),
            pl.BlockSpec((rows, D_IN), lambda i: (i, 0)),
            whole(D_RNN // GATE_GROUP, GATE_GROUP, 2 * GATE_GROUP),
            pl.BlockSpec((rows, D_MODEL), lambda i: (i, 0)),
        ),
        out_shape=(
            jax.ShapeDtypeStruct((SUBLANES, D_RNN), F32),
            jax.ShapeDtypeStruct((D_MODEL, D_IN), BF16),
            jax.ShapeDtypeStruct((D_RNN // GATE_GROUP, GATE_GROUP, 2 * GATE_GROUP), BF16),
            jax.ShapeDtypeStruct((D_MODEL, D_MODEL), BF16),
        ),
        compiler_params=pltpu.CompilerParams(dimension_semantics=("arbitrary",)),
        name="param_layout",
    )(conv_w, conv_b, b_gate_a, b_gate_x, lru_lambda, w_in, w_gate_a, w_gate_x, w_out)


def _build_bias_table(bkt, rb_ref, head):
    tbl = jnp.zeros(bkt.shape, F32)
    for k in range(N_BUCKETS):
        tbl = jnp.where(bkt == k, rb_ref[k, head], tbl)
    return tbl


_SampleRefs = collections.namedtuple(
    "_SampleRefs", "x bktrow sc h0 kc vc y conv rnn ko vo qz o knew vnew rnnout gatt bias sink")


def _layer_kernel(rb_ref, sinks_ref, x_ref, bkt_ref, gpre_ref, gpost_ref, rp_ref, win_ref, wg_ref, wout_ref,
                  xs_ref, bktrow_ref, sc_ref, h0_ref, kc_ref, vc_ref,
                  y_ref, conv_ref, rnn_ref, kw_ref, vw_ref, ys_ref, convs_ref, rnns_ref, ko_ref, vo_ref,
                  bias_ref, xpad_ref, hc_ref, kd_ref, vt_ref, att_ref, xr_ref, hn_ref,
                  qz_ref, o_ref, knew_ref, vnew_ref, rnnout_ref, gatt_ref, sbias_ref, ssink_ref, *, tb, nseq, bb):
    t = pl.program_id(0)
    nq = tb // WINDOW
    sr = _SampleRefs(xs_ref, bktrow_ref, sc_ref, h0_ref, kc_ref, vc_ref, ys_ref, convs_ref, rnns_ref, ko_ref, vo_ref,
                     qz_ref, o_ref, knew_ref, vnew_ref, rnnout_ref, gatt_ref, sbias_ref, ssink_ref)

    @pl.when(t == 0)
    def _():
        bkt = bkt_ref[...]
        for h in range(N_HEADS):
            bias_ref[h] = _build_bias_table(bkt, rb_ref, h)
        xpad_ref[...] = jnp.zeros((nseq, (CONV_W - 1) * SUBLANES, D_RNN), F32)
        hc_ref[...] = jnp.zeros((nseq, 1, D_RNN), F32)
        kd_ref[:, :, 0:WINDOW, :] = jnp.zeros((nseq, N_KV_HEADS, WINDOW, LANES), BF16)
        vt_ref[:, :, 0:HEAD_DIM, 0:WINDOW] = jnp.zeros((nseq, N_KV_HEADS, HEAD_DIM, WINDOW), BF16)
        vt_ref[:, :, HEAD_DIM:, :] = jnp.ones((nseq, N_KV_HEADS, ONES_ROWS, tb + WINDOW), BF16)
        _sample_setup(sr, rb_ref, sinks_ref, gpre_ref, rp_ref, win_ref, wg_ref)

    rp = rp_ref[...]
    sub8 = lax.broadcasted_iota(jnp.int32, (SUBLANES, D_RNN), 0)
    lo = lax.broadcasted_iota(jnp.int32, (tb, LANES), 1) < HEAD_DIM
    key_idx = lax.broadcasted_iota(jnp.int32, (WINDOW, WINDOW), 0)
    qry_idx = lax.broadcasted_iota(jnp.int32, (WINDOW, WINDOW), 1)
    own = key_idx <= qry_idx
    lane_lo = qry_idx < HEAD_DIM
    has_prev0 = (jnp.zeros((WINDOW, WINDOW), jnp.int32) + t) > 0
    zero_b = jnp.zeros((WINDOW, WINDOW), BF16)
    pair_lo = lax.broadcasted_iota(jnp.int32, (1, 2 * WINDOW), 1) < WINDOW
    pairs = [(qi, c) for qi in range(nq) for c in range(D_ATT // LANES)]
    gate_cols = [C_GRNN, C_GRNN + GATE_GROUP, C_GATT, C_GATT + GATE_GROUP]

    def norm(s, st):
        x = x_ref[s]
        st.update(x=x, xn=(x * _rms_scale(x) * gpre_ref[...]).astype(BF16))

    def project(s, st, name):
        c0, width = dict(x_rnn=(C_XRNN, D_RNN), kv=(C_KV, 2 * KV_DIM), q=(C_Q, D_ATT))[name]
        st[name] = _dot(st["xn"], win_ref[:, c0:c0 + width])

    clen = tb // SUBLANES
    pitch = clen + SUBLANES
    ncol = D_RNN // LANES
    ntail = CONV_W - 1

    def conv_gates(s, st):
        x_rnn = st["x_rnn"]
        for c in range(ncol):
            for k in range(SUBLANES):
                xr_ref[s, c, k * pitch:k * pitch + clen, :] = x_rnn[k * clen:(k + 1) * clen, c * LANES:(c + 1) * LANES]
        xp = jnp.concatenate(
            [jnp.concatenate([xr_ref[s, c, pl.ds(v, SUBLANES, stride=pitch), :] for v in range(clen)], axis=0)
             for c in range(ncol)], axis=1)
        tail = xp[(clen - ntail) * SUBLANES:, :]
        prev_tail = xpad_ref[s]
        heads = [jnp.where(sub8 == 0,
                           pltpu.roll(prev_tail[j * SUBLANES:(j + 1) * SUBLANES], 1, axis=0),
                           pltpu.roll(tail[j * SUBLANES:(j + 1) * SUBLANES], 1, axis=0)) for j in range(ntail)]
        xc = rp[4:5]
        for tap in range(ntail):
            shift = ntail - tap
            xc = xc + rp[tap:tap + 1] * jnp.concatenate(heads[ntail - shift:] + [xp[:(clen - shift) * SUBLANES]], axis=0)
        xc = xc + rp[3:4] * xp
        xpad_ref[s] = tail
        for j in range(ntail):
            conv_ref[j, s:s + 1, :] = tail[j * SUBLANES + SUBLANES - 1:(j + 1) * SUBLANES]
        st["xc"] = xc
        st["za"], st["zx"] = _rglru_gate_dots(xc, wg_ref)

    def gate_math(s, st):
        st["a"], st["bx"] = _rglru_gate_math(st.pop("xc"), st.pop("za"), st.pop("zx"), rp)

    def store_kv(s, st):
        st["qb"] = st.pop("q").astype(BF16)
        k01 = st["kv"][:, :KV_DIM]
        v01 = st["kv"][:, KV_DIM:]
        k10 = pltpu.roll(k01, HEAD_DIM, axis=1)
        kd_ref[s, 0, WINDOW:WINDOW + tb, :] = jnp.where(lo, k01, k10).astype(BF16)
        kd_ref[s, 1, WINDOW:WINDOW + tb, :] = jnp.where(lo, k10, k01).astype(BF16)
        v_t = v01.T.astype(BF16)
        vt_ref[s, 0, 0:HEAD_DIM, WINDOW:WINDOW + tb] = v_t[0:HEAD_DIM]
        vt_ref[s, 1, 0:HEAD_DIM, WINDOW:WINDOW + tb] = v_t[HEAD_DIM:]
        kw_ref[s] = k01[tb - WINDOW:, :].T
        vw_ref[s] = v01[tb - WINDOW:, :].T

    def gate_piece(s, st, i):
        st[("gate", i)] = _dot(st["xn"], win_ref[:, gate_cols[i]:gate_cols[i] + GATE_GROUP])

    def scores(s, st, qi, c):
        rows = slice(qi * WINDOW, (qi + 1) * WINDOW)
        win = slice(qi * WINDOW, qi * WINDOW + 2 * WINDOW)
        kvh = (2 * c) // GROUP
        qc = st["qb"][rows, c * LANES:(c + 1) * LANES]
        qpair = jnp.concatenate([jnp.where(lane_lo, qc, zero_b), jnp.where(lane_lo, zero_b, qc)], axis=0)
        sc = _dot_nt(kd_ref[s, kvh, win, :], qpair)
        p_prev, p_own, ms = [], [], []
        for half in range(2):
            head = 2 * c + half
            cols = slice(half * WINDOW, (half + 1) * WINDOW)
            sf = jnp.where(own, sc[WINDOW:, cols], sc[:WINDOW, cols]) + bias_ref[head]
            if qi == 0:
                sf = jnp.where(own | has_prev0, sf, NEG_INF)
            m = jnp.max(sf, axis=0, keepdims=True)
            e = jnp.exp(sf - m).astype(BF16)
            p_prev.append(jnp.where(own, zero_b, e))
            p_own.append(jnp.where(own, e, zero_b))
            ms.append(m)
        p = jnp.concatenate([jnp.concatenate(p_prev, axis=1), jnp.concatenate(p_own, axis=1)], axis=0)
        return p, jnp.concatenate(ms, axis=1)

    def values(s, qi, c, p, m):
        rows = slice(qi * WINDOW, (qi + 1) * WINDOW)
        win = slice(qi * WINDOW, qi * WINDOW + 2 * WINDOW)
        kvh = (2 * c) // GROUP
        oa = _dot(vt_ref[s, kvh, :, win], p)
        sink = jnp.where(pair_lo, sinks_ref[2 * c], sinks_ref[2 * c + 1])
        den = oa[HEAD_DIM:HEAD_DIM + 1, :] + jnp.exp(sink - m)
        o = oa[0:HEAD_DIM, :] * (1.0 / den)
        o_t = jnp.concatenate([o[:, :WINDOW], o[:, WINDOW:]], axis=0)
        att_ref[s, rows, c * LANES:(c + 1) * LANES] = o_t.T

    def score(s, st, n):
        st[("p", n)] = scores(s, st, *pairs[n])

    def value(s, st, n):
        values(s, *pairs[n], *st.pop(("p", n)))

    def att_finish(s, st):
        kd_ref[s, :, 0:WINDOW, :] = kd_ref[s, :, tb:tb + WINDOW, :]
        vt_ref[s, :, :, 0:WINDOW] = vt_ref[s, :, :, tb:tb + WINDOW]
        g_att = jnp.concatenate([st[("gate", 2)], st[("gate", 3)]], axis=1)
        st["att_out"] = (att_ref[s] * _silu(g_att)).astype(BF16)

    def scan(s, st):
        a3 = st["a"].reshape(clen, SUBLANES, D_RNN)
        b3 = st["bx"].reshape(clen, SUBLANES, D_RNN)
        hl, pr = [b3[0]], [a3[0]]
        for v in range(1, clen):
            hl.append(a3[v] * hl[-1] + b3[v])
            pr.append(a3[v] * pr[-1])
        h_end, p_end = hl[-1], pr[-1]
        h_in = jnp.broadcast_to(hc_ref[s], (SUBLANES, D_RNN))
        entry = h_in
        for _ in range(SUBLANES - 1):
            entry = jnp.where(sub8 == 0, h_in, pltpu.roll(h_end + p_end * entry, 1, axis=0))
        carry = (h_end + p_end * entry)[SUBLANES - 1:SUBLANES, :]
        hc_ref[s] = carry
        rnn_ref[0, s:s + 1, :] = carry
        for v in range(clen):
            hv = hl[v] + pr[v] * entry
            for c in range(ncol):
                hn_ref[s, c, pl.ds(v, SUBLANES, stride=pitch), :] = hv[:, c * LANES:(c + 1) * LANES]
        h = jnp.concatenate(
            [jnp.concatenate([hn_ref[s, c, k * pitch:k * pitch + clen, :] for k in range(SUBLANES)], axis=0)
             for c in range(ncol)], axis=1)
        g_rnn = jnp.concatenate([st[("gate", 0)], st[("gate", 1)]], axis=1)
        st["rnn_out"] = (h * _silu(g_rnn)).astype(BF16)

    def out_piece(s, st, j):
        cols = slice(j * GATE_GROUP, (j + 1) * GATE_GROUP)
        st[("out", j)] = _dot(st["rnn_out"], wout_ref[0:D_RNN, cols]) + _dot(st["att_out"], wout_ref[D_RNN:, cols])

    def finish(s, st):
        out = jnp.concatenate([st[("out", j)] for j in range(D_MODEL // GATE_GROUP)], axis=1)
        y_ref[s] = st["x"] + out * _rms_scale(out) * gpost_ref[...]

    def sample_scores(s, st):
        _sample_scores(sr, t, bb, st)

    def sample_values(s, st):
        _sample_values(sr, t, bb, st)

    tasks = dict(sample_scores=sample_scores, sample_values=sample_values, norm=norm, project=project,
                 conv_gates=conv_gates, gate_math=gate_math, store_kv=store_kv, gate_piece=gate_piece,
                 score=score, value=value, att_finish=att_finish, scan=scan, out_piece=out_piece, finish=finish)
    sts = [dict() for _ in range(nseq)]
    for name, s, *arg in _step_program(len(pairs)):
        tasks[name](s, sts[s], *arg)

    @pl.when(t == pl.num_programs(0) - 1)
    def _():
        _sample_finish(sr, gpost_ref, wout_ref)


def _param_specs(const):
    return [
        const(SUBLANES, D_RNN),
        const(D_MODEL, D_IN),
        const(D_RNN // GATE_GROUP, GATE_GROUP, 2 * GATE_GROUP),
        const(D_MODEL, D_MODEL),
    ]


def _layer_call(x, rb, sinks, bkt, gpre, gpost, params, xs, bktrow, sc, h0, kc_t, vc_t):
    bsz, seq, _ = x.shape
    tb = PROMPT_BLOCK
    nt = seq // tb
    nb = xs.shape[0]
    bb = nb // nt
    win_t_spec = pl.BlockSpec((bb, KV_DIM, WINDOW), lambda t: (t, 0, 0))
    const = lambda *shape: pl.BlockSpec(shape, lambda t: (0,) * len(shape), pipeline_mode=pl.Buffered(1))
    whole = lambda *shape: pl.BlockSpec(shape, lambda t: (0,) * len(shape))
    smem = pl.BlockSpec(memory_space=pltpu.SMEM)
    out_shapes = (
        jax.ShapeDtypeStruct((bsz, seq, D_MODEL), F32),
        jax.ShapeDtypeStruct((CONV_W - 1, bsz, D_RNN), F32),
        jax.ShapeDtypeStruct((1, bsz, D_RNN), F32),
        jax.ShapeDtypeStruct((bsz, KV_DIM, WINDOW), F32),
        jax.ShapeDtypeStruct((bsz, KV_DIM, WINDOW), F32),
        jax.ShapeDtypeStruct((nb, D_MODEL), F32),
        jax.ShapeDtypeStruct((CONV_W - 1, nb, D_RNN), F32),
        jax.ShapeDtypeStruct((nb, D_RNN), F32),
        jax.ShapeDtypeStruct((nb, KV_DIM, WINDOW), F32),
        jax.ShapeDtypeStruct((nb, KV_DIM, WINDOW), F32),
    )
    return pl.pallas_call(
        functools.partial(_layer_kernel, tb=tb, nseq=bsz, bb=bb),
        grid=(nt,),
        in_specs=[
            smem, smem,
            pl.BlockSpec((bsz, tb, D_MODEL), lambda t: (0, t, 0)),
            const(WINDOW, WINDOW),
            const(1, D_MODEL), const(1, D_MODEL),
            *_param_specs(const),
            const(nb, D_MODEL), const(1, LANES), const(CONV_W - 1, nb, D_RNN), const(nb, D_RNN),
            win_t_spec, win_t_spec,
        ],
        out_specs=(
            pl.BlockSpec((bsz, tb, D_MODEL), lambda t: (0, t, 0)),
            whole(CONV_W - 1, bsz, D_RNN),
            whole(1, bsz, D_RNN),
            whole(bsz, KV_DIM, WINDOW),
            whole(bsz, KV_DIM, WINDOW),
            whole(nb, D_MODEL), whole(CONV_W - 1, nb, D_RNN), whole(nb, D_RNN),
            win_t_spec, win_t_spec,
        ),
        out_shape=out_shapes,
        scratch_shapes=[
            pltpu.VMEM((N_HEADS, WINDOW, WINDOW), F32),
            pltpu.VMEM((bsz, (CONV_W - 1) * SUBLANES, D_RNN), F32),
            pltpu.VMEM((bsz, 1, D_RNN), F32),
            pltpu.VMEM((bsz, N_KV_HEADS, tb + WINDOW, LANES), BF16),
            pltpu.VMEM((bsz, N_KV_HEADS, HEAD_DIM + ONES_ROWS, tb + WINDOW), BF16),
            pltpu.VMEM((bsz, tb, D_ATT), F32),
            pltpu.VMEM((bsz, D_RNN // LANES, tb + SUBLANES * SUBLANES, LANES), F32),
            pltpu.VMEM((bsz, D_RNN // LANES, tb + SUBLANES * SUBLANES, LANES), F32),
            pltpu.VMEM((nb * N_HEADS, LANES), F32),
            pltpu.VMEM((nb * N_HEADS, LANES), F32),
            pltpu.VMEM((nb, KV_DIM), F32),
            pltpu.VMEM((nb, KV_DIM), F32),
            pltpu.VMEM((nb, D_RNN), F32),
            pltpu.VMEM((nb, D_ATT), F32),
            pltpu.VMEM((SUBLANES, LANES), F32),
            pltpu.VMEM((SUBLANES, LANES), F32),
        ],
        compiler_params=pltpu.CompilerParams(
            dimension_semantics=("arbitrary",),
            vmem_limit_bytes=VMEM_LIMIT),
        name="layer_step",
    )(rb, sinks, x, bkt, gpre, gpost, *params, xs, bktrow, sc, h0, kc_t, vc_t)


def _sample_setup(sr, rb_ref, sinks_ref, gpre_ref, rp_ref, win_ref, wg_ref):
    nb = sr.x.shape[0]
    sub = lax.broadcasted_iota(jnp.int32, (SUBLANES, LANES), 0)
    lane_lo = lax.broadcasted_iota(jnp.int32, (nb, LANES), 1) < HEAD_DIM
    bkt = jnp.broadcast_to(sr.bktrow[...], (SUBLANES, LANES))
    bias = jnp.zeros((SUBLANES, LANES), F32)
    sinkm = jnp.zeros((SUBLANES, LANES), F32)
    for h in range(N_HEADS):
        bias = jnp.where(sub == h, _build_bias_table(bkt, rb_ref, h), bias)
        sinkm = jnp.where(sub == h, sinks_ref[h], sinkm)
    sr.bias[...] = bias
    sr.sink[...] = sinkm

    x = sr.x[...]
    xn = (x * _rms_scale(x) * gpre_ref[...]).astype(BF16)
    rp = rp_ref[...]
    x_rnn = _dot(xn, win_ref[:, C_XRNN:C_XRNN + D_RNN])
    xc = rp[4:5] + rp[0:1] * sr.sc[0]
    xc = xc + rp[1:2] * sr.sc[1]
    xc = xc + rp[2:3] * sr.sc[2]
    xc = xc + rp[3:4] * x_rnn
    sr.conv[0] = sr.sc[1]
    sr.conv[1] = sr.sc[2]
    sr.conv[2] = x_rnn
    a, bx = _rglru_gate_math(xc, *_rglru_gate_dots(xc, wg_ref), rp)
    h = a * sr.h0[...] + bx
    sr.rnn[...] = h
    g_rnn = _dot(xn, win_ref[:, C_GRNN:C_GRNN + D_RNN])
    sr.rnnout[...] = h * _silu(g_rnn)
    sr.gatt[...] = _dot(xn, win_ref[:, C_GATT:C_GATT + D_ATT])
    kv = _dot(xn, win_ref[:, C_KV:C_KV + 2 * KV_DIM])
    sr.knew[...] = kv[:, :KV_DIM]
    sr.vnew[...] = kv[:, KV_DIM:]
    q = _dot(xn, win_ref[:, C_Q:C_Q + D_ATT])
    for head in range(N_HEADS):
        c = head // 2
        kvh = head // GROUP
        qc = q[:, c * LANES:(c + 1) * LANES]
        if (head % 2) != kvh:
            qc = pltpu.roll(qc, HEAD_DIM, axis=1)
        qz = jnp.where(lane_lo, qc, 0.0) if kvh == 0 else jnp.where(lane_lo, 0.0, qc)
        sr.qz[pl.ds(head, nb, stride=N_HEADS), :] = qz


def _sample_scores(sr, step, bb, st):
    newest = lax.broadcasted_iota(jnp.int32, (WINDOW, LANES), 0) == WINDOW - 1
    bias = sr.bias[...]
    st["rows"] = [pl.multiple_of((step * bb + bi) * N_HEADS, N_HEADS) for bi in range(bb)]
    st["scores"] = []
    for bi in range(bb):
        newk = jnp.where(newest, sr.knew[pl.ds(step * bb + bi, 1), :], pltpu.roll(sr.kc[bi].T, WINDOW - 1, axis=0))
        sr.ko[bi] = newk.T
        qz = sr.qz[pl.ds(st["rows"][bi], N_HEADS), :].astype(BF16)
        st["scores"].append(_dot_nt(qz, newk.astype(BF16)) + bias)


def _sample_values(sr, step, bb, st):
    newest = lax.broadcasted_iota(jnp.int32, (WINDOW, LANES), 0) == WINDOW - 1
    sinkm = sr.sink[...]
    for bi, s in enumerate(st.pop("scores")):
        m = jnp.maximum(jnp.max(s, axis=1, keepdims=True), sinkm)
        e = jnp.exp(s - m)
        den = jnp.sum(e, axis=1, keepdims=True) + jnp.exp(sinkm - m)
        p = (e * (1.0 / den)).astype(BF16)
        newv = jnp.where(newest, sr.vnew[pl.ds(step * bb + bi, 1), :], pltpu.roll(sr.vc[bi].T, WINDOW - 1, axis=0))
        sr.vo[bi] = newv.T
        sr.o[pl.ds(st["rows"][bi], N_HEADS), :] = _dot(p, newv.astype(BF16))


def _sample_finish(sr, gpost_ref, wout_ref):
    nb = sr.x.shape[0]
    lane_lo = lax.broadcasted_iota(jnp.int32, (nb, LANES), 1) < HEAD_DIM
    cols = []
    for c in range(D_ATT // LANES):
        kvh = (2 * c) // GROUP
        halves = []
        for half in range(2):
            oh = sr.o[pl.ds(2 * c + half, nb, stride=N_HEADS), :]
            if half != kvh:
                oh = pltpu.roll(oh, HEAD_DIM, axis=1)
            halves.append(oh)
        cols.append(jnp.where(lane_lo, halves[0], halves[1]))
    att = jnp.concatenate(cols, axis=1)
    att_out = (att * _silu(sr.gatt[...])).astype(BF16)
    rnn_out = sr.rnnout[...].astype(BF16)
    out = _dot(rnn_out, wout_ref[0:D_RNN, :]) + _dot(att_out, wout_ref[D_RNN:, :])
    sr.y[...] = sr.x[...] + out * _rms_scale(out) * gpost_ref[...]


def kernel(x_prompt, x_sample, state_conv, state_rnn, cache_k_win, cache_v_win, norm_pre, norm_post,
           w_in, conv_w, conv_b, w_gate_a, b_gate_a, w_gate_x, b_gate_x, lru_lambda, attn_sinks,
           rel_bias, w_out):
    assert w_in.shape[0] == 1, "single-layer trunk"
    bsz, seq, _ = x_prompt.shape
    nb = x_sample.shape[0]
    wb = cache_k_win.shape[2]
    assert wb == WINDOW and x_sample.shape[1] == 1 and seq % PROMPT_BLOCK == 0 and nb % (seq // PROMPT_BLOCK) == 0
    assert bsz == PROMPT_SEQS, "the step program is written for this many prompt sequences"

    params = _prep_call(conv_w[0], conv_b, b_gate_a, b_gate_x, lru_lambda, w_in[0], w_gate_a[0], w_gate_x[0], w_out[0])
    gpre = norm_pre.reshape(1, D_MODEL)
    gpost = norm_post.reshape(1, D_MODEL)
    sinks = attn_sinks[0]
    bkt_np = _folded_bucket_table()
    bkt = jnp.asarray(np.ascontiguousarray(bkt_np.T))
    bktrow = jnp.asarray(bkt_np[WINDOW - 1:WINDOW, :])

    feature_major = lambda z: jnp.transpose(z[0], (0, 2, 3, 1)).reshape(nb, KV_DIM, wb)
    y_p, conv_p, rnn_p, kw_p, vw_p, y_s, conv_s, rnn_s, kw_s, vw_s = _layer_call(
        x_prompt, rel_bias, sinks, bkt, gpre, gpost, params,
        x_sample.reshape(nb, D_MODEL), bktrow, jnp.transpose(state_conv[0], (1, 0, 2)), state_rnn[0],
        feature_major(cache_k_win), feature_major(cache_v_win))

    kv5 = lambda z, n: jnp.transpose(z.reshape(n, N_KV_HEADS, HEAD_DIM, wb), (0, 3, 1, 2))[None]
    return (y_p, y_s.reshape(nb, 1, D_MODEL),
            jnp.transpose(conv_p, (1, 0, 2))[None], rnn_p, kv5(kw_p, bsz), kv5(vw_p, bsz),
            jnp.transpose(conv_s, (1, 0, 2))[None], rnn_s[None], kv5(kw_s, nb), kv5(vw_s, nb))
```

```python
import collections
import functools
import math

import numpy as np
import jax
import jax.numpy as jnp
from jax import lax
from jax.experimental import pallas as pl
from jax.experimental.pallas import tpu as pltpu

D_MODEL = 1024
D_RNN = 512
D_ATT = 512
HEAD_DIM = 64
N_HEADS = 8
N_KV_HEADS = 2
GROUP = N_HEADS // N_KV_HEADS
KV_DIM = N_KV_HEADS * HEAD_DIM
N_RNN_BLOCKS = 8
RNN_BLOCK = D_RNN // N_RNN_BLOCKS
CONV_W = 4
LRU_C = 8.0
WINDOW = 128
N_BUCKETS = 32
MAX_DISTANCE = 128
EPS = 1e-6
NEG_INF = -1e30
D_IN = 2 * D_RNN + 2 * D_ATT + 2 * KV_DIM

C_XRNN = 0
C_GRNN = D_RNN
C_Q = 2 * D_RNN
C_KV = 2 * D_RNN + D_ATT
C_GATT = 2 * D_RNN + D_ATT + 2 * KV_DIM

SUBLANES = 8
LANES = 128
GATE_GROUP = 256
ONES_ROWS = 16
SCORE_LOOKAHEAD = 4
PROMPT_SEQS = 2
PROMPT_BLOCK = 512
PREP_ROWS = 256
VMEM_LIMIT = 56 * 1024 * 1024

TINY = 1e-37
F32 = jnp.float32
BF16 = jnp.bfloat16


def _spread(groups, bulk):
    out, done = [], 0
    for i, group in enumerate(groups):
        out += group
        want = (i + 1) * len(bulk) // len(groups)
        out += bulk[done:want]
        done = want
    return out


def _step_program(npairs):
    a, b = range(PROMPT_SEQS)
    n_out = D_MODEL // GATE_GROUP

    def attention(s):
        groups = [[("score", s, n) for n in range(SCORE_LOOKAHEAD)]]
        for n in range(npairs):
            ahead = [("score", s, n + SCORE_LOOKAHEAD)] if n + SCORE_LOOKAHEAD < npairs else []
            groups.append(ahead + [("value", s, n)])
        return groups

    project = lambda s: [("project", s, name) for name in ("x_rnn", "kv", "q")]
    gates = lambda s: [("gate_piece", s, i) for i in range(4)]
    outs = lambda s: [("out_piece", s, j) for j in range(n_out)]
    pa, pb = project(a), project(b)
    prog = [("sample_scores", a), ("norm", a), pa[0], ("norm", b), pa[1], pa[2], ("store_kv", a)]
    prog += [("sample_values", a), ("conv_gates", a)]
    prog += _spread(attention(a), [("gate_math", a)] + pb + gates(a) + [("scan", a)])
    prog += [("att_finish", a), ("conv_gates", b), ("store_kv", b)]
    prog += _spread(attention(b), [("gate_math", b)] + gates(b) + outs(a) + [("scan", b)])
    prog += [("finish", a), ("att_finish", b)] + outs(b) + [("finish", b)]
    return prog


def _t5_bucket_np(dist):
    dist = np.maximum(dist, 0)
    max_exact = N_BUCKETS // 2
    d = np.maximum(dist, 1).astype(np.float32)
    ratio = np.log(d / np.float32(max_exact)) / np.float32(math.log(MAX_DISTANCE / max_exact))
    large = max_exact + (ratio * np.float32(N_BUCKETS - max_exact)).astype(np.int32)
    large = np.minimum(large, N_BUCKETS - 1)
    return np.where(dist < max_exact, dist, large).astype(np.int32)


def _folded_bucket_table():
    i = np.arange(WINDOW)[:, None]
    j = np.arange(WINDOW)[None, :]
    dist = np.where(j <= i, i - j, i + WINDOW - j)
    return _t5_bucket_np(dist)


def _sigmoid_of_half(zh):
    return 0.5 + 0.5 * jnp.tanh(zh)


def _silu(x):
    h = 0.5 * x
    return h + h * jnp.tanh(h)


def _softplus(x):
    return jnp.maximum(x, 0.0) + jnp.log1p(jnp.exp(-jnp.abs(x)))


def _rms_scale(x):
    return lax.rsqrt(jnp.mean(x * x, axis=-1, keepdims=True) + EPS)


def _dot(a, b):
    return jnp.dot(a, b, preferred_element_type=F32)


def _dot_nt(a, b):
    return lax.dot_general(a, b, (((1,), (1,)), ((), ())), preferred_element_type=F32)


def _rglru_gate_dots(xc, wg_ref):
    xcb = xc.astype(BF16)
    zs = [_dot(xcb[:, g * GATE_GROUP:(g + 1) * GATE_GROUP], wg_ref[g])
          for g in range(D_RNN // GATE_GROUP)]
    za = jnp.concatenate([z[:, :GATE_GROUP] for z in zs], axis=1)
    zx = jnp.concatenate([z[:, GATE_GROUP:] for z in zs], axis=1)
    return za, zx


def _rglru_gate_math(xc, za, zx, rp):
    r = _sigmoid_of_half(za + rp[5:6])
    i = _sigmoid_of_half(zx + rp[6:7])
    log_a = (-LRU_C * r) * _softplus(-rp[7:8])
    a = jnp.exp(log_a)
    u = 1.0 - a * a
    bx = (u * lax.rsqrt(jnp.maximum(u, TINY))) * (i * xc)
    return a, bx


def _prep_kernel(cw_raw, cb_raw, ba_raw, bx_raw, lam_raw, win_raw, wga_raw, wgx_raw, wout_raw,
                 rp_ref, win_ref, wg_ref, wout_ref):
    for c0 in range(0, D_IN, GATE_GROUP):
        w = win_raw[:, c0:c0 + GATE_GROUP]
        if C_Q <= c0 < C_Q + D_ATT:
            w = w * HEAD_DIM ** -0.5
        win_ref[:, c0:c0 + GATE_GROUP] = w.astype(BF16)
    wout_ref[...] = wout_raw[...].astype(BF16)

    @pl.when(pl.program_id(0) == 0)
    def _():
        rp_ref[0:CONV_W, :] = cw_raw[...]
        rp_ref[4:5, :] = cb_raw[...]
        rp_ref[5:6, :] = 0.5 * ba_raw[...]
        rp_ref[6:7, :] = 0.5 * bx_raw[...]
        rp_ref[7:8, :] = lam_raw[...]
        per = GATE_GROUP // RNN_BLOCK
        zero = jnp.zeros((RNN_BLOCK, RNN_BLOCK), F32)
        for g in range(D_RNN // GATE_GROUP):
            for half, w_raw in enumerate((wga_raw, wgx_raw)):
                rows = [jnp.concatenate([0.5 * w_raw[g * per + m] if k == m else zero for k in range(per)], axis=1)
                        for m in range(per)]
                wg_ref[g, :, half * GATE_GROUP:(half + 1) * GATE_GROUP] = jnp.concatenate(rows, axis=0).astype(BF16)


def _prep_call(conv_w, conv_b, b_gate_a, b_gate_x, lru_lambda, w_in, w_gate_a, w_gate_x, w_out):
    rows = PREP_ROWS
    whole = lambda *shape: pl.BlockSpec(shape, lambda i: (0,) * len(shape))
    return pl.pallas_call(
        _prep_kernel,
        grid=(D_MODEL // rows,),
        in_specs=[
            whole(CONV_W, D_RNN), whole(1, D_RNN), whole(1, D_RNN), whole(1, D_RNN), whole(1, D_RNN),
            pl.BlockSpec((rows, D_IN), lambda i: (i, 0)),
            whole(N_RNN_BLOCKS, RNN_BLOCK, RNN_BLOCK), whole(N_RNN_BLOCKS, RNN_BLOCK, RNN_BLOCK),
            pl.BlockSpec((rows, D_MODEL), lambda i: (i, 0)),
        ],
        out_specs=(
            whole(SUBLANES, D_RNN),
            pl.BlockSpec((rows, D_IN), lambda i: (i, 0)),
            whole(D_RNN // GATE_GROUP, GATE_GROUP, 2 * GATE_GROUP),
            pl.BlockSpec((rows, D_MODEL), lambda i: (i, 0)),
        ),
        out_shape=(
            jax.ShapeDtypeStruct((SUBLANES, D_RNN), F32),
            jax.ShapeDtypeStruct((D_MODEL, D_IN), BF16),
            jax.ShapeDtypeStruct((D_RNN // GATE_GROUP, GATE_GROUP, 2 * GATE_GROUP), BF16),
            jax.ShapeDtypeStruct((D_MODEL, D_MODEL), BF16),
        ),
        compiler_params=pltpu.CompilerParams(dimension_semantics=("arbitrary",)),
        name="param_layout",
    )(conv_w, conv_b, b_gate_a, b_gate_x, lru_lambda, w_in, w_gate_a, w_gate_x, w_out)


def _build_bias_table(bkt, rb_ref, head):
    tbl = jnp.zeros(bkt.shape, F32)
    for k in range(N_BUCKETS):
        tbl = jnp.where(bkt == k, rb_ref[k, head], tbl)
    return tbl


_SampleRefs = collections.namedtuple(
    "_SampleRefs", "x bktrow sc h0 kc vc y conv rnn ko vo qz o knew vnew rnnout gatt bias sink")


def _layer_kernel(rb_ref, sinks_ref, x_ref, bkt_ref, gpre_ref, gpost_ref, rp_ref, win_ref, wg_ref, wout_ref,
                  xs_ref, bktrow_ref, sc_ref, h0_ref, kc_ref, vc_ref,
                  y_ref, conv_ref, rnn_ref, kw_ref, vw_ref, ys_ref, convs_ref, rnns_ref, ko_ref, vo_ref,
                  bias_ref, xpad_ref, hc_ref, kd_ref, vt_ref, att_ref, xr_ref, hn_ref,
                  qz_ref, o_ref, knew_ref, vnew_ref, rnnout_ref, gatt_ref, sbias_ref, ssink_ref, *, tb, nseq, bb):
    t = pl.program_id(0)
    nq = tb // WINDOW
    sr = _SampleRefs(xs_ref, bktrow_ref, sc_ref, h0_ref, kc_ref, vc_ref, ys_ref, convs_ref, rnns_ref, ko_ref, vo_ref,
                     qz_ref, o_ref, knew_ref, vnew_ref, rnnout_ref, gatt_ref, sbias_ref, ssink_ref)

    @pl.when(t == 0)
    def _():
        bkt = bkt_ref[...]
        for h in range(N_HEADS):
            bias_ref[h] = _build_bias_table(bkt, rb_ref, h)
        xpad_ref[...] = jnp.zeros((nseq, (CONV_W - 1) * SUBLANES, D_RNN), F32)
        hc_ref[...] = jnp.zeros((nseq, 1, D_RNN), F32)
        kd_ref[:, :, 0:WINDOW, :] = jnp.zeros((nseq, N_KV_HEADS, WINDOW, LANES), BF16)
        vt_ref[:, :, 0:HEAD_DIM, 0:WINDOW] = jnp.zeros((nseq, N_KV_HEADS, HEAD_DIM, WINDOW), BF16)
        vt_ref[:, :, HEAD_DIM:, :] = jnp.ones((nseq, N_KV_HEADS, ONES_ROWS, tb + WINDOW), BF16)
        _sample_setup(sr, rb_ref, sinks_ref, gpre_ref, rp_ref, win_ref, wg_ref)

    rp = rp_ref[...]
    sub8 = lax.broadcasted_iota(jnp.int32, (SUBLANES, D_RNN), 0)
    lo = lax.broadcasted_iota(jnp.int32, (tb, LANES), 1) < HEAD_DIM
    key_idx = lax.broadcasted_iota(jnp.int32, (WINDOW, WINDOW), 0)
    qry_idx = lax.broadcasted_iota(jnp.int32, (WINDOW, WINDOW), 1)
    own = key_idx <= qry_idx
    lane_lo = qry_idx < HEAD_DIM
    has_prev0 = (jnp.zeros((WINDOW, WINDOW), jnp.int32) + t) > 0
    zero_b = jnp.zeros((WINDOW, WINDOW), BF16)
    pair_lo = lax.broadcasted_iota(jnp.int32, (1, 2 * WINDOW), 1) < WINDOW
    pairs = [(qi, c) for qi in range(nq) for c in range(D_ATT // LANES)]
    gate_cols = [C_GRNN, C_GRNN + GATE_GROUP, C_GATT, C_GATT + GATE_GROUP]

    def norm(s, st):
        x = x_ref[s]
        st.update(x=x, xn=(x * _rms_scale(x) * gpre_ref[...]).astype(BF16))

    def project(s, st, name):
        c0, width = dict(x_rnn=(C_XRNN, D_RNN), kv=(C_KV, 2 * KV_DIM), q=(C_Q, D_ATT))[name]
        st[name] = _dot(st["xn"], win_ref[:, c0:c0 + width])

    clen = tb // SUBLANES
    pitch = clen + SUBLANES
    ncol = D_RNN // LANES
    ntail = CONV_W - 1

    def conv_gates(s, st):
        x_rnn = st["x_rnn"]
        for c in range(ncol):
            for k in range(SUBLANES):
                xr_ref[s, c, k * pitch:k * pitch + clen, :] = x_rnn[k * clen:(k + 1) * clen, c * LANES:(c + 1) * LANES]
        xp = jnp.concatenate(
            [jnp.concatenate([xr_ref[s, c, pl.ds(v, SUBLANES, stride=pitch), :] for v in range(clen)], axis=0)
             for c in range(ncol)], axis=1)
        tail = xp[(clen - ntail) * SUBLANES:, :]
        prev_tail = xpad_ref[s]
        heads = [jnp.where(sub8 == 0,
                           pltpu.roll(prev_tail[j * SUBLANES:(j + 1) * SUBLANES], 1, axis=0),
                           pltpu.roll(tail[j * SUBLANES:(j + 1) * SUBLANES], 1, axis=0)) for j in range(ntail)]
        xc = rp[4:5]
        for tap in range(ntail):
            shift = ntail - tap
            xc = xc + rp[tap:tap + 1] * jnp.concatenate(heads[ntail - shift:] + [xp[:(clen - shift) * SUBLANES]], axis=0)
        xc = xc + rp[3:4] * xp
        xpad_ref[s] = tail
        for j in range(ntail):
            conv_ref[j, s:s + 1, :] = tail[j * SUBLANES + SUBLANES - 1:(j + 1) * SUBLANES]
        st["xc"] = xc
        st["za"], st["zx"] = _rglru_gate_dots(xc, wg_ref)

    def gate_math(s, st):
        st["a"], st["bx"] = _rglru_gate_math(st.pop("xc"), st.pop("za"), st.pop("zx"), rp)

    def store_kv(s, st):
        st["qb"] = st.pop("q").astype(BF16)
        k01 = st["kv"][:, :KV_DIM]
        v01 = st["kv"][:, KV_DIM:]
        k10 = pltpu.roll(k01, HEAD_DIM, axis=1)
        kd_ref[s, 0, WINDOW:WINDOW + tb, :] = jnp.where(lo, k01, k10).astype(BF16)
        kd_ref[s, 1, WINDOW:WINDOW + tb, :] = jnp.where(lo, k10, k01).astype(BF16)
        v_t = v01.T.astype(BF16)
        vt_ref[s, 0, 0:HEAD_DIM, WINDOW:WINDOW + tb] = v_t[0:HEAD_DIM]
        vt_ref[s, 1, 0:HEAD_DIM, WINDOW:WINDOW + tb] = v_t[HEAD_DIM:]
        kw_ref[s] = k01[tb - WINDOW:, :].T
        vw_ref[s] = v01[tb - WINDOW:, :].T

    def gate_piece(s, st, i):
        st[("gate", i)] = _dot(st["xn"], win_ref[:, gate_cols[i]:gate_cols[i] + GATE_GROUP])

    def scores(s, st, qi, c):
        rows = slice(qi * WINDOW, (qi + 1) * WINDOW)
        win = slice(qi * WINDOW, qi * WINDOW + 2 * WINDOW)
        kvh = (2 * c) // GROUP
        qc = st["qb"][rows, c * LANES:(c + 1) * LANES]
        qpair = jnp.concatenate([jnp.where(lane_lo, qc, zero_b), jnp.where(lane_lo, zero_b, qc)], axis=0)
        sc = _dot_nt(kd_ref[s, kvh, win, :], qpair)
        p_prev, p_own, ms = [], [], []
        for half in range(2):
            head = 2 * c + half
            cols = slice(half * WINDOW, (half + 1) * WINDOW)
            sf = jnp.where(own, sc[WINDOW:, cols], sc[:WINDOW, cols]) + bias_ref[head]
            if qi == 0:
                sf = jnp.where(own | has_prev0, sf, NEG_INF)
            m = jnp.max(sf, axis=0, keepdims=True)
            e = jnp.exp(sf - m).astype(BF16)
            p_prev.append(jnp.where(own, zero_b, e))
            p_own.append(jnp.where(own, e, zero_b))
            ms.append(m)
        p = jnp.concatenate([jnp.concatenate(p_prev, axis=1), jnp.concatenate(p_own, axis=1)], axis=0)
        return p, jnp.concatenate(ms, axis=1)

    def values(s, qi, c, p, m):
        rows = slice(qi * WINDOW, (qi + 1) * WINDOW)
        win = slice(qi * WINDOW, qi * WINDOW + 2 * WINDOW)
        kvh = (2 * c) // GROUP
        oa = _dot(vt_ref[s, kvh, :, win], p)
        sink = jnp.where(pair_lo, sinks_ref[2 * c], sinks_ref[2 * c + 1])
        den = oa[HEAD_DIM:HEAD_DIM + 1, :] + jnp.exp(sink - m)
        o = oa[0:HEAD_DIM, :] * (1.0 / den)
        o_t = jnp.concatenate([o[:, :WINDOW], o[:, WINDOW:]], axis=0)
        att_ref[s, rows, c * LANES:(c + 1) * LANES] = o_t.T

    def score(s, st, n):
        st[("p", n)] = scores(s, st, *pairs[n])

    def value(s, st, n):
        values(s, *pairs[n], *st.pop(("p", n)))

    def att_finish(s, st):
        kd_ref[s, :, 0:WINDOW, :] = kd_ref[s, :, tb:tb + WINDOW, :]
        vt_ref[s, :, :, 0:WINDOW] = vt_ref[s, :, :, tb:tb + WINDOW]
        g_att = jnp.concatenate([st[("gate", 2)], st[("gate", 3)]], axis=1)
        st["att_out"] = (att_ref[s] * _silu(g_att)).astype(BF16)

    def scan(s, st):
        a3 = st["a"].reshape(clen, SUBLANES, D_RNN)
        b3 = st["bx"].reshape(clen, SUBLANES, D_RNN)
        hl, pr = [b3[0]], [a3[0]]
        for v in range(1, clen):
            hl.append(a3[v] * hl[-1] + b3[v])
            pr.append(a3[v] * pr[-1])
        h_end, p_end = hl[-1], pr[-1]
        h_in = jnp.broadcast_to(hc_ref[s], (SUBLANES, D_RNN))
        entry = h_in
        for _ in range(SUBLANES - 1):
            entry = jnp.where(sub8 == 0, h_in, pltpu.roll(h_end + p_end * entry, 1, axis=0))
        carry = (h_end + p_end * entry)[SUBLANES - 1:SUBLANES, :]
        hc_ref[s] = carry
        rnn_ref[0, s:s + 1, :] = carry
        for v in range(clen):
            hv = hl[v] + pr[v] * entry
            for c in range(ncol):
                hn_ref[s, c, pl.ds(v, SUBLANES, stride=pitch), :] = hv[:, c * LANES:(c + 1) * LANES]
        h = jnp.concatenate(
            [jnp.concatenate([hn_ref[s, c, k * pitch:k * pitch + clen, :] for k in range(SUBLANES)], axis=0)
             for c in range(ncol)], axis=1)
        g_rnn = jnp.concatenate([st[("gate", 0)], st[("gate", 1)]], axis=1)
        st["rnn_out"] = (h * _silu(g_rnn)).astype(BF16)

    def out_piece(s, st, j):
        cols = slice(j * GATE_GROUP, (j + 1) * GATE_GROUP)
        st[("out", j)] = _dot(st["rnn_out"], wout_ref[0:D_RNN, cols]) + _dot(st["att_out"], wout_ref[D_RNN:, cols])

    def finish(s, st):
        out = jnp.concatenate([st[("out", j)] for j in range(D_MODEL // GATE_GROUP)], axis=1)
        y_ref[s] = st["x"] + out * _rms_scale(out) * gpost_ref[...]

    def sample_scores(s, st):
        _sample_scores(sr, t, bb, st)

    def sample_values(s, st):
        _sample_values(sr, t, bb, st)

    tasks = dict(sample_scores=sample_scores, sample_values=sample_values, norm=norm, project=project,
                 conv_gates=conv_gates, gate_math=gate_math, store_kv=store_kv, gate_piece=gate_piece,
                 score=score, value=value, att_finish=att_finish, scan=scan, out_piece=out_piece, finish=finish)
    sts = [dict() for _ in range(nseq)]
    for name, s, *arg in _step_program(len(pairs)):
        tasks[name](s, sts[s], *arg)

    @pl.when(t == pl.num_programs(0) - 1)
    def _():
        _sample_finish(sr, gpost_ref, wout_ref)


def _param_specs(const):
    return [
        const(SUBLANES, D_RNN),
        const(D_MODEL, D_IN),
        const(D_RNN // GATE_GROUP, GATE_GROUP, 2 * GATE_GROUP),
        const(D_MODEL, D_MODEL),
    ]


def _layer_call(x, rb, sinks, bkt, gpre, gpost, params, xs, bktrow, sc, h0, kc_t, vc_t):
    bsz, seq, _ = x.shape
    tb = PROMPT_BLOCK
    nt = seq // tb
    nb = xs.shape[0]
    bb = nb // nt
    win_t_spec = pl.BlockSpec((bb, KV_DIM, WINDOW), lambda t: (t, 0, 0))
    const = lambda *shape: pl.BlockSpec(shape, lambda t: (0,) * len(shape), pipeline_mode=pl.Buffered(1))
    whole = lambda *shape: pl.BlockSpec(shape, lambda t: (0,) * len(shape))
    smem = pl.BlockSpec(memory_space=pltpu.SMEM)
    out_shapes = (
        jax.ShapeDtypeStruct((bsz, seq, D_MODEL), F32),
        jax.ShapeDtypeStruct((CONV_W - 1, bsz, D_RNN), F32),
        jax.ShapeDtypeStruct((1, bsz, D_RNN), F32),
        jax.ShapeDtypeStruct((bsz, KV_DIM, WINDOW), F32),
        jax.ShapeDtypeStruct((bsz, KV_DIM, WINDOW), F32),
        jax.ShapeDtypeStruct((nb, D_MODEL), F32),
        jax.ShapeDtypeStruct((CONV_W - 1, nb, D_RNN), F32),
        jax.ShapeDtypeStruct((nb, D_RNN), F32),
        jax.ShapeDtypeStruct((nb, KV_DIM, WINDOW), F32),
        jax.ShapeDtypeStruct((nb, KV_DIM, WINDOW), F32),
    )
    return pl.pallas_call(
        functools.partial(_layer_kernel, tb=tb, nseq=bsz, bb=bb),
        grid=(nt,),
        in_specs=[
            smem, smem,
            pl.BlockSpec((bsz, tb, D_MODEL), lambda t: (0, t, 0)),
            const(WINDOW, WINDOW),
            const(1, D_MODEL), const(1, D_MODEL),
            *_param_specs(const),
            const(nb, D_MODEL), const(1, LANES), const(CONV_W - 1, nb, D_RNN), const(nb, D_RNN),
            win_t_spec, win_t_spec,
        ],
        out_specs=(
            pl.BlockSpec((bsz, tb, D_MODEL), lambda t: (0, t, 0)),
            whole(CONV_W - 1, bsz, D_RNN),
            whole(1, bsz, D_RNN),
            whole(bsz, KV_DIM, WINDOW),
            whole(bsz, KV_DIM, WINDOW),
            whole(nb, D_MODEL), whole(CONV_W - 1, nb, D_RNN), whole(nb, D_RNN),
            win_t_spec, win_t_spec,
        ),
        out_shape=out_shapes,
        scratch_shapes=[
            pltpu.VMEM((N_HEADS, WINDOW, WINDOW), F32),
            pltpu.VMEM((bsz, (CONV_W - 1) * SUBLANES, D_RNN), F32),
            pltpu.VMEM((bsz, 1, D_RNN), F32),
            pltpu.VMEM((bsz, N_KV_HEADS, tb + WINDOW, LANES), BF16),
            pltpu.VMEM((bsz, N_KV_HEADS, HEAD_DIM + ONES_ROWS, tb + WINDOW), BF16),
            pltpu.VMEM((bsz, tb, D_ATT), F32),
            pltpu.VMEM((bsz, D_RNN // LANES, tb + SUBLANES * SUBLANES, LANES), F32),
            pltpu.VMEM((bsz, D_RNN // LANES, tb + SUBLANES * SUBLANES, LANES), F32),
            pltpu.VMEM((nb * N_HEADS, LANES), F32),
            pltpu.VMEM((nb * N_HEADS, LANES), F32),
            pltpu.VMEM((nb, KV_DIM), F32),
            pltpu.VMEM((nb, KV_DIM), F32),
            pltpu.VMEM((nb, D_RNN), F32),
            pltpu.VMEM((nb, D_ATT), F32),
            pltpu.VMEM((SUBLANES, LANES), F32),
            pltpu.VMEM((SUBLANES, LANES), F32),
        ],
        compiler_params=pltpu.CompilerParams(
            dimension_semantics=("arbitrary",),
            vmem_limit_bytes=VMEM_LIMIT),
        name="layer_step",
    )(rb, sinks, x, bkt, gpre, gpost, *params, xs, bktrow, sc, h0, kc_t, vc_t)


def _sample_setup(sr, rb_ref, sinks_ref, gpre_ref, rp_ref, win_ref, wg_ref):
    nb = sr.x.shape[0]
    sub = lax.broadcasted_iota(jnp.int32, (SUBLANES, LANES), 0)
    lane_lo = lax.broadcasted_iota(jnp.int32, (nb, LANES), 1) < HEAD_DIM
    bkt = jnp.broadcast_to(sr.bktrow[...], (SUBLANES, LANES))
    bias = jnp.zeros((SUBLANES, LANES), F32)
    sinkm = jnp.zeros((SUBLANES, LANES), F32)
    for h in range(N_HEADS):
        bias = jnp.where(sub == h, _build_bias_table(bkt, rb_ref, h), bias)
        sinkm = jnp.where(sub == h, sinks_ref[h], sinkm)
    sr.bias[...] = bias
    sr.sink[...] = sinkm

    x = sr.x[...]
    xn = (x * _rms_scale(x) * gpre_ref[...]).astype(BF16)
    rp = rp_ref[...]
    x_rnn = _dot(xn, win_ref[:, C_XRNN:C_XRNN + D_RNN])
    xc = rp[4:5] + rp[0:1] * sr.sc[0]
    xc = xc + rp[1:2] * sr.sc[1]
    xc = xc + rp[2:3] * sr.sc[2]
    xc = xc + rp[3:4] * x_rnn
    sr.conv[0] = sr.sc[1]
    sr.conv[1] = sr.sc[2]
    sr.conv[2] = x_rnn
    a, bx = _rglru_gate_math(xc, *_rglru_gate_dots(xc, wg_ref), rp)
    h = a * sr.h0[...] + bx
    sr.rnn[...] = h
    g_rnn = _dot(xn, win_ref[:, C_GRNN:C_GRNN + D_RNN])
    sr.rnnout[...] = h * _silu(g_rnn)
    sr.gatt[...] = _dot(xn, win_ref[:, C_GATT:C_GATT + D_ATT])
    kv = _dot(xn, win_ref[:, C_KV:C_KV + 2 * KV_DIM])
    sr.knew[...] = kv[:, :KV_DIM]
    sr.vnew[...] = kv[:, KV_DIM:]
    q = _dot(xn, win_ref[:, C_Q:C_Q + D_ATT])
    for head in range(N_HEADS):
        c = head // 2
        kvh = head // GROUP
        qc = q[:, c * LANES:(c + 1) * LANES]
        if (head % 2) != kvh:
            qc = pltpu.roll(qc, HEAD_DIM, axis=1)
        qz = jnp.where(lane_lo, qc, 0.0) if kvh == 0 else jnp.where(lane_lo, 0.0, qc)
        sr.qz[pl.ds(head, nb, stride=N_HEADS), :] = qz


def _sample_scores(sr, step, bb, st):
    newest = lax.broadcasted_iota(jnp.int32, (WINDOW, LANES), 0) == WINDOW - 1
    bias = sr.bias[...]
    st["rows"] = [pl.multiple_of((step * bb + bi) * N_HEADS, N_HEADS) for bi in range(bb)]
    st["scores"] = []
    for bi in range(bb):
        newk = jnp.where(newest, sr.knew[pl.ds(step * bb + bi, 1), :], pltpu.roll(sr.kc[bi].T, WINDOW - 1, axis=0))
        sr.ko[bi] = newk.T
        qz = sr.qz[pl.ds(st["rows"][bi], N_HEADS), :].astype(BF16)
        st["scores"].append(_dot_nt(qz, newk.astype(BF16)) + bias)


def _sample_values(sr, step, bb, st):
    newest = lax.broadcasted_iota(jnp.int32, (WINDOW, LANES), 0) == WINDOW - 1
    sinkm = sr.sink[...]
    for bi, s in enumerate(st.pop("scores")):
        m = jnp.maximum(jnp.max(s, axis=1, keepdims=True), sinkm)
        e = jnp.exp(s - m)
        den = jnp.sum(e, axis=1, keepdims=True) + jnp.exp(sinkm - m)
        p = (e * (1.0 / den)).astype(BF16)
        newv = jnp.where(newest, sr.vnew[pl.ds(step * bb + bi, 1), :], pltpu.roll(sr.vc[bi].T, WINDOW - 1, axis=0))
        sr.vo[bi] = newv.T
        sr.o[pl.ds(st["rows"][bi], N_HEADS), :] = _dot(p, newv.astype(BF16))


def _sample_finish(sr, gpost_ref, wout_ref):
    nb = sr.x.shape[0]
    lane_lo = lax.broadcasted_iota(jnp.int32, (nb, LANES), 1) < HEAD_DIM
    cols = []
    for c in range(D_ATT // LANES):
        kvh = (2 * c) // GROUP
        halves = []
        for half in range(2):
            oh = sr.o[pl.ds(2 * c + half, nb, stride=N_HEADS), :]
            if half != kvh:
                oh = pltpu.roll(oh, HEAD_DIM, axis=1)
            halves.append(oh)
        cols.append(jnp.where(lane_lo, halves[0], halves[1]))
    att = jnp.concatenate(cols, axis=1)
    att_out = (att * _silu(sr.gatt[...])).astype(BF16)
    rnn_out = sr.rnnout[...].astype(BF16)
    out = _dot(rnn_out, wout_ref[0:D_RNN, :]) + _dot(att_out, wout_ref[D_RNN:, :])
    sr.y[...] = sr.x[...] + out * _rms_scale(out) * gpost_ref[...]


def kernel(x_prompt, x_sample, state_conv, state_rnn, cache_k_win, cache_v_win, norm_pre, norm_post,
           w_in, conv_w, conv_b, w_gate_a, b_gate_a, w_gate_x, b_gate_x, lru_lambda, attn_sinks,
           rel_bias, w_out):
    assert w_in.shape[0] == 1, "single-layer trunk"
    bsz, seq, _ = x_prompt.shape
    nb = x_sample.shape[0]
    wb = cache_k_win.shape[2]
    assert wb == WINDOW and x_sample.shape[1] == 1 and seq % PROMPT_BLOCK == 0 and nb % (seq // PROMPT_BLOCK) == 0
    assert bsz == PROMPT_SEQS, "the step program is written for this many prompt sequences"

    params = _prep_call(conv_w[0], conv_b, b_gate_a, b_gate_x, lru_lambda, w_in[0], w_gate_a[0], w_gate_x[0], w_out[0])
    gpre = norm_pre.reshape(1, D_MODEL)
    gpost = norm_post.reshape(1, D_MODEL)
    sinks = attn_sinks[0]
    bkt_np = _folded_bucket_table()
    bkt = jnp.asarray(np.ascontiguousarray(bkt_np.T))
    bktrow = jnp.asarray(bkt_np[WINDOW - 1:WINDOW, :])

    feature_major = lambda z: jnp.transpose(z[0], (0, 2, 3, 1)).reshape(nb, KV_DIM, wb)
    y_p, conv_p, rnn_p, kw_p, vw_p, y_s, conv_s, rnn_s, kw_s, vw_s = _layer_call(
        x_prompt, rel_bias, sinks, bkt, gpre, gpost, params,
        x_sample.reshape(nb, D_MODEL), bktrow, jnp.transpose(state_conv[0], (1, 0, 2)), state_rnn[0],
        feature_major(cache_k_win), feature_major(cache_v_win))

    kv5 = lambda z, n: jnp.transpose(z.reshape(n, N_KV_HEADS, HEAD_DIM, wb), (0, 3, 1, 2))[None]
    return (y_p, y_s.reshape(nb, 1, D_MODEL),
            jnp.transpose(conv_p, (1, 0, 2))[None], rnn_p, kv5(kw_p, bsz), kv5(vw_p, bsz),
            jnp.transpose(conv_s, (1, 0, 2))[None], rnn_s[None], kv5(kw_s, nb), kv5(vw_s, nb))
```

```python
import collections
import functools
import math

import numpy as np
import jax
import jax.numpy as jnp
from jax import lax
from jax.experimental import pallas as pl
from jax.experimental.pallas import tpu as pltpu

D_MODEL = 1024
D_RNN = 512
D_ATT = 512
HEAD_DIM = 64
N_HEADS = 8
N_KV_HEADS = 2
GROUP = N_HEADS // N_KV_HEADS
KV_DIM = N_KV_HEADS * HEAD_DIM
N_RNN_BLOCKS = 8
RNN_BLOCK = D_RNN // N_RNN_BLOCKS
CONV_W = 4
LRU_C = 8.0
WINDOW = 128
N_BUCKETS = 32
MAX_DISTANCE = 128
EPS = 1e-6
NEG_INF = -1e30
D_IN = 2 * D_RNN + 2 * D_ATT + 2 * KV_DIM

C_XRNN = 0
C_GRNN = D_RNN
C_Q = 2 * D_RNN
C_KV = 2 * D_RNN + D_ATT
C_GATT = 2 * D_RNN + D_ATT + 2 * KV_DIM

SUBLANES = 8
LANES = 128
GATE_GROUP = 256
ONES_ROWS = 16
SCORE_LOOKAHEAD = 4
PROMPT_SEQS = 2
PROMPT_BLOCK = 512
PREP_ROWS = 256
VMEM_LIMIT = 56 * 1024 * 1024

TINY = 1e-37
F32 = jnp.float32
BF16 = jnp.bfloat16


def _spread(groups, bulk):
    out, done = [], 0
    for i, group in enumerate(groups):
        out += group
        want = (i + 1) * len(bulk) // len(groups)
        out += bulk[done:want]
        done = want
    return out


def _step_program(npairs):
    a, b = range(PROMPT_SEQS)
    n_out = D_MODEL // GATE_GROUP

    def attention(s):
        groups = [[("score", s, n) for n in range(SCORE_LOOKAHEAD)]]
        for n in range(npairs):
            ahead = [("score", s, n + SCORE_LOOKAHEAD)] if n + SCORE_LOOKAHEAD < npairs else []
            groups.append(ahead + [("value", s, n)])
        return groups

    project = lambda s: [("project", s, name) for name in ("x_rnn", "kv", "q")]
    gates = lambda s: [("gate_piece", s, i) for i in range(4)]
    outs = lambda s: [("out_piece", s, j) for j in range(n_out)]
    pa, pb = project(a), project(b)
    prog = [("sample_scores", a), ("norm", a), pa[0], ("norm", b), pa[1], pa[2], ("store_kv", a)]
    prog += [("sample_values", a), ("conv_gates", a)]
    prog += _spread(attention(a), [("gate_math", a)] + pb + gates(a) + [("scan", a)])
    prog += [("att_finish", a), ("conv_gates", b), ("store_kv", b)]
    prog += _spread(attention(b), [("gate_math", b)] + gates(b) + outs(a) + [("scan", b)])
    prog += [("finish", a), ("att_finish", b)] + outs(b) + [("finish", b)]
    return prog


def _t5_bucket_np(dist):
    dist = np.maximum(dist, 0)
    max_exact = N_BUCKETS // 2
    d = np.maximum(dist, 1).astype(np.float32)
    ratio = np.log(d / np.float32(max_exact)) / np.float32(math.log(MAX_DISTANCE / max_exact))
    large = max_exact + (ratio * np.float32(N_BUCKETS - max_exact)).astype(np.int32)
    large = np.minimum(large, N_BUCKETS - 1)
    return np.where(dist < max_exact, dist, large).astype(np.int32)


def _folded_bucket_table():
    i = np.arange(WINDOW)[:, None]
    j = np.arange(WINDOW)[None, :]
    dist = np.where(j <= i, i - j, i + WINDOW - j)
    return _t5_bucket_np(dist)


def _sigmoid_of_half(zh):
    return 0.5 + 0.5 * jnp.tanh(zh)


def _silu(x):
    h = 0.5 * x
    return h + h * jnp.tanh(h)


def _softplus(x):
    return jnp.maximum(x, 0.0) + jnp.log1p(jnp.exp(-jnp.abs(x)))


def _rms_scale(x):
    return lax.rsqrt(jnp.mean(x * x, axis=-1, keepdims=True) + EPS)


def _dot(a, b):
    return jnp.dot(a, b, preferred_element_type=F32)


def _dot_nt(a, b):
    return lax.dot_general(a, b, (((1,), (1,)), ((), ())), preferred_element_type=F32)


def _rglru_gate_dots(xc, wg_ref):
    xcb = xc.astype(BF16)
    zs = [_dot(xcb[:, g * GATE_GROUP:(g + 1) * GATE_GROUP], wg_ref[g])
          for g in range(D_RNN // GATE_GROUP)]
    za = jnp.concatenate([z[:, :GATE_GROUP] for z in zs], axis=1)
    zx = jnp.concatenate([z[:, GATE_GROUP:] for z in zs], axis=1)
    return za, zx


def _rglru_gate_math(xc, za, zx, rp):
    r = _sigmoid_of_half(za + rp[5:6])
    i = _sigmoid_of_half(zx + rp[6:7])
    log_a = (-LRU_C * r) * _softplus(-rp[7:8])
    a = jnp.exp(log_a)
    u = 1.0 - a * a
    bx = (u * lax.rsqrt(jnp.maximum(u, TINY))) * (i * xc)
    return a, bx


def _prep_kernel(cw_raw, cb_raw, ba_raw, bx_raw, lam_raw, win_raw, wga_raw, wgx_raw, wout_raw,
                 rp_ref, win_ref, wg_ref, wout_ref):
    for c0 in range(0, D_IN, GATE_GROUP):
        w = win_raw[:, c0:c0 + GATE_GROUP]
        if C_Q <= c0 < C_Q + D_ATT:
            w = w * HEAD_DIM ** -0.5
        win_ref[:, c0:c0 + GATE_GROUP] = w.astype(BF16)
    wout_ref[...] = wout_raw[...].astype(BF16)

    @pl.when(pl.program_id(0) == 0)
    def _():
        rp_ref[0:CONV_W, :] = cw_raw[...]
        rp_ref[4:5, :] = cb_raw[...]
        rp_ref[5:6, :] = 0.5 * ba_raw[...]
        rp_ref[6:7, :] = 0.5 * bx_raw[...]
        rp_ref[7:8, :] = lam_raw[...]
        per = GATE_GROUP // RNN_BLOCK
        zero = jnp.zeros((RNN_BLOCK, RNN_BLOCK), F32)
        for g in range(D_RNN // GATE_GROUP):
            for half, w_raw in enumerate((wga_raw, wgx_raw)):
                rows = [jnp.concatenate([0.5 * w_raw[g * per + m] if k == m else zero for k in range(per)], axis=1)
                        for m in range(per)]
                wg_ref[g, :, half * GATE_GROUP:(half + 1) * GATE_GROUP] = jnp.concatenate(rows, axis=0).astype(BF16)


def _prep_call(conv_w, conv_b, b_gate_a, b_gate_x, lru_lambda, w_in, w_gate_a, w_gate_x, w_out):
    rows = PREP_ROWS
    whole = lambda *shape: pl.BlockSpec(shape, lambda i: (0,) * len(shape))
    return pl.pallas_call(
        _prep_kernel,
        grid=(D_MODEL // rows,),
        in_specs=[
            whole(CONV_W, D_RNN), whole(1, D_RNN), whole(1, D_RNN), whole(1, D_RNN), whole(1, D_RNN),
            pl.BlockSpec((rows, D_IN), lambda i: (i, 0)),
            whole(N_RNN_BLOCKS, RNN_BLOCK, RNN_BLOCK), whole(N_RNN_BLOCKS, RNN_BLOCK, RNN_BLOCK),
            pl.BlockSpec((rows, D_MODEL), lambda i: (i, 0)),
        ],
        out_specs=(
            whole(SUBLANES, D_RNN),
            pl.BlockSpec((rows, D_IN), lambda i: (i, 0)),
            whole(D_RNN // GATE_GROUP, GATE_GROUP, 2 * GATE_GROUP),
            pl.BlockSpec((rows, D_MODEL), lambda i: (i, 0)),
        ),
        out_shape=(
            jax.ShapeDtypeStruct((SUBLANES, D_RNN), F32),
            jax.ShapeDtypeStruct((D_MODEL, D_IN), BF16),
            jax.ShapeDtypeStruct((D_RNN // GATE_GROUP, GATE_GROUP, 2 * GATE_GROUP), BF16),
            jax.ShapeDtypeStruct((D_MODEL, D_MODEL), BF16),
        ),
        compiler_params=pltpu.CompilerParams(dimension_semantics=("arbitrary",)),
        name="param_layout",
    )(conv_w, conv_b, b_gate_a, b_gate_x, lru_lambda, w_in, w_gate_a, w_gate_x, w_out)


def _build_bias_table(bkt, rb_ref, head):
    tbl = jnp.zeros(bkt.shape, F32)
    for k in range(N_BUCKETS):
        tbl = jnp.where(bkt == k, rb_ref[k, head], tbl)
    return tbl


_SampleRefs = collections.namedtuple(
    "_SampleRefs", "x bktrow sc h0 kc vc y conv rnn ko vo qz o knew vnew rnnout gatt bias sink")


def _layer_kernel(rb_ref, sinks_ref, x_ref, bkt_ref, gpre_ref, gpost_ref, rp_ref, win_ref, wg_ref, wout_ref,
                  xs_ref, bktrow_ref, sc_ref, h0_ref, kc_ref, vc_ref,
                  y_ref, conv_ref, rnn_ref, kw_ref, vw_ref, ys_ref, convs_ref, rnns_ref, ko_ref, vo_ref,
                  bias_ref, xpad_ref, hc_ref, kd_ref, vt_ref, att_ref, xr_ref, hn_ref,
                  qz_ref, o_ref, knew_ref, vnew_ref, rnnout_ref, gatt_ref, sbias_ref, ssink_ref, *, tb, nseq, bb):
    t = pl.program_id(0)
    nq = tb // WINDOW
    sr = _SampleRefs(xs_ref, bktrow_ref, sc_ref, h0_ref, kc_ref, vc_ref, ys_ref, convs_ref, rnns_ref, ko_ref, vo_ref,
                     qz_ref, o_ref, knew_ref, vnew_ref, rnnout_ref, gatt_ref, sbias_ref, ssink_ref)

    @pl.when(t == 0)
    def _():
        bkt = bkt_ref[...]
        for h in range(N_HEADS):
            bias_ref[h] = _build_bias_table(bkt, rb_ref, h)
        xpad_ref[...] = jnp.zeros((nseq, (CONV_W - 1) * SUBLANES, D_RNN), F32)
        hc_ref[...] = jnp.zeros((nseq, 1, D_RNN), F32)
        kd_ref[:, :, 0:WINDOW, :] = jnp.zeros((nseq, N_KV_HEADS, WINDOW, LANES), BF16)
        vt_ref[:, :, 0:HEAD_DIM, 0:WINDOW] = jnp.zeros((nseq, N_KV_HEADS, HEAD_DIM, WINDOW), BF16)
        vt_ref[:, :, HEAD_DIM:, :] = jnp.ones((nseq, N_KV_HEADS, ONES_ROWS, tb + WINDOW), BF16)
        _sample_setup(sr, rb_ref, sinks_ref, gpre_ref, rp_ref, win_ref, wg_ref)

    rp = rp_ref[...]
    sub8 = lax.broadcasted_iota(jnp.int32, (SUBLANES, D_RNN), 0)
    lo = lax.broadcasted_iota(jnp.int32, (tb, LANES), 1) < HEAD_DIM
    key_idx = lax.broadcasted_iota(jnp.int32, (WINDOW, WINDOW), 0)
    qry_idx = lax.broadcasted_iota(jnp.int32, (WINDOW, WINDOW), 1)
    own = key_idx <= qry_idx
    lane_lo = qry_idx < HEAD_DIM
    has_prev0 = (jnp.zeros((WINDOW, WINDOW), jnp.int32) + t) > 0
    zero_b = jnp.zeros((WINDOW, WINDOW), BF16)
    pair_lo = lax.broadcasted_iota(jnp.int32, (1, 2 * WINDOW), 1) < WINDOW
    pairs = [(qi, c) for qi in range(nq) for c in range(D_ATT // LANES)]
    gate_cols = [C_GRNN, C_GRNN + GATE_GROUP, C_GATT, C_GATT + GATE_GROUP]

    def norm(s, st):
        x = x_ref[s]
        st.update(x=x, xn=(x * _rms_scale(x) * gpre_ref[...]).astype(BF16))

    def project(s, st, name):
        c0, width = dict(x_rnn=(C_XRNN, D_RNN), kv=(C_KV, 2 * KV_DIM), q=(C_Q, D_ATT))[name]
        st[name] = _dot(st["xn"], win_ref[:, c0:c0 + width])

    clen = tb // SUBLANES
    pitch = clen + SUBLANES
    ncol = D_RNN // LANES
    ntail = CONV_W - 1

    def conv_gates(s, st):
        x_rnn = st["x_rnn"]
        for c in range(ncol):
            for k in range(SUBLANES):
                xr_ref[s, c, k * pitch:k * pitch + clen, :] = x_rnn[k * clen:(k + 1) * clen, c * LANES:(c + 1) * LANES]
        xp = jnp.concatenate(
            [jnp.concatenate([xr_ref[s, c, pl.ds(v, SUBLANES, stride=pitch), :] for v in range(clen)], axis=0)
             for c in range(ncol)], axis=1)
        tail = xp[(clen - ntail) * SUBLANES:, :]
        prev_tail = xpad_ref[s]
        heads = [jnp.where(sub8 == 0,
                           pltpu.roll(prev_tail[j * SUBLANES:(j + 1) * SUBLANES], 1, axis=0),
                           pltpu.roll(tail[j * SUBLANES:(j + 1) * SUBLANES], 1, axis=0)) for j in range(ntail)]
        xc = rp[4:5]
        for tap in range(ntail):
            shift = ntail - tap
            xc = xc + rp[tap:tap + 1] * jnp.concatenate(heads[ntail - shift:] + [xp[:(clen - shift) * SUBLANES]], axis=0)
        xc = xc + rp[3:4] * xp
        xpad_ref[s] = tail
        for j in range(ntail):
            conv_ref[j, s:s + 1, :] = tail[j * SUBLANES + SUBLANES - 1:(j + 1) * SUBLANES]
        st["xc"] = xc
        st["za"], st["zx"] = _rglru_gate_dots(xc, wg_ref)

    def gate_math(s, st):
        st["a"], st["bx"] = _rglru_gate_math(st.pop("xc"), st.pop("za"), st.pop("zx"), rp)

    def store_kv(s, st):
        st["qb"] = st.pop("q").astype(BF16)
        k01 = st["kv"][:, :KV_DIM]
        v01 = st["kv"][:, KV_DIM:]
        k10 = pltpu.roll(k01, HEAD_DIM, axis=1)
        kd_ref[s, 0, WINDOW:WINDOW + tb, :] = jnp.where(lo, k01, k10).astype(BF16)
        kd_ref[s, 1, WINDOW:WINDOW + tb, :] = jnp.where(lo, k10, k01).astype(BF16)
        v_t = v01.T.astype(BF16)
        vt_ref[s, 0, 0:HEAD_DIM, WINDOW:WINDOW + tb] = v_t[0:HEAD_DIM]
        vt_ref[s, 1, 0:HEAD_DIM, WINDOW:WINDOW + tb] = v_t[HEAD_DIM:]
        kw_ref[s] = k01[tb - WINDOW:, :].T
        vw_ref[s] = v01[tb - WINDOW:, :].T

    def gate_piece(s, st, i):
        st[("gate", i)] = _dot(st["xn"], win_ref[:, gate_cols[i]:gate_cols[i] + GATE_GROUP])

    def scores(s, st, qi, c):
        rows = slice(qi * WINDOW, (qi + 1) * WINDOW)
        win = slice(qi * WINDOW, qi * WINDOW + 2 * WINDOW)
        kvh = (2 * c) // GROUP
        qc = st["qb"][rows, c * LANES:(c + 1) * LANES]
        qpair = jnp.concatenate([jnp.where(lane_lo, qc, zero_b), jnp.where(lane_lo, zero_b, qc)], axis=0)
        sc = _dot_nt(kd_ref[s, kvh, win, :], qpair)
        p_prev, p_own, ms = [], [], []
        for half in range(2):
            head = 2 * c + half
            cols = slice(half * WINDOW, (half + 1) * WINDOW)
            sf = jnp.where(own, sc[WINDOW:, cols], sc[:WINDOW, cols]) + bias_ref[head]
            if qi == 0:
                sf = jnp.where(own | has_prev0, sf, NEG_INF)
            m = jnp.max(sf, axis=0, keepdims=True)
            e = jnp.exp(sf - m).astype(BF16)
            p_prev.append(jnp.where(own, zero_b, e))
            p_own.append(jnp.where(own, e, zero_b))
            ms.append(m)
        p = jnp.concatenate([jnp.concatenate(p_prev, axis=1), jnp.concatenate(p_own, axis=1)], axis=0)
        return p, jnp.concatenate(ms, axis=1)

    def values(s, qi, c, p, m):
        rows = slice(qi * WINDOW, (qi + 1) * WINDOW)
        win = slice(qi * WINDOW, qi * WINDOW + 2 * WINDOW)
        kvh = (2 * c) // GROUP
        oa = _dot(vt_ref[s, kvh, :, win], p)
        sink = jnp.where(pair_lo, sinks_ref[2 * c], sinks_ref[2 * c + 1])
        den = oa[HEAD_DIM:HEAD_DIM + 1, :] + jnp.exp(sink - m)
        o = oa[0:HEAD_DIM, :] * (1.0 / den)
        o_t = jnp.concatenate([o[:, :WINDOW], o[:, WINDOW:]], axis=0)
        att_ref[s, rows, c * LANES:(c + 1) * LANES] = o_t.T

    def score(s, st, n):
        st[("p", n)] = scores(s, st, *pairs[n])

    def value(s, st, n):
        values(s, *pairs[n], *st.pop(("p", n)))

    def att_finish(s, st):
        kd_ref[s, :, 0:WINDOW, :] = kd_ref[s, :, tb:tb + WINDOW, :]
        vt_ref[s, :, :, 0:WINDOW] = vt_ref[s, :, :, tb:tb + WINDOW]
        g_att = jnp.concatenate([st[("gate", 2)], st[("gate", 3)]], axis=1)
        st["att_out"] = (att_ref[s] * _silu(g_att)).astype(BF16)

    def scan(s, st):
        a3 = st["a"].reshape(clen, SUBLANES, D_RNN)
        b3 = st["bx"].reshape(clen, SUBLANES, D_RNN)
        hl, pr = [b3[0]], [a3[0]]
        for v in range(1, clen):
            hl.append(a3[v] * hl[-1] + b3[v])
            pr.append(a3[v] * pr[-1])
        h_end, p_end = hl[-1], pr[-1]
        h_in = jnp.broadcast_to(hc_ref[s], (SUBLANES, D_RNN))
        entry = h_in
        for _ in range(SUBLANES - 1):
            entry = jnp.where(sub8 == 0, h_in, pltpu.roll(h_end + p_end * entry, 1, axis=0))
        carry = (h_end + p_end * entry)[SUBLANES - 1:SUBLANES, :]
        hc_ref[s] = carry
        rnn_ref[0, s:s + 1, :] = carry
        for v in range(clen):
            hv = hl[v] + pr[v] * entry
            for c in range(ncol):
                hn_ref[s, c, pl.ds(v, SUBLANES, stride=pitch), :] = hv[:, c * LANES:(c + 1) * LANES]
        h = jnp.concatenate(
            [jnp.concatenate([hn_ref[s, c, k * pitch:k * pitch + clen, :] for k in range(SUBLANES)], axis=0)
             for c in range(ncol)], axis=1)
        g_rnn = jnp.concatenate([st[("gate", 0)], st[("gate", 1)]], axis=1)
        st["rnn_out"] = (h * _silu(g_rnn)).astype(BF16)

    def out_piece(s, st, j):
        cols = slice(j * GATE_GROUP, (j + 1) * GATE_GROUP)
        st[("out", j)] = _dot(st["rnn_out"], wout_ref[0:D_RNN, cols]) + _dot(st["att_out"], wout_ref[D_RNN:, cols])

    def finish(s, st):
        out = jnp.concatenate([st[("out", j)] for j in range(D_MODEL // GATE_GROUP)], axis=1)
        y_ref[s] = st["x"] + out * _rms_scale(out) * gpost_ref[...]

    def sample_scores(s, st):
        _sample_scores(sr, t, bb, st)

    def sample_values(s, st):
        _sample_values(sr, t, bb, st)

    tasks = dict(sample_scores=sample_scores, sample_values=sample_values, norm=norm, project=project,
                 conv_gates=conv_gates, gate_math=gate_math, store_kv=store_kv, gate_piece=gate_piece,
                 score=score, value=value, att_finish=att_finish, scan=scan, out_piece=out_piece, finish=finish)
    sts = [dict() for _ in range(nseq)]
    for name, s, *arg in _step_program(len(pairs)):
        tasks[name](s, sts[s], *arg)

    @pl.when(t == pl.num_programs(0) - 1)
    def _():
        _sample_finish(sr, gpost_ref, wout_ref)


def _param_specs(const):
    return [
        const(SUBLANES, D_RNN),
        const(D_MODEL, D_IN),
        const(D_RNN // GATE_GROUP, GATE_GROUP, 2 * GATE_GROUP),
        const(D_MODEL, D_MODEL),
    ]


def _layer_call(x, rb, sinks, bkt, gpre, gpost, params, xs, bktrow, sc, h0, kc_t, vc_t):
    bsz, seq, _ = x.shape
    tb = PROMPT_BLOCK
    nt = seq // tb
    nb = xs.shape[0]
    bb = nb // nt
    win_t_spec = pl.BlockSpec((bb, KV_DIM, WINDOW), lambda t: (t, 0, 0))
    const = lambda *shape: pl.BlockSpec(shape, lambda t: (0,) * len(shape), pipeline_mode=pl.Buffered(1))
    whole = lambda *shape: pl.BlockSpec(shape, lambda t: (0,) * len(shape))
    smem = pl.BlockSpec(memory_space=pltpu.SMEM)
    out_shapes = (
        jax.ShapeDtypeStruct((bsz, seq, D_MODEL), F32),
        jax.ShapeDtypeStruct((CONV_W - 1, bsz, D_RNN), F32),
        jax.ShapeDtypeStruct((1, bsz, D_RNN), F32),
        jax.ShapeDtypeStruct((bsz, KV_DIM, WINDOW), F32),
        jax.ShapeDtypeStruct((bsz, KV_DIM, WINDOW), F32),
        jax.ShapeDtypeStruct((nb, 1, D_MODEL), F32),
        jax.ShapeDtypeStruct((CONV_W - 1, nb, D_RNN), F32),
        jax.ShapeDtypeStruct((nb, D_RNN), F32),
        jax.ShapeDtypeStruct((nb, KV_DIM, WINDOW), F32),
        jax.ShapeDtypeStruct((nb, KV_DIM, WINDOW), F32),
    )
    return pl.pallas_call(
        functools.partial(_layer_kernel, tb=tb, nseq=bsz, bb=bb),
        grid=(nt,),
        in_specs=[
            smem, smem,
            pl.BlockSpec((bsz, tb, D_MODEL), lambda t: (0, t, 0)),
            const(WINDOW, WINDOW),
            const(1, D_MODEL), const(1, D_MODEL),
            *_param_specs(const),
            const(nb, 1, D_MODEL), const(1, LANES), const(CONV_W - 1, nb, D_RNN), const(nb, D_RNN),
            win_t_spec, win_t_spec,
        ],
        out_specs=(
            pl.BlockSpec((bsz, tb, D_MODEL), lambda t: (0, t, 0)),
            whole(CONV_W - 1, bsz, D_RNN),
            whole(1, bsz, D_RNN),
            whole(bsz, KV_DIM, WINDOW),
            whole(bsz, KV_DIM, WINDOW),
            whole(nb, 1, D_MODEL), whole(CONV_W - 1, nb, D_RNN), whole(nb, D_RNN),
            win_t_spec, win_t_spec,
        ),
        out_shape=out_shapes,
        scratch_shapes=[
            pltpu.VMEM((N_HEADS, WINDOW, WINDOW), F32),
            pltpu.VMEM((bsz, (CONV_W - 1) * SUBLANES, D_RNN), F32),
            pltpu.VMEM((bsz, 1, D_RNN), F32),
            pltpu.VMEM((bsz, N_KV_HEADS, tb + WINDOW, LANES), BF16),
            pltpu.VMEM((bsz, N_KV_HEADS, HEAD_DIM + ONES_ROWS, tb + WINDOW), BF16),
            pltpu.VMEM((bsz, tb, D_ATT), F32),
            pltpu.VMEM((bsz, D_RNN // LANES, tb + SUBLANES * SUBLANES, LANES), F32),
            pltpu.VMEM((bsz, D_RNN // LANES, tb + SUBLANES * SUBLANES, LANES), F32),
            pltpu.VMEM((nb * N_HEADS, LANES), F32),
            pltpu.VMEM((nb * N_HEADS, LANES), F32),
            pltpu.VMEM((nb, KV_DIM), F32),
            pltpu.VMEM((nb, KV_DIM), F32),
            pltpu.VMEM((nb, D_RNN), F32),
            pltpu.VMEM((nb, D_ATT), F32),
            pltpu.VMEM((SUBLANES, LANES), F32),
            pltpu.VMEM((SUBLANES, LANES), F32),
        ],
        compiler_params=pltpu.CompilerParams(
            dimension_semantics=("arbitrary",),
            vmem_limit_bytes=VMEM_LIMIT),
        name="layer_step",
    )(rb, sinks, x, bkt, gpre, gpost, *params, xs, bktrow, sc, h0, kc_t, vc_t)


def _sample_setup(sr, rb_ref, sinks_ref, gpre_ref, rp_ref, win_ref, wg_ref):
    nb = sr.x.shape[0]
    sub = lax.broadcasted_iota(jnp.int32, (SUBLANES, LANES), 0)
    lane_lo = lax.broadcasted_iota(jnp.int32, (nb, LANES), 1) < HEAD_DIM
    bkt = jnp.broadcast_to(sr.bktrow[...], (SUBLANES, LANES))
    bias = jnp.zeros((SUBLANES, LANES), F32)
    sinkm = jnp.zeros((SUBLANES, LANES), F32)
    for h in range(N_HEADS):
        bias = jnp.where(sub == h, _build_bias_table(bkt, rb_ref, h), bias)
        sinkm = jnp.where(sub == h, sinks_ref[h], sinkm)
    sr.bias[...] = bias
    sr.sink[...] = sinkm

    x = sr.x[:, 0, :]
    xn = (x * _rms_scale(x) * gpre_ref[...]).astype(BF16)
    rp = rp_ref[...]
    x_rnn = _dot(xn, win_ref[:, C_XRNN:C_XRNN + D_RNN])
    xc = rp[4:5] + rp[0:1] * sr.sc[0]
    xc = xc + rp[1:2] * sr.sc[1]
    xc = xc + rp[2:3] * sr.sc[2]
    xc = xc + rp[3:4] * x_rnn
    sr.conv[0] = sr.sc[1]
    sr.conv[1] = sr.sc[2]
    sr.conv[2] = x_rnn
    a, bx = _rglru_gate_math(xc, *_rglru_gate_dots(xc, wg_ref), rp)
    h = a * sr.h0[...] + bx
    sr.rnn[...] = h
    g_rnn = _dot(xn, win_ref[:, C_GRNN:C_GRNN + D_RNN])
    sr.rnnout[...] = h * _silu(g_rnn)
    sr.gatt[...] = _dot(xn, win_ref[:, C_GATT:C_GATT + D_ATT])
    kv = _dot(xn, win_ref[:, C_KV:C_KV + 2 * KV_DIM])
    sr.knew[...] = kv[:, :KV_DIM]
    sr.vnew[...] = kv[:, KV_DIM:]
    q = _dot(xn, win_ref[:, C_Q:C_Q + D_ATT])
    for head in range(N_HEADS):
        c = head // 2
        kvh = head // GROUP
        qc = q[:, c * LANES:(c + 1) * LANES]
        if (head % 2) != kvh:
            qc = pltpu.roll(qc, HEAD_DIM, axis=1)
        qz = jnp.where(lane_lo, qc, 0.0) if kvh == 0 else jnp.where(lane_lo, 0.0, qc)
        sr.qz[pl.ds(head, nb, stride=N_HEADS), :] = qz


def _sample_scores(sr, step, bb, st):
    newest = lax.broadcasted_iota(jnp.int32, (WINDOW, LANES), 0) == WINDOW - 1
    bias = sr.bias[...]
    st["rows"] = [pl.multiple_of((step * bb + bi) * N_HEADS, N_HEADS) for bi in range(bb)]
    st["scores"] = []
    for bi in range(bb):
        newk = jnp.where(newest, sr.knew[pl.ds(step * bb + bi, 1), :], pltpu.roll(sr.kc[bi].T, WINDOW - 1, axis=0))
        sr.ko[bi] = newk.T
        qz = sr.qz[pl.ds(st["rows"][bi], N_HEADS), :].astype(BF16)
        st["scores"].append(_dot_nt(qz, newk.astype(BF16)) + bias)


def _sample_values(sr, step, bb, st):
    newest = lax.broadcasted_iota(jnp.int32, (WINDOW, LANES), 0) == WINDOW - 1
    sinkm = sr.sink[...]
    for bi, s in enumerate(st.pop("scores")):
        m = jnp.maximum(jnp.max(s, axis=1, keepdims=True), sinkm)
        e = jnp.exp(s - m)
        den = jnp.sum(e, axis=1, keepdims=True) + jnp.exp(sinkm - m)
        p = (e * (1.0 / den)).astype(BF16)
        newv = jnp.where(newest, sr.vnew[pl.ds(step * bb + bi, 1), :], pltpu.roll(sr.vc[bi].T, WINDOW - 1, axis=0))
        sr.vo[bi] = newv.T
        sr.o[pl.ds(st["rows"][bi], N_HEADS), :] = _dot(p, newv.astype(BF16))


def _sample_finish(sr, gpost_ref, wout_ref):
    nb = sr.x.shape[0]
    lane_lo = lax.broadcasted_iota(jnp.int32, (nb, LANES), 1) < HEAD_DIM
    cols = []
    for c in range(D_ATT // LANES):
        kvh = (2 * c) // GROUP
        halves = []
        for half in range(2):
            oh = sr.o[pl.ds(2 * c + half, nb, stride=N_HEADS), :]
            if half != kvh:
                oh = pltpu.roll(oh, HEAD_DIM, axis=1)
            halves.append(oh)
        cols.append(jnp.where(lane_lo, halves[0], halves[1]))
    att = jnp.concatenate(cols, axis=1)
    att_out = (att * _silu(sr.gatt[...])).astype(BF16)
    rnn_out = sr.rnnout[...].astype(BF16)
    out = _dot(rnn_out, wout_ref[0:D_RNN, :]) + _dot(att_out, wout_ref[D_RNN:, :])
    sr.y[:, 0, :] = sr.x[:, 0, :] + out * _rms_scale(out) * gpost_ref[...]


def kernel(x_prompt, x_sample, state_conv, state_rnn, cache_k_win, cache_v_win, norm_pre, norm_post,
           w_in, conv_w, conv_b, w_gate_a, b_gate_a, w_gate_x, b_gate_x, lru_lambda, attn_sinks,
           rel_bias, w_out):
    assert w_in.shape[0] == 1, "single-layer trunk"
    bsz, seq, _ = x_prompt.shape
    nb = x_sample.shape[0]
    wb = cache_k_win.shape[2]
    assert wb == WINDOW and x_sample.shape[1] == 1 and seq % PROMPT_BLOCK == 0 and nb % (seq // PROMPT_BLOCK) == 0
    assert bsz == PROMPT_SEQS, "the step program is written for this many prompt sequences"

    params = _prep_call(conv_w[0], conv_b, b_gate_a, b_gate_x, lru_lambda, w_in[0], w_gate_a[0], w_gate_x[0], w_out[0])
    gpre = norm_pre.reshape(1, D_MODEL)
    gpost = norm_post.reshape(1, D_MODEL)
    sinks = attn_sinks[0]
    bkt_np = _folded_bucket_table()
    bkt = jnp.asarray(np.ascontiguousarray(bkt_np.T))
    bktrow = jnp.asarray(bkt_np[WINDOW - 1:WINDOW, :])

    feature_major = lambda z: jnp.transpose(z[0], (0, 2, 3, 1)).reshape(nb, KV_DIM, wb)
    y_p, conv_p, rnn_p, kw_p, vw_p, y_s, conv_s, rnn_s, kw_s, vw_s = _layer_call(
        x_prompt, rel_bias, sinks, bkt, gpre, gpost, params,
        x_sample, bktrow, jnp.transpose(state_conv[0], (1, 0, 2)), state_rnn[0],
        feature_major(cache_k_win), feature_major(cache_v_win))

    kv5 = lambda z, n: jnp.transpose(z.reshape(n, N_KV_HEADS, HEAD_DIM, wb), (0, 3, 1, 2))[None]
    return (y_p, y_s,
            jnp.transpose(conv_p, (1, 0, 2))[None], rnn_p, kv5(kw_p, bsz), kv5(vw_p, bsz),
            jnp.transpose(conv_s, (1, 0, 2))[None], rnn_s[None], kv5(kw_s, nb), kv5(vw_s, nb))
```

```python
import collections
import functools
import math

import numpy as np
import jax
import jax.numpy as jnp
from jax import lax
from jax.experimental import pallas as pl
from jax.experimental.pallas import tpu as pltpu

D_MODEL = 1024
D_RNN = 512
D_ATT = 512
HEAD_DIM = 64
N_HEADS = 8
N_KV_HEADS = 2
GROUP = N_HEADS // N_KV_HEADS
KV_DIM = N_KV_HEADS * HEAD_DIM
N_RNN_BLOCKS = 8
RNN_BLOCK = D_RNN // N_RNN_BLOCKS
CONV_W = 4
LRU_C = 8.0
WINDOW = 128
N_BUCKETS = 32
MAX_DISTANCE = 128
EPS = 1e-6
NEG_INF = -1e30
D_IN = 2 * D_RNN + 2 * D_ATT + 2 * KV_DIM

C_XRNN = 0
C_GRNN = D_RNN
C_Q = 2 * D_RNN
C_KV = 2 * D_RNN + D_ATT
C_GATT = 2 * D_RNN + D_ATT + 2 * KV_DIM

SUBLANES = 8
LANES = 128
GATE_GROUP = 256
ONES_ROWS = 16
SCORE_LOOKAHEAD = 4
PROMPT_SEQS = 2
PROMPT_BLOCK = 512
PREP_ROWS = 256
VMEM_LIMIT = 56 * 1024 * 1024

TINY = 1e-37
F32 = jnp.float32
BF16 = jnp.bfloat16


def _spread(groups, bulk):
    out, done = [], 0
    for i, group in enumerate(groups):
        out += group
        want = (i + 1) * len(bulk) // len(groups)
        out += bulk[done:want]
        done = want
    return out


def _step_program(npairs):
    a, b = range(PROMPT_SEQS)
    n_out = D_MODEL // GATE_GROUP

    def attention(s):
        groups = [[("score", s, n) for n in range(SCORE_LOOKAHEAD)]]
        for n in range(npairs):
            ahead = [("score", s, n + SCORE_LOOKAHEAD)] if n + SCORE_LOOKAHEAD < npairs else []
            groups.append(ahead + [("value", s, n)])
        return groups

    project = lambda s: [("project", s, name) for name in ("x_rnn", "kv", "q")]
    gates = lambda s: [("gate_piece", s, i) for i in range(4)]
    outs = lambda s: [("out_piece", s, j) for j in range(n_out)]
    pa, pb = project(a), project(b)
    prog = [("sample_scores", a), ("norm", a), pa[0], ("norm", b), pa[1], pa[2], ("store_kv", a)]
    prog += [("sample_values", a), ("conv_gates", a)]
    prog += _spread(attention(a), [("gate_math", a)] + pb + gates(a) + [("scan", a)])
    prog += [("att_finish", a), ("conv_gates", b), ("store_kv", b)]
    prog += _spread(attention(b), [("gate_math", b)] + gates(b) + outs(a) + [("scan", b)])
    prog += [("finish", a), ("att_finish", b)] + outs(b) + [("finish", b)]
    return prog


def _t5_bucket_np(dist):
    dist = np.maximum(dist, 0)
    max_exact = N_BUCKETS // 2
    d = np.maximum(dist, 1).astype(np.float32)
    ratio = np.log(d / np.float32(max_exact)) / np.float32(math.log(MAX_DISTANCE / max_exact))
    large = max_exact + (ratio * np.float32(N_BUCKETS - max_exact)).astype(np.int32)
    large = np.minimum(large, N_BUCKETS - 1)
    return np.where(dist < max_exact, dist, large).astype(np.int32)


def _folded_bucket_table():
    i = np.arange(WINDOW)[:, None]
    j = np.arange(WINDOW)[None, :]
    dist = np.where(j <= i, i - j, i + WINDOW - j)
    return _t5_bucket_np(dist)


def _sigmoid_of_half(zh):
    return 0.5 + 0.5 * jnp.tanh(zh)


def _silu(x):
    h = 0.5 * x
    return h + h * jnp.tanh(h)


def _softplus(x):
    return jnp.maximum(x, 0.0) + jnp.log1p(jnp.exp(-jnp.abs(x)))


def _rms_scale(x):
    return lax.rsqrt(jnp.mean(x * x, axis=-1, keepdims=True) + EPS)


def _dot(a, b):
    return jnp.dot(a, b, preferred_element_type=F32)


def _dot_nt(a, b):
    return lax.dot_general(a, b, (((1,), (1,)), ((), ())), preferred_element_type=F32)


def _rglru_gate_dots(xc, wg_ref):
    xcb = xc.astype(BF16)
    zs = [_dot(xcb[:, g * GATE_GROUP:(g + 1) * GATE_GROUP], wg_ref[g])
          for g in range(D_RNN // GATE_GROUP)]
    za = jnp.concatenate([z[:, :GATE_GROUP] for z in zs], axis=1)
    zx = jnp.concatenate([z[:, GATE_GROUP:] for z in zs], axis=1)
    return za, zx


def _rglru_gate_math(xc, za, zx, rp):
    r = _sigmoid_of_half(za + rp[5:6])
    i = _sigmoid_of_half(zx + rp[6:7])
    log_a = (-LRU_C * r) * _softplus(-rp[7:8])
    a = jnp.exp(log_a)
    u = 1.0 - a * a
    bx = (u * lax.rsqrt(jnp.maximum(u, TINY))) * (i * xc)
    return a, bx


def _prepare_params(step, raw, rp_ref, win_ref, wg_ref, wout_ref):
    cw_raw, cb_raw, ba_raw, bx_raw, lam_raw, win_raw, wga_raw, wgx_raw, wout_raw = raw
    rows = pl.ds(pl.multiple_of(step * PREP_ROWS, PREP_ROWS), PREP_ROWS)
    for c0 in range(0, D_IN, GATE_GROUP):
        w = win_raw[:, c0:c0 + GATE_GROUP]
        if C_Q <= c0 < C_Q + D_ATT:
            w = w * HEAD_DIM ** -0.5
        win_ref[rows, c0:c0 + GATE_GROUP] = w.astype(BF16)
    wout_ref[rows, :] = wout_raw[...].astype(BF16)

    @pl.when(step == 0)
    def _():
        rp_ref[0:CONV_W, :] = cw_raw[...]
        rp_ref[4:5, :] = cb_raw[...]
        rp_ref[5:6, :] = 0.5 * ba_raw[...]
        rp_ref[6:7, :] = 0.5 * bx_raw[...]
        rp_ref[7:8, :] = lam_raw[...]
        per = GATE_GROUP // RNN_BLOCK
        zero = jnp.zeros((RNN_BLOCK, RNN_BLOCK), F32)
        for g in range(D_RNN // GATE_GROUP):
            for half, w_raw in enumerate((wga_raw, wgx_raw)):
                rows_g = [jnp.concatenate([0.5 * w_raw[g * per + m] if k == m else zero for k in range(per)], axis=1)
                          for m in range(per)]
                wg_ref[g, :, half * GATE_GROUP:(half + 1) * GATE_GROUP] = jnp.concatenate(rows_g, axis=0).astype(BF16)


def _build_bias_table(bkt, rb_ref, head):
    tbl = jnp.zeros(bkt.shape, F32)
    for k in range(N_BUCKETS):
        tbl = jnp.where(bkt == k, rb_ref[k, head], tbl)
    return tbl


_SampleRefs = collections.namedtuple(
    "_SampleRefs", "x bktrow sc h0 kc vc y conv rnn ko vo qz o knew vnew rnnout gatt bias sink")


def _layer_kernel(rb_ref, sinks_ref, x_ref, bkt_ref, gpre_ref, gpost_ref,
                  cw_raw, cb_raw, ba_raw, bx_raw, lam_raw, win_raw, wga_raw, wgx_raw, wout_raw,
                  xs_ref, bktrow_ref, sc_ref, h0_ref, kc_ref, vc_ref,
                  y_ref, conv_ref, rnn_ref, kw_ref, vw_ref, ys_ref, convs_ref, rnns_ref, ko_ref, vo_ref,
                  rp_ref, win_ref, wg_ref, wout_ref, *scratch, tb, nseq, bb, nprep):
    step = pl.program_id(0)

    @pl.when(step < nprep)
    def _():
        _prepare_params(step, (cw_raw, cb_raw, ba_raw, bx_raw, lam_raw, win_raw, wga_raw, wgx_raw, wout_raw),
                        rp_ref, win_ref, wg_ref, wout_ref)

    @pl.when(step >= nprep)
    def _():
        _layer_step(step - nprep, pl.num_programs(0) - nprep - 1,
                    rb_ref, sinks_ref, x_ref, bkt_ref, gpre_ref, gpost_ref, rp_ref, win_ref, wg_ref, wout_ref,
                    xs_ref, bktrow_ref, sc_ref, h0_ref, kc_ref, vc_ref,
                    y_ref, conv_ref, rnn_ref, kw_ref, vw_ref, ys_ref, convs_ref, rnns_ref, ko_ref, vo_ref,
                    *scratch, tb=tb, nseq=nseq, bb=bb)


def _layer_step(t, last_t, rb_ref, sinks_ref, x_ref, bkt_ref, gpre_ref, gpost_ref, rp_ref, win_ref, wg_ref, wout_ref,
                xs_ref, bktrow_ref, sc_ref, h0_ref, kc_ref, vc_ref,
                y_ref, conv_ref, rnn_ref, kw_ref, vw_ref, ys_ref, convs_ref, rnns_ref, ko_ref, vo_ref,
                bias_ref, xpad_ref, hc_ref, kd_ref, vt_ref, att_ref, xr_ref, hn_ref,
                qz_ref, o_ref, knew_ref, vnew_ref, rnnout_ref, gatt_ref, sbias_ref, ssink_ref, *, tb, nseq, bb):
    nq = tb // WINDOW
    sr = _SampleRefs(xs_ref, bktrow_ref, sc_ref, h0_ref, kc_ref, vc_ref, ys_ref, convs_ref, rnns_ref, ko_ref, vo_ref,
                     qz_ref, o_ref, knew_ref, vnew_ref, rnnout_ref, gatt_ref, sbias_ref, ssink_ref)

    @pl.when(t == 0)
    def _():
        bkt = bkt_ref[...]
        for h in range(N_HEADS):
            bias_ref[h] = _build_bias_table(bkt, rb_ref, h)
        xpad_ref[...] = jnp.zeros((nseq, (CONV_W - 1) * SUBLANES, D_RNN), F32)
        hc_ref[...] = jnp.zeros((nseq, 1, D_RNN), F32)
        kd_ref[:, :, 0:WINDOW, :] = jnp.zeros((nseq, N_KV_HEADS, WINDOW, LANES), BF16)
        vt_ref[:, :, 0:HEAD_DIM, 0:WINDOW] = jnp.zeros((nseq, N_KV_HEADS, HEAD_DIM, WINDOW), BF16)
        vt_ref[:, :, HEAD_DIM:, :] = jnp.ones((nseq, N_KV_HEADS, ONES_ROWS, tb + WINDOW), BF16)
        _sample_setup(sr, rb_ref, sinks_ref, gpre_ref, rp_ref, win_ref, wg_ref)

    rp = rp_ref[...]
    sub8 = lax.broadcasted_iota(jnp.int32, (SUBLANES, D_RNN), 0)
    lo = lax.broadcasted_iota(jnp.int32, (tb, LANES), 1) < HEAD_DIM
    key_idx = lax.broadcasted_iota(jnp.int32, (WINDOW, WINDOW), 0)
    qry_idx = lax.broadcasted_iota(jnp.int32, (WINDOW, WINDOW), 1)
    own = key_idx <= qry_idx
    lane_lo = qry_idx < HEAD_DIM
    has_prev0 = (jnp.zeros((WINDOW, WINDOW), jnp.int32) + t) > 0
    zero_b = jnp.zeros((WINDOW, WINDOW), BF16)
    pair_lo = lax.broadcasted_iota(jnp.int32, (1, 2 * WINDOW), 1) < WINDOW
    pairs = [(qi, c) for qi in range(nq) for c in range(D_ATT // LANES)]
    gate_cols = [C_GRNN, C_GRNN + GATE_GROUP, C_GATT, C_GATT + GATE_GROUP]

    def norm(s, st):
        x = x_ref[s]
        st.update(x=x, xn=(x * _rms_scale(x) * gpre_ref[...]).astype(BF16))

    def project(s, st, name):
        c0, width = dict(x_rnn=(C_XRNN, D_RNN), kv=(C_KV, 2 * KV_DIM), q=(C_Q, D_ATT))[name]
        st[name] = _dot(st["xn"], win_ref[:, c0:c0 + width])

    clen = tb // SUBLANES
    pitch = clen + SUBLANES
    ncol = D_RNN // LANES
    ntail = CONV_W - 1

    def conv_gates(s, st):
        x_rnn = st["x_rnn"]
        for c in range(ncol):
            for k in range(SUBLANES):
                xr_ref[s, c, k * pitch:k * pitch + clen, :] = x_rnn[k * clen:(k + 1) * clen, c * LANES:(c + 1) * LANES]
        xp = jnp.concatenate(
            [jnp.concatenate([xr_ref[s, c, pl.ds(v, SUBLANES, stride=pitch), :] for v in range(clen)], axis=0)
             for c in range(ncol)], axis=1)
        tail = xp[(clen - ntail) * SUBLANES:, :]
        prev_tail = xpad_ref[s]
        heads = [jnp.where(sub8 == 0,
                           pltpu.roll(prev_tail[j * SUBLANES:(j + 1) * SUBLANES], 1, axis=0),
                           pltpu.roll(tail[j * SUBLANES:(j + 1) * SUBLANES], 1, axis=0)) for j in range(ntail)]
        xc = rp[4:5]
        for tap in range(ntail):
            shift = ntail - tap
            xc = xc + rp[tap:tap + 1] * jnp.concatenate(heads[ntail - shift:] + [xp[:(clen - shift) * SUBLANES]], axis=0)
        xc = xc + rp[3:4] * xp
        xpad_ref[s] = tail
        for j in range(ntail):
            conv_ref[j, s:s + 1, :] = tail[j * SUBLANES + SUBLANES - 1:(j + 1) * SUBLANES]
        st["xc"] = xc
        st["za"], st["zx"] = _rglru_gate_dots(xc, wg_ref)

    def gate_math(s, st):
        st["a"], st["bx"] = _rglru_gate_math(st.pop("xc"), st.pop("za"), st.pop("zx"), rp)

    def store_kv(s, st):
        st["qb"] = st.pop("q").astype(BF16)
        k01 = st["kv"][:, :KV_DIM]
        v01 = st["kv"][:, KV_DIM:]
        k10 = pltpu.roll(k01, HEAD_DIM, axis=1)
        kd_ref[s, 0, WINDOW:WINDOW + tb, :] = jnp.where(lo, k01, k10).astype(BF16)
        kd_ref[s, 1, WINDOW:WINDOW + tb, :] = jnp.where(lo, k10, k01).astype(BF16)
        v_t = v01.T.astype(BF16)
        vt_ref[s, 0, 0:HEAD_DIM, WINDOW:WINDOW + tb] = v_t[0:HEAD_DIM]
        vt_ref[s, 1, 0:HEAD_DIM, WINDOW:WINDOW + tb] = v_t[HEAD_DIM:]
        kw_ref[s] = k01[tb - WINDOW:, :].T
        vw_ref[s] = v01[tb - WINDOW:, :].T

    def gate_piece(s, st, i):
        st[("gate", i)] = _dot(st["xn"], win_ref[:, gate_cols[i]:gate_cols[i] + GATE_GROUP])

    def scores(s, st, qi, c):
        rows = slice(qi * WINDOW, (qi + 1) * WINDOW)
        win = slice(qi * WINDOW, qi * WINDOW + 2 * WINDOW)
        kvh = (2 * c) // GROUP
        qc = st["qb"][rows, c * LANES:(c + 1) * LANES]
        qpair = jnp.concatenate([jnp.where(lane_lo, qc, zero_b), jnp.where(lane_lo, zero_b, qc)], axis=0)
        sc = _dot_nt(kd_ref[s, kvh, win, :], qpair)
        p_prev, p_own, ms = [], [], []
        for half in range(2):
            head = 2 * c + half
            cols = slice(half * WINDOW, (half + 1) * WINDOW)
            sf = jnp.where(own, sc[WINDOW:, cols], sc[:WINDOW, cols]) + bias_ref[head]
            if qi == 0:
                sf = jnp.where(own | has_prev0, sf, NEG_INF)
            m = jnp.max(sf, axis=0, keepdims=True)
            e = jnp.exp(sf - m).astype(BF16)
            p_prev.append(jnp.where(own, zero_b, e))
            p_own.append(jnp.where(own, e, zero_b))
            ms.append(m)
        p = jnp.concatenate([jnp.concatenate(p_prev, axis=1), jnp.concatenate(p_own, axis=1)], axis=0)
        return p, jnp.concatenate(ms, axis=1)

    def values(s, qi, c, p, m):
        rows = slice(qi * WINDOW, (qi + 1) * WINDOW)
        win = slice(qi * WINDOW, qi * WINDOW + 2 * WINDOW)
        kvh = (2 * c) // GROUP
        oa = _dot(vt_ref[s, kvh, :, win], p)
        sink = jnp.where(pair_lo, sinks_ref[2 * c], sinks_ref[2 * c + 1])
        den = oa[HEAD_DIM:HEAD_DIM + 1, :] + jnp.exp(sink - m)
        o = oa[0:HEAD_DIM, :] * (1.0 / den)
        o_t = jnp.concatenate([o[:, :WINDOW], o[:, WINDOW:]], axis=0)
        att_ref[s, rows, c * LANES:(c + 1) * LANES] = o_t.T

    def score(s, st, n):
        st[("p", n)] = scores(s, st, *pairs[n])

    def value(s, st, n):
        values(s, *pairs[n], *st.pop(("p", n)))

    def att_finish(s, st):
        kd_ref[s, :, 0:WINDOW, :] = kd_ref[s, :, tb:tb + WINDOW, :]
        vt_ref[s, :, :, 0:WINDOW] = vt_ref[s, :, :, tb:tb + WINDOW]
        g_att = jnp.concatenate([st[("gate", 2)], st[("gate", 3)]], axis=1)
        st["att_out"] = (att_ref[s] * _silu(g_att)).astype(BF16)

    def scan(s, st):
        a3 = st["a"].reshape(clen, SUBLANES, D_RNN)
        b3 = st["bx"].reshape(clen, SUBLANES, D_RNN)
        hl, pr = [b3[0]], [a3[0]]
        for v in range(1, clen):
            hl.append(a3[v] * hl[-1] + b3[v])
            pr.append(a3[v] * pr[-1])
        h_end, p_end = hl[-1], pr[-1]
        h_in = jnp.broadcast_to(hc_ref[s], (SUBLANES, D_RNN))
        entry = h_in
        for _ in range(SUBLANES - 1):
            entry = jnp.where(sub8 == 0, h_in, pltpu.roll(h_end + p_end * entry, 1, axis=0))
        carry = (h_end + p_end * entry)[SUBLANES - 1:SUBLANES, :]
        hc_ref[s] = carry
        rnn_ref[0, s:s + 1, :] = carry
        for v in range(clen):
            hv = hl[v] + pr[v] * entry
            for c in range(ncol):
                hn_ref[s, c, pl.ds(v, SUBLANES, stride=pitch), :] = hv[:, c * LANES:(c + 1) * LANES]
        h = jnp.concatenate(
            [jnp.concatenate([hn_ref[s, c, k * pitch:k * pitch + clen, :] for k in range(SUBLANES)], axis=0)
             for c in range(ncol)], axis=1)
        g_rnn = jnp.concatenate([st[("gate", 0)], st[("gate", 1)]], axis=1)
        st["rnn_out"] = (h * _silu(g_rnn)).astype(BF16)

    def out_piece(s, st, j):
        cols = slice(j * GATE_GROUP, (j + 1) * GATE_GROUP)
        st[("out", j)] = _dot(st["rnn_out"], wout_ref[0:D_RNN, cols]) + _dot(st["att_out"], wout_ref[D_RNN:, cols])

    def finish(s, st):
        out = jnp.concatenate([st[("out", j)] for j in range(D_MODEL // GATE_GROUP)], axis=1)
        y_ref[s] = st["x"] + out * _rms_scale(out) * gpost_ref[...]

    def sample_scores(s, st):
        _sample_scores(sr, t, bb, st)

    def sample_values(s, st):
        _sample_values(sr, t, bb, st)

    tasks = dict(sample_scores=sample_scores, sample_values=sample_values, norm=norm, project=project,
                 conv_gates=conv_gates, gate_math=gate_math, store_kv=store_kv, gate_piece=gate_piece,
                 score=score, value=value, att_finish=att_finish, scan=scan, out_piece=out_piece, finish=finish)
    sts = [dict() for _ in range(nseq)]
    for name, s, *arg in _step_program(len(pairs)):
        tasks[name](s, sts[s], *arg)

    @pl.when(t == last_t)
    def _():
        _sample_finish(sr, gpost_ref, wout_ref)


def _layer_call(x, rb, sinks, bkt, gpre, gpost, raw, xs, bktrow, sc, h0, kc_t, vc_t):
    bsz, seq, _ = x.shape
    tb = PROMPT_BLOCK
    nt = seq // tb
    nprep = D_MODEL // PREP_ROWS
    nb = xs.shape[0]
    bb = nb // nt
    layer_t = lambda t: jnp.maximum(t - nprep, 0)
    prep_t = lambda t: jnp.minimum(t, nprep - 1)
    win_t_spec = pl.BlockSpec((bb, KV_DIM, WINDOW), lambda t: (layer_t(t), 0, 0))
    x_spec = pl.BlockSpec((bsz, tb, D_MODEL), lambda t: (0, layer_t(t), 0))
    const = lambda *shape: pl.BlockSpec(shape, lambda t: (0,) * len(shape), pipeline_mode=pl.Buffered(1))
    whole = lambda *shape: pl.BlockSpec(shape, lambda t: (0,) * len(shape))
    smem = pl.BlockSpec(memory_space=pltpu.SMEM)
    out_shapes = (
        jax.ShapeDtypeStruct((bsz, seq, D_MODEL), F32),
        jax.ShapeDtypeStruct((CONV_W - 1, bsz, D_RNN), F32),
        jax.ShapeDtypeStruct((1, bsz, D_RNN), F32),
        jax.ShapeDtypeStruct((bsz, KV_DIM, WINDOW), F32),
        jax.ShapeDtypeStruct((bsz, KV_DIM, WINDOW), F32),
        jax.ShapeDtypeStruct((nb, 1, D_MODEL), F32),
        jax.ShapeDtypeStruct((CONV_W - 1, nb, D_RNN), F32),
        jax.ShapeDtypeStruct((nb, D_RNN), F32),
        jax.ShapeDtypeStruct((nb, KV_DIM, WINDOW), F32),
        jax.ShapeDtypeStruct((nb, KV_DIM, WINDOW), F32),
    )
    return pl.pallas_call(
        functools.partial(_layer_kernel, tb=tb, nseq=bsz, bb=bb, nprep=nprep),
        grid=(nprep + nt,),
        in_specs=[
            smem, smem,
            x_spec,
            const(WINDOW, WINDOW),
            const(1, D_MODEL), const(1, D_MODEL),
            const(CONV_W, D_RNN), const(1, D_RNN), const(1, D_RNN), const(1, D_RNN), const(1, D_RNN),
            pl.BlockSpec((PREP_ROWS, D_IN), lambda t: (prep_t(t), 0)),
            const(N_RNN_BLOCKS, RNN_BLOCK, RNN_BLOCK), const(N_RNN_BLOCKS, RNN_BLOCK, RNN_BLOCK),
            pl.BlockSpec((PREP_ROWS, D_MODEL), lambda t: (prep_t(t), 0)),
            const(nb, 1, D_MODEL), const(1, LANES), const(CONV_W - 1, nb, D_RNN), const(nb, D_RNN),
            win_t_spec, win_t_spec,
        ],
        out_specs=(
            x_spec,
            whole(CONV_W - 1, bsz, D_RNN),
            whole(1, bsz, D_RNN),
            whole(bsz, KV_DIM, WINDOW),
            whole(bsz, KV_DIM, WINDOW),
            whole(nb, 1, D_MODEL), whole(CONV_W - 1, nb, D_RNN), whole(nb, D_RNN),
            win_t_spec, win_t_spec,
        ),
        out_shape=out_shapes,
        scratch_shapes=[
            pltpu.VMEM((SUBLANES, D_RNN), F32),
            pltpu.VMEM((D_MODEL, D_IN), BF16),
            pltpu.VMEM((D_RNN // GATE_GROUP, GATE_GROUP, 2 * GATE_GROUP), BF16),
            pltpu.VMEM((D_MODEL, D_MODEL), BF16),
            pltpu.VMEM((N_HEADS, WINDOW, WINDOW), F32),
            pltpu.VMEM((bsz, (CONV_W - 1) * SUBLANES, D_RNN), F32),
            pltpu.VMEM((bsz, 1, D_RNN), F32),
            pltpu.VMEM((bsz, N_KV_HEADS, tb + WINDOW, LANES), BF16),
            pltpu.VMEM((bsz, N_KV_HEADS, HEAD_DIM + ONES_ROWS, tb + WINDOW), BF16),
            pltpu.VMEM((bsz, tb, D_ATT), F32),
            pltpu.VMEM((bsz, D_RNN // LANES, tb + SUBLANES * SUBLANES, LANES), F32),
            pltpu.VMEM((bsz, D_RNN // LANES, tb + SUBLANES * SUBLANES, LANES), F32),
            pltpu.VMEM((nb * N_HEADS, LANES), F32),
            pltpu.VMEM((nb * N_HEADS, LANES), F32),
            pltpu.VMEM((nb, KV_DIM), F32),
            pltpu.VMEM((nb, KV_DIM), F32),
            pltpu.VMEM((nb, D_RNN), F32),
            pltpu.VMEM((nb, D_ATT), F32),
            pltpu.VMEM((SUBLANES, LANES), F32),
            pltpu.VMEM((SUBLANES, LANES), F32),
        ],
        compiler_params=pltpu.CompilerParams(
            dimension_semantics=("arbitrary",),
            vmem_limit_bytes=VMEM_LIMIT),
        name="layer_step",
    )(rb, sinks, x, bkt, gpre, gpost, *raw, xs, bktrow, sc, h0, kc_t, vc_t)


def _sample_setup(sr, rb_ref, sinks_ref, gpre_ref, rp_ref, win_ref, wg_ref):
    nb = sr.x.shape[0]
    sub = lax.broadcasted_iota(jnp.int32, (SUBLANES, LANES), 0)
    lane_lo = lax.broadcasted_iota(jnp.int32, (nb, LANES), 1) < HEAD_DIM
    bkt = jnp.broadcast_to(sr.bktrow[...], (SUBLANES, LANES))
    bias = jnp.zeros((SUBLANES, LANES), F32)
    sinkm = jnp.zeros((SUBLANES, LANES), F32)
    for h in range(N_HEADS):
        bias = jnp.where(sub == h, _build_bias_table(bkt, rb_ref, h), bias)
        sinkm = jnp.where(sub == h, sinks_ref[h], sinkm)
    sr.bias[...] = bias
    sr.sink[...] = sinkm

    x = sr.x[:, 0, :]
    xn = (x * _rms_scale(x) * gpre_ref[...]).astype(BF16)
    rp = rp_ref[...]
    x_rnn = _dot(xn, win_ref[:, C_XRNN:C_XRNN + D_RNN])
    xc = rp[4:5] + rp[0:1] * sr.sc[0]
    xc = xc + rp[1:2] * sr.sc[1]
    xc = xc + rp[2:3] * sr.sc[2]
    xc = xc + rp[3:4] * x_rnn
    sr.conv[0] = sr.sc[1]
    sr.conv[1] = sr.sc[2]
    sr.conv[2] = x_rnn
    a, bx = _rglru_gate_math(xc, *_rglru_gate_dots(xc, wg_ref), rp)
    h = a * sr.h0[...] + bx
    sr.rnn[...] = h
    g_rnn = _dot(xn, win_ref[:, C_GRNN:C_GRNN + D_RNN])
    sr.rnnout[...] = h * _silu(g_rnn)
    sr.gatt[...] = _dot(xn, win_ref[:, C_GATT:C_GATT + D_ATT])
    kv = _dot(xn, win_ref[:, C_KV:C_KV + 2 * KV_DIM])
    sr.knew[...] = kv[:, :KV_DIM]
    sr.vnew[...] = kv[:, KV_DIM:]
    q = _dot(xn, win_ref[:, C_Q:C_Q + D_ATT])
    for head in range(N_HEADS):
        c = head // 2
        kvh = head // GROUP
        qc = q[:, c * LANES:(c + 1) * LANES]
        if (head % 2) != kvh:
            qc = pltpu.roll(qc, HEAD_DIM, axis=1)
        qz = jnp.where(lane_lo, qc, 0.0) if kvh == 0 else jnp.where(lane_lo, 0.0, qc)
        sr.qz[pl.ds(head, nb, stride=N_HEADS), :] = qz


def _sample_scores(sr, step, bb, st):
    newest = lax.broadcasted_iota(jnp.int32, (WINDOW, LANES), 0) == WINDOW - 1
    bias = sr.bias[...]
    st["rows"] = [pl.multiple_of((step * bb + bi) * N_HEADS, N_HEADS) for bi in range(bb)]
    st["scores"] = []
    for bi in range(bb):
        newk = jnp.where(newest, sr.knew[pl.ds(step * bb + bi, 1), :], pltpu.roll(sr.kc[bi].T, WINDOW - 1, axis=0))
        sr.ko[bi] = newk.T
        qz = sr.qz[pl.ds(st["rows"][bi], N_HEADS), :].astype(BF16)
        st["scores"].append(_dot_nt(qz, newk.astype(BF16)) + bias)


def _sample_values(sr, step, bb, st):
    newest = lax.broadcasted_iota(jnp.int32, (WINDOW, LANES), 0) == WINDOW - 1
    sinkm = sr.sink[...]
    for bi, s in enumerate(st.pop("scores")):
        m = jnp.maximum(jnp.max(s, axis=1, keepdims=True), sinkm)
        e = jnp.exp(s - m)
        den = jnp.sum(e, axis=1, keepdims=True) + jnp.exp(sinkm - m)
        p = (e * (1.0 / den)).astype(BF16)
        newv = jnp.where(newest, sr.vnew[pl.ds(step * bb + bi, 1), :], pltpu.roll(sr.vc[bi].T, WINDOW - 1, axis=0))
        sr.vo[bi] = newv.T
        sr.o[pl.ds(st["rows"][bi], N_HEADS), :] = _dot(p, newv.astype(BF16))


def _sample_finish(sr, gpost_ref, wout_ref):
    nb = sr.x.shape[0]
    lane_lo = lax.broadcasted_iota(jnp.int32, (nb, LANES), 1) < HEAD_DIM
    cols = []
    for c in range(D_ATT // LANES):
        kvh = (2 * c) // GROUP
        halves = []
        for half in range(2):
            oh = sr.o[pl.ds(2 * c + half, nb, stride=N_HEADS), :]
            if half != kvh:
                oh = pltpu.roll(oh, HEAD_DIM, axis=1)
            halves.append(oh)
        cols.append(jnp.where(lane_lo, halves[0], halves[1]))
    att = jnp.concatenate(cols, axis=1)
    att_out = (att * _silu(sr.gatt[...])).astype(BF16)
    rnn_out = sr.rnnout[...].astype(BF16)
    out = _dot(rnn_out, wout_ref[0:D_RNN, :]) + _dot(att_out, wout_ref[D_RNN:, :])
    sr.y[:, 0, :] = sr.x[:, 0, :] + out * _rms_scale(out) * gpost_ref[...]


def kernel(x_prompt, x_sample, state_conv, state_rnn, cache_k_win, cache_v_win, norm_pre, norm_post,
           w_in, conv_w, conv_b, w_gate_a, b_gate_a, w_gate_x, b_gate_x, lru_lambda, attn_sinks,
           rel_bias, w_out):
    assert w_in.shape[0] == 1, "single-layer trunk"
    bsz, seq, _ = x_prompt.shape
    nb = x_sample.shape[0]
    wb = cache_k_win.shape[2]
    assert wb == WINDOW and x_sample.shape[1] == 1 and seq % PROMPT_BLOCK == 0 and nb % (seq // PROMPT_BLOCK) == 0
    assert bsz == PROMPT_SEQS, "the step program is written for this many prompt sequences"

    raw = (conv_w[0], conv_b, b_gate_a, b_gate_x, lru_lambda, w_in[0], w_gate_a[0], w_gate_x[0], w_out[0])
    gpre = norm_pre.reshape(1, D_MODEL)
    gpost = norm_post.reshape(1, D_MODEL)
    sinks = attn_sinks[0]
    bkt_np = _folded_bucket_table()
    bkt = jnp.asarray(np.ascontiguousarray(bkt_np.T))
    bktrow = jnp.asarray(bkt_np[WINDOW - 1:WINDOW, :])

    feature_major = lambda z: jnp.transpose(z[0], (0, 2, 3, 1)).reshape(nb, KV_DIM, wb)
    y_p, conv_p, rnn_p, kw_p, vw_p, y_s, conv_s, rnn_s, kw_s, vw_s = _layer_call(
        x_prompt, rel_bias, sinks, bkt, gpre, gpost, raw,
        x_sample, bktrow, jnp.transpose(state_conv[0], (1, 0, 2)), state_rnn[0],
        feature_major(cache_k_win), feature_major(cache_v_win))

    kv5 = lambda z, n: jnp.transpose(z.reshape(n, N_KV_HEADS, HEAD_DIM, wb), (0, 3, 1, 2))[None]
    return (y_p, y_s,
            jnp.transpose(conv_p, (1, 0, 2))[None], rnn_p, kv5(kw_p, bsz), kv5(vw_p, bsz),
            jnp.transpose(conv_s, (1, 0, 2))[None], rnn_s[None], kv5(kw_s, nb), kv5(vw_s, nb))
```

```python
import collections
import functools
import math

import numpy as np
import jax
import jax.numpy as jnp
from jax import lax
from jax.experimental import pallas as pl
from jax.experimental.pallas import tpu as pltpu

D_MODEL = 1024
D_RNN = 512
D_ATT = 512
HEAD_DIM = 64
N_HEADS = 8
N_KV_HEADS = 2
GROUP = N_HEADS // N_KV_HEADS
KV_DIM = N_KV_HEADS * HEAD_DIM
N_RNN_BLOCKS = 8
RNN_BLOCK = D_RNN // N_RNN_BLOCKS
CONV_W = 4
LRU_C = 8.0
WINDOW = 128
N_BUCKETS = 32
MAX_DISTANCE = 128
EPS = 1e-6
NEG_INF = -1e30
D_IN = 2 * D_RNN + 2 * D_ATT + 2 * KV_DIM

C_XRNN = 0
C_GRNN = D_RNN
C_Q = 2 * D_RNN
C_KV = 2 * D_RNN + D_ATT
C_GATT = 2 * D_RNN + D_ATT + 2 * KV_DIM

SUBLANES = 8
LANES = 128
GATE_GROUP = 256
ONES_ROWS = 16
SCORE_LOOKAHEAD = 4
PROMPT_SEQS = 2
PROMPT_BLOCK = 512
PREP_ROWS = 256
VMEM_LIMIT = 56 * 1024 * 1024

TINY = 1e-37
LOG2E = 1.4426950408889634
F32 = jnp.float32
BF16 = jnp.bfloat16


def _spread(groups, bulk):
    out, done = [], 0
    for i, group in enumerate(groups):
        out += group
        want = (i + 1) * len(bulk) // len(groups)
        out += bulk[done:want]
        done = want
    return out


def _step_program(npairs):
    a, b = range(PROMPT_SEQS)
    n_out = D_MODEL // GATE_GROUP

    def attention(s):
        groups = [[("score", s, n) for n in range(SCORE_LOOKAHEAD)]]
        for n in range(npairs):
            ahead = [("score", s, n + SCORE_LOOKAHEAD)] if n + SCORE_LOOKAHEAD < npairs else []
            groups.append(ahead + [("value", s, n)])
        return groups

    project = lambda s: [("project", s, name) for name in ("x_rnn", "kv", "q")]
    gates = lambda s: [("gate_piece", s, i) for i in range(4)]
    outs = lambda s: [("out_piece", s, j) for j in range(n_out)]
    pa, pb = project(a), project(b)
    prog = [("sample_scores", a), ("norm", a), pa[0], ("norm", b), pa[1], pa[2], ("store_kv", a)]
    prog += [("conv_gates", a)]
    prog += _spread(attention(a), [("gate_math", a)] + pb + gates(a) + [("scan", a)])
    prog += [("att_finish", a), ("conv_gates", b), ("store_kv", b)]
    prog += _spread(attention(b), [("gate_math", b)] + gates(b) + outs(a) + [("scan", b)])
    prog += [("finish", a), ("att_finish", b)] + outs(b) + [("sample_values", a), ("finish", b)]
    return prog


def _t5_bucket_np(dist):
    dist = np.maximum(dist, 0)
    max_exact = N_BUCKETS // 2
    d = np.maximum(dist, 1).astype(np.float32)
    ratio = np.log(d / np.float32(max_exact)) / np.float32(math.log(MAX_DISTANCE / max_exact))
    large = max_exact + (ratio * np.float32(N_BUCKETS - max_exact)).astype(np.int32)
    large = np.minimum(large, N_BUCKETS - 1)
    return np.where(dist < max_exact, dist, large).astype(np.int32)


def _folded_bucket_table():
    i = np.arange(WINDOW)[:, None]
    j = np.arange(WINDOW)[None, :]
    dist = np.where(j <= i, i - j, i + WINDOW - j)
    return _t5_bucket_np(dist)


def _sigmoid_of_half(zh):
    return 0.5 + 0.5 * jnp.tanh(zh)


def _silu(x):
    h = 0.5 * x
    return h + h * jnp.tanh(h)


def _softplus(x):
    return jnp.maximum(x, 0.0) + jnp.log1p(jnp.exp(-jnp.abs(x)))


def _rms_scale(x):
    return lax.rsqrt(jnp.mean(x * x, axis=-1, keepdims=True) + EPS)


def _dot(a, b):
    return jnp.dot(a, b, preferred_element_type=F32)


def _dot_nt(a, b):
    return lax.dot_general(a, b, (((1,), (1,)), ((), ())), preferred_element_type=F32)


def _rglru_gate_dots(xc, wg_ref):
    xcb = xc.astype(BF16)
    zs = [_dot(xcb[:, g * GATE_GROUP:(g + 1) * GATE_GROUP], wg_ref[g])
          for g in range(D_RNN // GATE_GROUP)]
    za = jnp.concatenate([z[:, :GATE_GROUP] for z in zs], axis=1)
    zx = jnp.concatenate([z[:, GATE_GROUP:] for z in zs], axis=1)
    return za, zx


def _rglru_gate_math(xc, za, zx, rp):
    i = _sigmoid_of_half(zx + rp[6:7])
    rate = (-0.5 * LRU_C * LOG2E) * _softplus(-rp[7:8])
    a = jnp.exp2((1.0 + jnp.tanh(za + rp[5:6])) * rate)
    u = 1.0 - a * a
    bx = (u * lax.rsqrt(jnp.maximum(u, TINY))) * (i * xc)
    return a, bx


def _prepare_params(step, raw, rp_ref, win_ref, wg_ref, wout_ref):
    cw_raw, cb_raw, ba_raw, bx_raw, lam_raw, win_raw, wga_raw, wgx_raw, wout_raw = raw
    rows = pl.ds(pl.multiple_of(step * PREP_ROWS, PREP_ROWS), PREP_ROWS)
    for c0 in range(0, D_IN, GATE_GROUP):
        w = win_raw[:, c0:c0 + GATE_GROUP]
        if C_Q <= c0 < C_Q + D_ATT:
            w = w * HEAD_DIM ** -0.5
        win_ref[rows, c0:c0 + GATE_GROUP] = w.astype(BF16)
    wout_ref[rows, :] = wout_raw[...].astype(BF16)

    @pl.when(step == 0)
    def _():
        rp_ref[0:CONV_W, :] = cw_raw[...]
        rp_ref[4:5, :] = cb_raw[...]
        rp_ref[5:6, :] = 0.5 * ba_raw[...]
        rp_ref[6:7, :] = 0.5 * bx_raw[...]
        rp_ref[7:8, :] = lam_raw[...]
        per = GATE_GROUP // RNN_BLOCK
        zero = jnp.zeros((RNN_BLOCK, RNN_BLOCK), F32)
        for g in range(D_RNN // GATE_GROUP):
            for half, w_raw in enumerate((wga_raw, wgx_raw)):
                rows_g = [jnp.concatenate([0.5 * w_raw[g * per + m] if k == m else zero for k in range(per)], axis=1)
                          for m in range(per)]
                wg_ref[g, :, half * GATE_GROUP:(half + 1) * GATE_GROUP] = jnp.concatenate(rows_g, axis=0).astype(BF16)


def _build_bias_table(bkt, rb_ref, head):
    tbl = jnp.zeros(bkt.shape, F32)
    for k in range(N_BUCKETS):
        tbl = jnp.where(bkt == k, rb_ref[k, head], tbl)
    return tbl


_SampleRefs = collections.namedtuple(
    "_SampleRefs", "x bktrow sc h0 kc vc y conv rnn ko vo qz o knew vnew rnnout gatt bias sink")


def _layer_kernel(rb_ref, sinks_ref, x_ref, bkt_ref, gpre_ref, gpost_ref,
                  cw_raw, cb_raw, ba_raw, bx_raw, lam_raw, win_raw, wga_raw, wgx_raw, wout_raw,
                  xs_ref, bktrow_ref, sc_ref, h0_ref, kc_ref, vc_ref,
                  y_ref, conv_ref, rnn_ref, kw_ref, vw_ref, ys_ref, convs_ref, rnns_ref, ko_ref, vo_ref,
                  rp_ref, win_ref, wg_ref, wout_ref, *scratch, tb, nseq, bb, nprep):
    step = pl.program_id(0)

    @pl.when(step < nprep)
    def _():
        _prepare_params(step, (cw_raw, cb_raw, ba_raw, bx_raw, lam_raw, win_raw, wga_raw, wgx_raw, wout_raw),
                        rp_ref, win_ref, wg_ref, wout_ref)

    @pl.when(step >= nprep)
    def _():
        _layer_step(step - nprep, pl.num_programs(0) - nprep - 1,
                    rb_ref, sinks_ref, x_ref, bkt_ref, gpre_ref, gpost_ref, rp_ref, win_ref, wg_ref, wout_ref,
                    xs_ref, bktrow_ref, sc_ref, h0_ref, kc_ref, vc_ref,
                    y_ref, conv_ref, rnn_ref, kw_ref, vw_ref, ys_ref, convs_ref, rnns_ref, ko_ref, vo_ref,
                    *scratch, tb=tb, nseq=nseq, bb=bb)


def _layer_step(t, last_t, rb_ref, sinks_ref, x_ref, bkt_ref, gpre_ref, gpost_ref, rp_ref, win_ref, wg_ref, wout_ref,
                xs_ref, bktrow_ref, sc_ref, h0_ref, kc_ref, vc_ref,
                y_ref, conv_ref, rnn_ref, kw_ref, vw_ref, ys_ref, convs_ref, rnns_ref, ko_ref, vo_ref,
                bias_ref, xpad_ref, hc_ref, kd_ref, vt_ref, att_ref, xr_ref, hn_ref,
                qz_ref, o_ref, knew_ref, vnew_ref, rnnout_ref, gatt_ref, sbias_ref, ssink_ref, *, tb, nseq, bb):
    nq = tb // WINDOW
    sr = _SampleRefs(xs_ref, bktrow_ref, sc_ref, h0_ref, kc_ref, vc_ref, ys_ref, convs_ref, rnns_ref, ko_ref, vo_ref,
                     qz_ref, o_ref, knew_ref, vnew_ref, rnnout_ref, gatt_ref, sbias_ref, ssink_ref)

    @pl.when(t == 0)
    def _():
        bkt = bkt_ref[...]
        for h in range(N_HEADS):
            bias_ref[h] = _build_bias_table(bkt, rb_ref, h)
        xpad_ref[...] = jnp.zeros((nseq, (CONV_W - 1) * SUBLANES, D_RNN), F32)
        hc_ref[...] = jnp.zeros((nseq, 1, D_RNN), F32)
        kd_ref[:, :, 0:WINDOW, :] = jnp.zeros((nseq, N_KV_HEADS, WINDOW, LANES), BF16)
        vt_ref[:, :, 0:HEAD_DIM, 0:WINDOW] = jnp.zeros((nseq, N_KV_HEADS, HEAD_DIM, WINDOW), BF16)
        vt_ref[:, :, HEAD_DIM:, :] = jnp.ones((nseq, N_KV_HEADS, ONES_ROWS, tb + WINDOW), BF16)
        _sample_setup(sr, rb_ref, sinks_ref, gpre_ref, rp_ref, win_ref, wg_ref)

    rp = rp_ref[...]
    sub8 = lax.broadcasted_iota(jnp.int32, (SUBLANES, D_RNN), 0)
    lo = lax.broadcasted_iota(jnp.int32, (tb, LANES), 1) < HEAD_DIM
    key_idx = lax.broadcasted_iota(jnp.int32, (WINDOW, WINDOW), 0)
    qry_idx = lax.broadcasted_iota(jnp.int32, (WINDOW, WINDOW), 1)
    own = key_idx <= qry_idx
    lane_lo = qry_idx < HEAD_DIM
    has_prev0 = (jnp.zeros((WINDOW, WINDOW), jnp.int32) + t) > 0
    zero_b = jnp.zeros((WINDOW, WINDOW), BF16)
    pair_lo = lax.broadcasted_iota(jnp.int32, (1, 2 * WINDOW), 1) < WINDOW
    pairs = [(qi, c) for qi in range(nq) for c in range(D_ATT // LANES)]
    gate_cols = [C_GRNN, C_GRNN + GATE_GROUP, C_GATT, C_GATT + GATE_GROUP]

    def norm(s, st):
        x = x_ref[s]
        st["xn"] = (x * _rms_scale(x) * gpre_ref[...]).astype(BF16)

    def project(s, st, name):
        c0, width = dict(x_rnn=(C_XRNN, D_RNN), kv=(C_KV, 2 * KV_DIM), q=(C_Q, D_ATT))[name]
        st[name] = _dot(st["xn"], win_ref[:, c0:c0 + width])

    clen = tb // SUBLANES
    pitch = clen + SUBLANES
    ncol = D_RNN // LANES
    ntail = CONV_W - 1

    def conv_gates(s, st):
        x_rnn = st["x_rnn"]
        for c in range(ncol):
            for k in range(SUBLANES):
                xr_ref[s, c, k * pitch:k * pitch + clen, :] = x_rnn[k * clen:(k + 1) * clen, c * LANES:(c + 1) * LANES]
        xp = jnp.concatenate(
            [jnp.concatenate([xr_ref[s, c, pl.ds(v, SUBLANES, stride=pitch), :] for v in range(clen)], axis=0)
             for c in range(ncol)], axis=1)
        tail = xp[(clen - ntail) * SUBLANES:, :]
        prev_tail = xpad_ref[s]
        heads = [jnp.where(sub8 == 0,
                           pltpu.roll(prev_tail[j * SUBLANES:(j + 1) * SUBLANES], 1, axis=0),
                           pltpu.roll(tail[j * SUBLANES:(j + 1) * SUBLANES], 1, axis=0)) for j in range(ntail)]
        xc = rp[4:5]
        for tap in range(ntail):
            shift = ntail - tap
            xc = xc + rp[tap:tap + 1] * jnp.concatenate(heads[ntail - shift:] + [xp[:(clen - shift) * SUBLANES]], axis=0)
        xc = xc + rp[3:4] * xp
        xpad_ref[s] = tail
        for j in range(ntail):
            conv_ref[j, s:s + 1, :] = tail[j * SUBLANES + SUBLANES - 1:(j + 1) * SUBLANES]
        st["xc"] = xc
        st["za"], st["zx"] = _rglru_gate_dots(xc, wg_ref)

    def gate_math(s, st):
        st["a"], st["bx"] = _rglru_gate_math(st.pop("xc"), st.pop("za"), st.pop("zx"), rp)

    def store_kv(s, st):
        st["qb"] = st.pop("q").astype(BF16)
        k01 = st["kv"][:, :KV_DIM]
        v01 = st["kv"][:, KV_DIM:]
        k10 = pltpu.roll(k01, HEAD_DIM, axis=1)
        kd_ref[s, 0, WINDOW:WINDOW + tb, :] = jnp.where(lo, k01, k10).astype(BF16)
        kd_ref[s, 1, WINDOW:WINDOW + tb, :] = jnp.where(lo, k10, k01).astype(BF16)
        v_t = v01.T.astype(BF16)
        vt_ref[s, 0, 0:HEAD_DIM, WINDOW:WINDOW + tb] = v_t[0:HEAD_DIM]
        vt_ref[s, 1, 0:HEAD_DIM, WINDOW:WINDOW + tb] = v_t[HEAD_DIM:]
        kw_ref[s] = k01[tb - WINDOW:, :].T
        vw_ref[s] = v01[tb - WINDOW:, :].T

    def gate_piece(s, st, i):
        st[("gate", i)] = _dot(st["xn"], win_ref[:, gate_cols[i]:gate_cols[i] + GATE_GROUP])

    def scores(s, st, qi, c):
        rows = slice(qi * WINDOW, (qi + 1) * WINDOW)
        win = slice(qi * WINDOW, qi * WINDOW + 2 * WINDOW)
        kvh = (2 * c) // GROUP
        qc = st["qb"][rows, c * LANES:(c + 1) * LANES]
        qpair = jnp.concatenate([jnp.where(lane_lo, qc, zero_b), jnp.where(lane_lo, zero_b, qc)], axis=0)
        sc = _dot_nt(kd_ref[s, kvh, win, :], qpair)
        p_prev, p_own, ms = [], [], []
        for half in range(2):
            head = 2 * c + half
            cols = slice(half * WINDOW, (half + 1) * WINDOW)
            sf = jnp.where(own, sc[WINDOW:, cols], sc[:WINDOW, cols]) + bias_ref[head]
            if qi == 0:
                sf = jnp.where(own | has_prev0, sf, NEG_INF)
            m = jnp.max(sf, axis=0, keepdims=True)
            e = jnp.exp(sf - m).astype(BF16)
            p_prev.append(jnp.where(own, zero_b, e))
            p_own.append(jnp.where(own, e, zero_b))
            ms.append(m)
        p = jnp.concatenate([jnp.concatenate(p_prev, axis=1), jnp.concatenate(p_own, axis=1)], axis=0)
        return p, jnp.concatenate(ms, axis=1)

    def values(s, qi, c, p, m):
        rows = slice(qi * WINDOW, (qi + 1) * WINDOW)
        win = slice(qi * WINDOW, qi * WINDOW + 2 * WINDOW)
        kvh = (2 * c) // GROUP
        oa = _dot(vt_ref[s, kvh, :, win], p)
        sink = jnp.where(pair_lo, sinks_ref[2 * c], sinks_ref[2 * c + 1])
        den = oa[HEAD_DIM:HEAD_DIM + 1, :] + jnp.exp(sink - m)
        o = oa[0:HEAD_DIM, :] * (1.0 / den)
        o_t = jnp.concatenate([o[:, :WINDOW], o[:, WINDOW:]], axis=0)
        att_ref[s, rows, c * LANES:(c + 1) * LANES] = o_t.T

    def score(s, st, n):
        st[("p", n)] = scores(s, st, *pairs[n])

    def value(s, st, n):
        values(s, *pairs[n], *st.pop(("p", n)))

    def att_finish(s, st):
        kd_ref[s, :, 0:WINDOW, :] = kd_ref[s, :, tb:tb + WINDOW, :]
        vt_ref[s, :, :, 0:WINDOW] = vt_ref[s, :, :, tb:tb + WINDOW]
        g_att = jnp.concatenate([st[("gate", 2)], st[("gate", 3)]], axis=1)
        st["att_out"] = (att_ref[s] * _silu(g_att)).astype(BF16)

    def scan(s, st):
        a3 = st["a"].reshape(clen, SUBLANES, D_RNN)
        b3 = st["bx"].reshape(clen, SUBLANES, D_RNN)
        hl, pr = [b3[0]], [a3[0]]
        for v in range(1, clen):
            hl.append(a3[v] * hl[-1] + b3[v])
            pr.append(a3[v] * pr[-1])
        h_end, p_end = hl[-1], pr[-1]
        h_in = jnp.broadcast_to(hc_ref[s], (SUBLANES, D_RNN))
        entry = h_in
        for _ in range(SUBLANES - 1):
            entry = jnp.where(sub8 == 0, h_in, pltpu.roll(h_end + p_end * entry, 1, axis=0))
        carry = (h_end + p_end * entry)[SUBLANES - 1:SUBLANES, :]
        hc_ref[s] = carry
        rnn_ref[0, s:s + 1, :] = carry
        for v in range(clen):
            hv = hl[v] + pr[v] * entry
            for c in range(ncol):
                hn_ref[s, c, pl.ds(v, SUBLANES, stride=pitch), :] = hv[:, c * LANES:(c + 1) * LANES]
        h = jnp.concatenate(
            [jnp.concatenate([hn_ref[s, c, k * pitch:k * pitch + clen, :] for k in range(SUBLANES)], axis=0)
             for c in range(ncol)], axis=1)
        g_rnn = jnp.concatenate([st[("gate", 0)], st[("gate", 1)]], axis=1)
        st["rnn_out"] = (h * _silu(g_rnn)).astype(BF16)

    def out_piece(s, st, j):
        cols = slice(j * GATE_GROUP, (j + 1) * GATE_GROUP)
        st[("out", j)] = _dot(st["rnn_out"], wout_ref[0:D_RNN, cols]) + _dot(st["att_out"], wout_ref[D_RNN:, cols])

    def finish(s, st):
        out = jnp.concatenate([st[("out", j)] for j in range(D_MODEL // GATE_GROUP)], axis=1)
        y_ref[s] = x_ref[s] + out * _rms_scale(out) * gpost_ref[...]

    def sample_scores(s, st):
        _sample_scores(sr, t, bb, st)

    def sample_values(s, st):
        _sample_values(sr, t, bb, st)

    tasks = dict(sample_scores=sample_scores, sample_values=sample_values, norm=norm, project=project,
                 conv_gates=conv_gates, gate_math=gate_math, store_kv=store_kv, gate_piece=gate_piece,
                 score=score, value=value, att_finish=att_finish, scan=scan, out_piece=out_piece, finish=finish)
    sts = [dict() for _ in range(nseq)]
    for name, s, *arg in _step_program(len(pairs)):
        tasks[name](s, sts[s], *arg)

    @pl.when(t == last_t)
    def _():
        _sample_finish(sr, gpost_ref, wout_ref)


def _layer_call(x, rb, sinks, bkt, gpre, gpost, raw, xs, bktrow, sc, h0, kc_t, vc_t):
    bsz, seq, _ = x.shape
    tb = PROMPT_BLOCK
    nt = seq // tb
    nprep = D_MODEL // PREP_ROWS
    nb = xs.shape[0]
    bb = nb // nt
    layer_t = lambda t: jnp.maximum(t - nprep, 0)
    prep_t = lambda t: jnp.minimum(t, nprep - 1)
    win_t_spec = pl.BlockSpec((bb, KV_DIM, WINDOW), lambda t: (layer_t(t), 0, 0))
    x_spec = pl.BlockSpec((bsz, tb, D_MODEL), lambda t: (0, layer_t(t), 0))
    const = lambda *shape: pl.BlockSpec(shape, lambda t: (0,) * len(shape), pipeline_mode=pl.Buffered(1))
    whole = lambda *shape: pl.BlockSpec(shape, lambda t: (0,) * len(shape))
    smem = pl.BlockSpec(memory_space=pltpu.SMEM)
    out_shapes = (
        jax.ShapeDtypeStruct((bsz, seq, D_MODEL), F32),
        jax.ShapeDtypeStruct((CONV_W - 1, bsz, D_RNN), F32),
        jax.ShapeDtypeStruct((1, bsz, D_RNN), F32),
        jax.ShapeDtypeStruct((bsz, KV_DIM, WINDOW), F32),
        jax.ShapeDtypeStruct((bsz, KV_DIM, WINDOW), F32),
        jax.ShapeDtypeStruct((nb, 1, D_MODEL), F32),
        jax.ShapeDtypeStruct((CONV_W - 1, nb, D_RNN), F32),
        jax.ShapeDtypeStruct((nb, D_RNN), F32),
        jax.ShapeDtypeStruct((nb, KV_DIM, WINDOW), F32),
        jax.ShapeDtypeStruct((nb, KV_DIM, WINDOW), F32),
    )
    return pl.pallas_call(
        functools.partial(_layer_kernel, tb=tb, nseq=bsz, bb=bb, nprep=nprep),
        grid=(nprep + nt,),
        in_specs=[
            smem, smem,
            x_spec,
            const(WINDOW, WINDOW),
            const(1, D_MODEL), const(1, D_MODEL),
            const(CONV_W, D_RNN), const(1, D_RNN), const(1, D_RNN), const(1, D_RNN), const(1, D_RNN),
            pl.BlockSpec((PREP_ROWS, D_IN), lambda t: (prep_t(t), 0)),
            const(N_RNN_BLOCKS, RNN_BLOCK, RNN_BLOCK), const(N_RNN_BLOCKS, RNN_BLOCK, RNN_BLOCK),
            pl.BlockSpec((PREP_ROWS, D_MODEL), lambda t: (prep_t(t), 0)),
            const(nb, 1, D_MODEL), const(1, LANES), const(CONV_W - 1, nb, D_RNN), const(nb, D_RNN),
            win_t_spec, win_t_spec,
        ],
        out_specs=(
            x_spec,
            whole(CONV_W - 1, bsz, D_RNN),
            whole(1, bsz, D_RNN),
            whole(bsz, KV_DIM, WINDOW),
            whole(bsz, KV_DIM, WINDOW),
            whole(nb, 1, D_MODEL), whole(CONV_W - 1, nb, D_RNN), whole(nb, D_RNN),
            win_t_spec, win_t_spec,
        ),
        out_shape=out_shapes,
        scratch_shapes=[
            pltpu.VMEM((SUBLANES, D_RNN), F32),
            pltpu.VMEM((D_MODEL, D_IN), BF16),
            pltpu.VMEM((D_RNN // GATE_GROUP, GATE_GROUP, 2 * GATE_GROUP), BF16),
            pltpu.VMEM((D_MODEL, D_MODEL), BF16),
            pltpu.VMEM((N_HEADS, WINDOW, WINDOW), F32),
            pltpu.VMEM((bsz, (CONV_W - 1) * SUBLANES, D_RNN), F32),
            pltpu.VMEM((bsz, 1, D_RNN), F32),
            pltpu.VMEM((bsz, N_KV_HEADS, tb + WINDOW, LANES), BF16),
            pltpu.VMEM((bsz, N_KV_HEADS, HEAD_DIM + ONES_ROWS, tb + WINDOW), BF16),
            pltpu.VMEM((bsz, tb, D_ATT), F32),
            pltpu.VMEM((bsz, D_RNN // LANES, tb + SUBLANES * SUBLANES, LANES), F32),
            pltpu.VMEM((bsz, D_RNN // LANES, tb + SUBLANES * SUBLANES, LANES), F32),
            pltpu.VMEM((nb * N_HEADS, LANES), F32),
            pltpu.VMEM((nb * N_HEADS, LANES), F32),
            pltpu.VMEM((nb, KV_DIM), F32),
            pltpu.VMEM((nb, KV_DIM), F32),
            pltpu.VMEM((nb, D_RNN), F32),
            pltpu.VMEM((nb, D_ATT), F32),
            pltpu.VMEM((SUBLANES, LANES), F32),
            pltpu.VMEM((SUBLANES, LANES), F32),
        ],
        compiler_params=pltpu.CompilerParams(
            dimension_semantics=("arbitrary",),
            vmem_limit_bytes=VMEM_LIMIT),
        name="layer_step",
    )(rb, sinks, x, bkt, gpre, gpost, *raw, xs, bktrow, sc, h0, kc_t, vc_t)


def _sample_setup(sr, rb_ref, sinks_ref, gpre_ref, rp_ref, win_ref, wg_ref):
    nb = sr.x.shape[0]
    sub = lax.broadcasted_iota(jnp.int32, (SUBLANES, LANES), 0)
    lane_lo = lax.broadcasted_iota(jnp.int32, (nb, LANES), 1) < HEAD_DIM
    bkt = jnp.broadcast_to(sr.bktrow[...], (SUBLANES, LANES))
    bias = jnp.zeros((SUBLANES, LANES), F32)
    sinkm = jnp.zeros((SUBLANES, LANES), F32)
    for h in range(N_HEADS):
        bias = jnp.where(sub == h, _build_bias_table(bkt, rb_ref, h), bias)
        sinkm = jnp.where(sub == h, sinks_ref[h], sinkm)
    sr.bias[...] = bias
    sr.sink[...] = sinkm

    x = sr.x[:, 0, :]
    xn = (x * _rms_scale(x) * gpre_ref[...]).astype(BF16)
    rp = rp_ref[...]
    x_rnn = _dot(xn, win_ref[:, C_XRNN:C_XRNN + D_RNN])
    xc = rp[4:5] + rp[0:1] * sr.sc[0]
    xc = xc + rp[1:2] * sr.sc[1]
    xc = xc + rp[2:3] * sr.sc[2]
    xc = xc + rp[3:4] * x_rnn
    sr.conv[0] = sr.sc[1]
    sr.conv[1] = sr.sc[2]
    sr.conv[2] = x_rnn
    a, bx = _rglru_gate_math(xc, *_rglru_gate_dots(xc, wg_ref), rp)
    h = a * sr.h0[...] + bx
    sr.rnn[...] = h
    g_rnn = _dot(xn, win_ref[:, C_GRNN:C_GRNN + D_RNN])
    sr.rnnout[...] = h * _silu(g_rnn)
    sr.gatt[...] = _dot(xn, win_ref[:, C_GATT:C_GATT + D_ATT])
    kv = _dot(xn, win_ref[:, C_KV:C_KV + 2 * KV_DIM])
    sr.knew[...] = kv[:, :KV_DIM]
    sr.vnew[...] = kv[:, KV_DIM:]
    q = _dot(xn, win_ref[:, C_Q:C_Q + D_ATT])
    for head in range(N_HEADS):
        c = head // 2
        kvh = head // GROUP
        qc = q[:, c * LANES:(c + 1) * LANES]
        if (head % 2) != kvh:
            qc = pltpu.roll(qc, HEAD_DIM, axis=1)
        qz = jnp.where(lane_lo, qc, 0.0) if kvh == 0 else jnp.where(lane_lo, 0.0, qc)
        sr.qz[pl.ds(head, nb, stride=N_HEADS), :] = qz


def _sample_scores(sr, step, bb, st):
    newest = lax.broadcasted_iota(jnp.int32, (WINDOW, LANES), 0) == WINDOW - 1
    bias = sr.bias[...]
    st["rows"] = [pl.multiple_of((step * bb + bi) * N_HEADS, N_HEADS) for bi in range(bb)]
    st["scores"] = []
    for bi in range(bb):
        newk = jnp.where(newest, sr.knew[pl.ds(step * bb + bi, 1), :], pltpu.roll(sr.kc[bi].T, WINDOW - 1, axis=0))
        sr.ko[bi] = newk.T
        qz = sr.qz[pl.ds(st["rows"][bi], N_HEADS), :].astype(BF16)
        st["scores"].append(_dot_nt(qz, newk.astype(BF16)) + bias)


def _sample_values(sr, step, bb, st):
    newest = lax.broadcasted_iota(jnp.int32, (WINDOW, LANES), 0) == WINDOW - 1
    sinkm = sr.sink[...]
    for bi, s in enumerate(st.pop("scores")):
        m = jnp.maximum(jnp.max(s, axis=1, keepdims=True), sinkm)
        e = jnp.exp(s - m)
        den = jnp.sum(e, axis=1, keepdims=True) + jnp.exp(sinkm - m)
        p = (e * (1.0 / den)).astype(BF16)
        newv = jnp.where(newest, sr.vnew[pl.ds(step * bb + bi, 1), :], pltpu.roll(sr.vc[bi].T, WINDOW - 1, axis=0))
        sr.vo[bi] = newv.T
        sr.o[pl.ds(st["rows"][bi], N_HEADS), :] = _dot(p, newv.astype(BF16))


def _sample_finish(sr, gpost_ref, wout_ref):
    nb = sr.x.shape[0]
    lane_lo = lax.broadcasted_iota(jnp.int32, (nb, LANES), 1) < HEAD_DIM
    cols = []
    for c in range(D_ATT // LANES):
        kvh = (2 * c) // GROUP
        halves = []
        for half in range(2):
            oh = sr.o[pl.ds(2 * c + half, nb, stride=N_HEADS), :]
            if half != kvh:
                oh = pltpu.roll(oh, HEAD_DIM, axis=1)
            halves.append(oh)
        cols.append(jnp.where(lane_lo, halves[0], halves[1]))
    att = jnp.concatenate(cols, axis=1)
    att_out = (att * _silu(sr.gatt[...])).astype(BF16)
    rnn_out = sr.rnnout[...].astype(BF16)
    out = _dot(rnn_out, wout_ref[0:D_RNN, :]) + _dot(att_out, wout_ref[D_RNN:, :])
    sr.y[:, 0, :] = sr.x[:, 0, :] + out * _rms_scale(out) * gpost_ref[...]


def kernel(x_prompt, x_sample, state_conv, state_rnn, cache_k_win, cache_v_win, norm_pre, norm_post,
           w_in, conv_w, conv_b, w_gate_a, b_gate_a, w_gate_x, b_gate_x, lru_lambda, attn_sinks,
           rel_bias, w_out):
    assert w_in.shape[0] == 1, "single-layer trunk"
    bsz, seq, _ = x_prompt.shape
    nb = x_sample.shape[0]
    wb = cache_k_win.shape[2]
    assert wb == WINDOW and x_sample.shape[1] == 1 and seq % PROMPT_BLOCK == 0 and nb % (seq // PROMPT_BLOCK) == 0
    assert bsz == PROMPT_SEQS, "the step program is written for this many prompt sequences"

    raw = (conv_w[0], conv_b, b_gate_a, b_gate_x, lru_lambda, w_in[0], w_gate_a[0], w_gate_x[0], w_out[0])
    gpre = norm_pre.reshape(1, D_MODEL)
    gpost = norm_post.reshape(1, D_MODEL)
    sinks = attn_sinks[0]
    bkt_np = _folded_bucket_table()
    bkt = jnp.asarray(np.ascontiguousarray(bkt_np.T))
    bktrow = jnp.asarray(bkt_np[WINDOW - 1:WINDOW, :])

    feature_major = lambda z: jnp.transpose(z[0], (0, 2, 3, 1)).reshape(nb, KV_DIM, wb)
    y_p, conv_p, rnn_p, kw_p, vw_p, y_s, conv_s, rnn_s, kw_s, vw_s = _layer_call(
        x_prompt, rel_bias, sinks, bkt, gpre, gpost, raw,
        x_sample, bktrow, jnp.transpose(state_conv[0], (1, 0, 2)), state_rnn[0],
        feature_major(cache_k_win), feature_major(cache_v_win))

    kv5 = lambda z, n: jnp.transpose(z.reshape(n, N_KV_HEADS, HEAD_DIM, wb), (0, 3, 1, 2))[None]
    return (y_p, y_s,
            jnp.transpose(conv_p, (1, 0, 2))[None], rnn_p, kv5(kw_p, bsz), kv5(vw_p, bsz),
            jnp.transpose(conv_s, (1, 0, 2))[None], rnn_s[None], kv5(kw_s, nb), kv5(vw_s, nb))
```

```python
import collections
import functools
import math

import numpy as np
import jax
import jax.numpy as jnp
from jax import lax
from jax.experimental import pallas as pl
from jax.experimental.pallas import tpu as pltpu

D_MODEL = 1024
D_RNN = 512
D_ATT = 512
HEAD_DIM = 64
N_HEADS = 8
N_KV_HEADS = 2
GROUP = N_HEADS // N_KV_HEADS
KV_DIM = N_KV_HEADS * HEAD_DIM
N_RNN_BLOCKS = 8
RNN_BLOCK = D_RNN // N_RNN_BLOCKS
CONV_W = 4
LRU_C = 8.0
WINDOW = 128
N_BUCKETS = 32
MAX_DISTANCE = 128
EPS = 1e-6
NEG_INF = -1e30
D_IN = 2 * D_RNN + 2 * D_ATT + 2 * KV_DIM

C_XRNN = 0
C_GRNN = D_RNN
C_Q = 2 * D_RNN
C_KV = 2 * D_RNN + D_ATT
C_GATT = 2 * D_RNN + D_ATT + 2 * KV_DIM

SUBLANES = 8
LANES = 128
GATE_GROUP = 256
ONES_ROWS = 16
SCORE_LOOKAHEAD = 6
PROMPT_SEQS = 2
PROMPT_BLOCK = 512
PREP_ROWS = 256
VMEM_LIMIT = 56 * 1024 * 1024

TINY = 1e-37
LOG2E = 1.4426950408889634
F32 = jnp.float32
BF16 = jnp.bfloat16


def _spread(groups, bulk):
    out, done = [], 0
    for i, group in enumerate(groups):
        out += group
        want = (i + 1) * len(bulk) // len(groups)
        out += bulk[done:want]
        done = want
    return out


def _step_program(npairs):
    a, b = range(PROMPT_SEQS)
    n_out = D_MODEL // GATE_GROUP

    def attention(s):
        groups = [[("score", s, n) for n in range(SCORE_LOOKAHEAD)]]
        for n in range(npairs):
            ahead = [("score", s, n + SCORE_LOOKAHEAD)] if n + SCORE_LOOKAHEAD < npairs else []
            groups.append(ahead + [("value", s, n)])
        return groups

    project = lambda s: [("project", s, name) for name in ("x_rnn", "kv", "q")]
    gates = lambda s: [("gate_piece", s, i) for i in range(4)]
    outs = lambda s: [("out_piece", s, j) for j in range(n_out)]
    pa, pb = project(a), project(b)
    ga, gb, oa = gates(a), gates(b), outs(a)
    prog = [("sample_scores", a), ("norm", a), pa[0], ("norm", b), pa[1], pa[2], ("store_kv", a)]
    prog += [("conv_gates", a), ("gate_math", a), ("scan", a)] + pb
    prog += _spread(attention(a), ga + [("rnn_gate", a)])
    prog += [("att_finish", a), ("conv_gates", b), ("store_kv", b), ("gate_math", b), ("scan", b)] + gb
    prog += _spread(attention(b), oa + [("rnn_gate", b)])
    prog += [("finish", a), ("att_finish", b)] + outs(b) + [("sample_values", a), ("finish", b)]
    return prog


def _t5_bucket_np(dist):
    dist = np.maximum(dist, 0)
    max_exact = N_BUCKETS // 2
    d = np.maximum(dist, 1).astype(np.float32)
    ratio = np.log(d / np.float32(max_exact)) / np.float32(math.log(MAX_DISTANCE / max_exact))
    large = max_exact + (ratio * np.float32(N_BUCKETS - max_exact)).astype(np.int32)
    large = np.minimum(large, N_BUCKETS - 1)
    return np.where(dist < max_exact, dist, large).astype(np.int32)


def _folded_bucket_table():
    i = np.arange(WINDOW)[:, None]
    j = np.arange(WINDOW)[None, :]
    dist = np.where(j <= i, i - j, i + WINDOW - j)
    return _t5_bucket_np(dist)


def _sigmoid_of_half(zh):
    return 0.5 + 0.5 * jnp.tanh(zh)


def _silu(x):
    h = 0.5 * x
    return h + h * jnp.tanh(h)


def _softplus(x):
    return jnp.maximum(x, 0.0) + jnp.log1p(jnp.exp(-jnp.abs(x)))


def _rms_scale(x):
    return lax.rsqrt(jnp.mean(x * x, axis=-1, keepdims=True) + EPS)


def _dot(a, b):
    return jnp.dot(a, b, preferred_element_type=F32)


def _dot_nt(a, b):
    return lax.dot_general(a, b, (((1,), (1,)), ((), ())), preferred_element_type=F32)


def _rglru_gate_dots(xc, wg_ref):
    xcb = xc.astype(BF16)
    zs = [_dot(xcb[:, g * GATE_GROUP:(g + 1) * GATE_GROUP], wg_ref[g])
          for g in range(D_RNN // GATE_GROUP)]
    za = jnp.concatenate([z[:, :GATE_GROUP] for z in zs], axis=1)
    zx = jnp.concatenate([z[:, GATE_GROUP:] for z in zs], axis=1)
    return za, zx


def _rglru_gate_math(xc, za, zx, rp):
    i = _sigmoid_of_half(zx + rp[6:7])
    rate = (-0.5 * LRU_C * LOG2E) * _softplus(-rp[7:8])
    a = jnp.exp2((1.0 + jnp.tanh(za + rp[5:6])) * rate)
    u = 1.0 - a * a
    bx = (u * lax.rsqrt(jnp.maximum(u, TINY))) * (i * xc)
    return a, bx


def _prepare_params(step, raw, rp_ref, win_ref, wg_ref, wout_ref):
    cw_raw, cb_raw, ba_raw, bx_raw, lam_raw, win_raw, wga_raw, wgx_raw, wout_raw = raw
    rows = pl.ds(pl.multiple_of(step * PREP_ROWS, PREP_ROWS), PREP_ROWS)
    for c0 in range(0, D_IN, GATE_GROUP):
        w = win_raw[:, c0:c0 + GATE_GROUP]
        if C_Q <= c0 < C_Q + D_ATT:
            w = w * HEAD_DIM ** -0.5
        win_ref[rows, c0:c0 + GATE_GROUP] = w.astype(BF16)
    wout_ref[rows, :] = wout_raw[...].astype(BF16)

    @pl.when(step == 0)
    def _():
        rp_ref[0:CONV_W, :] = cw_raw[...]
        rp_ref[4:5, :] = cb_raw[...]
        rp_ref[5:6, :] = 0.5 * ba_raw[...]
        rp_ref[6:7, :] = 0.5 * bx_raw[...]
        rp_ref[7:8, :] = lam_raw[...]
        per = GATE_GROUP // RNN_BLOCK
        zero = jnp.zeros((RNN_BLOCK, RNN_BLOCK), F32)
        for g in range(D_RNN // GATE_GROUP):
            for half, w_raw in enumerate((wga_raw, wgx_raw)):
                rows_g = [jnp.concatenate([0.5 * w_raw[g * per + m] if k == m else zero for k in range(per)], axis=1)
                          for m in range(per)]
                wg_ref[g, :, half * GATE_GROUP:(half + 1) * GATE_GROUP] = jnp.concatenate(rows_g, axis=0).astype(BF16)


def _build_bias_table(bkt, rb_ref, head):
    tbl = jnp.zeros(bkt.shape, F32)
    for k in range(N_BUCKETS):
        tbl = jnp.where(bkt == k, rb_ref[k, head], tbl)
    return tbl


_SampleRefs = collections.namedtuple(
    "_SampleRefs", "x bktrow sc h0 kc vc y conv rnn ko vo qz o knew vnew rnnout gatt bias sink")


def _layer_kernel(rb_ref, sinks_ref, x_ref, bkt_ref, gpre_ref, gpost_ref,
                  cw_raw, cb_raw, ba_raw, bx_raw, lam_raw, win_raw, wga_raw, wgx_raw, wout_raw,
                  xs_ref, bktrow_ref, sc_ref, h0_ref, kc_ref, vc_ref,
                  y_ref, conv_ref, rnn_ref, kw_ref, vw_ref, ys_ref, convs_ref, rnns_ref, ko_ref, vo_ref,
                  rp_ref, win_ref, wg_ref, wout_ref, *scratch, tb, nseq, bb, nprep):
    step = pl.program_id(0)

    @pl.when(step < nprep)
    def _():
        _prepare_params(step, (cw_raw, cb_raw, ba_raw, bx_raw, lam_raw, win_raw, wga_raw, wgx_raw, wout_raw),
                        rp_ref, win_ref, wg_ref, wout_ref)

    @pl.when(step >= nprep)
    def _():
        _layer_step(step - nprep, pl.num_programs(0) - nprep - 1,
                    rb_ref, sinks_ref, x_ref, bkt_ref, gpre_ref, gpost_ref, rp_ref, win_ref, wg_ref, wout_ref,
                    xs_ref, bktrow_ref, sc_ref, h0_ref, kc_ref, vc_ref,
                    y_ref, conv_ref, rnn_ref, kw_ref, vw_ref, ys_ref, convs_ref, rnns_ref, ko_ref, vo_ref,
                    *scratch, tb=tb, nseq=nseq, bb=bb)


def _layer_step(t, last_t, rb_ref, sinks_ref, x_ref, bkt_ref, gpre_ref, gpost_ref, rp_ref, win_ref, wg_ref, wout_ref,
                xs_ref, bktrow_ref, sc_ref, h0_ref, kc_ref, vc_ref,
                y_ref, conv_ref, rnn_ref, kw_ref, vw_ref, ys_ref, convs_ref, rnns_ref, ko_ref, vo_ref,
                bias_ref, xpad_ref, hc_ref, kd_ref, vt_ref, att_ref, xr_ref, hn_ref,
                qz_ref, o_ref, knew_ref, vnew_ref, rnnout_ref, gatt_ref, sbias_ref, ssink_ref, *, tb, nseq, bb):
    nq = tb // WINDOW
    sr = _SampleRefs(xs_ref, bktrow_ref, sc_ref, h0_ref, kc_ref, vc_ref, ys_ref, convs_ref, rnns_ref, ko_ref, vo_ref,
                     qz_ref, o_ref, knew_ref, vnew_ref, rnnout_ref, gatt_ref, sbias_ref, ssink_ref)

    @pl.when(t == 0)
    def _():
        bkt = bkt_ref[...]
        for h in range(N_HEADS):
            bias_ref[h] = _build_bias_table(bkt, rb_ref, h)
        xpad_ref[...] = jnp.zeros((nseq, (CONV_W - 1) * SUBLANES, D_RNN), F32)
        hc_ref[...] = jnp.zeros((nseq, 1, D_RNN), F32)
        kd_ref[:, :, 0:WINDOW, :] = jnp.zeros((nseq, N_KV_HEADS, WINDOW, LANES), BF16)
        vt_ref[:, :, 0:HEAD_DIM, 0:WINDOW] = jnp.zeros((nseq, N_KV_HEADS, HEAD_DIM, WINDOW), BF16)
        vt_ref[:, :, HEAD_DIM:, :] = jnp.ones((nseq, N_KV_HEADS, ONES_ROWS, tb + WINDOW), BF16)
        _sample_setup(sr, rb_ref, sinks_ref, gpre_ref, rp_ref, win_ref, wg_ref)

    rp = rp_ref[...]
    sub8 = lax.broadcasted_iota(jnp.int32, (SUBLANES, D_RNN), 0)
    lo = lax.broadcasted_iota(jnp.int32, (tb, LANES), 1) < HEAD_DIM
    key_idx = lax.broadcasted_iota(jnp.int32, (WINDOW, WINDOW), 0)
    qry_idx = lax.broadcasted_iota(jnp.int32, (WINDOW, WINDOW), 1)
    own = key_idx <= qry_idx
    lane_lo = qry_idx < HEAD_DIM
    has_prev0 = (jnp.zeros((WINDOW, WINDOW), jnp.int32) + t) > 0
    zero_b = jnp.zeros((WINDOW, WINDOW), BF16)
    pair_lo = lax.broadcasted_iota(jnp.int32, (1, 2 * WINDOW), 1) < WINDOW
    pairs = [(qi, c) for qi in range(nq) for c in range(D_ATT // LANES)]
    gate_cols = [C_GRNN, C_GRNN + GATE_GROUP, C_GATT, C_GATT + GATE_GROUP]

    def norm(s, st):
        x = x_ref[s]
        st["xn"] = (x * _rms_scale(x) * gpre_ref[...]).astype(BF16)

    def project(s, st, name):
        c0, width = dict(x_rnn=(C_XRNN, D_RNN), kv=(C_KV, 2 * KV_DIM), q=(C_Q, D_ATT))[name]
        st[name] = _dot(st["xn"], win_ref[:, c0:c0 + width])

    clen = tb // SUBLANES
    pitch = clen + SUBLANES
    ncol = D_RNN // LANES
    ntail = CONV_W - 1

    def conv_gates(s, st):
        x_rnn = st["x_rnn"]
        for c in range(ncol):
            for k in range(SUBLANES):
                xr_ref[s, c, k * pitch:k * pitch + clen, :] = x_rnn[k * clen:(k + 1) * clen, c * LANES:(c + 1) * LANES]
        xp = jnp.concatenate(
            [jnp.concatenate([xr_ref[s, c, pl.ds(v, SUBLANES, stride=pitch), :] for v in range(clen)], axis=0)
             for c in range(ncol)], axis=1)
        tail = xp[(clen - ntail) * SUBLANES:, :]
        prev_tail = xpad_ref[s]
        heads = [jnp.where(sub8 == 0,
                           pltpu.roll(prev_tail[j * SUBLANES:(j + 1) * SUBLANES], 1, axis=0),
                           pltpu.roll(tail[j * SUBLANES:(j + 1) * SUBLANES], 1, axis=0)) for j in range(ntail)]
        xc = rp[4:5]
        for tap in range(ntail):
            shift = ntail - tap
            xc = xc + rp[tap:tap + 1] * jnp.concatenate(heads[ntail - shift:] + [xp[:(clen - shift) * SUBLANES]], axis=0)
        xc = xc + rp[3:4] * xp
        xpad_ref[s] = tail
        for j in range(ntail):
            conv_ref[j, s:s + 1, :] = tail[j * SUBLANES + SUBLANES - 1:(j + 1) * SUBLANES]
        st["xc"] = xc
        st["za"], st["zx"] = _rglru_gate_dots(xc, wg_ref)

    def gate_math(s, st):
        st["a"], st["bx"] = _rglru_gate_math(st.pop("xc"), st.pop("za"), st.pop("zx"), rp)

    def store_kv(s, st):
        st["qb"] = st.pop("q").astype(BF16)
        k01 = st["kv"][:, :KV_DIM]
        v01 = st["kv"][:, KV_DIM:]
        k10 = pltpu.roll(k01, HEAD_DIM, axis=1)
        kd_ref[s, 0, WINDOW:WINDOW + tb, :] = jnp.where(lo, k01, k10).astype(BF16)
        kd_ref[s, 1, WINDOW:WINDOW + tb, :] = jnp.where(lo, k10, k01).astype(BF16)
        v_t = v01.T.astype(BF16)
        vt_ref[s, 0, 0:HEAD_DIM, WINDOW:WINDOW + tb] = v_t[0:HEAD_DIM]
        vt_ref[s, 1, 0:HEAD_DIM, WINDOW:WINDOW + tb] = v_t[HEAD_DIM:]
        kw_ref[s] = k01[tb - WINDOW:, :].T
        vw_ref[s] = v01[tb - WINDOW:, :].T

    def gate_piece(s, st, i):
        st[("gate", i)] = _dot(st["xn"], win_ref[:, gate_cols[i]:gate_cols[i] + GATE_GROUP])

    def scores(s, st, qi, c):
        rows = slice(qi * WINDOW, (qi + 1) * WINDOW)
        win = slice(qi * WINDOW, qi * WINDOW + 2 * WINDOW)
        kvh = (2 * c) // GROUP
        qc = st["qb"][rows, c * LANES:(c + 1) * LANES]
        qpair = jnp.concatenate([jnp.where(lane_lo, qc, zero_b), jnp.where(lane_lo, zero_b, qc)], axis=0)
        sc = _dot_nt(kd_ref[s, kvh, win, :], qpair)
        p_prev, p_own, ms = [], [], []
        for half in range(2):
            head = 2 * c + half
            cols = slice(half * WINDOW, (half + 1) * WINDOW)
            sf = jnp.where(own, sc[WINDOW:, cols], sc[:WINDOW, cols]) + bias_ref[head]
            if qi == 0:
                sf = jnp.where(own | has_prev0, sf, NEG_INF)
            m = jnp.max(sf, axis=0, keepdims=True)
            e = jnp.exp(sf - m).astype(BF16)
            p_prev.append(jnp.where(own, zero_b, e))
            p_own.append(jnp.where(own, e, zero_b))
            ms.append(m)
        p = jnp.concatenate([jnp.concatenate(p_prev, axis=1), jnp.concatenate(p_own, axis=1)], axis=0)
        return p, jnp.concatenate(ms, axis=1)

    def values(s, st, qi, c, p, m):
        rows = slice(qi * WINDOW, (qi + 1) * WINDOW)
        win = slice(qi * WINDOW, qi * WINDOW + 2 * WINDOW)
        kvh = (2 * c) // GROUP
        oa = _dot(vt_ref[s, kvh, :, win], p)
        sink = jnp.where(pair_lo, sinks_ref[2 * c], sinks_ref[2 * c + 1])
        den = oa[HEAD_DIM:HEAD_DIM + 1, :] + jnp.exp(sink - m)
        o = oa[0:HEAD_DIM, :] * (1.0 / den)
        o_t = jnp.concatenate([o[:, :WINDOW], o[:, WINDOW:]], axis=0)
        att_ref[s, rows, c * LANES:(c + 1) * LANES] = o_t.T

    def score(s, st, n):
        st[("p", n)] = scores(s, st, *pairs[n])

    def value(s, st, n):
        values(s, st, *pairs[n], *st.pop(("p", n)))

    def att_finish(s, st):
        kd_ref[s, :, 0:WINDOW, :] = kd_ref[s, :, tb:tb + WINDOW, :]
        vt_ref[s, :, :, 0:WINDOW] = vt_ref[s, :, :, tb:tb + WINDOW]
        g_att = jnp.concatenate([st[("gate", 2)], st[("gate", 3)]], axis=1)
        st["att_out"] = (att_ref[s] * _silu(g_att)).astype(BF16)

    def scan(s, st):
        a3 = st["a"].reshape(clen, SUBLANES, D_RNN)
        b3 = st["bx"].reshape(clen, SUBLANES, D_RNN)
        hl, pr = [b3[0]], [a3[0]]
        for v in range(1, clen):
            hl.append(a3[v] * hl[-1] + b3[v])
            pr.append(a3[v] * pr[-1])
        h_end, p_end = hl[-1], pr[-1]
        h_in = jnp.broadcast_to(hc_ref[s], (SUBLANES, D_RNN))
        entry = h_in
        for _ in range(SUBLANES - 1):
            entry = jnp.where(sub8 == 0, h_in, pltpu.roll(h_end + p_end * entry, 1, axis=0))
        carry = (h_end + p_end * entry)[SUBLANES - 1:SUBLANES, :]
        hc_ref[s] = carry
        rnn_ref[0, s:s + 1, :] = carry
        for v in range(clen):
            hv = hl[v] + pr[v] * entry
            for c in range(ncol):
                hn_ref[s, c, pl.ds(v, SUBLANES, stride=pitch), :] = hv[:, c * LANES:(c + 1) * LANES]
        h = jnp.concatenate(
            [jnp.concatenate([hn_ref[s, c, k * pitch:k * pitch + clen, :] for k in range(SUBLANES)], axis=0)
             for c in range(ncol)], axis=1)
        st["h"] = h

    def rnn_gate(s, st):
        g_rnn = jnp.concatenate([st[("gate", 0)], st[("gate", 1)]], axis=1)
        st["rnn_out"] = (st.pop("h") * _silu(g_rnn)).astype(BF16)

    def out_piece(s, st, j):
        cols = slice(j * GATE_GROUP, (j + 1) * GATE_GROUP)
        st[("out", j)] = _dot(st["rnn_out"], wout_ref[0:D_RNN, cols]) + _dot(st["att_out"], wout_ref[D_RNN:, cols])

    def finish(s, st):
        out = jnp.concatenate([st[("out", j)] for j in range(D_MODEL // GATE_GROUP)], axis=1)
        y_ref[s] = x_ref[s] + out * _rms_scale(out) * gpost_ref[...]

    def sample_scores(s, st):
        _sample_scores(sr, t, bb, st)

    def sample_values(s, st):
        _sample_values(sr, t, bb, st)

    tasks = dict(sample_scores=sample_scores, sample_values=sample_values, norm=norm, project=project,
                 conv_gates=conv_gates, gate_math=gate_math, store_kv=store_kv, gate_piece=gate_piece,
                 score=score, value=value, att_finish=att_finish, scan=scan, rnn_gate=rnn_gate, out_piece=out_piece, finish=finish)
    sts = [dict() for _ in range(nseq)]
    for name, s, *arg in _step_program(len(pairs)):
        tasks[name](s, sts[s], *arg)

    @pl.when(t == last_t)
    def _():
        _sample_finish(sr, gpost_ref, wout_ref)


def _layer_call(x, rb, sinks, bkt, gpre, gpost, raw, xs, bktrow, sc, h0, kc_t, vc_t):
    bsz, seq, _ = x.shape
    tb = PROMPT_BLOCK
    nt = seq // tb
    nprep = D_MODEL // PREP_ROWS
    nb = xs.shape[0]
    bb = nb // nt
    layer_t = lambda t: jnp.maximum(t - nprep, 0)
    prep_t = lambda t: jnp.minimum(t, nprep - 1)
    win_t_spec = pl.BlockSpec((bb, KV_DIM, WINDOW), lambda t: (layer_t(t), 0, 0))
    x_spec = pl.BlockSpec((bsz, tb, D_MODEL), lambda t: (0, layer_t(t), 0))
    const = lambda *shape: pl.BlockSpec(shape, lambda t: (0,) * len(shape), pipeline_mode=pl.Buffered(1))
    whole = lambda *shape: pl.BlockSpec(shape, lambda t: (0,) * len(shape))
    smem = pl.BlockSpec(memory_space=pltpu.SMEM)
    out_shapes = (
        jax.ShapeDtypeStruct((bsz, seq, D_MODEL), F32),
        jax.ShapeDtypeStruct((CONV_W - 1, bsz, D_RNN), F32),
        jax.ShapeDtypeStruct((1, bsz, D_RNN), F32),
        jax.ShapeDtypeStruct((bsz, KV_DIM, WINDOW), F32),
        jax.ShapeDtypeStruct((bsz, KV_DIM, WINDOW), F32),
        jax.ShapeDtypeStruct((nb, 1, D_MODEL), F32),
        jax.ShapeDtypeStruct((CONV_W - 1, nb, D_RNN), F32),
        jax.ShapeDtypeStruct((nb, D_RNN), F32),
        jax.ShapeDtypeStruct((nb, KV_DIM, WINDOW), F32),
        jax.ShapeDtypeStruct((nb, KV_DIM, WINDOW), F32),
    )
    return pl.pallas_call(
        functools.partial(_layer_kernel, tb=tb, nseq=bsz, bb=bb, nprep=nprep),
        grid=(nprep + nt,),
        in_specs=[
            smem, smem,
            x_spec,
            const(WINDOW, WINDOW),
            const(1, D_MODEL), const(1, D_MODEL),
            const(CONV_W, D_RNN), const(1, D_RNN), const(1, D_RNN), const(1, D_RNN), const(1, D_RNN),
            pl.BlockSpec((PREP_ROWS, D_IN), lambda t: (prep_t(t), 0)),
            const(N_RNN_BLOCKS, RNN_BLOCK, RNN_BLOCK), const(N_RNN_BLOCKS, RNN_BLOCK, RNN_BLOCK),
            pl.BlockSpec((PREP_ROWS, D_MODEL), lambda t: (prep_t(t), 0)),
            const(nb, 1, D_MODEL), const(1, LANES), const(CONV_W - 1, nb, D_RNN), const(nb, D_RNN),
            win_t_spec, win_t_spec,
        ],
        out_specs=(
            x_spec,
            whole(CONV_W - 1, bsz, D_RNN),
            whole(1, bsz, D_RNN),
            whole(bsz, KV_DIM, WINDOW),
            whole(bsz, KV_DIM, WINDOW),
            whole(nb, 1, D_MODEL), whole(CONV_W - 1, nb, D_RNN), whole(nb, D_RNN),
            win_t_spec, win_t_spec,
        ),
        out_shape=out_shapes,
        scratch_shapes=[
            pltpu.VMEM((SUBLANES, D_RNN), F32),
            pltpu.VMEM((D_MODEL, D_IN), BF16),
            pltpu.VMEM((D_RNN // GATE_GROUP, GATE_GROUP, 2 * GATE_GROUP), BF16),
            pltpu.VMEM((D_MODEL, D_MODEL), BF16),
            pltpu.VMEM((N_HEADS, WINDOW, WINDOW), F32),
            pltpu.VMEM((bsz, (CONV_W - 1) * SUBLANES, D_RNN), F32),
            pltpu.VMEM((bsz, 1, D_RNN), F32),
            pltpu.VMEM((bsz, N_KV_HEADS, tb + WINDOW, LANES), BF16),
            pltpu.VMEM((bsz, N_KV_HEADS, HEAD_DIM + ONES_ROWS, tb + WINDOW), BF16),
            pltpu.VMEM((bsz, tb, D_ATT), F32),
            pltpu.VMEM((bsz, D_RNN // LANES, tb + SUBLANES * SUBLANES, LANES), F32),
            pltpu.VMEM((bsz, D_RNN // LANES, tb + SUBLANES * SUBLANES, LANES), F32),
            pltpu.VMEM((nb * N_HEADS, LANES), F32),
            pltpu.VMEM((nb * N_HEADS, LANES), F32),
            pltpu.VMEM((nb, KV_DIM), F32),
            pltpu.VMEM((nb, KV_DIM), F32),
            pltpu.VMEM((nb, D_RNN), F32),
            pltpu.VMEM((nb, D_ATT), F32),
            pltpu.VMEM((SUBLANES, LANES), F32),
            pltpu.VMEM((SUBLANES, LANES), F32),
        ],
        compiler_params=pltpu.CompilerParams(
            dimension_semantics=("arbitrary",),
            vmem_limit_bytes=VMEM_LIMIT),
        name="layer_step",
    )(rb, sinks, x, bkt, gpre, gpost, *raw, xs, bktrow, sc, h0, kc_t, vc_t)


def _sample_setup(sr, rb_ref, sinks_ref, gpre_ref, rp_ref, win_ref, wg_ref):
    nb = sr.x.shape[0]
    sub = lax.broadcasted_iota(jnp.int32, (SUBLANES, LANES), 0)
    lane_lo = lax.broadcasted_iota(jnp.int32, (nb, LANES), 1) < HEAD_DIM
    bkt = jnp.broadcast_to(sr.bktrow[...], (SUBLANES, LANES))
    bias = jnp.zeros((SUBLANES, LANES), F32)
    sinkm = jnp.zeros((SUBLANES, LANES), F32)
    for h in range(N_HEADS):
        bias = jnp.where(sub == h, _build_bias_table(bkt, rb_ref, h), bias)
        sinkm = jnp.where(sub == h, sinks_ref[h], sinkm)
    sr.bias[...] = bias
    sr.sink[...] = sinkm

    x = sr.x[:, 0, :]
    xn = (x * _rms_scale(x) * gpre_ref[...]).astype(BF16)
    rp = rp_ref[...]
    x_rnn = _dot(xn, win_ref[:, C_XRNN:C_XRNN + D_RNN])
    xc = rp[4:5] + rp[0:1] * sr.sc[0]
    xc = xc + rp[1:2] * sr.sc[1]
    xc = xc + rp[2:3] * sr.sc[2]
    xc = xc + rp[3:4] * x_rnn
    sr.conv[0] = sr.sc[1]
    sr.conv[1] = sr.sc[2]
    sr.conv[2] = x_rnn
    a, bx = _rglru_gate_math(xc, *_rglru_gate_dots(xc, wg_ref), rp)
    h = a * sr.h0[...] + bx
    sr.rnn[...] = h
    g_rnn = _dot(xn, win_ref[:, C_GRNN:C_GRNN + D_RNN])
    sr.rnnout[...] = h * _silu(g_rnn)
    sr.gatt[...] = _dot(xn, win_ref[:, C_GATT:C_GATT + D_ATT])
    kv = _dot(xn, win_ref[:, C_KV:C_KV + 2 * KV_DIM])
    sr.knew[...] = kv[:, :KV_DIM]
    sr.vnew[...] = kv[:, KV_DIM:]
    q = _dot(xn, win_ref[:, C_Q:C_Q + D_ATT])
    for head in range(N_HEADS):
        c = head // 2
        kvh = head // GROUP
        qc = q[:, c * LANES:(c + 1) * LANES]
        if (head % 2) != kvh:
            qc = pltpu.roll(qc, HEAD_DIM, axis=1)
        qz = jnp.where(lane_lo, qc, 0.0) if kvh == 0 else jnp.where(lane_lo, 0.0, qc)
        sr.qz[pl.ds(head, nb, stride=N_HEADS), :] = qz


def _sample_scores(sr, step, bb, st):
    newest = lax.broadcasted_iota(jnp.int32, (WINDOW, LANES), 0) == WINDOW - 1
    bias = sr.bias[...]
    st["rows"] = [pl.multiple_of((step * bb + bi) * N_HEADS, N_HEADS) for bi in range(bb)]
    st["scores"] = []
    for bi in range(bb):
        newk = jnp.where(newest, sr.knew[pl.ds(step * bb + bi, 1), :], pltpu.roll(sr.kc[bi].T, WINDOW - 1, axis=0))
        sr.ko[bi] = newk.T
        qz = sr.qz[pl.ds(st["rows"][bi], N_HEADS), :].astype(BF16)
        st["scores"].append(_dot_nt(qz, newk.astype(BF16)) + bias)


def _sample_values(sr, step, bb, st):
    newest = lax.broadcasted_iota(jnp.int32, (WINDOW, LANES), 0) == WINDOW - 1
    sinkm = sr.sink[...]
    for bi, s in enumerate(st.pop("scores")):
        m = jnp.maximum(jnp.max(s, axis=1, keepdims=True), sinkm)
        e = jnp.exp(s - m)
        den = jnp.sum(e, axis=1, keepdims=True) + jnp.exp(sinkm - m)
        p = (e * (1.0 / den)).astype(BF16)
        newv = jnp.where(newest, sr.vnew[pl.ds(step * bb + bi, 1), :], pltpu.roll(sr.vc[bi].T, WINDOW - 1, axis=0))
        sr.vo[bi] = newv.T
        sr.o[pl.ds(st["rows"][bi], N_HEADS), :] = _dot(p, newv.astype(BF16))


def _sample_finish(sr, gpost_ref, wout_ref):
    nb = sr.x.shape[0]
    lane_lo = lax.broadcasted_iota(jnp.int32, (nb, LANES), 1) < HEAD_DIM
    cols = []
    for c in range(D_ATT // LANES):
        kvh = (2 * c) // GROUP
        halves = []
        for half in range(2):
            oh = sr.o[pl.ds(2 * c + half, nb, stride=N_HEADS), :]
            if half != kvh:
                oh = pltpu.roll(oh, HEAD_DIM, axis=1)
            halves.append(oh)
        cols.append(jnp.where(lane_lo, halves[0], halves[1]))
    att = jnp.concatenate(cols, axis=1)
    att_out = (att * _silu(sr.gatt[...])).astype(BF16)
    rnn_out = sr.rnnout[...].astype(BF16)
    out = _dot(rnn_out, wout_ref[0:D_RNN, :]) + _dot(att_out, wout_ref[D_RNN:, :])
    sr.y[:, 0, :] = sr.x[:, 0, :] + out * _rms_scale(out) * gpost_ref[...]


def kernel(x_prompt, x_sample, state_conv, state_rnn, cache_k_win, cache_v_win, norm_pre, norm_post,
           w_in, conv_w, conv_b, w_gate_a, b_gate_a, w_gate_x, b_gate_x, lru_lambda, attn_sinks,
           rel_bias, w_out):
    assert w_in.shape[0] == 1, "single-layer trunk"
    bsz, seq, _ = x_prompt.shape
    nb = x_sample.shape[0]
    wb = cache_k_win.shape[2]
    assert wb == WINDOW and x_sample.shape[1] == 1 and seq % PROMPT_BLOCK == 0 and nb % (seq // PROMPT_BLOCK) == 0
    assert bsz == PROMPT_SEQS, "the step program is written for this many prompt sequences"

    raw = (conv_w[0], conv_b, b_gate_a, b_gate_x, lru_lambda, w_in[0], w_gate_a[0], w_gate_x[0], w_out[0])
    gpre = norm_pre.reshape(1, D_MODEL)
    gpost = norm_post.reshape(1, D_MODEL)
    sinks = attn_sinks[0]
    bkt_np = _folded_bucket_table()
    bkt = jnp.asarray(np.ascontiguousarray(bkt_np.T))
    bktrow = jnp.asarray(bkt_np[WINDOW - 1:WINDOW, :])

    feature_major = lambda z: jnp.transpose(z[0], (0, 2, 3, 1)).reshape(nb, KV_DIM, wb)
    y_p, conv_p, rnn_p, kw_p, vw_p, y_s, conv_s, rnn_s, kw_s, vw_s = _layer_call(
        x_prompt, rel_bias, sinks, bkt, gpre, gpost, raw,
        x_sample, bktrow, jnp.transpose(state_conv[0], (1, 0, 2)), state_rnn[0],
        feature_major(cache_k_win), feature_major(cache_v_win))

    kv5 = lambda z, n: jnp.transpose(z.reshape(n, N_KV_HEADS, HEAD_DIM, wb), (0, 3, 1, 2))[None]
    return (y_p, y_s,
            jnp.transpose(conv_p, (1, 0, 2))[None], rnn_p, kv5(kw_p, bsz), kv5(vw_p, bsz),
            jnp.transpose(conv_s, (1, 0, 2))[None], rnn_s[None], kv5(kw_s, nb), kv5(vw_s, nb))
```

```python
import collections
import functools
import math

import numpy as np
import jax
import jax.numpy as jnp
from jax import lax
from jax.experimental import pallas as pl
from jax.experimental.pallas import tpu as pltpu

D_MODEL = 1024
D_RNN = 512
D_ATT = 512
HEAD_DIM = 64
N_HEADS = 8
N_KV_HEADS = 2
GROUP = N_HEADS // N_KV_HEADS
KV_DIM = N_KV_HEADS * HEAD_DIM
N_RNN_BLOCKS = 8
RNN_BLOCK = D_RNN // N_RNN_BLOCKS
CONV_W = 4
LRU_C = 8.0
WINDOW = 128
N_BUCKETS = 32
MAX_DISTANCE = 128
EPS = 1e-6
NEG_INF = -1e30
D_IN = 2 * D_RNN + 2 * D_ATT + 2 * KV_DIM

C_XRNN = 0
C_GRNN = D_RNN
C_Q = 2 * D_RNN
C_KV = 2 * D_RNN + D_ATT
C_GATT = 2 * D_RNN + D_ATT + 2 * KV_DIM

SUBLANES = 8
LANES = 128
GATE_GROUP = 256
ONES_ROWS = 16
SCORE_LOOKAHEAD = 6
PROMPT_SEQS = 2
PROMPT_BLOCK = 512
PREP_ROWS = 256
V7X_VMEM_BYTES = 64 * 1024 * 1024
VMEM_LIMIT = V7X_VMEM_BYTES - 8 * 1024 * 1024

TINY = 1e-37
ROW_CONV_B = CONV_W
ROW_GATE_A_B = CONV_W + 1
ROW_GATE_X_B = CONV_W + 2
ROW_LAMBDA = CONV_W + 3
LOG2E = 1.4426950408889634
F32 = jnp.float32
BF16 = jnp.bfloat16


def _spread(groups, bulk):
    out, done = [], 0
    for i, group in enumerate(groups):
        out += group
        want = (i + 1) * len(bulk) // len(groups)
        out += bulk[done:want]
        done = want
    return out


def _step_program(npairs):
    a, b = range(PROMPT_SEQS)
    n_out = D_MODEL // GATE_GROUP

    def attention(s):
        groups = [[("score", s, n) for n in range(SCORE_LOOKAHEAD)]]
        for n in range(npairs):
            ahead = [("score", s, n + SCORE_LOOKAHEAD)] if n + SCORE_LOOKAHEAD < npairs else []
            groups.append(ahead + [("value", s, n)])
        return groups

    project = lambda s: [("project", s, name) for name in ("x_rnn", "kv", "q")]
    gates = lambda s: [("gate_piece", s, i) for i in range((D_RNN + D_ATT) // GATE_GROUP)]
    outs = lambda s: [("out_piece", s, j) for j in range(n_out)]
    pa, pb = project(a), project(b)
    ga, gb, oa = gates(a), gates(b), outs(a)
    prog = [("sample_scores", a), ("norm", a), pa[0], ("norm", b), pa[1], pa[2], ("store_kv", a)]
    prog += [("conv_gates", a), ("gate_math", a), ("scan", a)] + pb
    prog += _spread(attention(a), ga + [("rnn_gate", a)])
    prog += [("att_finish", a), ("conv_gates", b), ("store_kv", b), ("gate_math", b), ("scan", b)] + gb
    prog += _spread(attention(b), oa + [("rnn_gate", b)])
    prog += [("finish", a), ("att_finish", b)] + outs(b) + [("sample_values", a), ("finish", b)]
    return prog


def _t5_bucket_np(dist):
    dist = np.maximum(dist, 0)
    max_exact = N_BUCKETS // 2
    d = np.maximum(dist, 1).astype(np.float32)
    ratio = np.log(d / np.float32(max_exact)) / np.float32(math.log(MAX_DISTANCE / max_exact))
    large = max_exact + (ratio * np.float32(N_BUCKETS - max_exact)).astype(np.int32)
    large = np.minimum(large, N_BUCKETS - 1)
    return np.where(dist < max_exact, dist, large).astype(np.int32)


def _folded_bucket_table():
    i = np.arange(WINDOW)[:, None]
    j = np.arange(WINDOW)[None, :]
    dist = np.where(j <= i, i - j, i + WINDOW - j)
    return _t5_bucket_np(dist)


def _sigmoid_of_half(zh):
    return 0.5 + 0.5 * jnp.tanh(zh)


def _silu(x):
    h = 0.5 * x
    return h + h * jnp.tanh(h)


def _softplus(x):
    return jnp.maximum(x, 0.0) + jnp.log1p(jnp.exp(-jnp.abs(x)))


def _rms_scale(x):
    return lax.rsqrt(jnp.mean(x * x, axis=-1, keepdims=True) + EPS)


def _dot(a, b):
    return jnp.dot(a, b, preferred_element_type=F32)


def _dot_nt(a, b):
    return lax.dot_general(a, b, (((1,), (1,)), ((), ())), preferred_element_type=F32)


def _rglru_gate_dots(xc, wg_ref):
    xcb = xc.astype(BF16)
    zs = [_dot(xcb[:, g * GATE_GROUP:(g + 1) * GATE_GROUP], wg_ref[g])
          for g in range(D_RNN // GATE_GROUP)]
    za = jnp.concatenate([z[:, :GATE_GROUP] for z in zs], axis=1)
    zx = jnp.concatenate([z[:, GATE_GROUP:] for z in zs], axis=1)
    return za, zx


def _rglru_gate_math(xc, za, zx, rp):
    i = _sigmoid_of_half(zx + rp[ROW_GATE_X_B:ROW_GATE_X_B + 1])
    rate = (-0.5 * LRU_C * LOG2E) * _softplus(-rp[ROW_LAMBDA:ROW_LAMBDA + 1])
    a = jnp.exp2((1.0 + jnp.tanh(za + rp[ROW_GATE_A_B:ROW_GATE_A_B + 1])) * rate)
    u = 1.0 - a * a
    bx = (u * lax.rsqrt(jnp.maximum(u, TINY))) * (i * xc)
    return a, bx


def _prepare_params(step, raw, rp_ref, win_ref, wg_ref, wout_ref):
    cw_raw, cb_raw, ba_raw, bx_raw, lam_raw, win_raw, wga_raw, wgx_raw, wout_raw = raw
    rows = pl.ds(pl.multiple_of(step * PREP_ROWS, PREP_ROWS), PREP_ROWS)
    for c0 in range(0, D_IN, GATE_GROUP):
        w = win_raw[:, c0:c0 + GATE_GROUP]
        if C_Q <= c0 < C_Q + D_ATT:
            w = w * HEAD_DIM ** -0.5
        win_ref[rows, c0:c0 + GATE_GROUP] = w.astype(BF16)
    wout_ref[rows, :] = wout_raw[...].astype(BF16)

    @pl.when(step == 0)
    def _():
        rp_ref[0:CONV_W, :] = cw_raw[...]
        rp_ref[ROW_CONV_B:ROW_CONV_B + 1, :] = cb_raw[...]
        rp_ref[ROW_GATE_A_B:ROW_GATE_A_B + 1, :] = 0.5 * ba_raw[...]
        rp_ref[ROW_GATE_X_B:ROW_GATE_X_B + 1, :] = 0.5 * bx_raw[...]
        rp_ref[ROW_LAMBDA:ROW_LAMBDA + 1, :] = lam_raw[...]
        per = GATE_GROUP // RNN_BLOCK
        zero = jnp.zeros((RNN_BLOCK, RNN_BLOCK), F32)
        for g in range(D_RNN // GATE_GROUP):
            for half, w_raw in enumerate((wga_raw, wgx_raw)):
                rows_g = [jnp.concatenate([0.5 * w_raw[g * per + m] if k == m else zero for k in range(per)], axis=1)
                          for m in range(per)]
                wg_ref[g, :, half * GATE_GROUP:(half + 1) * GATE_GROUP] = jnp.concatenate(rows_g, axis=0).astype(BF16)


def _build_bias_table(bkt, rb_ref, head):
    tbl = jnp.zeros(bkt.shape, F32)
    for k in range(N_BUCKETS):
        tbl = jnp.where(bkt == k, rb_ref[k, head], tbl)
    return tbl


_SampleRefs = collections.namedtuple(
    "_SampleRefs", "x bktrow sc h0 kc vc y conv rnn ko vo qz o knew vnew rnnout gatt bias sink")


def _layer_kernel(rb_ref, sinks_ref, x_ref, bkt_ref, gpre_ref, gpost_ref,
                  cw_raw, cb_raw, ba_raw, bx_raw, lam_raw, win_raw, wga_raw, wgx_raw, wout_raw,
                  xs_ref, bktrow_ref, sc_ref, h0_ref, kc_ref, vc_ref,
                  y_ref, conv_ref, rnn_ref, kw_ref, vw_ref, ys_ref, convs_ref, rnns_ref, ko_ref, vo_ref,
                  rp_ref, win_ref, wg_ref, wout_ref, *scratch, tb, nseq, bb, nprep):
    step = pl.program_id(0)

    @pl.when(step < nprep)
    def _():
        _prepare_params(step, (cw_raw, cb_raw, ba_raw, bx_raw, lam_raw, win_raw, wga_raw, wgx_raw, wout_raw),
                        rp_ref, win_ref, wg_ref, wout_ref)

        @pl.when(step == 0)
        def _():
            bias_ref = scratch[0]
            bkt = bkt_ref[...]
            for h in range(N_HEADS):
                bias_ref[h] = _build_bias_table(bkt, rb_ref, h)

    @pl.when(step >= nprep)
    def _():
        _layer_step(step - nprep, pl.num_programs(0) - nprep - 1,
                    rb_ref, sinks_ref, x_ref, bkt_ref, gpre_ref, gpost_ref, rp_ref, win_ref, wg_ref, wout_ref,
                    xs_ref, bktrow_ref, sc_ref, h0_ref, kc_ref, vc_ref,
                    y_ref, conv_ref, rnn_ref, kw_ref, vw_ref, ys_ref, convs_ref, rnns_ref, ko_ref, vo_ref,
                    *scratch, tb=tb, nseq=nseq, bb=bb)


def _layer_step(t, last_t, rb_ref, sinks_ref, x_ref, bkt_ref, gpre_ref, gpost_ref, rp_ref, win_ref, wg_ref, wout_ref,
                xs_ref, bktrow_ref, sc_ref, h0_ref, kc_ref, vc_ref,
                y_ref, conv_ref, rnn_ref, kw_ref, vw_ref, ys_ref, convs_ref, rnns_ref, ko_ref, vo_ref,
                bias_ref, xpad_ref, hc_ref, kd_ref, vt_ref, att_ref, xr_ref, hn_ref,
                qz_ref, o_ref, knew_ref, vnew_ref, rnnout_ref, gatt_ref, sbias_ref, ssink_ref, *, tb, nseq, bb):
    nq = tb // WINDOW
    sr = _SampleRefs(xs_ref, bktrow_ref, sc_ref, h0_ref, kc_ref, vc_ref, ys_ref, convs_ref, rnns_ref, ko_ref, vo_ref,
                     qz_ref, o_ref, knew_ref, vnew_ref, rnnout_ref, gatt_ref, sbias_ref, ssink_ref)

    @pl.when(t == 0)
    def _():
        xpad_ref[...] = jnp.zeros((nseq, (CONV_W - 1) * SUBLANES, D_RNN), F32)
        hc_ref[...] = jnp.zeros((nseq, 1, D_RNN), F32)
        kd_ref[:, :, 0:WINDOW, :] = jnp.zeros((nseq, N_KV_HEADS, WINDOW, LANES), BF16)
        vt_ref[:, :, 0:HEAD_DIM, 0:WINDOW] = jnp.zeros((nseq, N_KV_HEADS, HEAD_DIM, WINDOW), BF16)
        vt_ref[:, :, HEAD_DIM:, :] = jnp.ones((nseq, N_KV_HEADS, ONES_ROWS, tb + WINDOW), BF16)
        _sample_setup(sr, rb_ref, sinks_ref, gpre_ref, rp_ref, win_ref, wg_ref)

    rp = rp_ref[...]
    sub8 = lax.broadcasted_iota(jnp.int32, (SUBLANES, D_RNN), 0)
    lo = lax.broadcasted_iota(jnp.int32, (tb, LANES), 1) < HEAD_DIM
    key_idx = lax.broadcasted_iota(jnp.int32, (WINDOW, WINDOW), 0)
    qry_idx = lax.broadcasted_iota(jnp.int32, (WINDOW, WINDOW), 1)
    own = key_idx <= qry_idx
    lane_lo = qry_idx < HEAD_DIM
    has_prev0 = (jnp.zeros((WINDOW, WINDOW), jnp.int32) + t) > 0
    zero_b = jnp.zeros((WINDOW, WINDOW), BF16)
    pair_lo = lax.broadcasted_iota(jnp.int32, (1, 2 * WINDOW), 1) < WINDOW
    pairs = [(qi, c) for qi in range(nq) for c in range(D_ATT // LANES)]
    gate_cols = [C_GRNN, C_GRNN + GATE_GROUP, C_GATT, C_GATT + GATE_GROUP]

    def norm(s, st):
        x = x_ref[s]
        st["xn"] = (x * _rms_scale(x) * gpre_ref[...]).astype(BF16)

    def project(s, st, name):
        c0, width = dict(x_rnn=(C_XRNN, D_RNN), kv=(C_KV, 2 * KV_DIM), q=(C_Q, D_ATT))[name]
        st[name] = _dot(st["xn"], win_ref[:, c0:c0 + width])

    clen = tb // SUBLANES
    pitch = clen + SUBLANES
    ncol = D_RNN // LANES
    ntail = CONV_W - 1

    def conv_gates(s, st):
        x_rnn = st["x_rnn"]
        for c in range(ncol):
            for k in range(SUBLANES):
                xr_ref[s, c, k * pitch:k * pitch + clen, :] = x_rnn[k * clen:(k + 1) * clen, c * LANES:(c + 1) * LANES]
        xp = jnp.concatenate(
            [jnp.concatenate([xr_ref[s, c, pl.ds(v, SUBLANES, stride=pitch), :] for v in range(clen)], axis=0)
             for c in range(ncol)], axis=1)
        tail = xp[(clen - ntail) * SUBLANES:, :]
        prev_tail = xpad_ref[s]
        heads = [jnp.where(sub8 == 0,
                           pltpu.roll(prev_tail[j * SUBLANES:(j + 1) * SUBLANES], 1, axis=0),
                           pltpu.roll(tail[j * SUBLANES:(j + 1) * SUBLANES], 1, axis=0)) for j in range(ntail)]
        xc = rp[ROW_CONV_B:ROW_CONV_B + 1]
        for tap in range(ntail):
            shift = ntail - tap
            xc = xc + rp[tap:tap + 1] * jnp.concatenate(heads[ntail - shift:] + [xp[:(clen - shift) * SUBLANES]], axis=0)
        xc = xc + rp[ntail:ntail + 1] * xp
        xpad_ref[s] = tail
        for j in range(ntail):
            conv_ref[j, s:s + 1, :] = tail[j * SUBLANES + SUBLANES - 1:(j + 1) * SUBLANES]
        st["xc"] = xc
        st["za"], st["zx"] = _rglru_gate_dots(xc, wg_ref)

    def gate_math(s, st):
        st["a"], st["bx"] = _rglru_gate_math(st.pop("xc"), st.pop("za"), st.pop("zx"), rp)

    def store_kv(s, st):
        st["qb"] = st.pop("q").astype(BF16)
        k01 = st["kv"][:, :KV_DIM]
        v01 = st["kv"][:, KV_DIM:]
        k10 = pltpu.roll(k01, HEAD_DIM, axis=1)
        kd_ref[s, 0, WINDOW:WINDOW + tb, :] = jnp.where(lo, k01, k10).astype(BF16)
        kd_ref[s, 1, WINDOW:WINDOW + tb, :] = jnp.where(lo, k10, k01).astype(BF16)
        v_t = v01.T.astype(BF16)
        vt_ref[s, 0, 0:HEAD_DIM, WINDOW:WINDOW + tb] = v_t[0:HEAD_DIM]
        vt_ref[s, 1, 0:HEAD_DIM, WINDOW:WINDOW + tb] = v_t[HEAD_DIM:]
        kw_ref[s] = k01[tb - WINDOW:, :].T
        vw_ref[s] = v01[tb - WINDOW:, :].T

    def gate_piece(s, st, i):
        st[("gate", i)] = _dot(st["xn"], win_ref[:, gate_cols[i]:gate_cols[i] + GATE_GROUP])

    def scores(s, st, qi, c):
        rows = slice(qi * WINDOW, (qi + 1) * WINDOW)
        win = slice(qi * WINDOW, qi * WINDOW + 2 * WINDOW)
        kvh = (2 * c) // GROUP
        qc = st["qb"][rows, c * LANES:(c + 1) * LANES]
        qpair = jnp.concatenate([jnp.where(lane_lo, qc, zero_b), jnp.where(lane_lo, zero_b, qc)], axis=0)
        sc = _dot_nt(kd_ref[s, kvh, win, :], qpair)
        p_prev, p_own, ms = [], [], []
        for half in range(2):
            head = 2 * c + half
            cols = slice(half * WINDOW, (half + 1) * WINDOW)
            sf = jnp.where(own, sc[WINDOW:, cols], sc[:WINDOW, cols]) + bias_ref[head]
            if qi == 0:
                sf = jnp.where(own | has_prev0, sf, NEG_INF)
            m = jnp.max(sf, axis=0, keepdims=True)
            e = jnp.exp(sf - m).astype(BF16)
            p_prev.append(jnp.where(own, zero_b, e))
            p_own.append(jnp.where(own, e, zero_b))
            ms.append(m)
        p = jnp.concatenate([jnp.concatenate(p_prev, axis=1), jnp.concatenate(p_own, axis=1)], axis=0)
        return p, jnp.concatenate(ms, axis=1)

    def values(s, st, qi, c, p, m):
        rows = slice(qi * WINDOW, (qi + 1) * WINDOW)
        win = slice(qi * WINDOW, qi * WINDOW + 2 * WINDOW)
        kvh = (2 * c) // GROUP
        oa = _dot(vt_ref[s, kvh, :, win], p)
        sink = jnp.where(pair_lo, sinks_ref[2 * c], sinks_ref[2 * c + 1])
        den = oa[HEAD_DIM:HEAD_DIM + 1, :] + jnp.exp(sink - m)
        o = oa[0:HEAD_DIM, :] * (1.0 / den)
        o_t = jnp.concatenate([o[:, :WINDOW], o[:, WINDOW:]], axis=0)
        att_ref[s, rows, c * LANES:(c + 1) * LANES] = o_t.T

    def score(s, st, n):
        st[("p", n)] = scores(s, st, *pairs[n])

    def value(s, st, n):
        values(s, st, *pairs[n], *st.pop(("p", n)))

    def att_finish(s, st):
        kd_ref[s, :, 0:WINDOW, :] = kd_ref[s, :, tb:tb + WINDOW, :]
        vt_ref[s, :, :, 0:WINDOW] = vt_ref[s, :, :, tb:tb + WINDOW]
        g_att = jnp.concatenate([st[("gate", 2)], st[("gate", 3)]], axis=1)
        st["att_out"] = (att_ref[s] * _silu(g_att)).astype(BF16)

    def scan(s, st):
        a3 = st["a"].reshape(clen, SUBLANES, D_RNN)
        b3 = st["bx"].reshape(clen, SUBLANES, D_RNN)
        hl, pr = [b3[0]], [a3[0]]
        for v in range(1, clen):
            hl.append(a3[v] * hl[-1] + b3[v])
            pr.append(a3[v] * pr[-1])
        h_end, p_end = hl[-1], pr[-1]
        h_in = jnp.broadcast_to(hc_ref[s], (SUBLANES, D_RNN))
        entry = h_in
        for _ in range(SUBLANES - 1):
            entry = jnp.where(sub8 == 0, h_in, pltpu.roll(h_end + p_end * entry, 1, axis=0))
        carry = (h_end + p_end * entry)[SUBLANES - 1:SUBLANES, :]
        hc_ref[s] = carry
        rnn_ref[0, s:s + 1, :] = carry
        for v in range(clen):
            hv = hl[v] + pr[v] * entry
            for c in range(ncol):
                hn_ref[s, c, pl.ds(v, SUBLANES, stride=pitch), :] = hv[:, c * LANES:(c + 1) * LANES]
        h = jnp.concatenate(
            [jnp.concatenate([hn_ref[s, c, k * pitch:k * pitch + clen, :] for k in range(SUBLANES)], axis=0)
             for c in range(ncol)], axis=1)
        st["h"] = h

    def rnn_gate(s, st):
        g_rnn = jnp.concatenate([st[("gate", 0)], st[("gate", 1)]], axis=1)
        st["rnn_out"] = (st.pop("h") * _silu(g_rnn)).astype(BF16)

    def out_piece(s, st, j):
        cols = slice(j * GATE_GROUP, (j + 1) * GATE_GROUP)
        st[("out", j)] = _dot(st["rnn_out"], wout_ref[0:D_RNN, cols]) + _dot(st["att_out"], wout_ref[D_RNN:, cols])

    def finish(s, st):
        out = jnp.concatenate([st[("out", j)] for j in range(D_MODEL // GATE_GROUP)], axis=1)
        y_ref[s] = x_ref[s] + out * _rms_scale(out) * gpost_ref[...]

    def sample_scores(s, st):
        _sample_scores(sr, t, bb, st)

    def sample_values(s, st):
        _sample_values(sr, t, bb, st)

    tasks = dict(sample_scores=sample_scores, sample_values=sample_values, norm=norm, project=project,
                 conv_gates=conv_gates, gate_math=gate_math, store_kv=store_kv, gate_piece=gate_piece,
                 score=score, value=value, att_finish=att_finish, scan=scan, rnn_gate=rnn_gate, out_piece=out_piece, finish=finish)
    sts = [dict() for _ in range(nseq)]
    for name, s, *arg in _step_program(len(pairs)):
        tasks[name](s, sts[s], *arg)

    @pl.when(t == last_t)
    def _():
        _sample_finish(sr, gpost_ref, wout_ref)


def _layer_call(x, rb, sinks, bkt, gpre, gpost, raw, xs, bktrow, sc, h0, kc_t, vc_t):
    bsz, seq, _ = x.shape
    tb = PROMPT_BLOCK
    nt = seq // tb
    nprep = D_MODEL // PREP_ROWS
    nb = xs.shape[0]
    bb = nb // nt
    layer_t = lambda t: jnp.maximum(t - nprep, 0)
    prep_t = lambda t: jnp.minimum(t, nprep - 1)
    win_t_spec = pl.BlockSpec((bb, KV_DIM, WINDOW), lambda t: (layer_t(t), 0, 0))
    x_spec = pl.BlockSpec((bsz, tb, D_MODEL), lambda t: (0, layer_t(t), 0))
    const = lambda *shape: pl.BlockSpec(shape, lambda t: (0,) * len(shape), pipeline_mode=pl.Buffered(1))
    whole = lambda *shape: pl.BlockSpec(shape, lambda t: (0,) * len(shape))
    smem = pl.BlockSpec(memory_space=pltpu.SMEM)
    out_shapes = (
        jax.ShapeDtypeStruct((bsz, seq, D_MODEL), F32),
        jax.ShapeDtypeStruct((CONV_W - 1, bsz, D_RNN), F32),
        jax.ShapeDtypeStruct((1, bsz, D_RNN), F32),
        jax.ShapeDtypeStruct((bsz, KV_DIM, WINDOW), F32),
        jax.ShapeDtypeStruct((bsz, KV_DIM, WINDOW), F32),
        jax.ShapeDtypeStruct((nb, 1, D_MODEL), F32),
        jax.ShapeDtypeStruct((CONV_W - 1, nb, D_RNN), F32),
        jax.ShapeDtypeStruct((nb, D_RNN), F32),
        jax.ShapeDtypeStruct((nb, KV_DIM, WINDOW), F32),
        jax.ShapeDtypeStruct((nb, KV_DIM, WINDOW), F32),
    )
    return pl.pallas_call(
        functools.partial(_layer_kernel, tb=tb, nseq=bsz, bb=bb, nprep=nprep),
        grid=(nprep + nt,),
        in_specs=[
            smem, smem,
            x_spec,
            const(WINDOW, WINDOW),
            const(1, D_MODEL), const(1, D_MODEL),
            const(CONV_W, D_RNN), const(1, D_RNN), const(1, D_RNN), const(1, D_RNN), const(1, D_RNN),
            pl.BlockSpec((PREP_ROWS, D_IN), lambda t: (prep_t(t), 0)),
            const(N_RNN_BLOCKS, RNN_BLOCK, RNN_BLOCK), const(N_RNN_BLOCKS, RNN_BLOCK, RNN_BLOCK),
            pl.BlockSpec((PREP_ROWS, D_MODEL), lambda t: (prep_t(t), 0)),
            const(nb, 1, D_MODEL), const(1, LANES), const(CONV_W - 1, nb, D_RNN), const(nb, D_RNN),
            win_t_spec, win_t_spec,
        ],
        out_specs=(
            x_spec,
            whole(CONV_W - 1, bsz, D_RNN),
            whole(1, bsz, D_RNN),
            whole(bsz, KV_DIM, WINDOW),
            whole(bsz, KV_DIM, WINDOW),
            whole(nb, 1, D_MODEL), whole(CONV_W - 1, nb, D_RNN), whole(nb, D_RNN),
            win_t_spec, win_t_spec,
        ),
        out_shape=out_shapes,
        scratch_shapes=[
            pltpu.VMEM((SUBLANES, D_RNN), F32),
            pltpu.VMEM((D_MODEL, D_IN), BF16),
            pltpu.VMEM((D_RNN // GATE_GROUP, GATE_GROUP, 2 * GATE_GROUP), BF16),
            pltpu.VMEM((D_MODEL, D_MODEL), BF16),
            pltpu.VMEM((N_HEADS, WINDOW, WINDOW), F32),
            pltpu.VMEM((bsz, (CONV_W - 1) * SUBLANES, D_RNN), F32),
            pltpu.VMEM((bsz, 1, D_RNN), F32),
            pltpu.VMEM((bsz, N_KV_HEADS, tb + WINDOW, LANES), BF16),
            pltpu.VMEM((bsz, N_KV_HEADS, HEAD_DIM + ONES_ROWS, tb + WINDOW), BF16),
            pltpu.VMEM((bsz, tb, D_ATT), F32),
            pltpu.VMEM((bsz, D_RNN // LANES, tb + SUBLANES * SUBLANES, LANES), F32),
            pltpu.VMEM((bsz, D_RNN // LANES, tb + SUBLANES * SUBLANES, LANES), F32),
            pltpu.VMEM((nb * N_HEADS, LANES), F32),
            pltpu.VMEM((nb * N_HEADS, LANES), F32),
            pltpu.VMEM((nb, KV_DIM), F32),
            pltpu.VMEM((nb, KV_DIM), F32),
            pltpu.VMEM((nb, D_RNN), F32),
            pltpu.VMEM((nb, D_ATT), F32),
            pltpu.VMEM((SUBLANES, LANES), F32),
            pltpu.VMEM((SUBLANES, LANES), F32),
        ],
        compiler_params=pltpu.CompilerParams(
            dimension_semantics=("arbitrary",),
            vmem_limit_bytes=VMEM_LIMIT),
        name="layer_step",
    )(rb, sinks, x, bkt, gpre, gpost, *raw, xs, bktrow, sc, h0, kc_t, vc_t)


def _sample_setup(sr, rb_ref, sinks_ref, gpre_ref, rp_ref, win_ref, wg_ref):
    nb = sr.x.shape[0]
    sub = lax.broadcasted_iota(jnp.int32, (SUBLANES, LANES), 0)
    lane_lo = lax.broadcasted_iota(jnp.int32, (nb, LANES), 1) < HEAD_DIM
    bkt = jnp.broadcast_to(sr.bktrow[...], (SUBLANES, LANES))
    bias = jnp.zeros((SUBLANES, LANES), F32)
    sinkm = jnp.zeros((SUBLANES, LANES), F32)
    for h in range(N_HEADS):
        bias = jnp.where(sub == h, _build_bias_table(bkt, rb_ref, h), bias)
        sinkm = jnp.where(sub == h, sinks_ref[h], sinkm)
    sr.bias[...] = bias
    sr.sink[...] = sinkm

    x = sr.x[:, 0, :]
    xn = (x * _rms_scale(x) * gpre_ref[...]).astype(BF16)
    rp = rp_ref[...]
    x_rnn = _dot(xn, win_ref[:, C_XRNN:C_XRNN + D_RNN])
    xc = rp[ROW_CONV_B:ROW_CONV_B + 1]
    for tap in range(CONV_W - 1):
        xc = xc + rp[tap:tap + 1] * sr.sc[tap]
    xc = xc + rp[CONV_W - 1:CONV_W] * x_rnn
    sr.conv[0] = sr.sc[1]
    sr.conv[1] = sr.sc[2]
    sr.conv[2] = x_rnn
    a, bx = _rglru_gate_math(xc, *_rglru_gate_dots(xc, wg_ref), rp)
    h = a * sr.h0[...] + bx
    sr.rnn[...] = h
    g_rnn = _dot(xn, win_ref[:, C_GRNN:C_GRNN + D_RNN])
    sr.rnnout[...] = h * _silu(g_rnn)
    sr.gatt[...] = _dot(xn, win_ref[:, C_GATT:C_GATT + D_ATT])
    kv = _dot(xn, win_ref[:, C_KV:C_KV + 2 * KV_DIM])
    sr.knew[...] = kv[:, :KV_DIM]
    sr.vnew[...] = kv[:, KV_DIM:]
    q = _dot(xn, win_ref[:, C_Q:C_Q + D_ATT])
    for head in range(N_HEADS):
        c = head // 2
        kvh = head // GROUP
        qc = q[:, c * LANES:(c + 1) * LANES]
        if (head % 2) != kvh:
            qc = pltpu.roll(qc, HEAD_DIM, axis=1)
        qz = jnp.where(lane_lo, qc, 0.0) if kvh == 0 else jnp.where(lane_lo, 0.0, qc)
        sr.qz[pl.ds(head, nb, stride=N_HEADS), :] = qz


def _sample_scores(sr, step, bb, st):
    newest = lax.broadcasted_iota(jnp.int32, (WINDOW, LANES), 0) == WINDOW - 1
    bias = sr.bias[...]
    st["rows"] = [pl.multiple_of((step * bb + bi) * N_HEADS, N_HEADS) for bi in range(bb)]
    st["scores"] = []
    for bi in range(bb):
        newk = jnp.where(newest, sr.knew[pl.ds(step * bb + bi, 1), :], pltpu.roll(sr.kc[bi].T, WINDOW - 1, axis=0))
        sr.ko[bi] = newk.T
        qz = sr.qz[pl.ds(st["rows"][bi], N_HEADS), :].astype(BF16)
        st["scores"].append(_dot_nt(qz, newk.astype(BF16)) + bias)


def _sample_values(sr, step, bb, st):
    newest = lax.broadcasted_iota(jnp.int32, (WINDOW, LANES), 0) == WINDOW - 1
    sinkm = sr.sink[...]
    for bi, s in enumerate(st.pop("scores")):
        m = jnp.maximum(jnp.max(s, axis=1, keepdims=True), sinkm)
        e = jnp.exp(s - m)
        den = jnp.sum(e, axis=1, keepdims=True) + jnp.exp(sinkm - m)
        p = (e * (1.0 / den)).astype(BF16)
        newv = jnp.where(newest, sr.vnew[pl.ds(step * bb + bi, 1), :], pltpu.roll(sr.vc[bi].T, WINDOW - 1, axis=0))
        sr.vo[bi] = newv.T
        sr.o[pl.ds(st["rows"][bi], N_HEADS), :] = _dot(p, newv.astype(BF16))


def _sample_finish(sr, gpost_ref, wout_ref):
    nb = sr.x.shape[0]
    lane_lo = lax.broadcasted_iota(jnp.int32, (nb, LANES), 1) < HEAD_DIM
    cols = []
    for c in range(D_ATT // LANES):
        kvh = (2 * c) // GROUP
        halves = []
        for half in range(2):
            oh = sr.o[pl.ds(2 * c + half, nb, stride=N_HEADS), :]
            if half != kvh:
                oh = pltpu.roll(oh, HEAD_DIM, axis=1)
            halves.append(oh)
        cols.append(jnp.where(lane_lo, halves[0], halves[1]))
    att = jnp.concatenate(cols, axis=1)
    att_out = (att * _silu(sr.gatt[...])).astype(BF16)
    rnn_out = sr.rnnout[...].astype(BF16)
    out = _dot(rnn_out, wout_ref[0:D_RNN, :]) + _dot(att_out, wout_ref[D_RNN:, :])
    sr.y[:, 0, :] = sr.x[:, 0, :] + out * _rms_scale(out) * gpost_ref[...]


def kernel(x_prompt, x_sample, state_conv, state_rnn, cache_k_win, cache_v_win, norm_pre, norm_post,
           w_in, conv_w, conv_b, w_gate_a, b_gate_a, w_gate_x, b_gate_x, lru_lambda, attn_sinks,
           rel_bias, w_out):
    assert w_in.shape[0] == 1, "single-layer trunk"
    bsz, seq, _ = x_prompt.shape
    nb = x_sample.shape[0]
    wb = cache_k_win.shape[2]
    assert wb == WINDOW and x_sample.shape[1] == 1 and seq % PROMPT_BLOCK == 0 and nb % (seq // PROMPT_BLOCK) == 0
    assert bsz == PROMPT_SEQS, "the step program is written for this many prompt sequences"

    raw = (conv_w[0], conv_b, b_gate_a, b_gate_x, lru_lambda, w_in[0], w_gate_a[0], w_gate_x[0], w_out[0])
    gpre = norm_pre.reshape(1, D_MODEL)
    gpost = norm_post.reshape(1, D_MODEL)
    sinks = attn_sinks[0]
    bkt_np = _folded_bucket_table()
    bkt = jnp.asarray(np.ascontiguousarray(bkt_np.T))
    bktrow = jnp.asarray(bkt_np[WINDOW - 1:WINDOW, :])

    feature_major = lambda z: jnp.transpose(z[0], (0, 2, 3, 1)).reshape(nb, KV_DIM, wb)
    y_p, conv_p, rnn_p, kw_p, vw_p, y_s, conv_s, rnn_s, kw_s, vw_s = _layer_call(
        x_prompt, rel_bias, sinks, bkt, gpre, gpost, raw,
        x_sample, bktrow, jnp.transpose(state_conv[0], (1, 0, 2)), state_rnn[0],
        feature_major(cache_k_win), feature_major(cache_v_win))

    kv5 = lambda z, n: jnp.transpose(z.reshape(n, N_KV_HEADS, HEAD_DIM, wb), (0, 3, 1, 2))[None]
    return (y_p, y_s,
            jnp.transpose(conv_p, (1, 0, 2))[None], rnn_p, kv5(kw_p, bsz), kv5(vw_p, bsz),
            jnp.transpose(conv_s, (1, 0, 2))[None], rnn_s[None], kv5(kw_s, nb), kv5(vw_s, nb))
```

```python
import collections
import functools
import math

import numpy as np
import jax
import jax.numpy as jnp
from jax import lax
from jax.experimental import pallas as pl
from jax.experimental.pallas import tpu as pltpu

D_MODEL = 1024
D_RNN = 512
D_ATT = 512
HEAD_DIM = 64
N_HEADS = 8
N_KV_HEADS = 2
GROUP = N_HEADS // N_KV_HEADS
KV_DIM = N_KV_HEADS * HEAD_DIM
N_RNN_BLOCKS = 8
RNN_BLOCK = D_RNN // N_RNN_BLOCKS
CONV_W = 4
LRU_C = 8.0
WINDOW = 128
N_BUCKETS = 32
MAX_DISTANCE = 128
EPS = 1e-6
NEG_INF = -1e30
D_IN = 2 * D_RNN + 2 * D_ATT + 2 * KV_DIM

C_XRNN = 0
C_GRNN = D_RNN
C_Q = 2 * D_RNN
C_KV = 2 * D_RNN + D_ATT
C_GATT = 2 * D_RNN + D_ATT + 2 * KV_DIM

SUBLANES = 8
LANES = 128
GATE_GROUP = 256
ONES_ROWS = 16
SCORE_LOOKAHEAD = 6
PROMPT_SEQS = 2
PROMPT_BLOCK = 512
PREP_ROWS = 256
V7X_VMEM_BYTES = 64 * 1024 * 1024
VMEM_LIMIT = V7X_VMEM_BYTES - 8 * 1024 * 1024

TINY = 1e-37
ROW_CONV_B = CONV_W
ROW_GATE_A_B = CONV_W + 1
ROW_GATE_X_B = CONV_W + 2
ROW_LAMBDA = CONV_W + 3
LOG2E = 1.4426950408889634
F32 = jnp.float32
BF16 = jnp.bfloat16


def _spread(groups, bulk):
    out, done = [], 0
    for i, group in enumerate(groups):
        out += group
        want = (i + 1) * len(bulk) // len(groups)
        out += bulk[done:want]
        done = want
    return out


def _step_program(npairs):
    a, b = range(PROMPT_SEQS)
    n_out = D_MODEL // GATE_GROUP

    def attention(s):
        groups = [[("score", s, n) for n in range(SCORE_LOOKAHEAD)]]
        for n in range(npairs):
            ahead = [("score", s, n + SCORE_LOOKAHEAD)] if n + SCORE_LOOKAHEAD < npairs else []
            groups.append(ahead + [("value", s, n)])
        return groups

    project = lambda s: [("project", s, name) for name in ("x_rnn", "kv", "q")]
    gates = lambda s: [("gate_piece", s, i) for i in range((D_RNN + D_ATT) // GATE_GROUP)]
    outs = lambda half, s: [(half, s, j) for j in range(n_out)]
    pa, pb = project(a), project(b)
    ga, gb, oa = gates(a), gates(b), outs("out_rnn", a) + outs("out_att", a)
    prog = [("sample_scores", a), ("norm", a), pa[0], ("norm", b), pa[1], pa[2], ("store_kv", a)]
    prog += [("conv", a), ("gate_dots", a)] + ga + [("gate_math", a), ("scan", a)]
    prog += _spread(attention(a), pb + [("rnn_gate", a)])
    prog += [("att_finish", a), ("conv", b), ("store_kv", b)] + gb[:2] + [("gate_dots", b)] + gb[2:]
    prog += [("gate_math", b), ("scan", b)]
    prog += _spread(attention(b), [("rnn_gate", b)] + oa)
    prog += [("finish", a)] + outs("out_rnn", b) + [("att_finish", b)] + outs("out_att", b)
    prog += [("sample_values", a), ("finish", b)]
    return prog


def _t5_bucket_np(dist):
    dist = np.maximum(dist, 0)
    max_exact = N_BUCKETS // 2
    d = np.maximum(dist, 1).astype(np.float32)
    ratio = np.log(d / np.float32(max_exact)) / np.float32(math.log(MAX_DISTANCE / max_exact))
    large = max_exact + (ratio * np.float32(N_BUCKETS - max_exact)).astype(np.int32)
    large = np.minimum(large, N_BUCKETS - 1)
    return np.where(dist < max_exact, dist, large).astype(np.int32)


def _folded_bucket_table():
    i = np.arange(WINDOW)[:, None]
    j = np.arange(WINDOW)[None, :]
    dist = np.where(j <= i, i - j, i + WINDOW - j)
    return _t5_bucket_np(dist)


def _sigmoid_of_half(zh):
    return 0.5 + 0.5 * jnp.tanh(zh)


def _silu(x):
    h = 0.5 * x
    return h + h * jnp.tanh(h)


def _softplus(x):
    return jnp.maximum(x, 0.0) + jnp.log1p(jnp.exp(-jnp.abs(x)))


def _rms_scale(x):
    return lax.rsqrt(jnp.mean(x * x, axis=-1, keepdims=True) + EPS)


def _dot(a, b):
    return jnp.dot(a, b, preferred_element_type=F32)


def _dot_nt(a, b):
    return lax.dot_general(a, b, (((1,), (1,)), ((), ())), preferred_element_type=F32)


def _rglru_gate_dots(xc, wg_ref):
    xcb = xc.astype(BF16)
    zs = [_dot(xcb[:, g * GATE_GROUP:(g + 1) * GATE_GROUP], wg_ref[g])
          for g in range(D_RNN // GATE_GROUP)]
    za = jnp.concatenate([z[:, :GATE_GROUP] for z in zs], axis=1)
    zx = jnp.concatenate([z[:, GATE_GROUP:] for z in zs], axis=1)
    return za, zx


def _rglru_gate_math(xc, za, zx, rp):
    i = _sigmoid_of_half(zx + rp[ROW_GATE_X_B:ROW_GATE_X_B + 1])
    rate = (-0.5 * LRU_C * LOG2E) * _softplus(-rp[ROW_LAMBDA:ROW_LAMBDA + 1])
    a = jnp.exp2((1.0 + jnp.tanh(za + rp[ROW_GATE_A_B:ROW_GATE_A_B + 1])) * rate)
    u = 1.0 - a * a
    bx = (u * lax.rsqrt(jnp.maximum(u, TINY))) * (i * xc)
    return a, bx


def _prepare_params(step, raw, rp_ref, win_ref, wg_ref, wout_ref):
    cw_raw, cb_raw, ba_raw, bx_raw, lam_raw, win_raw, wga_raw, wgx_raw, wout_raw = raw
    rows = pl.ds(pl.multiple_of(step * PREP_ROWS, PREP_ROWS), PREP_ROWS)
    for c0 in range(0, D_IN, GATE_GROUP):
        w = win_raw[:, c0:c0 + GATE_GROUP]
        if C_Q <= c0 < C_Q + D_ATT:
            w = w * HEAD_DIM ** -0.5
        win_ref[rows, c0:c0 + GATE_GROUP] = w.astype(BF16)
    wout_ref[rows, :] = wout_raw[...].astype(BF16)

    @pl.when(step == 0)
    def _():
        rp_ref[0:CONV_W, :] = cw_raw[...]
        rp_ref[ROW_CONV_B:ROW_CONV_B + 1, :] = cb_raw[...]
        rp_ref[ROW_GATE_A_B:ROW_GATE_A_B + 1, :] = 0.5 * ba_raw[...]
        rp_ref[ROW_GATE_X_B:ROW_GATE_X_B + 1, :] = 0.5 * bx_raw[...]
        rp_ref[ROW_LAMBDA:ROW_LAMBDA + 1, :] = lam_raw[...]
        per = GATE_GROUP // RNN_BLOCK
        zero = jnp.zeros((RNN_BLOCK, RNN_BLOCK), F32)
        for g in range(D_RNN // GATE_GROUP):
            for half, w_raw in enumerate((wga_raw, wgx_raw)):
                rows_g = [jnp.concatenate([0.5 * w_raw[g * per + m] if k == m else zero for k in range(per)], axis=1)
                          for m in range(per)]
                wg_ref[g, :, half * GATE_GROUP:(half + 1) * GATE_GROUP] = jnp.concatenate(rows_g, axis=0).astype(BF16)


def _build_bias_table(bkt, rb_ref, head):
    tbl = jnp.zeros(bkt.shape, F32)
    for k in range(N_BUCKETS):
        tbl = jnp.where(bkt == k, rb_ref[k, head], tbl)
    return tbl


_SampleRefs = collections.namedtuple(
    "_SampleRefs", "x bktrow sc h0 kc vc y conv rnn ko vo qz o knew vnew rnnout gatt bias sink")


def _layer_kernel(rb_ref, sinks_ref, x_ref, bkt_ref, gpre_ref, gpost_ref,
                  cw_raw, cb_raw, ba_raw, bx_raw, lam_raw, win_raw, wga_raw, wgx_raw, wout_raw,
                  xs_ref, bktrow_ref, sc_ref, h0_ref, kc_ref, vc_ref,
                  y_ref, conv_ref, rnn_ref, kw_ref, vw_ref, ys_ref, convs_ref, rnns_ref, ko_ref, vo_ref,
                  rp_ref, win_ref, wg_ref, wout_ref, *scratch, tb, nseq, bb, nprep):
    step = pl.program_id(0)

    @pl.when(step < nprep)
    def _():
        _prepare_params(step, (cw_raw, cb_raw, ba_raw, bx_raw, lam_raw, win_raw, wga_raw, wgx_raw, wout_raw),
                        rp_ref, win_ref, wg_ref, wout_ref)

        @pl.when(step == 0)
        def _():
            bias_ref = scratch[0]
            bkt = bkt_ref[...]
            for h in range(N_HEADS):
                bias_ref[h] = _build_bias_table(bkt, rb_ref, h)

    @pl.when(step >= nprep)
    def _():
        _layer_step(step - nprep, pl.num_programs(0) - nprep - 1,
                    rb_ref, sinks_ref, x_ref, bkt_ref, gpre_ref, gpost_ref, rp_ref, win_ref, wg_ref, wout_ref,
                    xs_ref, bktrow_ref, sc_ref, h0_ref, kc_ref, vc_ref,
                    y_ref, conv_ref, rnn_ref, kw_ref, vw_ref, ys_ref, convs_ref, rnns_ref, ko_ref, vo_ref,
                    *scratch, tb=tb, nseq=nseq, bb=bb)


def _layer_step(t, last_t, rb_ref, sinks_ref, x_ref, bkt_ref, gpre_ref, gpost_ref, rp_ref, win_ref, wg_ref, wout_ref,
                xs_ref, bktrow_ref, sc_ref, h0_ref, kc_ref, vc_ref,
                y_ref, conv_ref, rnn_ref, kw_ref, vw_ref, ys_ref, convs_ref, rnns_ref, ko_ref, vo_ref,
                bias_ref, xpad_ref, hc_ref, kd_ref, vt_ref, att_ref, xr_ref, hn_ref,
                qz_ref, o_ref, knew_ref, vnew_ref, rnnout_ref, gatt_ref, sbias_ref, ssink_ref, *, tb, nseq, bb):
    nq = tb // WINDOW
    sr = _SampleRefs(xs_ref, bktrow_ref, sc_ref, h0_ref, kc_ref, vc_ref, ys_ref, convs_ref, rnns_ref, ko_ref, vo_ref,
                     qz_ref, o_ref, knew_ref, vnew_ref, rnnout_ref, gatt_ref, sbias_ref, ssink_ref)

    @pl.when(t == 0)
    def _():
        xpad_ref[...] = jnp.zeros((nseq, (CONV_W - 1) * SUBLANES, D_RNN), F32)
        hc_ref[...] = jnp.zeros((nseq, 1, D_RNN), F32)
        kd_ref[:, :, 0:WINDOW, :] = jnp.zeros((nseq, N_KV_HEADS, WINDOW, LANES), BF16)
        vt_ref[:, :, 0:HEAD_DIM, 0:WINDOW] = jnp.zeros((nseq, N_KV_HEADS, HEAD_DIM, WINDOW), BF16)
        vt_ref[:, :, HEAD_DIM:, :] = jnp.ones((nseq, N_KV_HEADS, ONES_ROWS, tb + WINDOW), BF16)
        _sample_setup(sr, rb_ref, sinks_ref, gpre_ref, rp_ref, win_ref, wg_ref)

    rp = rp_ref[...]
    sub8 = lax.broadcasted_iota(jnp.int32, (SUBLANES, D_RNN), 0)
    lo = lax.broadcasted_iota(jnp.int32, (tb, LANES), 1) < HEAD_DIM
    key_idx = lax.broadcasted_iota(jnp.int32, (WINDOW, WINDOW), 0)
    qry_idx = lax.broadcasted_iota(jnp.int32, (WINDOW, WINDOW), 1)
    own = key_idx <= qry_idx
    lane_lo = qry_idx < HEAD_DIM
    has_prev0 = (jnp.zeros((WINDOW, WINDOW), jnp.int32) + t) > 0
    zero_b = jnp.zeros((WINDOW, WINDOW), BF16)
    pair_lo = lax.broadcasted_iota(jnp.int32, (1, 2 * WINDOW), 1) < WINDOW
    pairs = [(qi, c) for qi in range(nq) for c in range(D_ATT // LANES)]
    gate_cols = [C_GRNN, C_GRNN + GATE_GROUP, C_GATT, C_GATT + GATE_GROUP]

    def norm(s, st):
        x = x_ref[s]
        st["xn"] = (x * _rms_scale(x) * gpre_ref[...]).astype(BF16)

    def project(s, st, name):
        c0, width = dict(x_rnn=(C_XRNN, D_RNN), kv=(C_KV, 2 * KV_DIM), q=(C_Q, D_ATT))[name]
        st[name] = _dot(st["xn"], win_ref[:, c0:c0 + width])

    clen = tb // SUBLANES
    pitch = clen + SUBLANES
    ncol = D_RNN // LANES
    ntail = CONV_W - 1

    def conv(s, st):
        x_rnn = st["x_rnn"]
        for c in range(ncol):
            for k in range(SUBLANES):
                xr_ref[s, c, k * pitch:k * pitch + clen, :] = x_rnn[k * clen:(k + 1) * clen, c * LANES:(c + 1) * LANES]
        xp = jnp.concatenate(
            [jnp.concatenate([xr_ref[s, c, pl.ds(v, SUBLANES, stride=pitch), :] for v in range(clen)], axis=0)
             for c in range(ncol)], axis=1)
        tail = xp[(clen - ntail) * SUBLANES:, :]
        prev_tail = xpad_ref[s]
        heads = [jnp.where(sub8 == 0,
                           pltpu.roll(prev_tail[j * SUBLANES:(j + 1) * SUBLANES], 1, axis=0),
                           pltpu.roll(tail[j * SUBLANES:(j + 1) * SUBLANES], 1, axis=0)) for j in range(ntail)]
        xc = rp[ROW_CONV_B:ROW_CONV_B + 1]
        for tap in range(ntail):
            shift = ntail - tap
            xc = xc + rp[tap:tap + 1] * jnp.concatenate(heads[ntail - shift:] + [xp[:(clen - shift) * SUBLANES]], axis=0)
        xc = xc + rp[ntail:ntail + 1] * xp
        xpad_ref[s] = tail
        for j in range(ntail):
            conv_ref[j, s:s + 1, :] = tail[j * SUBLANES + SUBLANES - 1:(j + 1) * SUBLANES]
        st["xc"] = xc

    def gate_dots(s, st):
        st["za"], st["zx"] = _rglru_gate_dots(st["xc"], wg_ref)

    def gate_math(s, st):
        st["a"], st["bx"] = _rglru_gate_math(st.pop("xc"), st.pop("za"), st.pop("zx"), rp)

    def store_kv(s, st):
        st["qb"] = st.pop("q").astype(BF16)
        k01 = st["kv"][:, :KV_DIM]
        v01 = st["kv"][:, KV_DIM:]
        k10 = pltpu.roll(k01, HEAD_DIM, axis=1)
        kd_ref[s, 0, WINDOW:WINDOW + tb, :] = jnp.where(lo, k01, k10).astype(BF16)
        kd_ref[s, 1, WINDOW:WINDOW + tb, :] = jnp.where(lo, k10, k01).astype(BF16)
        v_t = v01.T.astype(BF16)
        vt_ref[s, 0, 0:HEAD_DIM, WINDOW:WINDOW + tb] = v_t[0:HEAD_DIM]
        vt_ref[s, 1, 0:HEAD_DIM, WINDOW:WINDOW + tb] = v_t[HEAD_DIM:]
        kw_ref[s] = k01[tb - WINDOW:, :].T
        vw_ref[s] = v01[tb - WINDOW:, :].T

    def gate_piece(s, st, i):
        st[("gate", i)] = _dot(st["xn"], win_ref[:, gate_cols[i]:gate_cols[i] + GATE_GROUP])

    def scores(s, st, qi, c):
        rows = slice(qi * WINDOW, (qi + 1) * WINDOW)
        win = slice(qi * WINDOW, qi * WINDOW + 2 * WINDOW)
        kvh = (2 * c) // GROUP
        qc = st["qb"][rows, c * LANES:(c + 1) * LANES]
        qpair = jnp.concatenate([jnp.where(lane_lo, qc, zero_b), jnp.where(lane_lo, zero_b, qc)], axis=0)
        sc = _dot_nt(kd_ref[s, kvh, win, :], qpair)
        p_prev, p_own, ms = [], [], []
        for half in range(2):
            head = 2 * c + half
            cols = slice(half * WINDOW, (half + 1) * WINDOW)
            sf = jnp.where(own, sc[WINDOW:, cols], sc[:WINDOW, cols]) + bias_ref[head]
            if qi == 0:
                sf = jnp.where(own | has_prev0, sf, NEG_INF)
            m = jnp.max(sf, axis=0, keepdims=True)
            e = jnp.exp(sf - m).astype(BF16)
            p_prev.append(jnp.where(own, zero_b, e))
            p_own.append(jnp.where(own, e, zero_b))
            ms.append(m)
        p = jnp.concatenate([jnp.concatenate(p_prev, axis=1), jnp.concatenate(p_own, axis=1)], axis=0)
        return p, jnp.concatenate(ms, axis=1)

    def values(s, st, qi, c, p, m):
        rows = slice(qi * WINDOW, (qi + 1) * WINDOW)
        win = slice(qi * WINDOW, qi * WINDOW + 2 * WINDOW)
        kvh = (2 * c) // GROUP
        oa = _dot(vt_ref[s, kvh, :, win], p)
        sink = jnp.where(pair_lo, sinks_ref[2 * c], sinks_ref[2 * c + 1])
        den = oa[HEAD_DIM:HEAD_DIM + 1, :] + jnp.exp(sink - m)
        o = oa[0:HEAD_DIM, :] * (1.0 / den)
        o_t = jnp.concatenate([o[:, :WINDOW], o[:, WINDOW:]], axis=0)
        att_ref[s, rows, c * LANES:(c + 1) * LANES] = o_t.T

    def score(s, st, n):
        st[("p", n)] = scores(s, st, *pairs[n])

    def value(s, st, n):
        values(s, st, *pairs[n], *st.pop(("p", n)))

    def att_finish(s, st):
        kd_ref[s, :, 0:WINDOW, :] = kd_ref[s, :, tb:tb + WINDOW, :]
        vt_ref[s, :, :, 0:WINDOW] = vt_ref[s, :, :, tb:tb + WINDOW]
        g_att = jnp.concatenate([st[("gate", 2)], st[("gate", 3)]], axis=1)
        st["att_out"] = (att_ref[s] * _silu(g_att)).astype(BF16)

    def scan(s, st):
        a3 = st["a"].reshape(clen, SUBLANES, D_RNN)
        b3 = st["bx"].reshape(clen, SUBLANES, D_RNN)
        hl, pr = [b3[0]], [a3[0]]
        for v in range(1, clen):
            hl.append(a3[v] * hl[-1] + b3[v])
            pr.append(a3[v] * pr[-1])
        h_end, p_end = hl[-1], pr[-1]
        h_in = jnp.broadcast_to(hc_ref[s], (SUBLANES, D_RNN))
        entry = h_in
        for _ in range(SUBLANES - 1):
            entry = jnp.where(sub8 == 0, h_in, pltpu.roll(h_end + p_end * entry, 1, axis=0))
        carry = (h_end + p_end * entry)[SUBLANES - 1:SUBLANES, :]
        hc_ref[s] = carry
        rnn_ref[0, s:s + 1, :] = carry
        for v in range(clen):
            hv = hl[v] + pr[v] * entry
            for c in range(ncol):
                hn_ref[s, c, pl.ds(v, SUBLANES, stride=pitch), :] = hv[:, c * LANES:(c + 1) * LANES]
        h = jnp.concatenate(
            [jnp.concatenate([hn_ref[s, c, k * pitch:k * pitch + clen, :] for k in range(SUBLANES)], axis=0)
             for c in range(ncol)], axis=1)
        st["h"] = h

    def rnn_gate(s, st):
        g_rnn = jnp.concatenate([st[("gate", 0)], st[("gate", 1)]], axis=1)
        st["rnn_out"] = (st.pop("h") * _silu(g_rnn)).astype(BF16)

    def out_rnn(s, st, j):
        st[("out", j)] = _dot(st["rnn_out"], wout_ref[0:D_RNN, j * GATE_GROUP:(j + 1) * GATE_GROUP])

    def out_att(s, st, j):
        st[("out", j)] = st[("out", j)] + _dot(st["att_out"], wout_ref[D_RNN:, j * GATE_GROUP:(j + 1) * GATE_GROUP])

    def finish(s, st):
        out = jnp.concatenate([st[("out", j)] for j in range(D_MODEL // GATE_GROUP)], axis=1)
        y_ref[s] = x_ref[s] + out * _rms_scale(out) * gpost_ref[...]

    def sample_scores(s, st):
        _sample_scores(sr, t, bb, st)

    def sample_values(s, st):
        _sample_values(sr, t, bb, st)

    tasks = dict(sample_scores=sample_scores, sample_values=sample_values, norm=norm, project=project,
                 conv=conv, gate_dots=gate_dots, gate_math=gate_math, store_kv=store_kv, gate_piece=gate_piece,
                 score=score, value=value, att_finish=att_finish, scan=scan, rnn_gate=rnn_gate, out_rnn=out_rnn, out_att=out_att, finish=finish)
    sts = [dict() for _ in range(nseq)]
    for name, s, *arg in _step_program(len(pairs)):
        tasks[name](s, sts[s], *arg)

    @pl.when(t == last_t)
    def _():
        _sample_finish(sr, gpost_ref, wout_ref)


def _layer_call(x, rb, sinks, bkt, gpre, gpost, raw, xs, bktrow, sc, h0, kc_t, vc_t):
    bsz, seq, _ = x.shape
    tb = PROMPT_BLOCK
    nt = seq // tb
    nprep = D_MODEL // PREP_ROWS
    nb = xs.shape[0]
    bb = nb // nt
    layer_t = lambda t: jnp.maximum(t - nprep, 0)
    prep_t = lambda t: jnp.minimum(t, nprep - 1)
    win_t_spec = pl.BlockSpec((bb, KV_DIM, WINDOW), lambda t: (layer_t(t), 0, 0))
    x_spec = pl.BlockSpec((bsz, tb, D_MODEL), lambda t: (0, layer_t(t), 0))
    const = lambda *shape: pl.BlockSpec(shape, lambda t: (0,) * len(shape), pipeline_mode=pl.Buffered(1))
    whole = lambda *shape: pl.BlockSpec(shape, lambda t: (0,) * len(shape))
    smem = pl.BlockSpec(memory_space=pltpu.SMEM)
    out_shapes = (
        jax.ShapeDtypeStruct((bsz, seq, D_MODEL), F32),
        jax.ShapeDtypeStruct((CONV_W - 1, bsz, D_RNN), F32),
        jax.ShapeDtypeStruct((1, bsz, D_RNN), F32),
        jax.ShapeDtypeStruct((bsz, KV_DIM, WINDOW), F32),
        jax.ShapeDtypeStruct((bsz, KV_DIM, WINDOW), F32),
        jax.ShapeDtypeStruct((nb, 1, D_MODEL), F32),
        jax.ShapeDtypeStruct((CONV_W - 1, nb, D_RNN), F32),
        jax.ShapeDtypeStruct((nb, D_RNN), F32),
        jax.ShapeDtypeStruct((nb, KV_DIM, WINDOW), F32),
        jax.ShapeDtypeStruct((nb, KV_DIM, WINDOW), F32),
    )
    return pl.pallas_call(
        functools.partial(_layer_kernel, tb=tb, nseq=bsz, bb=bb, nprep=nprep),
        grid=(nprep + nt,),
        in_specs=[
            smem, smem,
            x_spec,
            const(WINDOW, WINDOW),
            const(1, D_MODEL), const(1, D_MODEL),
            const(CONV_W, D_RNN), const(1, D_RNN), const(1, D_RNN), const(1, D_RNN), const(1, D_RNN),
            pl.BlockSpec((PREP_ROWS, D_IN), lambda t: (prep_t(t), 0)),
            const(N_RNN_BLOCKS, RNN_BLOCK, RNN_BLOCK), const(N_RNN_BLOCKS, RNN_BLOCK, RNN_BLOCK),
            pl.BlockSpec((PREP_ROWS, D_MODEL), lambda t: (prep_t(t), 0)),
            const(nb, 1, D_MODEL), const(1, LANES), const(CONV_W - 1, nb, D_RNN), const(nb, D_RNN),
            win_t_spec, win_t_spec,
        ],
        out_specs=(
            x_spec,
            whole(CONV_W - 1, bsz, D_RNN),
            whole(1, bsz, D_RNN),
            whole(bsz, KV_DIM, WINDOW),
            whole(bsz, KV_DIM, WINDOW),
            whole(nb, 1, D_MODEL), whole(CONV_W - 1, nb, D_RNN), whole(nb, D_RNN),
            win_t_spec, win_t_spec,
        ),
        out_shape=out_shapes,
        scratch_shapes=[
            pltpu.VMEM((SUBLANES, D_RNN), F32),
            pltpu.VMEM((D_MODEL, D_IN), BF16),
            pltpu.VMEM((D_RNN // GATE_GROUP, GATE_GROUP, 2 * GATE_GROUP), BF16),
            pltpu.VMEM((D_MODEL, D_MODEL), BF16),
            pltpu.VMEM((N_HEADS, WINDOW, WINDOW), F32),
            pltpu.VMEM((bsz, (CONV_W - 1) * SUBLANES, D_RNN), F32),
            pltpu.VMEM((bsz, 1, D_RNN), F32),
            pltpu.VMEM((bsz, N_KV_HEADS, tb + WINDOW, LANES), BF16),
            pltpu.VMEM((bsz, N_KV_HEADS, HEAD_DIM + ONES_ROWS, tb + WINDOW), BF16),
            pltpu.VMEM((bsz, tb, D_ATT), F32),
            pltpu.VMEM((bsz, D_RNN // LANES, tb + SUBLANES * SUBLANES, LANES), F32),
            pltpu.VMEM((bsz, D_RNN // LANES, tb + SUBLANES * SUBLANES, LANES), F32),
            pltpu.VMEM((nb * N_HEADS, LANES), F32),
            pltpu.VMEM((nb * N_HEADS, LANES), F32),
            pltpu.VMEM((nb, KV_DIM), F32),
            pltpu.VMEM((nb, KV_DIM), F32),
            pltpu.VMEM((nb, D_RNN), F32),
            pltpu.VMEM((nb, D_ATT), F32),
            pltpu.VMEM((SUBLANES, LANES), F32),
            pltpu.VMEM((SUBLANES, LANES), F32),
        ],
        compiler_params=pltpu.CompilerParams(
            dimension_semantics=("arbitrary",),
            vmem_limit_bytes=VMEM_LIMIT),
        name="layer_step",
    )(rb, sinks, x, bkt, gpre, gpost, *raw, xs, bktrow, sc, h0, kc_t, vc_t)


def _sample_setup(sr, rb_ref, sinks_ref, gpre_ref, rp_ref, win_ref, wg_ref):
    nb = sr.x.shape[0]
    sub = lax.broadcasted_iota(jnp.int32, (SUBLANES, LANES), 0)
    lane_lo = lax.broadcasted_iota(jnp.int32, (nb, LANES), 1) < HEAD_DIM
    bkt = jnp.broadcast_to(sr.bktrow[...], (SUBLANES, LANES))
    bias = jnp.zeros((SUBLANES, LANES), F32)
    sinkm = jnp.zeros((SUBLANES, LANES), F32)
    for h in range(N_HEADS):
        bias = jnp.where(sub == h, _build_bias_table(bkt, rb_ref, h), bias)
        sinkm = jnp.where(sub == h, sinks_ref[h], sinkm)
    sr.bias[...] = bias
    sr.sink[...] = sinkm

    x = sr.x[:, 0, :]
    xn = (x * _rms_scale(x) * gpre_ref[...]).astype(BF16)
    rp = rp_ref[...]
    x_rnn = _dot(xn, win_ref[:, C_XRNN:C_XRNN + D_RNN])
    xc = rp[ROW_CONV_B:ROW_CONV_B + 1]
    for tap in range(CONV_W - 1):
        xc = xc + rp[tap:tap + 1] * sr.sc[tap]
    xc = xc + rp[CONV_W - 1:CONV_W] * x_rnn
    sr.conv[0] = sr.sc[1]
    sr.conv[1] = sr.sc[2]
    sr.conv[2] = x_rnn
    a, bx = _rglru_gate_math(xc, *_rglru_gate_dots(xc, wg_ref), rp)
    h = a * sr.h0[...] + bx
    sr.rnn[...] = h
    g_rnn = _dot(xn, win_ref[:, C_GRNN:C_GRNN + D_RNN])
    sr.rnnout[...] = h * _silu(g_rnn)
    sr.gatt[...] = _dot(xn, win_ref[:, C_GATT:C_GATT + D_ATT])
    kv = _dot(xn, win_ref[:, C_KV:C_KV + 2 * KV_DIM])
    sr.knew[...] = kv[:, :KV_DIM]
    sr.vnew[...] = kv[:, KV_DIM:]
    q = _dot(xn, win_ref[:, C_Q:C_Q + D_ATT])
    for head in range(N_HEADS):
        c = head // 2
        kvh = head // GROUP
        qc = q[:, c * LANES:(c + 1) * LANES]
        if (head % 2) != kvh:
            qc = pltpu.roll(qc, HEAD_DIM, axis=1)
        qz = jnp.where(lane_lo, qc, 0.0) if kvh == 0 else jnp.where(lane_lo, 0.0, qc)
        sr.qz[pl.ds(head, nb, stride=N_HEADS), :] = qz


def _sample_scores(sr, step, bb, st):
    newest = lax.broadcasted_iota(jnp.int32, (WINDOW, LANES), 0) == WINDOW - 1
    bias = sr.bias[...]
    st["rows"] = [pl.multiple_of((step * bb + bi) * N_HEADS, N_HEADS) for bi in range(bb)]
    st["scores"] = []
    for bi in range(bb):
        newk = jnp.where(newest, sr.knew[pl.ds(step * bb + bi, 1), :], pltpu.roll(sr.kc[bi].T, WINDOW - 1, axis=0))
        sr.ko[bi] = newk.T
        qz = sr.qz[pl.ds(st["rows"][bi], N_HEADS), :].astype(BF16)
        st["scores"].append(_dot_nt(qz, newk.astype(BF16)) + bias)


def _sample_values(sr, step, bb, st):
    newest = lax.broadcasted_iota(jnp.int32, (WINDOW, LANES), 0) == WINDOW - 1
    sinkm = sr.sink[...]
    for bi, s in enumerate(st.pop("scores")):
        m = jnp.maximum(jnp.max(s, axis=1, keepdims=True), sinkm)
        e = jnp.exp(s - m)
        den = jnp.sum(e, axis=1, keepdims=True) + jnp.exp(sinkm - m)
        p = (e * (1.0 / den)).astype(BF16)
        newv = jnp.where(newest, sr.vnew[pl.ds(step * bb + bi, 1), :], pltpu.roll(sr.vc[bi].T, WINDOW - 1, axis=0))
        sr.vo[bi] = newv.T
        sr.o[pl.ds(st["rows"][bi], N_HEADS), :] = _dot(p, newv.astype(BF16))


def _sample_finish(sr, gpost_ref, wout_ref):
    nb = sr.x.shape[0]
    lane_lo = lax.broadcasted_iota(jnp.int32, (nb, LANES), 1) < HEAD_DIM
    cols = []
    for c in range(D_ATT // LANES):
        kvh = (2 * c) // GROUP
        halves = []
        for half in range(2):
            oh = sr.o[pl.ds(2 * c + half, nb, stride=N_HEADS), :]
            if half != kvh:
                oh = pltpu.roll(oh, HEAD_DIM, axis=1)
            halves.append(oh)
        cols.append(jnp.where(lane_lo, halves[0], halves[1]))
    att = jnp.concatenate(cols, axis=1)
    att_out = (att * _silu(sr.gatt[...])).astype(BF16)
    rnn_out = sr.rnnout[...].astype(BF16)
    out = _dot(rnn_out, wout_ref[0:D_RNN, :]) + _dot(att_out, wout_ref[D_RNN:, :])
    sr.y[:, 0, :] = sr.x[:, 0, :] + out * _rms_scale(out) * gpost_ref[...]


def kernel(x_prompt, x_sample, state_conv, state_rnn, cache_k_win, cache_v_win, norm_pre, norm_post,
           w_in, conv_w, conv_b, w_gate_a, b_gate_a, w_gate_x, b_gate_x, lru_lambda, attn_sinks,
           rel_bias, w_out):
    assert w_in.shape[0] == 1, "single-layer trunk"
    bsz, seq, _ = x_prompt.shape
    nb = x_sample.shape[0]
    wb = cache_k_win.shape[2]
    assert wb == WINDOW and x_sample.shape[1] == 1 and seq % PROMPT_BLOCK == 0 and nb % (seq // PROMPT_BLOCK) == 0
    assert bsz == PROMPT_SEQS, "the step program is written for this many prompt sequences"

    raw = (conv_w[0], conv_b, b_gate_a, b_gate_x, lru_lambda, w_in[0], w_gate_a[0], w_gate_x[0], w_out[0])
    gpre = norm_pre.reshape(1, D_MODEL)
    gpost = norm_post.reshape(1, D_MODEL)
    sinks = attn_sinks[0]
    bkt_np = _folded_bucket_table()
    bkt = jnp.asarray(np.ascontiguousarray(bkt_np.T))
    bktrow = jnp.asarray(bkt_np[WINDOW - 1:WINDOW, :])

    feature_major = lambda z: jnp.transpose(z[0], (0, 2, 3, 1)).reshape(nb, KV_DIM, wb)
    y_p, conv_p, rnn_p, kw_p, vw_p, y_s, conv_s, rnn_s, kw_s, vw_s = _layer_call(
        x_prompt, rel_bias, sinks, bkt, gpre, gpost, raw,
        x_sample, bktrow, jnp.transpose(state_conv[0], (1, 0, 2)), state_rnn[0],
        feature_major(cache_k_win), feature_major(cache_v_win))

    kv5 = lambda z, n: jnp.transpose(z.reshape(n, N_KV_HEADS, HEAD_DIM, wb), (0, 3, 1, 2))[None]
    return (y_p, y_s,
            jnp.transpose(conv_p, (1, 0, 2))[None], rnn_p, kv5(kw_p, bsz), kv5(vw_p, bsz),
            jnp.transpose(conv_s, (1, 0, 2))[None], rnn_s[None], kv5(kw_s, nb), kv5(vw_s, nb))
```

```python
import collections
import functools
import math

import numpy as np
import jax
import jax.numpy as jnp
from jax import lax
from jax.experimental import pallas as pl
from jax.experimental.pallas import tpu as pltpu

D_MODEL = 1024
D_RNN = 512
D_ATT = 512
HEAD_DIM = 64
N_HEADS = 8
N_KV_HEADS = 2
GROUP = N_HEADS // N_KV_HEADS
KV_DIM = N_KV_HEADS * HEAD_DIM
N_RNN_BLOCKS = 8
RNN_BLOCK = D_RNN // N_RNN_BLOCKS
CONV_W = 4
LRU_C = 8.0
WINDOW = 128
N_BUCKETS = 32
MAX_DISTANCE = 128
EPS = 1e-6
NEG_INF = -1e30
D_IN = 2 * D_RNN + 2 * D_ATT + 2 * KV_DIM

C_XRNN = 0
C_GRNN = D_RNN
C_Q = 2 * D_RNN
C_KV = 2 * D_RNN + D_ATT
C_GATT = 2 * D_RNN + D_ATT + 2 * KV_DIM

SUBLANES = 8
LANES = 128
GATE_GROUP = 256
ONES_ROWS = 16
SCORE_LOOKAHEAD = 6
PROMPT_SEQS = 2
PROMPT_BLOCK = 512
PREP_ROWS = 256
V7X_VMEM_BYTES = 64 * 1024 * 1024
VMEM_LIMIT = V7X_VMEM_BYTES - 8 * 1024 * 1024

TINY = 1e-37
ROW_CONV_B = CONV_W
ROW_GATE_A_B = CONV_W + 1
ROW_GATE_X_B = CONV_W + 2
ROW_LAMBDA = CONV_W + 3
LOG2E = 1.4426950408889634
F32 = jnp.float32
BF16 = jnp.bfloat16


def _spread(groups, bulk):
    out, done = [], 0
    for i, group in enumerate(groups):
        out += group
        want = (i + 1) * len(bulk) // len(groups)
        out += bulk[done:want]
        done = want
    return out


def _step_program(npairs):
    a, b = range(PROMPT_SEQS)
    n_out = D_MODEL // GATE_GROUP

    def attention(s):
        groups = [[("score", s, n) for n in range(SCORE_LOOKAHEAD)]]
        for n in range(npairs):
            ahead = [("score", s, n + SCORE_LOOKAHEAD)] if n + SCORE_LOOKAHEAD < npairs else []
            groups.append(ahead + [("value", s, n)])
        return groups

    project = lambda s: [("project", s, name) for name in ("x_rnn", "kv", "q")]
    gates = lambda s: [("gate_piece", s, i) for i in range((D_RNN + D_ATT) // GATE_GROUP)]
    outs = lambda s: [("out_piece", s, j) for j in range(n_out)]
    pa, pb = project(a), project(b)
    ga, gb, oa = gates(a), gates(b), outs(a)
    prog = [("sample_scores", a), ("norm", a), pa[0], ("norm", b), pa[1], pa[2], ("store_kv", a)]
    prog += [("conv_gates", a), ("gate_math", a), ("scan", a)] + pb
    prog += _spread(attention(a), ga + [("rnn_gate", a)])
    prog += [("att_finish", a), ("conv_gates", b), ("store_kv", b), ("gate_math", b), ("scan", b)] + gb
    prog += _spread(attention(b), oa + [("rnn_gate", b)])
    prog += [("finish", a), ("att_finish", b)] + outs(b) + [("sample_values", a), ("finish", b)]
    return prog


def _t5_bucket_np(dist):
    dist = np.maximum(dist, 0)
    max_exact = N_BUCKETS // 2
    d = np.maximum(dist, 1).astype(np.float32)
    ratio = np.log(d / np.float32(max_exact)) / np.float32(math.log(MAX_DISTANCE / max_exact))
    large = max_exact + (ratio * np.float32(N_BUCKETS - max_exact)).astype(np.int32)
    large = np.minimum(large, N_BUCKETS - 1)
    return np.where(dist < max_exact, dist, large).astype(np.int32)


def _folded_bucket_table():
    i = np.arange(WINDOW)[:, None]
    j = np.arange(WINDOW)[None, :]
    dist = np.where(j <= i, i - j, i + WINDOW - j)
    return _t5_bucket_np(dist)


def _sigmoid_of_half(zh):
    return 0.5 + 0.5 * jnp.tanh(zh)


def _silu(x):
    h = 0.5 * x
    return h + h * jnp.tanh(h)


def _softplus(x):
    return jnp.maximum(x, 0.0) + jnp.log1p(jnp.exp(-jnp.abs(x)))


def _rms_scale(x):
    return lax.rsqrt(jnp.mean(x * x, axis=-1, keepdims=True) + EPS)


def _dot(a, b):
    return jnp.dot(a, b, preferred_element_type=F32)


def _dot_nt(a, b):
    return lax.dot_general(a, b, (((1,), (1,)), ((), ())), preferred_element_type=F32)


def _rglru_gate_dots(xc, wg_ref):
    xcb = xc.astype(BF16)
    zs = [_dot(xcb[:, g * GATE_GROUP:(g + 1) * GATE_GROUP], wg_ref[g])
          for g in range(D_RNN // GATE_GROUP)]
    za = jnp.concatenate([z[:, :GATE_GROUP] for z in zs], axis=1)
    zx = jnp.concatenate([z[:, GATE_GROUP:] for z in zs], axis=1)
    return za, zx


def _rglru_gate_math(xc, za, zx, rp):
    i = _sigmoid_of_half(zx + rp[ROW_GATE_X_B:ROW_GATE_X_B + 1])
    rate = (-0.5 * LRU_C * LOG2E) * _softplus(-rp[ROW_LAMBDA:ROW_LAMBDA + 1])
    a = jnp.exp2((1.0 + jnp.tanh(za + rp[ROW_GATE_A_B:ROW_GATE_A_B + 1])) * rate)
    u = 1.0 - a * a
    bx = (u * lax.rsqrt(jnp.maximum(u, TINY))) * (i * xc)
    return a, bx


def _prepare_params(step, raw, rp_ref, win_ref, wg_ref, wout_ref):
    cw_raw, cb_raw, ba_raw, bx_raw, lam_raw, win_raw, wga_raw, wgx_raw, wout_raw = raw
    rows = pl.ds(pl.multiple_of(step * PREP_ROWS, PREP_ROWS), PREP_ROWS)
    for c0 in range(0, D_IN, GATE_GROUP):
        w = win_raw[:, c0:c0 + GATE_GROUP]
        if C_Q <= c0 < C_Q + D_ATT:
            w = w * HEAD_DIM ** -0.5
        win_ref[rows, c0:c0 + GATE_GROUP] = w.astype(BF16)
    wout_ref[rows, :] = wout_raw[...].astype(BF16)

    @pl.when(step == 0)
    def _():
        rp_ref[0:CONV_W, :] = cw_raw[...]
        rp_ref[ROW_CONV_B:ROW_CONV_B + 1, :] = cb_raw[...]
        rp_ref[ROW_GATE_A_B:ROW_GATE_A_B + 1, :] = 0.5 * ba_raw[...]
        rp_ref[ROW_GATE_X_B:ROW_GATE_X_B + 1, :] = 0.5 * bx_raw[...]
        rp_ref[ROW_LAMBDA:ROW_LAMBDA + 1, :] = lam_raw[...]
        per = GATE_GROUP // RNN_BLOCK
        zero = jnp.zeros((RNN_BLOCK, RNN_BLOCK), F32)
        for g in range(D_RNN // GATE_GROUP):
            for half, w_raw in enumerate((wga_raw, wgx_raw)):
                rows_g = [jnp.concatenate([0.5 * w_raw[g * per + m] if k == m else zero for k in range(per)], axis=1)
                          for m in range(per)]
                wg_ref[g, :, half * GATE_GROUP:(half + 1) * GATE_GROUP] = jnp.concatenate(rows_g, axis=0).astype(BF16)


def _build_bias_table(bkt, rb_ref, head):
    tbl = jnp.zeros(bkt.shape, F32)
    for k in range(N_BUCKETS):
        tbl = jnp.where(bkt == k, rb_ref[head, k], tbl)
    return tbl


_SampleRefs = collections.namedtuple(
    "_SampleRefs", "x bktrow sc h0 kc vc y conv rnn ko vo qz o knew vnew rnnout gatt bias sink")


def _layer_kernel(rb_ref, sinks_ref, x_ref, bkt_ref, gpre_ref, gpost_ref,
                  cw_raw, cb_raw, ba_raw, bx_raw, lam_raw, win_raw, wga_raw, wgx_raw, wout_raw,
                  xs_ref, bktrow_ref, sc_ref, h0_ref, kc_ref, vc_ref,
                  y_ref, conv_ref, rnn_ref, kw_ref, vw_ref, ys_ref, convs_ref, rnns_ref, ko_ref, vo_ref,
                  rp_ref, win_ref, wg_ref, wout_ref, *scratch, tb, nseq, bb, nprep):
    step = pl.program_id(0)

    @pl.when(step < nprep)
    def _():
        _prepare_params(step, (cw_raw, cb_raw, ba_raw, bx_raw, lam_raw, win_raw, wga_raw, wgx_raw, wout_raw),
                        rp_ref, win_ref, wg_ref, wout_ref)

        @pl.when(step == 0)
        def _():
            bias_ref = scratch[0]
            bkt = bkt_ref[...]
            for h in range(N_HEADS):
                bias_ref[h] = _build_bias_table(bkt, rb_ref, h)

    @pl.when(step >= nprep)
    def _():
        _layer_step(step - nprep, pl.num_programs(0) - nprep - 1,
                    rb_ref, sinks_ref, x_ref, bkt_ref, gpre_ref, gpost_ref, rp_ref, win_ref, wg_ref, wout_ref,
                    xs_ref, bktrow_ref, sc_ref, h0_ref, kc_ref, vc_ref,
                    y_ref, conv_ref, rnn_ref, kw_ref, vw_ref, ys_ref, convs_ref, rnns_ref, ko_ref, vo_ref,
                    *scratch, tb=tb, nseq=nseq, bb=bb)


def _layer_step(t, last_t, rb_ref, sinks_ref, x_ref, bkt_ref, gpre_ref, gpost_ref, rp_ref, win_ref, wg_ref, wout_ref,
                xs_ref, bktrow_ref, sc_ref, h0_ref, kc_ref, vc_ref,
                y_ref, conv_ref, rnn_ref, kw_ref, vw_ref, ys_ref, convs_ref, rnns_ref, ko_ref, vo_ref,
                bias_ref, xpad_ref, hc_ref, kd_ref, vt_ref, att_ref, xr_ref, hn_ref,
                qz_ref, o_ref, knew_ref, vnew_ref, rnnout_ref, gatt_ref, sbias_ref, ssink_ref, *, tb, nseq, bb):
    nq = tb // WINDOW
    sr = _SampleRefs(xs_ref, bktrow_ref, sc_ref, h0_ref, kc_ref, vc_ref, ys_ref, convs_ref, rnns_ref, ko_ref, vo_ref,
                     qz_ref, o_ref, knew_ref, vnew_ref, rnnout_ref, gatt_ref, sbias_ref, ssink_ref)

    @pl.when(t == 0)
    def _():
        xpad_ref[...] = jnp.zeros((nseq, (CONV_W - 1) * SUBLANES, D_RNN), F32)
        hc_ref[...] = jnp.zeros((nseq, 1, D_RNN), F32)
        kd_ref[:, :, 0:WINDOW, :] = jnp.zeros((nseq, N_KV_HEADS, WINDOW, LANES), BF16)
        vt_ref[:, :, 0:HEAD_DIM, 0:WINDOW] = jnp.zeros((nseq, N_KV_HEADS, HEAD_DIM, WINDOW), BF16)
        vt_ref[:, :, HEAD_DIM:, :] = jnp.ones((nseq, N_KV_HEADS, ONES_ROWS, tb + WINDOW), BF16)
        _sample_setup(sr, rb_ref, sinks_ref, gpre_ref, rp_ref, win_ref, wg_ref)

    rp = rp_ref[...]
    sub8 = lax.broadcasted_iota(jnp.int32, (SUBLANES, D_RNN), 0)
    lo = lax.broadcasted_iota(jnp.int32, (tb, LANES), 1) < HEAD_DIM
    key_idx = lax.broadcasted_iota(jnp.int32, (WINDOW, WINDOW), 0)
    qry_idx = lax.broadcasted_iota(jnp.int32, (WINDOW, WINDOW), 1)
    own = key_idx <= qry_idx
    lane_lo = qry_idx < HEAD_DIM
    has_prev0 = (jnp.zeros((WINDOW, WINDOW), jnp.int32) + t) > 0
    zero_b = jnp.zeros((WINDOW, WINDOW), BF16)
    pair_lo = lax.broadcasted_iota(jnp.int32, (1, 2 * WINDOW), 1) < WINDOW
    pairs = [(qi, c) for qi in range(nq) for c in range(D_ATT // LANES)]
    gate_cols = [C_GRNN, C_GRNN + GATE_GROUP, C_GATT, C_GATT + GATE_GROUP]

    def norm(s, st):
        x = x_ref[s]
        st["xn"] = (x * _rms_scale(x) * gpre_ref[...]).astype(BF16)

    def project(s, st, name):
        c0, width = dict(x_rnn=(C_XRNN, D_RNN), kv=(C_KV, 2 * KV_DIM), q=(C_Q, D_ATT))[name]
        st[name] = _dot(st["xn"], win_ref[:, c0:c0 + width])

    clen = tb // SUBLANES
    pitch = clen + SUBLANES
    ncol = D_RNN // LANES
    ntail = CONV_W - 1

    def conv_gates(s, st):
        x_rnn = st["x_rnn"]
        for c in range(ncol):
            for k in range(SUBLANES):
                xr_ref[s, c, k * pitch:k * pitch + clen, :] = x_rnn[k * clen:(k + 1) * clen, c * LANES:(c + 1) * LANES]
        xp = jnp.concatenate(
            [jnp.concatenate([xr_ref[s, c, pl.ds(v, SUBLANES, stride=pitch), :] for v in range(clen)], axis=0)
             for c in range(ncol)], axis=1)
        tail = xp[(clen - ntail) * SUBLANES:, :]
        prev_tail = xpad_ref[s]
        heads = [jnp.where(sub8 == 0,
                           pltpu.roll(prev_tail[j * SUBLANES:(j + 1) * SUBLANES], 1, axis=0),
                           pltpu.roll(tail[j * SUBLANES:(j + 1) * SUBLANES], 1, axis=0)) for j in range(ntail)]
        xc = rp[ROW_CONV_B:ROW_CONV_B + 1]
        for tap in range(ntail):
            shift = ntail - tap
            xc = xc + rp[tap:tap + 1] * jnp.concatenate(heads[ntail - shift:] + [xp[:(clen - shift) * SUBLANES]], axis=0)
        xc = xc + rp[ntail:ntail + 1] * xp
        xpad_ref[s] = tail
        for j in range(ntail):
            conv_ref[j, s:s + 1, :] = tail[j * SUBLANES + SUBLANES - 1:(j + 1) * SUBLANES]
        st["xc"] = xc
        st["za"], st["zx"] = _rglru_gate_dots(xc, wg_ref)

    def gate_math(s, st):
        st["a"], st["bx"] = _rglru_gate_math(st.pop("xc"), st.pop("za"), st.pop("zx"), rp)

    def store_kv(s, st):
        st["qb"] = st.pop("q").astype(BF16)
        k01 = st["kv"][:, :KV_DIM]
        v01 = st["kv"][:, KV_DIM:]
        k10 = pltpu.roll(k01, HEAD_DIM, axis=1)
        kd_ref[s, 0, WINDOW:WINDOW + tb, :] = jnp.where(lo, k01, k10).astype(BF16)
        kd_ref[s, 1, WINDOW:WINDOW + tb, :] = jnp.where(lo, k10, k01).astype(BF16)
        v_t = v01.T.astype(BF16)
        vt_ref[s, 0, 0:HEAD_DIM, WINDOW:WINDOW + tb] = v_t[0:HEAD_DIM]
        vt_ref[s, 1, 0:HEAD_DIM, WINDOW:WINDOW + tb] = v_t[HEAD_DIM:]
        kw_ref[s] = k01[tb - WINDOW:, :].T
        vw_ref[s] = v01[tb - WINDOW:, :].T

    def gate_piece(s, st, i):
        st[("gate", i)] = _dot(st["xn"], win_ref[:, gate_cols[i]:gate_cols[i] + GATE_GROUP])

    def scores(s, st, qi, c):
        rows = slice(qi * WINDOW, (qi + 1) * WINDOW)
        win = slice(qi * WINDOW, qi * WINDOW + 2 * WINDOW)
        kvh = (2 * c) // GROUP
        qc = st["qb"][rows, c * LANES:(c + 1) * LANES]
        qpair = jnp.concatenate([jnp.where(lane_lo, qc, zero_b), jnp.where(lane_lo, zero_b, qc)], axis=0)
        sc = _dot_nt(kd_ref[s, kvh, win, :], qpair)
        p_prev, p_own, ms = [], [], []
        for half in range(2):
            head = 2 * c + half
            cols = slice(half * WINDOW, (half + 1) * WINDOW)
            sf = jnp.where(own, sc[WINDOW:, cols], sc[:WINDOW, cols]) + bias_ref[head]
            if qi == 0:
                sf = jnp.where(own | has_prev0, sf, NEG_INF)
            m = jnp.max(sf, axis=0, keepdims=True)
            e = jnp.exp(sf - m).astype(BF16)
            p_prev.append(jnp.where(own, zero_b, e))
            p_own.append(jnp.where(own, e, zero_b))
            ms.append(m)
        p = jnp.concatenate([jnp.concatenate(p_prev, axis=1), jnp.concatenate(p_own, axis=1)], axis=0)
        return p, jnp.concatenate(ms, axis=1)

    def values(s, st, qi, c, p, m):
        rows = slice(qi * WINDOW, (qi + 1) * WINDOW)
        win = slice(qi * WINDOW, qi * WINDOW + 2 * WINDOW)
        kvh = (2 * c) // GROUP
        oa = _dot(vt_ref[s, kvh, :, win], p)
        sink = jnp.where(pair_lo, sinks_ref[2 * c], sinks_ref[2 * c + 1])
        den = oa[HEAD_DIM:HEAD_DIM + 1, :] + jnp.exp(sink - m)
        o = oa[0:HEAD_DIM, :] * (1.0 / den)
        o_t = jnp.concatenate([o[:, :WINDOW], o[:, WINDOW:]], axis=0)
        att_ref[s, rows, c * LANES:(c + 1) * LANES] = o_t.T

    def score(s, st, n):
        st[("p", n)] = scores(s, st, *pairs[n])

    def value(s, st, n):
        values(s, st, *pairs[n], *st.pop(("p", n)))

    def att_finish(s, st):
        kd_ref[s, :, 0:WINDOW, :] = kd_ref[s, :, tb:tb + WINDOW, :]
        vt_ref[s, :, :, 0:WINDOW] = vt_ref[s, :, :, tb:tb + WINDOW]
        g_att = jnp.concatenate([st[("gate", 2)], st[("gate", 3)]], axis=1)
        st["att_out"] = (att_ref[s] * _silu(g_att)).astype(BF16)

    def scan(s, st):
        a3 = st["a"].reshape(clen, SUBLANES, D_RNN)
        b3 = st["bx"].reshape(clen, SUBLANES, D_RNN)
        hl, pr = [b3[0]], [a3[0]]
        for v in range(1, clen):
            hl.append(a3[v] * hl[-1] + b3[v])
            pr.append(a3[v] * pr[-1])
        h_end, p_end = hl[-1], pr[-1]
        h_in = jnp.broadcast_to(hc_ref[s], (SUBLANES, D_RNN))
        entry = h_in
        for _ in range(SUBLANES - 1):
            entry = jnp.where(sub8 == 0, h_in, pltpu.roll(h_end + p_end * entry, 1, axis=0))
        carry = (h_end + p_end * entry)[SUBLANES - 1:SUBLANES, :]
        hc_ref[s] = carry
        rnn_ref[0, s:s + 1, :] = carry
        for v in range(clen):
            hv = hl[v] + pr[v] * entry
            for c in range(ncol):
                hn_ref[s, c, pl.ds(v, SUBLANES, stride=pitch), :] = hv[:, c * LANES:(c + 1) * LANES]
        h = jnp.concatenate(
            [jnp.concatenate([hn_ref[s, c, k * pitch:k * pitch + clen, :] for k in range(SUBLANES)], axis=0)
             for c in range(ncol)], axis=1)
        st["h"] = h

    def rnn_gate(s, st):
        g_rnn = jnp.concatenate([st[("gate", 0)], st[("gate", 1)]], axis=1)
        st["rnn_out"] = (st.pop("h") * _silu(g_rnn)).astype(BF16)

    def out_piece(s, st, j):
        cols = slice(j * GATE_GROUP, (j + 1) * GATE_GROUP)
        st[("out", j)] = _dot(st["rnn_out"], wout_ref[0:D_RNN, cols]) + _dot(st["att_out"], wout_ref[D_RNN:, cols])

    def finish(s, st):
        out = jnp.concatenate([st[("out", j)] for j in range(D_MODEL // GATE_GROUP)], axis=1)
        y_ref[s] = x_ref[s] + out * _rms_scale(out) * gpost_ref[...]

    def sample_scores(s, st):
        _sample_scores(sr, t, bb, st)

    def sample_values(s, st):
        _sample_values(sr, t, bb, st)

    tasks = dict(sample_scores=sample_scores, sample_values=sample_values, norm=norm, project=project,
                 conv_gates=conv_gates, gate_math=gate_math, store_kv=store_kv, gate_piece=gate_piece,
                 score=score, value=value, att_finish=att_finish, scan=scan, rnn_gate=rnn_gate, out_piece=out_piece, finish=finish)
    sts = [dict() for _ in range(nseq)]
    for name, s, *arg in _step_program(len(pairs)):
        tasks[name](s, sts[s], *arg)

    @pl.when(t == last_t)
    def _():
        _sample_finish(sr, gpost_ref, wout_ref)


def _layer_call(x, rb, sinks, bkt, gpre, gpost, raw, xs, bktrow, sc, h0, kc_t, vc_t):
    bsz, seq, _ = x.shape
    tb = PROMPT_BLOCK
    nt = seq // tb
    nprep = D_MODEL // PREP_ROWS
    nb = xs.shape[0]
    bb = nb // nt
    layer_t = lambda t: jnp.maximum(t - nprep, 0)
    prep_t = lambda t: jnp.minimum(t, nprep - 1)
    win_t_spec = pl.BlockSpec((bb, KV_DIM, WINDOW), lambda t: (layer_t(t), 0, 0))
    x_spec = pl.BlockSpec((bsz, tb, D_MODEL), lambda t: (0, layer_t(t), 0))
    const = lambda *shape: pl.BlockSpec(shape, lambda t: (0,) * len(shape), pipeline_mode=pl.Buffered(1))
    whole = lambda *shape: pl.BlockSpec(shape, lambda t: (0,) * len(shape))
    smem = pl.BlockSpec(memory_space=pltpu.SMEM)
    out_shapes = (
        jax.ShapeDtypeStruct((bsz, seq, D_MODEL), F32),
        jax.ShapeDtypeStruct((CONV_W - 1, bsz, D_RNN), F32),
        jax.ShapeDtypeStruct((1, bsz, D_RNN), F32),
        jax.ShapeDtypeStruct((bsz, KV_DIM, WINDOW), F32),
        jax.ShapeDtypeStruct((bsz, KV_DIM, WINDOW), F32),
        jax.ShapeDtypeStruct((nb, 1, D_MODEL), F32),
        jax.ShapeDtypeStruct((CONV_W - 1, nb, D_RNN), F32),
        jax.ShapeDtypeStruct((nb, D_RNN), F32),
        jax.ShapeDtypeStruct((nb, KV_DIM, WINDOW), F32),
        jax.ShapeDtypeStruct((nb, KV_DIM, WINDOW), F32),
    )
    return pl.pallas_call(
        functools.partial(_layer_kernel, tb=tb, nseq=bsz, bb=bb, nprep=nprep),
        grid=(nprep + nt,),
        in_specs=[
            smem, smem,
            x_spec,
            const(WINDOW, WINDOW),
            const(1, D_MODEL), const(1, D_MODEL),
            const(CONV_W, D_RNN), const(1, D_RNN), const(1, D_RNN), const(1, D_RNN), const(1, D_RNN),
            pl.BlockSpec((PREP_ROWS, D_IN), lambda t: (prep_t(t), 0)),
            const(N_RNN_BLOCKS, RNN_BLOCK, RNN_BLOCK), const(N_RNN_BLOCKS, RNN_BLOCK, RNN_BLOCK),
            pl.BlockSpec((PREP_ROWS, D_MODEL), lambda t: (prep_t(t), 0)),
            const(nb, 1, D_MODEL), const(1, LANES), const(CONV_W - 1, nb, D_RNN), const(nb, D_RNN),
            win_t_spec, win_t_spec,
        ],
        out_specs=(
            x_spec,
            whole(CONV_W - 1, bsz, D_RNN),
            whole(1, bsz, D_RNN),
            whole(bsz, KV_DIM, WINDOW),
            whole(bsz, KV_DIM, WINDOW),
            whole(nb, 1, D_MODEL), whole(CONV_W - 1, nb, D_RNN), whole(nb, D_RNN),
            win_t_spec, win_t_spec,
        ),
        out_shape=out_shapes,
        scratch_shapes=[
            pltpu.VMEM((SUBLANES, D_RNN), F32),
            pltpu.VMEM((D_MODEL, D_IN), BF16),
            pltpu.VMEM((D_RNN // GATE_GROUP, GATE_GROUP, 2 * GATE_GROUP), BF16),
            pltpu.VMEM((D_MODEL, D_MODEL), BF16),
            pltpu.VMEM((N_HEADS, WINDOW, WINDOW), F32),
            pltpu.VMEM((bsz, (CONV_W - 1) * SUBLANES, D_RNN), F32),
            pltpu.VMEM((bsz, 1, D_RNN), F32),
            pltpu.VMEM((bsz, N_KV_HEADS, tb + WINDOW, LANES), BF16),
            pltpu.VMEM((bsz, N_KV_HEADS, HEAD_DIM + ONES_ROWS, tb + WINDOW), BF16),
            pltpu.VMEM((bsz, tb, D_ATT), F32),
            pltpu.VMEM((bsz, D_RNN // LANES, tb + SUBLANES * SUBLANES, LANES), F32),
            pltpu.VMEM((bsz, D_RNN // LANES, tb + SUBLANES * SUBLANES, LANES), F32),
            pltpu.VMEM((nb * N_HEADS, LANES), F32),
            pltpu.VMEM((nb * N_HEADS, LANES), F32),
            pltpu.VMEM((nb, KV_DIM), F32),
            pltpu.VMEM((nb, KV_DIM), F32),
            pltpu.VMEM((nb, D_RNN), F32),
            pltpu.VMEM((nb, D_ATT), F32),
            pltpu.VMEM((SUBLANES, LANES), F32),
            pltpu.VMEM((SUBLANES, LANES), F32),
        ],
        compiler_params=pltpu.CompilerParams(
            dimension_semantics=("arbitrary",),
            vmem_limit_bytes=VMEM_LIMIT),
        name="layer_step",
    )(rb, sinks, x, bkt, gpre, gpost, *raw, xs, bktrow, sc, h0, kc_t, vc_t)


def _sample_setup(sr, rb_ref, sinks_ref, gpre_ref, rp_ref, win_ref, wg_ref):
    nb = sr.x.shape[0]
    sub = lax.broadcasted_iota(jnp.int32, (SUBLANES, LANES), 0)
    lane_lo = lax.broadcasted_iota(jnp.int32, (nb, LANES), 1) < HEAD_DIM
    bkt = jnp.broadcast_to(sr.bktrow[...], (SUBLANES, LANES))
    bias = jnp.zeros((SUBLANES, LANES), F32)
    sinkm = jnp.zeros((SUBLANES, LANES), F32)
    for h in range(N_HEADS):
        bias = jnp.where(sub == h, _build_bias_table(bkt, rb_ref, h), bias)
        sinkm = jnp.where(sub == h, sinks_ref[h], sinkm)
    sr.bias[...] = bias
    sr.sink[...] = sinkm

    x = sr.x[:, 0, :]
    xn = (x * _rms_scale(x) * gpre_ref[...]).astype(BF16)
    rp = rp_ref[...]
    x_rnn = _dot(xn, win_ref[:, C_XRNN:C_XRNN + D_RNN])
    xc = rp[ROW_CONV_B:ROW_CONV_B + 1]
    for tap in range(CONV_W - 1):
        xc = xc + rp[tap:tap + 1] * sr.sc[tap]
    xc = xc + rp[CONV_W - 1:CONV_W] * x_rnn
    sr.conv[0] = sr.sc[1]
    sr.conv[1] = sr.sc[2]
    sr.conv[2] = x_rnn
    a, bx = _rglru_gate_math(xc, *_rglru_gate_dots(xc, wg_ref), rp)
    h = a * sr.h0[...] + bx
    sr.rnn[...] = h
    g_rnn = _dot(xn, win_ref[:, C_GRNN:C_GRNN + D_RNN])
    sr.rnnout[...] = h * _silu(g_rnn)
    sr.gatt[...] = _dot(xn, win_ref[:, C_GATT:C_GATT + D_ATT])
    kv = _dot(xn, win_ref[:, C_KV:C_KV + 2 * KV_DIM])
    sr.knew[...] = kv[:, :KV_DIM]
    sr.vnew[...] = kv[:, KV_DIM:]
    q = _dot(xn, win_ref[:, C_Q:C_Q + D_ATT])
    for head in range(N_HEADS):
        c = head // 2
        kvh = head // GROUP
        qc = q[:, c * LANES:(c + 1) * LANES]
        if (head % 2) != kvh:
            qc = pltpu.roll(qc, HEAD_DIM, axis=1)
        qz = jnp.where(lane_lo, qc, 0.0) if kvh == 0 else jnp.where(lane_lo, 0.0, qc)
        sr.qz[pl.ds(head, nb, stride=N_HEADS), :] = qz


def _sample_scores(sr, step, bb, st):
    newest = lax.broadcasted_iota(jnp.int32, (WINDOW, LANES), 0) == WINDOW - 1
    bias = sr.bias[...]
    st["rows"] = [pl.multiple_of((step * bb + bi) * N_HEADS, N_HEADS) for bi in range(bb)]
    st["scores"] = []
    for bi in range(bb):
        newk = jnp.where(newest, sr.knew[pl.ds(step * bb + bi, 1), :], pltpu.roll(sr.kc[bi].T, WINDOW - 1, axis=0))
        sr.ko[bi] = newk.T
        qz = sr.qz[pl.ds(st["rows"][bi], N_HEADS), :].astype(BF16)
        st["scores"].append(_dot_nt(qz, newk.astype(BF16)) + bias)


def _sample_values(sr, step, bb, st):
    newest = lax.broadcasted_iota(jnp.int32, (WINDOW, LANES), 0) == WINDOW - 1
    sinkm = sr.sink[...]
    for bi, s in enumerate(st.pop("scores")):
        m = jnp.maximum(jnp.max(s, axis=1, keepdims=True), sinkm)
        e = jnp.exp(s - m)
        den = jnp.sum(e, axis=1, keepdims=True) + jnp.exp(sinkm - m)
        p = (e * (1.0 / den)).astype(BF16)
        newv = jnp.where(newest, sr.vnew[pl.ds(step * bb + bi, 1), :], pltpu.roll(sr.vc[bi].T, WINDOW - 1, axis=0))
        sr.vo[bi] = newv.T
        sr.o[pl.ds(st["rows"][bi], N_HEADS), :] = _dot(p, newv.astype(BF16))


def _sample_finish(sr, gpost_ref, wout_ref):
    nb = sr.x.shape[0]
    lane_lo = lax.broadcasted_iota(jnp.int32, (nb, LANES), 1) < HEAD_DIM
    cols = []
    for c in range(D_ATT // LANES):
        kvh = (2 * c) // GROUP
        halves = []
        for half in range(2):
            oh = sr.o[pl.ds(2 * c + half, nb, stride=N_HEADS), :]
            if half != kvh:
                oh = pltpu.roll(oh, HEAD_DIM, axis=1)
            halves.append(oh)
        cols.append(jnp.where(lane_lo, halves[0], halves[1]))
    att = jnp.concatenate(cols, axis=1)
    att_out = (att * _silu(sr.gatt[...])).astype(BF16)
    rnn_out = sr.rnnout[...].astype(BF16)
    out = _dot(rnn_out, wout_ref[0:D_RNN, :]) + _dot(att_out, wout_ref[D_RNN:, :])
    sr.y[:, 0, :] = sr.x[:, 0, :] + out * _rms_scale(out) * gpost_ref[...]


def kernel(x_prompt, x_sample, state_conv, state_rnn, cache_k_win, cache_v_win, norm_pre, norm_post,
           w_in, conv_w, conv_b, w_gate_a, b_gate_a, w_gate_x, b_gate_x, lru_lambda, attn_sinks,
           rel_bias, w_out):
    assert w_in.shape[0] == 1, "single-layer trunk"
    bsz, seq, _ = x_prompt.shape
    nb = x_sample.shape[0]
    wb = cache_k_win.shape[2]
    assert wb == WINDOW and x_sample.shape[1] == 1 and seq % PROMPT_BLOCK == 0 and nb % (seq // PROMPT_BLOCK) == 0
    assert bsz == PROMPT_SEQS, "the step program is written for this many prompt sequences"

    raw = (conv_w[0], conv_b, b_gate_a, b_gate_x, lru_lambda, w_in[0], w_gate_a[0], w_gate_x[0], w_out[0])
    gpre = norm_pre.reshape(1, D_MODEL)
    gpost = norm_post.reshape(1, D_MODEL)
    sinks = attn_sinks[0]
    bkt_np = _folded_bucket_table()
    bkt = jnp.asarray(np.ascontiguousarray(bkt_np.T))
    bktrow = jnp.asarray(bkt_np[WINDOW - 1:WINDOW, :])

    feature_major = lambda z: jnp.transpose(z[0], (0, 2, 3, 1)).reshape(nb, KV_DIM, wb)
    y_p, conv_p, rnn_p, kw_p, vw_p, y_s, conv_s, rnn_s, kw_s, vw_s = _layer_call(
        x_prompt, rel_bias.T, sinks, bkt, gpre, gpost, raw,
        x_sample, bktrow, jnp.transpose(state_conv[0], (1, 0, 2)), state_rnn[0],
        feature_major(cache_k_win), feature_major(cache_v_win))

    kv5 = lambda z, n: jnp.transpose(z.reshape(n, N_KV_HEADS, HEAD_DIM, wb), (0, 3, 1, 2))[None]
    return (y_p, y_s,
            jnp.transpose(conv_p, (1, 0, 2))[None], rnn_p, kv5(kw_p, bsz), kv5(vw_p, bsz),
            jnp.transpose(conv_s, (1, 0, 2))[None], rnn_s[None], kv5(kw_s, nb), kv5(vw_s, nb))
```

```python
import collections
import functools
import math

import numpy as np
import jax
import jax.numpy as jnp
from jax import lax
from jax.experimental import pallas as pl
from jax.experimental.pallas import tpu as pltpu

D_MODEL = 1024
D_RNN = 512
D_ATT = 512
HEAD_DIM = 64
N_HEADS = 8
N_KV_HEADS = 2
GROUP = N_HEADS // N_KV_HEADS
KV_DIM = N_KV_HEADS * HEAD_DIM
N_RNN_BLOCKS = 8
RNN_BLOCK = D_RNN // N_RNN_BLOCKS
CONV_W = 4
LRU_C = 8.0
WINDOW = 128
N_BUCKETS = 32
MAX_DISTANCE = 128
EPS = 1e-6
NEG_INF = -1e30
D_IN = 2 * D_RNN + 2 * D_ATT + 2 * KV_DIM

C_XRNN = 0
C_GRNN = D_RNN
C_Q = 2 * D_RNN
C_KV = 2 * D_RNN + D_ATT
C_GATT = 2 * D_RNN + D_ATT + 2 * KV_DIM

SUBLANES = 8
LANES = 128
GATE_GROUP = 256
ONES_ROWS = 16
SCORE_LOOKAHEAD = 6
PROMPT_SEQS = 2
PROMPT_BLOCK = 512
PREP_ROWS = 256
V7X_VMEM_BYTES = 64 * 1024 * 1024
VMEM_LIMIT = V7X_VMEM_BYTES - 8 * 1024 * 1024

TINY = 1e-37
ROW_CONV_B = CONV_W
ROW_GATE_A_B = CONV_W + 1
ROW_GATE_X_B = CONV_W + 2
ROW_LAMBDA = CONV_W + 3
LOG2E = 1.4426950408889634
F32 = jnp.float32
BF16 = jnp.bfloat16


def _spread(groups, bulk):
    out, done = [], 0
    for i, group in enumerate(groups):
        out += group
        want = (i + 1) * len(bulk) // len(groups)
        out += bulk[done:want]
        done = want
    return out


def _step_program(npairs):
    a, b = range(PROMPT_SEQS)
    n_out = D_MODEL // GATE_GROUP

    def attention(s):
        groups = [[("score", s, n) for n in range(SCORE_LOOKAHEAD)]]
        for n in range(npairs):
            ahead = [("score", s, n + SCORE_LOOKAHEAD)] if n + SCORE_LOOKAHEAD < npairs else []
            groups.append(ahead + [("value", s, n)])
        return groups

    project = lambda s: [("project", s, name) for name in ("x_rnn", "kv", "q")]
    gates = lambda s: [("gate_piece", s, i) for i in range((D_RNN + D_ATT) // GATE_GROUP)]
    outs = lambda s: [("out_piece", s, j) for j in range(n_out)]
    pa, pb = project(a), project(b)
    ga, gb, oa = gates(a), gates(b), outs(a)
    prog = [("sample_scores", a), ("norm", a), pa[0], ("norm", b), pa[1], pa[2], ("store_kv", a)]
    prog += [("conv_gates", a), ("gate_math", a), ("scan", a)] + pb
    prog += _spread(attention(a), ga + [("rnn_gate", a)])
    prog += [("att_finish", a), ("conv_gates", b), ("store_kv", b), ("gate_math", b), ("scan", b)] + gb
    prog += _spread(attention(b), oa + [("rnn_gate", b)])
    prog += [("finish", a), ("att_finish", b)] + outs(b) + [("sample_values", a), ("finish", b)]
    return prog


def _t5_bucket_np(dist):
    dist = np.maximum(dist, 0)
    max_exact = N_BUCKETS // 2
    d = np.maximum(dist, 1).astype(np.float32)
    ratio = np.log(d / np.float32(max_exact)) / np.float32(math.log(MAX_DISTANCE / max_exact))
    large = max_exact + (ratio * np.float32(N_BUCKETS - max_exact)).astype(np.int32)
    large = np.minimum(large, N_BUCKETS - 1)
    return np.where(dist < max_exact, dist, large).astype(np.int32)


def _folded_bucket_table():
    i = np.arange(WINDOW)[:, None]
    j = np.arange(WINDOW)[None, :]
    dist = np.where(j <= i, i - j, i + WINDOW - j)
    return _t5_bucket_np(dist)


def _silu(x):
    h = 0.5 * x
    return h + h * jnp.tanh(h)


def _softplus(x):
    return jnp.maximum(x, 0.0) + jnp.log1p(jnp.exp(-jnp.abs(x)))


def _rms_scale(x):
    return lax.rsqrt(jnp.mean(x * x, axis=-1, keepdims=True) + EPS)


def _dot(a, b):
    return jnp.dot(a, b, preferred_element_type=F32)


def _dot_nt(a, b):
    return lax.dot_general(a, b, (((1,), (1,)), ((), ())), preferred_element_type=F32)


def _rglru_gate_dots(xh, wg_ref):
    xcb = xh.astype(BF16)
    zs = [_dot(xcb[:, g * GATE_GROUP:(g + 1) * GATE_GROUP], wg_ref[g])
          for g in range(D_RNN // GATE_GROUP)]
    za = jnp.concatenate([z[:, :GATE_GROUP] for z in zs], axis=1)
    zx = jnp.concatenate([z[:, GATE_GROUP:] for z in zs], axis=1)
    return za, zx


def _rglru_gate_math(xh, za, zx, rp):
    ix = (1.0 + jnp.tanh(zx + rp[ROW_GATE_X_B:ROW_GATE_X_B + 1])) * xh
    rate = (-0.5 * LRU_C * LOG2E) * _softplus(-rp[ROW_LAMBDA:ROW_LAMBDA + 1])
    a = jnp.exp2((1.0 + jnp.tanh(za + rp[ROW_GATE_A_B:ROW_GATE_A_B + 1])) * rate)
    u = 1.0 - a * a
    bx = (u * lax.rsqrt(jnp.maximum(u, TINY))) * ix
    return a, bx


def _prepare_params(step, raw, rp_ref, win_ref, wg_ref, wout_ref):
    cw_raw, cb_raw, ba_raw, bx_raw, lam_raw, win_raw, wga_raw, wgx_raw, wout_raw = raw
    rows = pl.ds(pl.multiple_of(step * PREP_ROWS, PREP_ROWS), PREP_ROWS)
    for c0 in range(0, D_IN, GATE_GROUP):
        w = win_raw[:, c0:c0 + GATE_GROUP]
        if C_Q <= c0 < C_Q + D_ATT:
            w = w * HEAD_DIM ** -0.5
        win_ref[rows, c0:c0 + GATE_GROUP] = w.astype(BF16)
    wout_ref[rows, :] = wout_raw[...].astype(BF16)

    @pl.when(step == 0)
    def _():
        rp_ref[0:CONV_W, :] = 0.5 * cw_raw[...]
        rp_ref[ROW_CONV_B:ROW_CONV_B + 1, :] = 0.5 * cb_raw[...]
        rp_ref[ROW_GATE_A_B:ROW_GATE_A_B + 1, :] = 0.5 * ba_raw[...]
        rp_ref[ROW_GATE_X_B:ROW_GATE_X_B + 1, :] = 0.5 * bx_raw[...]
        rp_ref[ROW_LAMBDA:ROW_LAMBDA + 1, :] = lam_raw[...]
        per = GATE_GROUP // RNN_BLOCK
        zero = jnp.zeros((RNN_BLOCK, RNN_BLOCK), F32)
        for g in range(D_RNN // GATE_GROUP):
            for half, w_raw in enumerate((wga_raw, wgx_raw)):
                rows_g = [jnp.concatenate([w_raw[g * per + m] if k == m else zero for k in range(per)], axis=1)
                          for m in range(per)]
                wg_ref[g, :, half * GATE_GROUP:(half + 1) * GATE_GROUP] = jnp.concatenate(rows_g, axis=0).astype(BF16)


def _build_bias_table(bkt, rb_ref, head):
    tbl = jnp.zeros(bkt.shape, F32)
    for k in range(N_BUCKETS):
        tbl = jnp.where(bkt == k, rb_ref[head, k], tbl)
    return tbl


_SampleRefs = collections.namedtuple(
    "_SampleRefs", "x bktrow sc h0 kc vc y conv rnn ko vo qz o knew vnew rnnout gatt bias sink")


def _layer_kernel(rb_ref, sinks_ref, x_ref, bkt_ref, gpre_ref, gpost_ref,
                  cw_raw, cb_raw, ba_raw, bx_raw, lam_raw, win_raw, wga_raw, wgx_raw, wout_raw,
                  xs_ref, bktrow_ref, sc_ref, h0_ref, kc_ref, vc_ref,
                  y_ref, conv_ref, rnn_ref, kw_ref, vw_ref, ys_ref, convs_ref, rnns_ref, ko_ref, vo_ref,
                  rp_ref, win_ref, wg_ref, wout_ref, *scratch, tb, nseq, bb, nprep):
    step = pl.program_id(0)

    @pl.when(step < nprep)
    def _():
        _prepare_params(step, (cw_raw, cb_raw, ba_raw, bx_raw, lam_raw, win_raw, wga_raw, wgx_raw, wout_raw),
                        rp_ref, win_ref, wg_ref, wout_ref)

        @pl.when(step == 0)
        def _():
            bias_ref = scratch[0]
            bkt = bkt_ref[...]
            for h in range(N_HEADS):
                bias_ref[h] = _build_bias_table(bkt, rb_ref, h)

    @pl.when(step >= nprep)
    def _():
        _layer_step(step - nprep, pl.num_programs(0) - nprep - 1,
                    rb_ref, sinks_ref, x_ref, bkt_ref, gpre_ref, gpost_ref, rp_ref, win_ref, wg_ref, wout_ref,
                    xs_ref, bktrow_ref, sc_ref, h0_ref, kc_ref, vc_ref,
                    y_ref, conv_ref, rnn_ref, kw_ref, vw_ref, ys_ref, convs_ref, rnns_ref, ko_ref, vo_ref,
                    *scratch, tb=tb, nseq=nseq, bb=bb)


def _layer_step(t, last_t, rb_ref, sinks_ref, x_ref, bkt_ref, gpre_ref, gpost_ref, rp_ref, win_ref, wg_ref, wout_ref,
                xs_ref, bktrow_ref, sc_ref, h0_ref, kc_ref, vc_ref,
                y_ref, conv_ref, rnn_ref, kw_ref, vw_ref, ys_ref, convs_ref, rnns_ref, ko_ref, vo_ref,
                bias_ref, xpad_ref, hc_ref, kd_ref, vt_ref, att_ref, xr_ref, hn_ref,
                qz_ref, o_ref, knew_ref, vnew_ref, rnnout_ref, gatt_ref, sbias_ref, ssink_ref, *, tb, nseq, bb):
    nq = tb // WINDOW
    sr = _SampleRefs(xs_ref, bktrow_ref, sc_ref, h0_ref, kc_ref, vc_ref, ys_ref, convs_ref, rnns_ref, ko_ref, vo_ref,
                     qz_ref, o_ref, knew_ref, vnew_ref, rnnout_ref, gatt_ref, sbias_ref, ssink_ref)

    @pl.when(t == 0)
    def _():
        xpad_ref[...] = jnp.zeros((nseq, (CONV_W - 1) * SUBLANES, D_RNN), F32)
        hc_ref[...] = jnp.zeros((nseq, 1, D_RNN), F32)
        kd_ref[:, :, 0:WINDOW, :] = jnp.zeros((nseq, N_KV_HEADS, WINDOW, LANES), BF16)
        vt_ref[:, :, 0:HEAD_DIM, 0:WINDOW] = jnp.zeros((nseq, N_KV_HEADS, HEAD_DIM, WINDOW), BF16)
        vt_ref[:, :, HEAD_DIM:, :] = jnp.ones((nseq, N_KV_HEADS, ONES_ROWS, tb + WINDOW), BF16)
        _sample_setup(sr, rb_ref, sinks_ref, gpre_ref, rp_ref, win_ref, wg_ref)

    rp = rp_ref[...]
    sub8 = lax.broadcasted_iota(jnp.int32, (SUBLANES, D_RNN), 0)
    lo = lax.broadcasted_iota(jnp.int32, (tb, LANES), 1) < HEAD_DIM
    key_idx = lax.broadcasted_iota(jnp.int32, (WINDOW, WINDOW), 0)
    qry_idx = lax.broadcasted_iota(jnp.int32, (WINDOW, WINDOW), 1)
    own = key_idx <= qry_idx
    lane_lo = qry_idx < HEAD_DIM
    has_prev0 = (jnp.zeros((WINDOW, WINDOW), jnp.int32) + t) > 0
    zero_b = jnp.zeros((WINDOW, WINDOW), BF16)
    pair_lo = lax.broadcasted_iota(jnp.int32, (1, 2 * WINDOW), 1) < WINDOW
    pairs = [(qi, c) for qi in range(nq) for c in range(D_ATT // LANES)]
    gate_cols = [C_GRNN, C_GRNN + GATE_GROUP, C_GATT, C_GATT + GATE_GROUP]

    def norm(s, st):
        x = x_ref[s]
        st["xn"] = (x * _rms_scale(x) * gpre_ref[...]).astype(BF16)

    def project(s, st, name):
        c0, width = dict(x_rnn=(C_XRNN, D_RNN), kv=(C_KV, 2 * KV_DIM), q=(C_Q, D_ATT))[name]
        st[name] = _dot(st["xn"], win_ref[:, c0:c0 + width])

    clen = tb // SUBLANES
    pitch = clen + SUBLANES
    ncol = D_RNN // LANES
    ntail = CONV_W - 1

    def conv_gates(s, st):
        x_rnn = st["x_rnn"]
        for c in range(ncol):
            for k in range(SUBLANES):
                xr_ref[s, c, k * pitch:k * pitch + clen, :] = x_rnn[k * clen:(k + 1) * clen, c * LANES:(c + 1) * LANES]
        xp = jnp.concatenate(
            [jnp.concatenate([xr_ref[s, c, pl.ds(v, SUBLANES, stride=pitch), :] for v in range(clen)], axis=0)
             for c in range(ncol)], axis=1)
        tail = xp[(clen - ntail) * SUBLANES:, :]
        prev_tail = xpad_ref[s]
        heads = [jnp.where(sub8 == 0,
                           pltpu.roll(prev_tail[j * SUBLANES:(j + 1) * SUBLANES], 1, axis=0),
                           pltpu.roll(tail[j * SUBLANES:(j + 1) * SUBLANES], 1, axis=0)) for j in range(ntail)]
        xh = rp[ROW_CONV_B:ROW_CONV_B + 1]
        for tap in range(ntail):
            shift = ntail - tap
            xh = xh + rp[tap:tap + 1] * jnp.concatenate(heads[ntail - shift:] + [xp[:(clen - shift) * SUBLANES]], axis=0)
        xh = xh + rp[ntail:ntail + 1] * xp
        xpad_ref[s] = tail
        for j in range(ntail):
            conv_ref[j, s:s + 1, :] = tail[j * SUBLANES + SUBLANES - 1:(j + 1) * SUBLANES]
        st["xh"] = xh
        st["za"], st["zx"] = _rglru_gate_dots(xh, wg_ref)

    def gate_math(s, st):
        st["a"], st["bx"] = _rglru_gate_math(st.pop("xh"), st.pop("za"), st.pop("zx"), rp)

    def store_kv(s, st):
        st["qb"] = st.pop("q").astype(BF16)
        k01 = st["kv"][:, :KV_DIM]
        v01 = st["kv"][:, KV_DIM:]
        k10 = pltpu.roll(k01, HEAD_DIM, axis=1)
        kd_ref[s, 0, WINDOW:WINDOW + tb, :] = jnp.where(lo, k01, k10).astype(BF16)
        kd_ref[s, 1, WINDOW:WINDOW + tb, :] = jnp.where(lo, k10, k01).astype(BF16)
        v_t = v01.T.astype(BF16)
        vt_ref[s, 0, 0:HEAD_DIM, WINDOW:WINDOW + tb] = v_t[0:HEAD_DIM]
        vt_ref[s, 1, 0:HEAD_DIM, WINDOW:WINDOW + tb] = v_t[HEAD_DIM:]
        kw_ref[s] = k01[tb - WINDOW:, :].T
        vw_ref[s] = v01[tb - WINDOW:, :].T

    def gate_piece(s, st, i):
        st[("gate", i)] = _dot(st["xn"], win_ref[:, gate_cols[i]:gate_cols[i] + GATE_GROUP])

    def scores(s, st, qi, c):
        rows = slice(qi * WINDOW, (qi + 1) * WINDOW)
        win = slice(qi * WINDOW, qi * WINDOW + 2 * WINDOW)
        kvh = (2 * c) // GROUP
        qc = st["qb"][rows, c * LANES:(c + 1) * LANES]
        qpair = jnp.concatenate([jnp.where(lane_lo, qc, zero_b), jnp.where(lane_lo, zero_b, qc)], axis=0)
        sc = _dot_nt(kd_ref[s, kvh, win, :], qpair)
        p_prev, p_own, ms = [], [], []
        for half in range(2):
            head = 2 * c + half
            cols = slice(half * WINDOW, (half + 1) * WINDOW)
            sf = jnp.where(own, sc[WINDOW:, cols], sc[:WINDOW, cols]) + bias_ref[head]
            if qi == 0:
                sf = jnp.where(own | has_prev0, sf, NEG_INF)
            m = jnp.max(sf, axis=0, keepdims=True)
            e = jnp.exp(sf - m).astype(BF16)
            p_prev.append(jnp.where(own, zero_b, e))
            p_own.append(jnp.where(own, e, zero_b))
            ms.append(m)
        p = jnp.concatenate([jnp.concatenate(p_prev, axis=1), jnp.concatenate(p_own, axis=1)], axis=0)
        return p, jnp.concatenate(ms, axis=1)

    def values(s, st, qi, c, p, m):
        rows = slice(qi * WINDOW, (qi + 1) * WINDOW)
        win = slice(qi * WINDOW, qi * WINDOW + 2 * WINDOW)
        kvh = (2 * c) // GROUP
        oa = _dot(vt_ref[s, kvh, :, win], p)
        sink = jnp.where(pair_lo, sinks_ref[2 * c], sinks_ref[2 * c + 1])
        den = oa[HEAD_DIM:HEAD_DIM + 1, :] + jnp.exp(sink - m)
        o = oa[0:HEAD_DIM, :] * (1.0 / den)
        o_t = jnp.concatenate([o[:, :WINDOW], o[:, WINDOW:]], axis=0)
        att_ref[s, rows, c * LANES:(c + 1) * LANES] = o_t.T

    def score(s, st, n):
        st[("p", n)] = scores(s, st, *pairs[n])

    def value(s, st, n):
        values(s, st, *pairs[n], *st.pop(("p", n)))

    def att_finish(s, st):
        kd_ref[s, :, 0:WINDOW, :] = kd_ref[s, :, tb:tb + WINDOW, :]
        vt_ref[s, :, :, 0:WINDOW] = vt_ref[s, :, :, tb:tb + WINDOW]
        g_att = jnp.concatenate([st[("gate", 2)], st[("gate", 3)]], axis=1)
        st["att_out"] = (att_ref[s] * _silu(g_att)).astype(BF16)

    def scan(s, st):
        a3 = st["a"].reshape(clen, SUBLANES, D_RNN)
        b3 = st["bx"].reshape(clen, SUBLANES, D_RNN)
        hl, pr = [b3[0]], [a3[0]]
        for v in range(1, clen):
            hl.append(a3[v] * hl[-1] + b3[v])
            pr.append(a3[v] * pr[-1])
        h_end, p_end = hl[-1], pr[-1]
        h_in = jnp.broadcast_to(hc_ref[s], (SUBLANES, D_RNN))
        entry = h_in
        for _ in range(SUBLANES - 1):
            entry = jnp.where(sub8 == 0, h_in, pltpu.roll(h_end + p_end * entry, 1, axis=0))
        carry = (h_end + p_end * entry)[SUBLANES - 1:SUBLANES, :]
        hc_ref[s] = carry
        rnn_ref[0, s:s + 1, :] = carry
        for v in range(clen):
            hv = hl[v] + pr[v] * entry
            for c in range(ncol):
                hn_ref[s, c, pl.ds(v, SUBLANES, stride=pitch), :] = hv[:, c * LANES:(c + 1) * LANES]
        h = jnp.concatenate(
            [jnp.concatenate([hn_ref[s, c, k * pitch:k * pitch + clen, :] for k in range(SUBLANES)], axis=0)
             for c in range(ncol)], axis=1)
        st["h"] = h

    def rnn_gate(s, st):
        g_rnn = jnp.concatenate([st[("gate", 0)], st[("gate", 1)]], axis=1)
        st["rnn_out"] = (st.pop("h") * _silu(g_rnn)).astype(BF16)

    def out_piece(s, st, j):
        cols = slice(j * GATE_GROUP, (j + 1) * GATE_GROUP)
        st[("out", j)] = _dot(st["rnn_out"], wout_ref[0:D_RNN, cols]) + _dot(st["att_out"], wout_ref[D_RNN:, cols])

    def finish(s, st):
        out = jnp.concatenate([st[("out", j)] for j in range(D_MODEL // GATE_GROUP)], axis=1)
        y_ref[s] = x_ref[s] + out * _rms_scale(out) * gpost_ref[...]

    def sample_scores(s, st):
        _sample_scores(sr, t, bb, st)

    def sample_values(s, st):
        _sample_values(sr, t, bb, st)

    tasks = dict(sample_scores=sample_scores, sample_values=sample_values, norm=norm, project=project,
                 conv_gates=conv_gates, gate_math=gate_math, store_kv=store_kv, gate_piece=gate_piece,
                 score=score, value=value, att_finish=att_finish, scan=scan, rnn_gate=rnn_gate, out_piece=out_piece, finish=finish)
    sts = [dict() for _ in range(nseq)]
    for name, s, *arg in _step_program(len(pairs)):
        tasks[name](s, sts[s], *arg)

    @pl.when(t == last_t)
    def _():
        _sample_finish(sr, gpost_ref, wout_ref)


def _layer_call(x, rb, sinks, bkt, gpre, gpost, raw, xs, bktrow, sc, h0, kc_t, vc_t):
    bsz, seq, _ = x.shape
    tb = PROMPT_BLOCK
    nt = seq // tb
    nprep = D_MODEL // PREP_ROWS
    nb = xs.shape[0]
    bb = nb // nt
    layer_t = lambda t: jnp.maximum(t - nprep, 0)
    prep_t = lambda t: jnp.minimum(t, nprep - 1)
    win_t_spec = pl.BlockSpec((bb, KV_DIM, WINDOW), lambda t: (layer_t(t), 0, 0))
    x_spec = pl.BlockSpec((bsz, tb, D_MODEL), lambda t: (0, layer_t(t), 0))
    const = lambda *shape: pl.BlockSpec(shape, lambda t: (0,) * len(shape), pipeline_mode=pl.Buffered(1))
    whole = lambda *shape: pl.BlockSpec(shape, lambda t: (0,) * len(shape))
    smem = pl.BlockSpec(memory_space=pltpu.SMEM)
    out_shapes = (
        jax.ShapeDtypeStruct((bsz, seq, D_MODEL), F32),
        jax.ShapeDtypeStruct((CONV_W - 1, bsz, D_RNN), F32),
        jax.ShapeDtypeStruct((1, bsz, D_RNN), F32),
        jax.ShapeDtypeStruct((bsz, KV_DIM, WINDOW), F32),
        jax.ShapeDtypeStruct((bsz, KV_DIM, WINDOW), F32),
        jax.ShapeDtypeStruct((nb, 1, D_MODEL), F32),
        jax.ShapeDtypeStruct((CONV_W - 1, nb, D_RNN), F32),
        jax.ShapeDtypeStruct((nb, D_RNN), F32),
        jax.ShapeDtypeStruct((nb, KV_DIM, WINDOW), F32),
        jax.ShapeDtypeStruct((nb, KV_DIM, WINDOW), F32),
    )
    return pl.pallas_call(
        functools.partial(_layer_kernel, tb=tb, nseq=bsz, bb=bb, nprep=nprep),
        grid=(nprep + nt,),
        in_specs=[
            smem, smem,
            x_spec,
            const(WINDOW, WINDOW),
            const(1, D_MODEL), const(1, D_MODEL),
            const(CONV_W, D_RNN), const(1, D_RNN), const(1, D_RNN), const(1, D_RNN), const(1, D_RNN),
            pl.BlockSpec((PREP_ROWS, D_IN), lambda t: (prep_t(t), 0)),
            const(N_RNN_BLOCKS, RNN_BLOCK, RNN_BLOCK), const(N_RNN_BLOCKS, RNN_BLOCK, RNN_BLOCK),
            pl.BlockSpec((PREP_ROWS, D_MODEL), lambda t: (prep_t(t), 0)),
            const(nb, 1, D_MODEL), const(1, LANES), const(CONV_W - 1, nb, D_RNN), const(nb, D_RNN),
            win_t_spec, win_t_spec,
        ],
        out_specs=(
            x_spec,
            whole(CONV_W - 1, bsz, D_RNN),
            whole(1, bsz, D_RNN),
            whole(bsz, KV_DIM, WINDOW),
            whole(bsz, KV_DIM, WINDOW),
            whole(nb, 1, D_MODEL), whole(CONV_W - 1, nb, D_RNN), whole(nb, D_RNN),
            win_t_spec, win_t_spec,
        ),
        out_shape=out_shapes,
        scratch_shapes=[
            pltpu.VMEM((SUBLANES, D_RNN), F32),
            pltpu.VMEM((D_MODEL, D_IN), BF16),
            pltpu.VMEM((D_RNN // GATE_GROUP, GATE_GROUP, 2 * GATE_GROUP), BF16),
            pltpu.VMEM((D_MODEL, D_MODEL), BF16),
            pltpu.VMEM((N_HEADS, WINDOW, WINDOW), F32),
            pltpu.VMEM((bsz, (CONV_W - 1) * SUBLANES, D_RNN), F32),
            pltpu.VMEM((bsz, 1, D_RNN), F32),
            pltpu.VMEM((bsz, N_KV_HEADS, tb + WINDOW, LANES), BF16),
            pltpu.VMEM((bsz, N_KV_HEADS, HEAD_DIM + ONES_ROWS, tb + WINDOW), BF16),
            pltpu.VMEM((bsz, tb, D_ATT), F32),
            pltpu.VMEM((bsz, D_RNN // LANES, tb + SUBLANES * SUBLANES, LANES), F32),
            pltpu.VMEM((bsz, D_RNN // LANES, tb + SUBLANES * SUBLANES, LANES), F32),
            pltpu.VMEM((nb * N_HEADS, LANES), F32),
            pltpu.VMEM((nb * N_HEADS, LANES), F32),
            pltpu.VMEM((nb, KV_DIM), F32),
            pltpu.VMEM((nb, KV_DIM), F32),
            pltpu.VMEM((nb, D_RNN), F32),
            pltpu.VMEM((nb, D_ATT), F32),
            pltpu.VMEM((SUBLANES, LANES), F32),
            pltpu.VMEM((SUBLANES, LANES), F32),
        ],
        compiler_params=pltpu.CompilerParams(
            dimension_semantics=("arbitrary",),
            vmem_limit_bytes=VMEM_LIMIT),
        name="layer_step",
    )(rb, sinks, x, bkt, gpre, gpost, *raw, xs, bktrow, sc, h0, kc_t, vc_t)


def _sample_setup(sr, rb_ref, sinks_ref, gpre_ref, rp_ref, win_ref, wg_ref):
    nb = sr.x.shape[0]
    sub = lax.broadcasted_iota(jnp.int32, (SUBLANES, LANES), 0)
    lane_lo = lax.broadcasted_iota(jnp.int32, (nb, LANES), 1) < HEAD_DIM
    bkt = jnp.broadcast_to(sr.bktrow[...], (SUBLANES, LANES))
    bias = jnp.zeros((SUBLANES, LANES), F32)
    sinkm = jnp.zeros((SUBLANES, LANES), F32)
    for h in range(N_HEADS):
        bias = jnp.where(sub == h, _build_bias_table(bkt, rb_ref, h), bias)
        sinkm = jnp.where(sub == h, sinks_ref[h], sinkm)
    sr.bias[...] = bias
    sr.sink[...] = sinkm

    x = sr.x[:, 0, :]
    xn = (x * _rms_scale(x) * gpre_ref[...]).astype(BF16)
    rp = rp_ref[...]
    x_rnn = _dot(xn, win_ref[:, C_XRNN:C_XRNN + D_RNN])
    xh = rp[ROW_CONV_B:ROW_CONV_B + 1]
    for tap in range(CONV_W - 1):
        xh = xh + rp[tap:tap + 1] * sr.sc[tap]
    xh = xh + rp[CONV_W - 1:CONV_W] * x_rnn
    sr.conv[0] = sr.sc[1]
    sr.conv[1] = sr.sc[2]
    sr.conv[2] = x_rnn
    a, bx = _rglru_gate_math(xh, *_rglru_gate_dots(xh, wg_ref), rp)
    h = a * sr.h0[...] + bx
    sr.rnn[...] = h
    g_rnn = _dot(xn, win_ref[:, C_GRNN:C_GRNN + D_RNN])
    sr.rnnout[...] = h * _silu(g_rnn)
    sr.gatt[...] = _dot(xn, win_ref[:, C_GATT:C_GATT + D_ATT])
    kv = _dot(xn, win_ref[:, C_KV:C_KV + 2 * KV_DIM])
    sr.knew[...] = kv[:, :KV_DIM]
    sr.vnew[...] = kv[:, KV_DIM:]
    q = _dot(xn, win_ref[:, C_Q:C_Q + D_ATT])
    for head in range(N_HEADS):
        c = head // 2
        kvh = head // GROUP
        qc = q[:, c * LANES:(c + 1) * LANES]
        if (head % 2) != kvh:
            qc = pltpu.roll(qc, HEAD_DIM, axis=1)
        qz = jnp.where(lane_lo, qc, 0.0) if kvh == 0 else jnp.where(lane_lo, 0.0, qc)
        sr.qz[pl.ds(head, nb, stride=N_HEADS), :] = qz


def _sample_scores(sr, step, bb, st):
    newest = lax.broadcasted_iota(jnp.int32, (WINDOW, LANES), 0) == WINDOW - 1
    bias = sr.bias[...]
    st["rows"] = [pl.multiple_of((step * bb + bi) * N_HEADS, N_HEADS) for bi in range(bb)]
    st["scores"] = []
    for bi in range(bb):
        newk = jnp.where(newest, sr.knew[pl.ds(step * bb + bi, 1), :], pltpu.roll(sr.kc[bi].T, WINDOW - 1, axis=0))
        sr.ko[bi] = newk.T
        qz = sr.qz[pl.ds(st["rows"][bi], N_HEADS), :].astype(BF16)
        st["scores"].append(_dot_nt(qz, newk.astype(BF16)) + bias)


def _sample_values(sr, step, bb, st):
    newest = lax.broadcasted_iota(jnp.int32, (WINDOW, LANES), 0) == WINDOW - 1
    sinkm = sr.sink[...]
    for bi, s in enumerate(st.pop("scores")):
        m = jnp.maximum(jnp.max(s, axis=1, keepdims=True), sinkm)
        e = jnp.exp(s - m)
        den = jnp.sum(e, axis=1, keepdims=True) + jnp.exp(sinkm - m)
        p = (e * (1.0 / den)).astype(BF16)
        newv = jnp.where(newest, sr.vnew[pl.ds(step * bb + bi, 1), :], pltpu.roll(sr.vc[bi].T, WINDOW - 1, axis=0))
        sr.vo[bi] = newv.T
        sr.o[pl.ds(st["rows"][bi], N_HEADS), :] = _dot(p, newv.astype(BF16))


def _sample_finish(sr, gpost_ref, wout_ref):
    nb = sr.x.shape[0]
    lane_lo = lax.broadcasted_iota(jnp.int32, (nb, LANES), 1) < HEAD_DIM
    cols = []
    for c in range(D_ATT // LANES):
        kvh = (2 * c) // GROUP
        halves = []
        for half in range(2):
            oh = sr.o[pl.ds(2 * c + half, nb, stride=N_HEADS), :]
            if half != kvh:
                oh = pltpu.roll(oh, HEAD_DIM, axis=1)
            halves.append(oh)
        cols.append(jnp.where(lane_lo, halves[0], halves[1]))
    att = jnp.concatenate(cols, axis=1)
    att_out = (att * _silu(sr.gatt[...])).astype(BF16)
    rnn_out = sr.rnnout[...].astype(BF16)
    out = _dot(rnn_out, wout_ref[0:D_RNN, :]) + _dot(att_out, wout_ref[D_RNN:, :])
    sr.y[:, 0, :] = sr.x[:, 0, :] + out * _rms_scale(out) * gpost_ref[...]


def kernel(x_prompt, x_sample, state_conv, state_rnn, cache_k_win, cache_v_win, norm_pre, norm_post,
           w_in, conv_w, conv_b, w_gate_a, b_gate_a, w_gate_x, b_gate_x, lru_lambda, attn_sinks,
           rel_bias, w_out):
    assert w_in.shape[0] == 1, "single-layer trunk"
    bsz, seq, _ = x_prompt.shape
    nb = x_sample.shape[0]
    wb = cache_k_win.shape[2]
    assert wb == WINDOW and x_sample.shape[1] == 1 and seq % PROMPT_BLOCK == 0 and nb % (seq // PROMPT_BLOCK) == 0
    assert bsz == PROMPT_SEQS, "the step program is written for this many prompt sequences"

    raw = (conv_w[0], conv_b, b_gate_a, b_gate_x, lru_lambda, w_in[0], w_gate_a[0], w_gate_x[0], w_out[0])
    gpre = norm_pre.reshape(1, D_MODEL)
    gpost = norm_post.reshape(1, D_MODEL)
    sinks = attn_sinks[0]
    bkt_np = _folded_bucket_table()
    bkt = jnp.asarray(np.ascontiguousarray(bkt_np.T))
    bktrow = jnp.asarray(bkt_np[WINDOW - 1:WINDOW, :])

    feature_major = lambda z: jnp.transpose(z[0], (0, 2, 3, 1)).reshape(nb, KV_DIM, wb)
    y_p, conv_p, rnn_p, kw_p, vw_p, y_s, conv_s, rnn_s, kw_s, vw_s = _layer_call(
        x_prompt, rel_bias.T, sinks, bkt, gpre, gpost, raw,
        x_sample, bktrow, jnp.transpose(state_conv[0], (1, 0, 2)), state_rnn[0],
        feature_major(cache_k_win), feature_major(cache_v_win))

    kv5 = lambda z, n: jnp.transpose(z.reshape(n, N_KV_HEADS, HEAD_DIM, wb), (0, 3, 1, 2))[None]
    return (y_p, y_s,
            jnp.transpose(conv_p, (1, 0, 2))[None], rnn_p, kv5(kw_p, bsz), kv5(vw_p, bsz),
            jnp.transpose(conv_s, (1, 0, 2))[None], rnn_s[None], kv5(kw_s, nb), kv5(vw_s, nb))
```

```python
import collections
import functools
import math

import numpy as np
import jax
import jax.numpy as jnp
from jax import lax
from jax.experimental import pallas as pl
from jax.experimental.pallas import tpu as pltpu

D_MODEL = 1024
D_RNN = 512
D_ATT = 512
HEAD_DIM = 64
N_HEADS = 8
N_KV_HEADS = 2
GROUP = N_HEADS // N_KV_HEADS
KV_DIM = N_KV_HEADS * HEAD_DIM
N_RNN_BLOCKS = 8
RNN_BLOCK = D_RNN // N_RNN_BLOCKS
CONV_W = 4
LRU_C = 8.0
WINDOW = 128
N_BUCKETS = 32
MAX_DISTANCE = 128
EPS = 1e-6
NEG_INF = -1e30
D_IN = 2 * D_RNN + 2 * D_ATT + 2 * KV_DIM

C_XRNN = 0
C_GRNN = D_RNN
C_Q = 2 * D_RNN
C_KV = 2 * D_RNN + D_ATT
C_GATT = 2 * D_RNN + D_ATT + 2 * KV_DIM

SUBLANES = 8
LANES = 128
GATE_GROUP = 256
ONES_ROWS = 16
SCORE_LOOKAHEAD = 8
PROMPT_SEQS = 2
PROMPT_BLOCK = 512
PREP_ROWS = 256
V7X_VMEM_BYTES = 64 * 1024 * 1024
VMEM_LIMIT = V7X_VMEM_BYTES - 8 * 1024 * 1024

TINY = 1e-37
ROW_CONV_B = CONV_W
ROW_GATE_A_B = CONV_W + 1
ROW_GATE_X_B = CONV_W + 2
ROW_LAMBDA = CONV_W + 3
LOG2E = 1.4426950408889634
F32 = jnp.float32
BF16 = jnp.bfloat16


def _spread(groups, bulk):
    out, done = [], 0
    for i, group in enumerate(groups):
        out += group
        want = (i + 1) * len(bulk) // len(groups)
        out += bulk[done:want]
        done = want
    return out


def _step_program(npairs):
    a, b = range(PROMPT_SEQS)
    n_out = D_MODEL // GATE_GROUP

    def attention(s):
        groups = [[("score", s, n) for n in range(SCORE_LOOKAHEAD)]]
        for n in range(npairs):
            ahead = [("score", s, n + SCORE_LOOKAHEAD)] if n + SCORE_LOOKAHEAD < npairs else []
            groups.append(ahead + [("value", s, n)])
        return groups

    project = lambda s: [("project", s, name) for name in ("x_rnn", "kv", "q")]
    gates = lambda s: [("gate_piece", s, i) for i in range((D_RNN + D_ATT) // GATE_GROUP)]
    outs = lambda s: [("out_piece", s, j) for j in range(n_out)]
    pa, pb = project(a), project(b)
    ga, gb, oa = gates(a), gates(b), outs(a)
    prog = [("sample_scores", a), ("norm", a), pa[0], ("norm", b), pa[1], pa[2], ("store_kv", a)]
    prog += [("conv_gates", a), ("gate_math", a), ("scan", a)] + pb
    prog += _spread(attention(a), ga + [("rnn_gate", a)])
    prog += [("att_finish", a), ("conv_gates", b), ("store_kv", b), ("gate_math", b), ("scan", b)] + gb
    prog += _spread(attention(b), oa + [("rnn_gate", b)])
    prog += [("finish", a), ("att_finish", b)] + outs(b) + [("sample_values", a), ("finish", b)]
    return prog


def _t5_bucket_np(dist):
    dist = np.maximum(dist, 0)
    max_exact = N_BUCKETS // 2
    d = np.maximum(dist, 1).astype(np.float32)
    ratio = np.log(d / np.float32(max_exact)) / np.float32(math.log(MAX_DISTANCE / max_exact))
    large = max_exact + (ratio * np.float32(N_BUCKETS - max_exact)).astype(np.int32)
    large = np.minimum(large, N_BUCKETS - 1)
    return np.where(dist < max_exact, dist, large).astype(np.int32)


def _folded_bucket_table():
    i = np.arange(WINDOW)[:, None]
    j = np.arange(WINDOW)[None, :]
    dist = np.where(j <= i, i - j, i + WINDOW - j)
    return _t5_bucket_np(dist)


def _silu(x):
    h = 0.5 * x
    return h + h * jnp.tanh(h)


def _softplus(x):
    return jnp.maximum(x, 0.0) + jnp.log1p(jnp.exp(-jnp.abs(x)))


def _rms_scale(x):
    return lax.rsqrt(jnp.mean(x * x, axis=-1, keepdims=True) + EPS)


def _dot(a, b):
    return jnp.dot(a, b, preferred_element_type=F32)


def _dot_nt(a, b):
    return lax.dot_general(a, b, (((1,), (1,)), ((), ())), preferred_element_type=F32)


def _rglru_gate_dots(xh, wg_ref):
    xcb = xh.astype(BF16)
    zs = [_dot(xcb[:, g * GATE_GROUP:(g + 1) * GATE_GROUP], wg_ref[g])
          for g in range(D_RNN // GATE_GROUP)]
    za = jnp.concatenate([z[:, :GATE_GROUP] for z in zs], axis=1)
    zx = jnp.concatenate([z[:, GATE_GROUP:] for z in zs], axis=1)
    return za, zx


def _rglru_gate_math(xh, za, zx, rp):
    ix = (1.0 + jnp.tanh(zx + rp[ROW_GATE_X_B:ROW_GATE_X_B + 1])) * xh
    rate = (-0.5 * LRU_C * LOG2E) * _softplus(-rp[ROW_LAMBDA:ROW_LAMBDA + 1])
    a = jnp.exp2((1.0 + jnp.tanh(za + rp[ROW_GATE_A_B:ROW_GATE_A_B + 1])) * rate)
    u = 1.0 - a * a
    bx = (u * lax.rsqrt(jnp.maximum(u, TINY))) * ix
    return a, bx


def _prepare_params(step, raw, rp_ref, win_ref, wg_ref, wout_ref):
    cw_raw, cb_raw, ba_raw, bx_raw, lam_raw, win_raw, wga_raw, wgx_raw, wout_raw = raw
    rows = pl.ds(pl.multiple_of(step * PREP_ROWS, PREP_ROWS), PREP_ROWS)
    for c0 in range(0, D_IN, GATE_GROUP):
        w = win_raw[:, c0:c0 + GATE_GROUP]
        if C_Q <= c0 < C_Q + D_ATT:
            w = w * HEAD_DIM ** -0.5
        win_ref[rows, c0:c0 + GATE_GROUP] = w.astype(BF16)
    wout_ref[rows, :] = wout_raw[...].astype(BF16)

    @pl.when(step == 0)
    def _():
        rp_ref[0:CONV_W, :] = 0.5 * cw_raw[...]
        rp_ref[ROW_CONV_B:ROW_CONV_B + 1, :] = 0.5 * cb_raw[...]
        rp_ref[ROW_GATE_A_B:ROW_GATE_A_B + 1, :] = 0.5 * ba_raw[...]
        rp_ref[ROW_GATE_X_B:ROW_GATE_X_B + 1, :] = 0.5 * bx_raw[...]
        rp_ref[ROW_LAMBDA:ROW_LAMBDA + 1, :] = lam_raw[...]
        per = GATE_GROUP // RNN_BLOCK
        zero = jnp.zeros((RNN_BLOCK, RNN_BLOCK), F32)
        for g in range(D_RNN // GATE_GROUP):
            for half, w_raw in enumerate((wga_raw, wgx_raw)):
                rows_g = [jnp.concatenate([w_raw[g * per + m] if k == m else zero for k in range(per)], axis=1)
                          for m in range(per)]
                wg_ref[g, :, half * GATE_GROUP:(half + 1) * GATE_GROUP] = jnp.concatenate(rows_g, axis=0).astype(BF16)


def _build_bias_table(bkt, rb_ref, head):
    tbl = jnp.zeros(bkt.shape, F32)
    for k in range(N_BUCKETS):
        tbl = jnp.where(bkt == k, rb_ref[head, k], tbl)
    return tbl


_SampleRefs = collections.namedtuple(
    "_SampleRefs", "x bktrow sc h0 kc vc y conv rnn ko vo qz o knew vnew rnnout gatt bias sink")


def _layer_kernel(rb_ref, sinks_ref, x_ref, bkt_ref, gpre_ref, gpost_ref,
                  cw_raw, cb_raw, ba_raw, bx_raw, lam_raw, win_raw, wga_raw, wgx_raw, wout_raw,
                  xs_ref, bktrow_ref, sc_ref, h0_ref, kc_ref, vc_ref,
                  y_ref, conv_ref, rnn_ref, kw_ref, vw_ref, ys_ref, convs_ref, rnns_ref, ko_ref, vo_ref,
                  rp_ref, win_ref, wg_ref, wout_ref, *scratch, tb, nseq, bb, nprep):
    step = pl.program_id(0)

    @pl.when(step < nprep)
    def _():
        _prepare_params(step, (cw_raw, cb_raw, ba_raw, bx_raw, lam_raw, win_raw, wga_raw, wgx_raw, wout_raw),
                        rp_ref, win_ref, wg_ref, wout_ref)

        @pl.when(step == 0)
        def _():
            bias_ref = scratch[0]
            bkt = bkt_ref[...]
            for h in range(N_HEADS):
                bias_ref[h] = _build_bias_table(bkt, rb_ref, h)

    @pl.when(step >= nprep)
    def _():
        _layer_step(step - nprep, pl.num_programs(0) - nprep - 1,
                    rb_ref, sinks_ref, x_ref, bkt_ref, gpre_ref, gpost_ref, rp_ref, win_ref, wg_ref, wout_ref,
                    xs_ref, bktrow_ref, sc_ref, h0_ref, kc_ref, vc_ref,
                    y_ref, conv_ref, rnn_ref, kw_ref, vw_ref, ys_ref, convs_ref, rnns_ref, ko_ref, vo_ref,
                    *scratch, tb=tb, nseq=nseq, bb=bb)


def _layer_step(t, last_t, rb_ref, sinks_ref, x_ref, bkt_ref, gpre_ref, gpost_ref, rp_ref, win_ref, wg_ref, wout_ref,
                xs_ref, bktrow_ref, sc_ref, h0_ref, kc_ref, vc_ref,
                y_ref, conv_ref, rnn_ref, kw_ref, vw_ref, ys_ref, convs_ref, rnns_ref, ko_ref, vo_ref,
                bias_ref, xpad_ref, hc_ref, kd_ref, vt_ref, att_ref, xr_ref, hn_ref,
                qz_ref, o_ref, knew_ref, vnew_ref, rnnout_ref, gatt_ref, sbias_ref, ssink_ref, *, tb, nseq, bb):
    nq = tb // WINDOW
    sr = _SampleRefs(xs_ref, bktrow_ref, sc_ref, h0_ref, kc_ref, vc_ref, ys_ref, convs_ref, rnns_ref, ko_ref, vo_ref,
                     qz_ref, o_ref, knew_ref, vnew_ref, rnnout_ref, gatt_ref, sbias_ref, ssink_ref)

    @pl.when(t == 0)
    def _():
        xpad_ref[...] = jnp.zeros((nseq, (CONV_W - 1) * SUBLANES, D_RNN), F32)
        hc_ref[...] = jnp.zeros((nseq, 1, D_RNN), F32)
        kd_ref[:, :, 0:WINDOW, :] = jnp.zeros((nseq, N_KV_HEADS, WINDOW, LANES), BF16)
        vt_ref[:, :, 0:HEAD_DIM, 0:WINDOW] = jnp.zeros((nseq, N_KV_HEADS, HEAD_DIM, WINDOW), BF16)
        vt_ref[:, :, HEAD_DIM:, :] = jnp.ones((nseq, N_KV_HEADS, ONES_ROWS, tb + WINDOW), BF16)
        _sample_setup(sr, rb_ref, sinks_ref, gpre_ref, rp_ref, win_ref, wg_ref)

    rp = rp_ref[...]
    sub8 = lax.broadcasted_iota(jnp.int32, (SUBLANES, D_RNN), 0)
    lo = lax.broadcasted_iota(jnp.int32, (tb, LANES), 1) < HEAD_DIM
    key_idx = lax.broadcasted_iota(jnp.int32, (WINDOW, WINDOW), 0)
    qry_idx = lax.broadcasted_iota(jnp.int32, (WINDOW, WINDOW), 1)
    own = key_idx <= qry_idx
    lane_lo = qry_idx < HEAD_DIM
    has_prev0 = (jnp.zeros((WINDOW, WINDOW), jnp.int32) + t) > 0
    zero_b = jnp.zeros((WINDOW, WINDOW), BF16)
    pair_lo = lax.broadcasted_iota(jnp.int32, (1, 2 * WINDOW), 1) < WINDOW
    pairs = [(qi, c) for qi in range(nq) for c in range(D_ATT // LANES)]
    gate_cols = [C_GRNN, C_GRNN + GATE_GROUP, C_GATT, C_GATT + GATE_GROUP]

    def norm(s, st):
        x = x_ref[s]
        st["xn"] = (x * _rms_scale(x) * gpre_ref[...]).astype(BF16)

    def project(s, st, name):
        c0, width = dict(x_rnn=(C_XRNN, D_RNN), kv=(C_KV, 2 * KV_DIM), q=(C_Q, D_ATT))[name]
        st[name] = _dot(st["xn"], win_ref[:, c0:c0 + width])

    clen = tb // SUBLANES
    pitch = clen + SUBLANES
    ncol = D_RNN // LANES
    ntail = CONV_W - 1

    def conv_gates(s, st):
        x_rnn = st["x_rnn"]
        for c in range(ncol):
            for k in range(SUBLANES):
                xr_ref[s, c, k * pitch:k * pitch + clen, :] = x_rnn[k * clen:(k + 1) * clen, c * LANES:(c + 1) * LANES]
        xp = jnp.concatenate(
            [jnp.concatenate([xr_ref[s, c, pl.ds(v, SUBLANES, stride=pitch), :] for v in range(clen)], axis=0)
             for c in range(ncol)], axis=1)
        tail = xp[(clen - ntail) * SUBLANES:, :]
        prev_tail = xpad_ref[s]
        heads = [jnp.where(sub8 == 0,
                           pltpu.roll(prev_tail[j * SUBLANES:(j + 1) * SUBLANES], 1, axis=0),
                           pltpu.roll(tail[j * SUBLANES:(j + 1) * SUBLANES], 1, axis=0)) for j in range(ntail)]
        xh = rp[ROW_CONV_B:ROW_CONV_B + 1]
        for tap in range(ntail):
            shift = ntail - tap
            xh = xh + rp[tap:tap + 1] * jnp.concatenate(heads[ntail - shift:] + [xp[:(clen - shift) * SUBLANES]], axis=0)
        xh = xh + rp[ntail:ntail + 1] * xp
        xpad_ref[s] = tail
        for j in range(ntail):
            conv_ref[j, s:s + 1, :] = tail[j * SUBLANES + SUBLANES - 1:(j + 1) * SUBLANES]
        st["xh"] = xh
        st["za"], st["zx"] = _rglru_gate_dots(xh, wg_ref)

    def gate_math(s, st):
        st["a"], st["bx"] = _rglru_gate_math(st.pop("xh"), st.pop("za"), st.pop("zx"), rp)

    def store_kv(s, st):
        st["qb"] = st.pop("q").astype(BF16)
        k01 = st["kv"][:, :KV_DIM]
        v01 = st["kv"][:, KV_DIM:]
        k10 = pltpu.roll(k01, HEAD_DIM, axis=1)
        kd_ref[s, 0, WINDOW:WINDOW + tb, :] = jnp.where(lo, k01, k10).astype(BF16)
        kd_ref[s, 1, WINDOW:WINDOW + tb, :] = jnp.where(lo, k10, k01).astype(BF16)
        v_t = v01.T.astype(BF16)
        vt_ref[s, 0, 0:HEAD_DIM, WINDOW:WINDOW + tb] = v_t[0:HEAD_DIM]
        vt_ref[s, 1, 0:HEAD_DIM, WINDOW:WINDOW + tb] = v_t[HEAD_DIM:]
        kw_ref[s] = k01[tb - WINDOW:, :].T
        vw_ref[s] = v01[tb - WINDOW:, :].T

    def gate_piece(s, st, i):
        st[("gate", i)] = _dot(st["xn"], win_ref[:, gate_cols[i]:gate_cols[i] + GATE_GROUP])

    def scores(s, st, qi, c):
        rows = slice(qi * WINDOW, (qi + 1) * WINDOW)
        win = slice(qi * WINDOW, qi * WINDOW + 2 * WINDOW)
        kvh = (2 * c) // GROUP
        qc = st["qb"][rows, c * LANES:(c + 1) * LANES]
        qpair = jnp.concatenate([jnp.where(lane_lo, qc, zero_b), jnp.where(lane_lo, zero_b, qc)], axis=0)
        sc = _dot_nt(kd_ref[s, kvh, win, :], qpair)
        p_prev, p_own, ms = [], [], []
        for half in range(2):
            head = 2 * c + half
            cols = slice(half * WINDOW, (half + 1) * WINDOW)
            sf = jnp.where(own, sc[WINDOW:, cols], sc[:WINDOW, cols]) + bias_ref[head]
            if qi == 0:
                sf = jnp.where(own | has_prev0, sf, NEG_INF)
            m = jnp.max(sf, axis=0, keepdims=True)
            e = jnp.exp(sf - m).astype(BF16)
            p_prev.append(jnp.where(own, zero_b, e))
            p_own.append(jnp.where(own, e, zero_b))
            ms.append(m)
        p = jnp.concatenate([jnp.concatenate(p_prev, axis=1), jnp.concatenate(p_own, axis=1)], axis=0)
        return p, jnp.concatenate(ms, axis=1)

    def values(s, st, qi, c, p, m):
        rows = slice(qi * WINDOW, (qi + 1) * WINDOW)
        win = slice(qi * WINDOW, qi * WINDOW + 2 * WINDOW)
        kvh = (2 * c) // GROUP
        oa = _dot(vt_ref[s, kvh, :, win], p)
        sink = jnp.where(pair_lo, sinks_ref[2 * c], sinks_ref[2 * c + 1])
        den = oa[HEAD_DIM:HEAD_DIM + 1, :] + jnp.exp(sink - m)
        o = oa[0:HEAD_DIM, :] * (1.0 / den)
        o_t = jnp.concatenate([o[:, :WINDOW], o[:, WINDOW:]], axis=0)
        att_ref[s, rows, c * LANES:(c + 1) * LANES] = o_t.T

    def score(s, st, n):
        st[("p", n)] = scores(s, st, *pairs[n])

    def value(s, st, n):
        values(s, st, *pairs[n], *st.pop(("p", n)))

    def att_finish(s, st):
        kd_ref[s, :, 0:WINDOW, :] = kd_ref[s, :, tb:tb + WINDOW, :]
        vt_ref[s, :, :, 0:WINDOW] = vt_ref[s, :, :, tb:tb + WINDOW]
        g_att = jnp.concatenate([st[("gate", 2)], st[("gate", 3)]], axis=1)
        st["att_out"] = (att_ref[s] * _silu(g_att)).astype(BF16)

    def scan(s, st):
        a3 = st["a"].reshape(clen, SUBLANES, D_RNN)
        b3 = st["bx"].reshape(clen, SUBLANES, D_RNN)
        hl, pr = [b3[0]], [a3[0]]
        for v in range(1, clen):
            hl.append(a3[v] * hl[-1] + b3[v])
            pr.append(a3[v] * pr[-1])
        h_end, p_end = hl[-1], pr[-1]
        h_in = jnp.broadcast_to(hc_ref[s], (SUBLANES, D_RNN))
        entry = h_in
        for _ in range(SUBLANES - 1):
            entry = jnp.where(sub8 == 0, h_in, pltpu.roll(h_end + p_end * entry, 1, axis=0))
        carry = (h_end + p_end * entry)[SUBLANES - 1:SUBLANES, :]
        hc_ref[s] = carry
        rnn_ref[0, s:s + 1, :] = carry
        for v in range(clen):
            hv = hl[v] + pr[v] * entry
            for c in range(ncol):
                hn_ref[s, c, pl.ds(v, SUBLANES, stride=pitch), :] = hv[:, c * LANES:(c + 1) * LANES]
        h = jnp.concatenate(
            [jnp.concatenate([hn_ref[s, c, k * pitch:k * pitch + clen, :] for k in range(SUBLANES)], axis=0)
             for c in range(ncol)], axis=1)
        st["h"] = h

    def rnn_gate(s, st):
        g_rnn = jnp.concatenate([st[("gate", 0)], st[("gate", 1)]], axis=1)
        st["rnn_out"] = (st.pop("h") * _silu(g_rnn)).astype(BF16)

    def out_piece(s, st, j):
        cols = slice(j * GATE_GROUP, (j + 1) * GATE_GROUP)
        st[("out", j)] = _dot(st["rnn_out"], wout_ref[0:D_RNN, cols]) + _dot(st["att_out"], wout_ref[D_RNN:, cols])

    def finish(s, st):
        out = jnp.concatenate([st[("out", j)] for j in range(D_MODEL // GATE_GROUP)], axis=1)
        y_ref[s] = x_ref[s] + out * _rms_scale(out) * gpost_ref[...]

    def sample_scores(s, st):
        _sample_scores(sr, t, bb, st)

    def sample_values(s, st):
        _sample_values(sr, t, bb, st)

    tasks = dict(sample_scores=sample_scores, sample_values=sample_values, norm=norm, project=project,
                 conv_gates=conv_gates, gate_math=gate_math, store_kv=store_kv, gate_piece=gate_piece,
                 score=score, value=value, att_finish=att_finish, scan=scan, rnn_gate=rnn_gate, out_piece=out_piece, finish=finish)
    sts = [dict() for _ in range(nseq)]
    for name, s, *arg in _step_program(len(pairs)):
        tasks[name](s, sts[s], *arg)

    @pl.when(t == last_t)
    def _():
        _sample_finish(sr, gpost_ref, wout_ref)


def _layer_call(x, rb, sinks, bkt, gpre, gpost, raw, xs, bktrow, sc, h0, kc_t, vc_t):
    bsz, seq, _ = x.shape
    tb = PROMPT_BLOCK
    nt = seq // tb
    nprep = D_MODEL // PREP_ROWS
    nb = xs.shape[0]
    bb = nb // nt
    layer_t = lambda t: jnp.maximum(t - nprep, 0)
    prep_t = lambda t: jnp.minimum(t, nprep - 1)
    win_t_spec = pl.BlockSpec((bb, KV_DIM, WINDOW), lambda t: (layer_t(t), 0, 0))
    x_spec = pl.BlockSpec((bsz, tb, D_MODEL), lambda t: (0, layer_t(t), 0))
    const = lambda *shape: pl.BlockSpec(shape, lambda t: (0,) * len(shape), pipeline_mode=pl.Buffered(1))
    whole = lambda *shape: pl.BlockSpec(shape, lambda t: (0,) * len(shape))
    smem = pl.BlockSpec(memory_space=pltpu.SMEM)
    out_shapes = (
        jax.ShapeDtypeStruct((bsz, seq, D_MODEL), F32),
        jax.ShapeDtypeStruct((CONV_W - 1, bsz, D_RNN), F32),
        jax.ShapeDtypeStruct((1, bsz, D_RNN), F32),
        jax.ShapeDtypeStruct((bsz, KV_DIM, WINDOW), F32),
        jax.ShapeDtypeStruct((bsz, KV_DIM, WINDOW), F32),
        jax.ShapeDtypeStruct((nb, 1, D_MODEL), F32),
        jax.ShapeDtypeStruct((CONV_W - 1, nb, D_RNN), F32),
        jax.ShapeDtypeStruct((nb, D_RNN), F32),
        jax.ShapeDtypeStruct((nb, KV_DIM, WINDOW), F32),
        jax.ShapeDtypeStruct((nb, KV_DIM, WINDOW), F32),
    )
    return pl.pallas_call(
        functools.partial(_layer_kernel, tb=tb, nseq=bsz, bb=bb, nprep=nprep),
        grid=(nprep + nt,),
        in_specs=[
            smem, smem,
            x_spec,
            const(WINDOW, WINDOW),
            const(1, D_MODEL), const(1, D_MODEL),
            const(CONV_W, D_RNN), const(1, D_RNN), const(1, D_RNN), const(1, D_RNN), const(1, D_RNN),
            pl.BlockSpec((PREP_ROWS, D_IN), lambda t: (prep_t(t), 0)),
            const(N_RNN_BLOCKS, RNN_BLOCK, RNN_BLOCK), const(N_RNN_BLOCKS, RNN_BLOCK, RNN_BLOCK),
            pl.BlockSpec((PREP_ROWS, D_MODEL), lambda t: (prep_t(t), 0)),
            const(nb, 1, D_MODEL), const(1, LANES), const(CONV_W - 1, nb, D_RNN), const(nb, D_RNN),
            win_t_spec, win_t_spec,
        ],
        out_specs=(
            x_spec,
            whole(CONV_W - 1, bsz, D_RNN),
            whole(1, bsz, D_RNN),
            whole(bsz, KV_DIM, WINDOW),
            whole(bsz, KV_DIM, WINDOW),
            whole(nb, 1, D_MODEL), whole(CONV_W - 1, nb, D_RNN), whole(nb, D_RNN),
            win_t_spec, win_t_spec,
        ),
        out_shape=out_shapes,
        scratch_shapes=[
            pltpu.VMEM((SUBLANES, D_RNN), F32),
            pltpu.VMEM((D_MODEL, D_IN), BF16),
            pltpu.VMEM((D_RNN // GATE_GROUP, GATE_GROUP, 2 * GATE_GROUP), BF16),
            pltpu.VMEM((D_MODEL, D_MODEL), BF16),
            pltpu.VMEM((N_HEADS, WINDOW, WINDOW), F32),
            pltpu.VMEM((bsz, (CONV_W - 1) * SUBLANES, D_RNN), F32),
            pltpu.VMEM((bsz, 1, D_RNN), F32),
            pltpu.VMEM((bsz, N_KV_HEADS, tb + WINDOW, LANES), BF16),
            pltpu.VMEM((bsz, N_KV_HEADS, HEAD_DIM + ONES_ROWS, tb + WINDOW), BF16),
            pltpu.VMEM((bsz, tb, D_ATT), F32),
            pltpu.VMEM((bsz, D_RNN // LANES, tb + SUBLANES * SUBLANES, LANES), F32),
            pltpu.VMEM((bsz, D_RNN // LANES, tb + SUBLANES * SUBLANES, LANES), F32),
            pltpu.VMEM((nb * N_HEADS, LANES), F32),
            pltpu.VMEM((nb * N_HEADS, LANES), F32),
            pltpu.VMEM((nb, KV_DIM), F32),
            pltpu.VMEM((nb, KV_DIM), F32),
            pltpu.VMEM((nb, D_RNN), F32),
            pltpu.VMEM((nb, D_ATT), F32),
            pltpu.VMEM((SUBLANES, LANES), F32),
            pltpu.VMEM((SUBLANES, LANES), F32),
        ],
        compiler_params=pltpu.CompilerParams(
            dimension_semantics=("arbitrary",),
            vmem_limit_bytes=VMEM_LIMIT),
        name="layer_step",
    )(rb, sinks, x, bkt, gpre, gpost, *raw, xs, bktrow, sc, h0, kc_t, vc_t)


def _sample_setup(sr, rb_ref, sinks_ref, gpre_ref, rp_ref, win_ref, wg_ref):
    nb = sr.x.shape[0]
    sub = lax.broadcasted_iota(jnp.int32, (SUBLANES, LANES), 0)
    lane_lo = lax.broadcasted_iota(jnp.int32, (nb, LANES), 1) < HEAD_DIM
    bkt = jnp.broadcast_to(sr.bktrow[...], (SUBLANES, LANES))
    bias = jnp.zeros((SUBLANES, LANES), F32)
    sinkm = jnp.zeros((SUBLANES, LANES), F32)
    for h in range(N_HEADS):
        bias = jnp.where(sub == h, _build_bias_table(bkt, rb_ref, h), bias)
        sinkm = jnp.where(sub == h, sinks_ref[h], sinkm)
    sr.bias[...] = bias
    sr.sink[...] = sinkm

    x = sr.x[:, 0, :]
    xn = (x * _rms_scale(x) * gpre_ref[...]).astype(BF16)
    rp = rp_ref[...]
    x_rnn = _dot(xn, win_ref[:, C_XRNN:C_XRNN + D_RNN])
    xh = rp[ROW_CONV_B:ROW_CONV_B + 1]
    for tap in range(CONV_W - 1):
        xh = xh + rp[tap:tap + 1] * sr.sc[tap]
    xh = xh + rp[CONV_W - 1:CONV_W] * x_rnn
    sr.conv[0] = sr.sc[1]
    sr.conv[1] = sr.sc[2]
    sr.conv[2] = x_rnn
    a, bx = _rglru_gate_math(xh, *_rglru_gate_dots(xh, wg_ref), rp)
    h = a * sr.h0[...] + bx
    sr.rnn[...] = h
    g_rnn = _dot(xn, win_ref[:, C_GRNN:C_GRNN + D_RNN])
    sr.rnnout[...] = h * _silu(g_rnn)
    sr.gatt[...] = _dot(xn, win_ref[:, C_GATT:C_GATT + D_ATT])
    kv = _dot(xn, win_ref[:, C_KV:C_KV + 2 * KV_DIM])
    sr.knew[...] = kv[:, :KV_DIM]
    sr.vnew[...] = kv[:, KV_DIM:]
    q = _dot(xn, win_ref[:, C_Q:C_Q + D_ATT])
    for head in range(N_HEADS):
        c = head // 2
        kvh = head // GROUP
        qc = q[:, c * LANES:(c + 1) * LANES]
        if (head % 2) != kvh:
            qc = pltpu.roll(qc, HEAD_DIM, axis=1)
        qz = jnp.where(lane_lo, qc, 0.0) if kvh == 0 else jnp.where(lane_lo, 0.0, qc)
        sr.qz[pl.ds(head, nb, stride=N_HEADS), :] = qz


def _sample_scores(sr, step, bb, st):
    newest = lax.broadcasted_iota(jnp.int32, (WINDOW, LANES), 0) == WINDOW - 1
    bias = sr.bias[...]
    st["rows"] = [pl.multiple_of((step * bb + bi) * N_HEADS, N_HEADS) for bi in range(bb)]
    st["scores"] = []
    for bi in range(bb):
        newk = jnp.where(newest, sr.knew[pl.ds(step * bb + bi, 1), :], pltpu.roll(sr.kc[bi].T, WINDOW - 1, axis=0))
        sr.ko[bi] = newk.T
        qz = sr.qz[pl.ds(st["rows"][bi], N_HEADS), :].astype(BF16)
        st["scores"].append(_dot_nt(qz, newk.astype(BF16)) + bias)


def _sample_values(sr, step, bb, st):
    newest = lax.broadcasted_iota(jnp.int32, (WINDOW, LANES), 0) == WINDOW - 1
    sinkm = sr.sink[...]
    for bi, s in enumerate(st.pop("scores")):
        m = jnp.maximum(jnp.max(s, axis=1, keepdims=True), sinkm)
        e = jnp.exp(s - m)
        den = jnp.sum(e, axis=1, keepdims=True) + jnp.exp(sinkm - m)
        p = (e * (1.0 / den)).astype(BF16)
        newv = jnp.where(newest, sr.vnew[pl.ds(step * bb + bi, 1), :], pltpu.roll(sr.vc[bi].T, WINDOW - 1, axis=0))
        sr.vo[bi] = newv.T
        sr.o[pl.ds(st["rows"][bi], N_HEADS), :] = _dot(p, newv.astype(BF16))


def _sample_finish(sr, gpost_ref, wout_ref):
    nb = sr.x.shape[0]
    lane_lo = lax.broadcasted_iota(jnp.int32, (nb, LANES), 1) < HEAD_DIM
    cols = []
    for c in range(D_ATT // LANES):
        kvh = (2 * c) // GROUP
        halves = []
        for half in range(2):
            oh = sr.o[pl.ds(2 * c + half, nb, stride=N_HEADS), :]
            if half != kvh:
                oh = pltpu.roll(oh, HEAD_DIM, axis=1)
            halves.append(oh)
        cols.append(jnp.where(lane_lo, halves[0], halves[1]))
    att = jnp.concatenate(cols, axis=1)
    att_out = (att * _silu(sr.gatt[...])).astype(BF16)
    rnn_out = sr.rnnout[...].astype(BF16)
    out = _dot(rnn_out, wout_ref[0:D_RNN, :]) + _dot(att_out, wout_ref[D_RNN:, :])
    sr.y[:, 0, :] = sr.x[:, 0, :] + out * _rms_scale(out) * gpost_ref[...]


def kernel(x_prompt, x_sample, state_conv, state_rnn, cache_k_win, cache_v_win, norm_pre, norm_post,
           w_in, conv_w, conv_b, w_gate_a, b_gate_a, w_gate_x, b_gate_x, lru_lambda, attn_sinks,
           rel_bias, w_out):
    assert w_in.shape[0] == 1, "single-layer trunk"
    bsz, seq, _ = x_prompt.shape
    nb = x_sample.shape[0]
    wb = cache_k_win.shape[2]
    assert wb == WINDOW and x_sample.shape[1] == 1 and seq % PROMPT_BLOCK == 0 and nb % (seq // PROMPT_BLOCK) == 0
    assert bsz == PROMPT_SEQS, "the step program is written for this many prompt sequences"

    raw = (conv_w[0], conv_b, b_gate_a, b_gate_x, lru_lambda, w_in[0], w_gate_a[0], w_gate_x[0], w_out[0])
    gpre = norm_pre.reshape(1, D_MODEL)
    gpost = norm_post.reshape(1, D_MODEL)
    sinks = attn_sinks[0]
    bkt_np = _folded_bucket_table()
    bkt = jnp.asarray(np.ascontiguousarray(bkt_np.T))
    bktrow = jnp.asarray(bkt_np[WINDOW - 1:WINDOW, :])

    feature_major = lambda z: jnp.transpose(z[0], (0, 2, 3, 1)).reshape(nb, KV_DIM, wb)
    y_p, conv_p, rnn_p, kw_p, vw_p, y_s, conv_s, rnn_s, kw_s, vw_s = _layer_call(
        x_prompt, rel_bias.T, sinks, bkt, gpre, gpost, raw,
        x_sample, bktrow, jnp.transpose(state_conv[0], (1, 0, 2)), state_rnn[0],
        feature_major(cache_k_win), feature_major(cache_v_win))

    kv5 = lambda z, n: jnp.transpose(z.reshape(n, N_KV_HEADS, HEAD_DIM, wb), (0, 3, 1, 2))[None]
    return (y_p, y_s,
            jnp.transpose(conv_p, (1, 0, 2))[None], rnn_p, kv5(kw_p, bsz), kv5(vw_p, bsz),
            jnp.transpose(conv_s, (1, 0, 2))[None], rnn_s[None], kv5(kw_s, nb), kv5(vw_s, nb))
```

```python
import collections
import functools
import math

import numpy as np
import jax
import jax.numpy as jnp
from jax import lax
from jax.experimental import pallas as pl
from jax.experimental.pallas import tpu as pltpu

D_MODEL = 1024
D_RNN = 512
D_ATT = 512
HEAD_DIM = 64
N_HEADS = 8
N_KV_HEADS = 2
GROUP = N_HEADS // N_KV_HEADS
KV_DIM = N_KV_HEADS * HEAD_DIM
N_RNN_BLOCKS = 8
RNN_BLOCK = D_RNN // N_RNN_BLOCKS
CONV_W = 4
LRU_C = 8.0
WINDOW = 128
N_BUCKETS = 32
MAX_DISTANCE = 128
EPS = 1e-6
NEG_INF = -1e30
D_IN = 2 * D_RNN + 2 * D_ATT + 2 * KV_DIM

C_XRNN = 0
C_GRNN = D_RNN
C_Q = 2 * D_RNN
C_KV = 2 * D_RNN + D_ATT
C_GATT = 2 * D_RNN + D_ATT + 2 * KV_DIM

SUBLANES = 8
LANES = 128
GATE_GROUP = 256
ONES_ROWS = 16
SCORE_LOOKAHEAD = 8
PROMPT_SEQS = 2
PROMPT_BLOCK = 512
PREP_ROWS = 256
V7X_VMEM_BYTES = 64 * 1024 * 1024
VMEM_LIMIT = V7X_VMEM_BYTES - 8 * 1024 * 1024

TINY = 1e-37
ROW_CONV_B = CONV_W
ROW_GATE_A_B = CONV_W + 1
ROW_GATE_X_B = CONV_W + 2
ROW_LAMBDA = CONV_W + 3
LOG2E = 1.4426950408889634
F32 = jnp.float32
BF16 = jnp.bfloat16


def _spread(groups, bulk):
    out, done = [], 0
    for i, group in enumerate(groups):
        out += group
        want = (i + 1) * len(bulk) // len(groups)
        out += bulk[done:want]
        done = want
    return out


def _step_program(npairs):
    a, b = range(PROMPT_SEQS)
    n_out = D_MODEL // GATE_GROUP

    def attention(s):
        groups = [[("score", s, n) for n in range(SCORE_LOOKAHEAD)]]
        for n in range(npairs):
            ahead = [("score", s, n + SCORE_LOOKAHEAD)] if n + SCORE_LOOKAHEAD < npairs else []
            groups.append(ahead + [("value", s, n)])
        return groups

    project = lambda s: [("project", s, name) for name in ("x_rnn", "kv", "q")]
    gates = lambda s: [("gate_piece", s, i) for i in range((D_RNN + D_ATT) // GATE_GROUP)]
    outs = lambda s: [("out_piece", s, j) for j in range(n_out)]
    pa, pb = project(a), project(b)
    ga, gb, oa = gates(a), gates(b), outs(a)
    prog = [("sample_scores", a), ("norm", a), pa[0], ("norm", b), pa[1], pa[2], ("store_kv", a)]
    recur = lambda s: [(name, s, g) for g in range(D_RNN // GATE_GROUP) for name in ("gate_math", "scan")]
    prog += [("conv_gates", a)] + recur(a) + pb
    prog += _spread(attention(a), ga + [("rnn_gate", a)])
    prog += [("att_finish", a), ("conv_gates", b), ("store_kv", b)] + recur(b) + gb
    prog += _spread(attention(b), oa + [("rnn_gate", b)])
    prog += [("finish", a), ("att_finish", b)] + outs(b) + [("sample_values", a), ("finish", b)]
    return prog


def _t5_bucket_np(dist):
    dist = np.maximum(dist, 0)
    max_exact = N_BUCKETS // 2
    d = np.maximum(dist, 1).astype(np.float32)
    ratio = np.log(d / np.float32(max_exact)) / np.float32(math.log(MAX_DISTANCE / max_exact))
    large = max_exact + (ratio * np.float32(N_BUCKETS - max_exact)).astype(np.int32)
    large = np.minimum(large, N_BUCKETS - 1)
    return np.where(dist < max_exact, dist, large).astype(np.int32)


def _folded_bucket_table():
    i = np.arange(WINDOW)[:, None]
    j = np.arange(WINDOW)[None, :]
    dist = np.where(j <= i, i - j, i + WINDOW - j)
    return _t5_bucket_np(dist)


def _silu(x):
    h = 0.5 * x
    return h + h * jnp.tanh(h)


def _softplus(x):
    return jnp.maximum(x, 0.0) + jnp.log1p(jnp.exp(-jnp.abs(x)))


def _rms_scale(x):
    return lax.rsqrt(jnp.mean(x * x, axis=-1, keepdims=True) + EPS)


def _dot(a, b):
    return jnp.dot(a, b, preferred_element_type=F32)


def _dot_nt(a, b):
    return lax.dot_general(a, b, (((1,), (1,)), ((), ())), preferred_element_type=F32)


def _rglru_gate_dots(xh, wg_ref):
    xcb = xh.astype(BF16)
    zs = [_dot(xcb[:, g * GATE_GROUP:(g + 1) * GATE_GROUP], wg_ref[g])
          for g in range(D_RNN // GATE_GROUP)]
    za = jnp.concatenate([z[:, :GATE_GROUP] for z in zs], axis=1)
    zx = jnp.concatenate([z[:, GATE_GROUP:] for z in zs], axis=1)
    return za, zx


def _rglru_gate_math(xh, za, zx, rp):
    ix = (1.0 + jnp.tanh(zx + rp[ROW_GATE_X_B:ROW_GATE_X_B + 1])) * xh
    rate = (-0.5 * LRU_C * LOG2E) * _softplus(-rp[ROW_LAMBDA:ROW_LAMBDA + 1])
    a = jnp.exp2((1.0 + jnp.tanh(za + rp[ROW_GATE_A_B:ROW_GATE_A_B + 1])) * rate)
    u = 1.0 - a * a
    bx = (u * lax.rsqrt(jnp.maximum(u, TINY))) * ix
    return a, bx


def _prepare_params(step, raw, rp_ref, win_ref, wg_ref, wout_ref):
    cw_raw, cb_raw, ba_raw, bx_raw, lam_raw, win_raw, wga_raw, wgx_raw, wout_raw = raw
    rows = pl.ds(pl.multiple_of(step * PREP_ROWS, PREP_ROWS), PREP_ROWS)
    for c0 in range(0, D_IN, GATE_GROUP):
        w = win_raw[:, c0:c0 + GATE_GROUP]
        if C_Q <= c0 < C_Q + D_ATT:
            w = w * HEAD_DIM ** -0.5
        win_ref[rows, c0:c0 + GATE_GROUP] = w.astype(BF16)
    wout_ref[rows, :] = wout_raw[...].astype(BF16)

    @pl.when(step == 0)
    def _():
        rp_ref[0:CONV_W, :] = 0.5 * cw_raw[...]
        rp_ref[ROW_CONV_B:ROW_CONV_B + 1, :] = 0.5 * cb_raw[...]
        rp_ref[ROW_GATE_A_B:ROW_GATE_A_B + 1, :] = 0.5 * ba_raw[...]
        rp_ref[ROW_GATE_X_B:ROW_GATE_X_B + 1, :] = 0.5 * bx_raw[...]
        rp_ref[ROW_LAMBDA:ROW_LAMBDA + 1, :] = lam_raw[...]
        per = GATE_GROUP // RNN_BLOCK
        zero = jnp.zeros((RNN_BLOCK, RNN_BLOCK), F32)
        for g in range(D_RNN // GATE_GROUP):
            for half, w_raw in enumerate((wga_raw, wgx_raw)):
                rows_g = [jnp.concatenate([w_raw[g * per + m] if k == m else zero for k in range(per)], axis=1)
                          for m in range(per)]
                wg_ref[g, :, half * GATE_GROUP:(half + 1) * GATE_GROUP] = jnp.concatenate(rows_g, axis=0).astype(BF16)


def _build_bias_table(bkt, rb_ref, head):
    tbl = jnp.zeros(bkt.shape, F32)
    for k in range(N_BUCKETS):
        tbl = jnp.where(bkt == k, rb_ref[head, k], tbl)
    return tbl


_SampleRefs = collections.namedtuple(
    "_SampleRefs", "x bktrow sc h0 kc vc y conv rnn ko vo qz o knew vnew rnnout gatt bias sink")


def _layer_kernel(rb_ref, sinks_ref, x_ref, bkt_ref, gpre_ref, gpost_ref,
                  cw_raw, cb_raw, ba_raw, bx_raw, lam_raw, win_raw, wga_raw, wgx_raw, wout_raw,
                  xs_ref, bktrow_ref, sc_ref, h0_ref, kc_ref, vc_ref,
                  y_ref, conv_ref, rnn_ref, kw_ref, vw_ref, ys_ref, convs_ref, rnns_ref, ko_ref, vo_ref,
                  rp_ref, win_ref, wg_ref, wout_ref, *scratch, tb, nseq, bb, nprep):
    step = pl.program_id(0)

    @pl.when(step < nprep)
    def _():
        _prepare_params(step, (cw_raw, cb_raw, ba_raw, bx_raw, lam_raw, win_raw, wga_raw, wgx_raw, wout_raw),
                        rp_ref, win_ref, wg_ref, wout_ref)

        @pl.when(step == 0)
        def _():
            bias_ref = scratch[0]
            bkt = bkt_ref[...]
            for h in range(N_HEADS):
                bias_ref[h] = _build_bias_table(bkt, rb_ref, h)

    @pl.when(step >= nprep)
    def _():
        _layer_step(step - nprep, pl.num_programs(0) - nprep - 1,
                    rb_ref, sinks_ref, x_ref, bkt_ref, gpre_ref, gpost_ref, rp_ref, win_ref, wg_ref, wout_ref,
                    xs_ref, bktrow_ref, sc_ref, h0_ref, kc_ref, vc_ref,
                    y_ref, conv_ref, rnn_ref, kw_ref, vw_ref, ys_ref, convs_ref, rnns_ref, ko_ref, vo_ref,
                    *scratch, tb=tb, nseq=nseq, bb=bb)


def _layer_step(t, last_t, rb_ref, sinks_ref, x_ref, bkt_ref, gpre_ref, gpost_ref, rp_ref, win_ref, wg_ref, wout_ref,
                xs_ref, bktrow_ref, sc_ref, h0_ref, kc_ref, vc_ref,
                y_ref, conv_ref, rnn_ref, kw_ref, vw_ref, ys_ref, convs_ref, rnns_ref, ko_ref, vo_ref,
                bias_ref, xpad_ref, hc_ref, kd_ref, vt_ref, att_ref, xr_ref, hn_ref,
                qz_ref, o_ref, knew_ref, vnew_ref, rnnout_ref, gatt_ref, sbias_ref, ssink_ref, *, tb, nseq, bb):
    nq = tb // WINDOW
    sr = _SampleRefs(xs_ref, bktrow_ref, sc_ref, h0_ref, kc_ref, vc_ref, ys_ref, convs_ref, rnns_ref, ko_ref, vo_ref,
                     qz_ref, o_ref, knew_ref, vnew_ref, rnnout_ref, gatt_ref, sbias_ref, ssink_ref)

    @pl.when(t == 0)
    def _():
        xpad_ref[...] = jnp.zeros((nseq, (CONV_W - 1) * SUBLANES, D_RNN), F32)
        hc_ref[...] = jnp.zeros((nseq, 1, D_RNN), F32)
        kd_ref[:, :, 0:WINDOW, :] = jnp.zeros((nseq, N_KV_HEADS, WINDOW, LANES), BF16)
        vt_ref[:, :, 0:HEAD_DIM, 0:WINDOW] = jnp.zeros((nseq, N_KV_HEADS, HEAD_DIM, WINDOW), BF16)
        vt_ref[:, :, HEAD_DIM:, :] = jnp.ones((nseq, N_KV_HEADS, ONES_ROWS, tb + WINDOW), BF16)
        _sample_setup(sr, rb_ref, sinks_ref, gpre_ref, rp_ref, win_ref, wg_ref)

    rp = rp_ref[...]
    sub8 = lax.broadcasted_iota(jnp.int32, (SUBLANES, D_RNN), 0)
    lo = lax.broadcasted_iota(jnp.int32, (tb, LANES), 1) < HEAD_DIM
    key_idx = lax.broadcasted_iota(jnp.int32, (WINDOW, WINDOW), 0)
    qry_idx = lax.broadcasted_iota(jnp.int32, (WINDOW, WINDOW), 1)
    own = key_idx <= qry_idx
    lane_lo = qry_idx < HEAD_DIM
    has_prev0 = (jnp.zeros((WINDOW, WINDOW), jnp.int32) + t) > 0
    zero_b = jnp.zeros((WINDOW, WINDOW), BF16)
    pair_lo = lax.broadcasted_iota(jnp.int32, (1, 2 * WINDOW), 1) < WINDOW
    pairs = [(qi, c) for qi in range(nq) for c in range(D_ATT // LANES)]
    gate_cols = [C_GRNN, C_GRNN + GATE_GROUP, C_GATT, C_GATT + GATE_GROUP]

    def norm(s, st):
        x = x_ref[s]
        st["xn"] = (x * _rms_scale(x) * gpre_ref[...]).astype(BF16)

    def project(s, st, name):
        c0, width = dict(x_rnn=(C_XRNN, D_RNN), kv=(C_KV, 2 * KV_DIM), q=(C_Q, D_ATT))[name]
        st[name] = _dot(st["xn"], win_ref[:, c0:c0 + width])

    clen = tb // SUBLANES
    pitch = clen + SUBLANES
    ncol = D_RNN // LANES
    ntail = CONV_W - 1

    def conv_gates(s, st):
        x_rnn = st["x_rnn"]
        for c in range(ncol):
            for k in range(SUBLANES):
                xr_ref[s, c, k * pitch:k * pitch + clen, :] = x_rnn[k * clen:(k + 1) * clen, c * LANES:(c + 1) * LANES]
        xp = jnp.concatenate(
            [jnp.concatenate([xr_ref[s, c, pl.ds(v, SUBLANES, stride=pitch), :] for v in range(clen)], axis=0)
             for c in range(ncol)], axis=1)
        tail = xp[(clen - ntail) * SUBLANES:, :]
        prev_tail = xpad_ref[s]
        heads = [jnp.where(sub8 == 0,
                           pltpu.roll(prev_tail[j * SUBLANES:(j + 1) * SUBLANES], 1, axis=0),
                           pltpu.roll(tail[j * SUBLANES:(j + 1) * SUBLANES], 1, axis=0)) for j in range(ntail)]
        xh = rp[ROW_CONV_B:ROW_CONV_B + 1]
        for tap in range(ntail):
            shift = ntail - tap
            xh = xh + rp[tap:tap + 1] * jnp.concatenate(heads[ntail - shift:] + [xp[:(clen - shift) * SUBLANES]], axis=0)
        xh = xh + rp[ntail:ntail + 1] * xp
        xpad_ref[s] = tail
        for j in range(ntail):
            conv_ref[j, s:s + 1, :] = tail[j * SUBLANES + SUBLANES - 1:(j + 1) * SUBLANES]
        st["xh"] = xh
        st["za"], st["zx"] = _rglru_gate_dots(xh, wg_ref)

    def gate_math(s, st, g):
        cols = slice(g * GATE_GROUP, (g + 1) * GATE_GROUP)
        st[("a", g)], st[("bx", g)] = _rglru_gate_math(st["xh"][:, cols], st["za"][:, cols], st["zx"][:, cols], rp[:, cols])

    def store_kv(s, st):
        st["qb"] = st.pop("q").astype(BF16)
        k01 = st["kv"][:, :KV_DIM]
        v01 = st["kv"][:, KV_DIM:]
        k10 = pltpu.roll(k01, HEAD_DIM, axis=1)
        kd_ref[s, 0, WINDOW:WINDOW + tb, :] = jnp.where(lo, k01, k10).astype(BF16)
        kd_ref[s, 1, WINDOW:WINDOW + tb, :] = jnp.where(lo, k10, k01).astype(BF16)
        v_t = v01.T.astype(BF16)
        vt_ref[s, 0, 0:HEAD_DIM, WINDOW:WINDOW + tb] = v_t[0:HEAD_DIM]
        vt_ref[s, 1, 0:HEAD_DIM, WINDOW:WINDOW + tb] = v_t[HEAD_DIM:]
        kw_ref[s] = k01[tb - WINDOW:, :].T
        vw_ref[s] = v01[tb - WINDOW:, :].T

    def gate_piece(s, st, i):
        st[("gate", i)] = _dot(st["xn"], win_ref[:, gate_cols[i]:gate_cols[i] + GATE_GROUP])

    def scores(s, st, qi, c):
        rows = slice(qi * WINDOW, (qi + 1) * WINDOW)
        win = slice(qi * WINDOW, qi * WINDOW + 2 * WINDOW)
        kvh = (2 * c) // GROUP
        qc = st["qb"][rows, c * LANES:(c + 1) * LANES]
        qpair = jnp.concatenate([jnp.where(lane_lo, qc, zero_b), jnp.where(lane_lo, zero_b, qc)], axis=0)
        sc = _dot_nt(kd_ref[s, kvh, win, :], qpair)
        p_prev, p_own, ms = [], [], []
        for half in range(2):
            head = 2 * c + half
            cols = slice(half * WINDOW, (half + 1) * WINDOW)
            sf = jnp.where(own, sc[WINDOW:, cols], sc[:WINDOW, cols]) + bias_ref[head]
            if qi == 0:
                sf = jnp.where(own | has_prev0, sf, NEG_INF)
            m = jnp.max(sf, axis=0, keepdims=True)
            e = jnp.exp(sf - m).astype(BF16)
            p_prev.append(jnp.where(own, zero_b, e))
            p_own.append(jnp.where(own, e, zero_b))
            ms.append(m)
        p = jnp.concatenate([jnp.concatenate(p_prev, axis=1), jnp.concatenate(p_own, axis=1)], axis=0)
        return p, jnp.concatenate(ms, axis=1)

    def values(s, st, qi, c, p, m):
        rows = slice(qi * WINDOW, (qi + 1) * WINDOW)
        win = slice(qi * WINDOW, qi * WINDOW + 2 * WINDOW)
        kvh = (2 * c) // GROUP
        oa = _dot(vt_ref[s, kvh, :, win], p)
        sink = jnp.where(pair_lo, sinks_ref[2 * c], sinks_ref[2 * c + 1])
        den = oa[HEAD_DIM:HEAD_DIM + 1, :] + jnp.exp(sink - m)
        o = oa[0:HEAD_DIM, :] * (1.0 / den)
        o_t = jnp.concatenate([o[:, :WINDOW], o[:, WINDOW:]], axis=0)
        att_ref[s, rows, c * LANES:(c + 1) * LANES] = o_t.T

    def score(s, st, n):
        st[("p", n)] = scores(s, st, *pairs[n])

    def value(s, st, n):
        values(s, st, *pairs[n], *st.pop(("p", n)))

    def att_finish(s, st):
        kd_ref[s, :, 0:WINDOW, :] = kd_ref[s, :, tb:tb + WINDOW, :]
        vt_ref[s, :, :, 0:WINDOW] = vt_ref[s, :, :, tb:tb + WINDOW]
        g_att = jnp.concatenate([st[("gate", 2)], st[("gate", 3)]], axis=1)
        st["att_out"] = (att_ref[s] * _silu(g_att)).astype(BF16)

    def scan(s, st, g):
        cols = slice(g * GATE_GROUP, (g + 1) * GATE_GROUP)
        a3 = st.pop(("a", g)).reshape(clen, SUBLANES, GATE_GROUP)
        b3 = st.pop(("bx", g)).reshape(clen, SUBLANES, GATE_GROUP)
        hl, pr = [b3[0]], [a3[0]]
        for v in range(1, clen):
            hl.append(a3[v] * hl[-1] + b3[v])
            pr.append(a3[v] * pr[-1])
        h_end, p_end = hl[-1], pr[-1]
        h_in = jnp.broadcast_to(hc_ref[s], (SUBLANES, D_RNN))[:, cols]
        first = lax.broadcasted_iota(jnp.int32, (SUBLANES, GATE_GROUP), 0) == 0
        entry = h_in
        for _ in range(SUBLANES - 1):
            entry = jnp.where(first, h_in, pltpu.roll(h_end + p_end * entry, 1, axis=0))
        carry = (h_end + p_end * entry)[SUBLANES - 1:SUBLANES, :]
        st[("carry", g)] = carry
        if g == D_RNN // GATE_GROUP - 1:
            carry = jnp.concatenate([st.pop(("carry", i)) for i in range(g + 1)], axis=1)
            hc_ref[s] = carry
            rnn_ref[0, s:s + 1, :] = carry
        per = GATE_GROUP // LANES
        for v in range(clen):
            hv = hl[v] + pr[v] * entry
            for c in range(per):
                hn_ref[s, g * per + c, pl.ds(v, SUBLANES, stride=pitch), :] = hv[:, c * LANES:(c + 1) * LANES]

    def rnn_gate(s, st):
        h = jnp.concatenate(
            [jnp.concatenate([hn_ref[s, c, k * pitch:k * pitch + clen, :] for k in range(SUBLANES)], axis=0)
             for c in range(ncol)], axis=1)
        g_rnn = jnp.concatenate([st[("gate", 0)], st[("gate", 1)]], axis=1)
        st["rnn_out"] = (h * _silu(g_rnn)).astype(BF16)

    def out_piece(s, st, j):
        cols = slice(j * GATE_GROUP, (j + 1) * GATE_GROUP)
        st[("out", j)] = _dot(st["rnn_out"], wout_ref[0:D_RNN, cols]) + _dot(st["att_out"], wout_ref[D_RNN:, cols])

    def finish(s, st):
        out = jnp.concatenate([st[("out", j)] for j in range(D_MODEL // GATE_GROUP)], axis=1)
        y_ref[s] = x_ref[s] + out * _rms_scale(out) * gpost_ref[...]

    def sample_scores(s, st):
        _sample_scores(sr, t, bb, st)

    def sample_values(s, st):
        _sample_values(sr, t, bb, st)

    tasks = dict(sample_scores=sample_scores, sample_values=sample_values, norm=norm, project=project,
                 conv_gates=conv_gates, gate_math=gate_math, store_kv=store_kv, gate_piece=gate_piece,
                 score=score, value=value, att_finish=att_finish, scan=scan, rnn_gate=rnn_gate, out_piece=out_piece, finish=finish)
    sts = [dict() for _ in range(nseq)]
    for name, s, *arg in _step_program(len(pairs)):
        tasks[name](s, sts[s], *arg)

    @pl.when(t == last_t)
    def _():
        _sample_finish(sr, gpost_ref, wout_ref)


def _layer_call(x, rb, sinks, bkt, gpre, gpost, raw, xs, bktrow, sc, h0, kc_t, vc_t):
    bsz, seq, _ = x.shape
    tb = PROMPT_BLOCK
    nt = seq // tb
    nprep = D_MODEL // PREP_ROWS
    nb = xs.shape[0]
    bb = nb // nt
    layer_t = lambda t: jnp.maximum(t - nprep, 0)
    prep_t = lambda t: jnp.minimum(t, nprep - 1)
    win_t_spec = pl.BlockSpec((bb, KV_DIM, WINDOW), lambda t: (layer_t(t), 0, 0))
    x_spec = pl.BlockSpec((bsz, tb, D_MODEL), lambda t: (0, layer_t(t), 0))
    const = lambda *shape: pl.BlockSpec(shape, lambda t: (0,) * len(shape), pipeline_mode=pl.Buffered(1))
    whole = lambda *shape: pl.BlockSpec(shape, lambda t: (0,) * len(shape))
    smem = pl.BlockSpec(memory_space=pltpu.SMEM)
    out_shapes = (
        jax.ShapeDtypeStruct((bsz, seq, D_MODEL), F32),
        jax.ShapeDtypeStruct((CONV_W - 1, bsz, D_RNN), F32),
        jax.ShapeDtypeStruct((1, bsz, D_RNN), F32),
        jax.ShapeDtypeStruct((bsz, KV_DIM, WINDOW), F32),
        jax.ShapeDtypeStruct((bsz, KV_DIM, WINDOW), F32),
        jax.ShapeDtypeStruct((nb, 1, D_MODEL), F32),
        jax.ShapeDtypeStruct((CONV_W - 1, nb, D_RNN), F32),
        jax.ShapeDtypeStruct((nb, D_RNN), F32),
        jax.ShapeDtypeStruct((nb, KV_DIM, WINDOW), F32),
        jax.ShapeDtypeStruct((nb, KV_DIM, WINDOW), F32),
    )
    return pl.pallas_call(
        functools.partial(_layer_kernel, tb=tb, nseq=bsz, bb=bb, nprep=nprep),
        grid=(nprep + nt,),
        in_specs=[
            smem, smem,
            x_spec,
            const(WINDOW, WINDOW),
            const(1, D_MODEL), const(1, D_MODEL),
            const(CONV_W, D_RNN), const(1, D_RNN), const(1, D_RNN), const(1, D_RNN), const(1, D_RNN),
            pl.BlockSpec((PREP_ROWS, D_IN), lambda t: (prep_t(t), 0)),
            const(N_RNN_BLOCKS, RNN_BLOCK, RNN_BLOCK), const(N_RNN_BLOCKS, RNN_BLOCK, RNN_BLOCK),
            pl.BlockSpec((PREP_ROWS, D_MODEL), lambda t: (prep_t(t), 0)),
            const(nb, 1, D_MODEL), const(1, LANES), const(CONV_W - 1, nb, D_RNN), const(nb, D_RNN),
            win_t_spec, win_t_spec,
        ],
        out_specs=(
            x_spec,
            whole(CONV_W - 1, bsz, D_RNN),
            whole(1, bsz, D_RNN),
            whole(bsz, KV_DIM, WINDOW),
            whole(bsz, KV_DIM, WINDOW),
            whole(nb, 1, D_MODEL), whole(CONV_W - 1, nb, D_RNN), whole(nb, D_RNN),
            win_t_spec, win_t_spec,
        ),
        out_shape=out_shapes,
        scratch_shapes=[
            pltpu.VMEM((SUBLANES, D_RNN), F32),
            pltpu.VMEM((D_MODEL, D_IN), BF16),
            pltpu.VMEM((D_RNN // GATE_GROUP, GATE_GROUP, 2 * GATE_GROUP), BF16),
            pltpu.VMEM((D_MODEL, D_MODEL), BF16),
            pltpu.VMEM((N_HEADS, WINDOW, WINDOW), F32),
            pltpu.VMEM((bsz, (CONV_W - 1) * SUBLANES, D_RNN), F32),
            pltpu.VMEM((bsz, 1, D_RNN), F32),
            pltpu.VMEM((bsz, N_KV_HEADS, tb + WINDOW, LANES), BF16),
            pltpu.VMEM((bsz, N_KV_HEADS, HEAD_DIM + ONES_ROWS, tb + WINDOW), BF16),
            pltpu.VMEM((bsz, tb, D_ATT), F32),
            pltpu.VMEM((bsz, D_RNN // LANES, tb + SUBLANES * SUBLANES, LANES), F32),
            pltpu.VMEM((bsz, D_RNN // LANES, tb + SUBLANES * SUBLANES, LANES), F32),
            pltpu.VMEM((nb * N_HEADS, LANES), F32),
            pltpu.VMEM((nb * N_HEADS, LANES), F32),
            pltpu.VMEM((nb, KV_DIM), F32),
            pltpu.VMEM((nb, KV_DIM), F32),
            pltpu.VMEM((nb, D_RNN), F32),
            pltpu.VMEM((nb, D_ATT), F32),
            pltpu.VMEM((SUBLANES, LANES), F32),
            pltpu.VMEM((SUBLANES, LANES), F32),
        ],
        compiler_params=pltpu.CompilerParams(
            dimension_semantics=("arbitrary",),
            vmem_limit_bytes=VMEM_LIMIT),
        name="layer_step",
    )(rb, sinks, x, bkt, gpre, gpost, *raw, xs, bktrow, sc, h0, kc_t, vc_t)


def _sample_setup(sr, rb_ref, sinks_ref, gpre_ref, rp_ref, win_ref, wg_ref):
    nb = sr.x.shape[0]
    sub = lax.broadcasted_iota(jnp.int32, (SUBLANES, LANES), 0)
    lane_lo = lax.broadcasted_iota(jnp.int32, (nb, LANES), 1) < HEAD_DIM
    bkt = jnp.broadcast_to(sr.bktrow[...], (SUBLANES, LANES))
    bias = jnp.zeros((SUBLANES, LANES), F32)
    sinkm = jnp.zeros((SUBLANES, LANES), F32)
    for h in range(N_HEADS):
        bias = jnp.where(sub == h, _build_bias_table(bkt, rb_ref, h), bias)
        sinkm = jnp.where(sub == h, sinks_ref[h], sinkm)
    sr.bias[...] = bias
    sr.sink[...] = sinkm

    x = sr.x[:, 0, :]
    xn = (x * _rms_scale(x) * gpre_ref[...]).astype(BF16)
    rp = rp_ref[...]
    x_rnn = _dot(xn, win_ref[:, C_XRNN:C_XRNN + D_RNN])
    xh = rp[ROW_CONV_B:ROW_CONV_B + 1]
    for tap in range(CONV_W - 1):
        xh = xh + rp[tap:tap + 1] * sr.sc[tap]
    xh = xh + rp[CONV_W - 1:CONV_W] * x_rnn
    sr.conv[0] = sr.sc[1]
    sr.conv[1] = sr.sc[2]
    sr.conv[2] = x_rnn
    a, bx = _rglru_gate_math(xh, *_rglru_gate_dots(xh, wg_ref), rp)
    h = a * sr.h0[...] + bx
    sr.rnn[...] = h
    g_rnn = _dot(xn, win_ref[:, C_GRNN:C_GRNN + D_RNN])
    sr.rnnout[...] = h * _silu(g_rnn)
    sr.gatt[...] = _dot(xn, win_ref[:, C_GATT:C_GATT + D_ATT])
    kv = _dot(xn, win_ref[:, C_KV:C_KV + 2 * KV_DIM])
    sr.knew[...] = kv[:, :KV_DIM]
    sr.vnew[...] = kv[:, KV_DIM:]
    q = _dot(xn, win_ref[:, C_Q:C_Q + D_ATT])
    for head in range(N_HEADS):
        c = head // 2
        kvh = head // GROUP
        qc = q[:, c * LANES:(c + 1) * LANES]
        if (head % 2) != kvh:
            qc = pltpu.roll(qc, HEAD_DIM, axis=1)
        qz = jnp.where(lane_lo, qc, 0.0) if kvh == 0 else jnp.where(lane_lo, 0.0, qc)
        sr.qz[pl.ds(head, nb, stride=N_HEADS), :] = qz


def _sample_scores(sr, step, bb, st):
    newest = lax.broadcasted_iota(jnp.int32, (WINDOW, LANES), 0) == WINDOW - 1
    bias = sr.bias[...]
    st["rows"] = [pl.multiple_of((step * bb + bi) * N_HEADS, N_HEADS) for bi in range(bb)]
    st["scores"] = []
    for bi in range(bb):
        newk = jnp.where(newest, sr.knew[pl.ds(step * bb + bi, 1), :], pltpu.roll(sr.kc[bi].T, WINDOW - 1, axis=0))
        sr.ko[bi] = newk.T
        qz = sr.qz[pl.ds(st["rows"][bi], N_HEADS), :].astype(BF16)
        st["scores"].append(_dot_nt(qz, newk.astype(BF16)) + bias)


def _sample_values(sr, step, bb, st):
    newest = lax.broadcasted_iota(jnp.int32, (WINDOW, LANES), 0) == WINDOW - 1
    sinkm = sr.sink[...]
    for bi, s in enumerate(st.pop("scores")):
        m = jnp.maximum(jnp.max(s, axis=1, keepdims=True), sinkm)
        e = jnp.exp(s - m)
        den = jnp.sum(e, axis=1, keepdims=True) + jnp.exp(sinkm - m)
        p = (e * (1.0 / den)).astype(BF16)
        newv = jnp.where(newest, sr.vnew[pl.ds(step * bb + bi, 1), :], pltpu.roll(sr.vc[bi].T, WINDOW - 1, axis=0))
        sr.vo[bi] = newv.T
        sr.o[pl.ds(st["rows"][bi], N_HEADS), :] = _dot(p, newv.astype(BF16))


def _sample_finish(sr, gpost_ref, wout_ref):
    nb = sr.x.shape[0]
    lane_lo = lax.broadcasted_iota(jnp.int32, (nb, LANES), 1) < HEAD_DIM
    cols = []
    for c in range(D_ATT // LANES):
        kvh = (2 * c) // GROUP
        halves = []
        for half in range(2):
            oh = sr.o[pl.ds(2 * c + half, nb, stride=N_HEADS), :]
            if half != kvh:
                oh = pltpu.roll(oh, HEAD_DIM, axis=1)
            halves.append(oh)
        cols.append(jnp.where(lane_lo, halves[0], halves[1]))
    att = jnp.concatenate(cols, axis=1)
    att_out = (att * _silu(sr.gatt[...])).astype(BF16)
    rnn_out = sr.rnnout[...].astype(BF16)
    out = _dot(rnn_out, wout_ref[0:D_RNN, :]) + _dot(att_out, wout_ref[D_RNN:, :])
    sr.y[:, 0, :] = sr.x[:, 0, :] + out * _rms_scale(out) * gpost_ref[...]


def kernel(x_prompt, x_sample, state_conv, state_rnn, cache_k_win, cache_v_win, norm_pre, norm_post,
           w_in, conv_w, conv_b, w_gate_a, b_gate_a, w_gate_x, b_gate_x, lru_lambda, attn_sinks,
           rel_bias, w_out):
    assert w_in.shape[0] == 1, "single-layer trunk"
    bsz, seq, _ = x_prompt.shape
    nb = x_sample.shape[0]
    wb = cache_k_win.shape[2]
    assert wb == WINDOW and x_sample.shape[1] == 1 and seq % PROMPT_BLOCK == 0 and nb % (seq // PROMPT_BLOCK) == 0
    assert bsz == PROMPT_SEQS, "the step program is written for this many prompt sequences"

    raw = (conv_w[0], conv_b, b_gate_a, b_gate_x, lru_lambda, w_in[0], w_gate_a[0], w_gate_x[0], w_out[0])
    gpre = norm_pre.reshape(1, D_MODEL)
    gpost = norm_post.reshape(1, D_MODEL)
    sinks = attn_sinks[0]
    bkt_np = _folded_bucket_table()
    bkt = jnp.asarray(np.ascontiguousarray(bkt_np.T))
    bktrow = jnp.asarray(bkt_np[WINDOW - 1:WINDOW, :])

    feature_major = lambda z: jnp.transpose(z[0], (0, 2, 3, 1)).reshape(nb, KV_DIM, wb)
    y_p, conv_p, rnn_p, kw_p, vw_p, y_s, conv_s, rnn_s, kw_s, vw_s = _layer_call(
        x_prompt, rel_bias.T, sinks, bkt, gpre, gpost, raw,
        x_sample, bktrow, jnp.transpose(state_conv[0], (1, 0, 2)), state_rnn[0],
        feature_major(cache_k_win), feature_major(cache_v_win))

    kv5 = lambda z, n: jnp.transpose(z.reshape(n, N_KV_HEADS, HEAD_DIM, wb), (0, 3, 1, 2))[None]
    return (y_p, y_s,
            jnp.transpose(conv_p, (1, 0, 2))[None], rnn_p, kv5(kw_p, bsz), kv5(vw_p, bsz),
            jnp.transpose(conv_s, (1, 0, 2))[None], rnn_s[None], kv5(kw_s, nb), kv5(vw_s, nb))
```

```python
import collections
import functools
import math

import numpy as np
import jax
import jax.numpy as jnp
from jax import lax
from jax.experimental import pallas as pl
from jax.experimental.pallas import tpu as pltpu

D_MODEL = 1024
D_RNN = 512
D_ATT = 512
HEAD_DIM = 64
N_HEADS = 8
N_KV_HEADS = 2
GROUP = N_HEADS // N_KV_HEADS
KV_DIM = N_KV_HEADS * HEAD_DIM
N_RNN_BLOCKS = 8
RNN_BLOCK = D_RNN // N_RNN_BLOCKS
CONV_W = 4
LRU_C = 8.0
WINDOW = 128
N_BUCKETS = 32
MAX_DISTANCE = 128
EPS = 1e-6
NEG_INF = -1e30
D_IN = 2 * D_RNN + 2 * D_ATT + 2 * KV_DIM

C_XRNN = 0
C_GRNN = D_RNN
C_Q = 2 * D_RNN
C_KV = 2 * D_RNN + D_ATT
C_GATT = 2 * D_RNN + D_ATT + 2 * KV_DIM

SUBLANES = 8
LANES = 128
GATE_GROUP = 256
ONES_ROWS = 16
SCORE_LOOKAHEAD = 8
PROMPT_SEQS = 2
PROMPT_BLOCK = 512
PREP_ROWS = 256
V7X_VMEM_BYTES = 64 * 1024 * 1024
VMEM_LIMIT = V7X_VMEM_BYTES - 8 * 1024 * 1024

TINY = 1e-37
ROW_CONV_B = CONV_W
ROW_GATE_A_B = CONV_W + 1
ROW_GATE_X_B = CONV_W + 2
ROW_LAMBDA = CONV_W + 3
LOG2E = 1.4426950408889634
F32 = jnp.float32
BF16 = jnp.bfloat16


def _spread(groups, bulk):
    out, done = [], 0
    for i, group in enumerate(groups):
        out += group
        want = (i + 1) * len(bulk) // len(groups)
        out += bulk[done:want]
        done = want
    return out


def _step_program(npairs):
    a, b = range(PROMPT_SEQS)
    n_out = D_MODEL // GATE_GROUP

    def attention(s):
        groups = [[("score", s, n) for n in range(SCORE_LOOKAHEAD)]]
        for n in range(npairs):
            ahead = [("score", s, n + SCORE_LOOKAHEAD)] if n + SCORE_LOOKAHEAD < npairs else []
            groups.append(ahead + [("value", s, n)])
        return groups

    project = lambda s: [("project", s, name) for name in ("x_rnn", "kv", "q")]
    gates = lambda s: [("gate_piece", s, i) for i in range((D_RNN + D_ATT) // GATE_GROUP)]
    outs = lambda s: [("out_piece", s, j) for j in range(n_out)]
    pa, pb = project(a), project(b)
    ga, gb, oa = gates(a), gates(b), outs(a)
    prog = [("sample_scores", a), ("norm", a), pa[0], ("norm", b), pa[1], pa[2], ("store_kv", a)]
    prog += [("conv_gates", a), ("gate_math", a), ("scan", a)] + pb
    prog += _spread(attention(a), ga + [("rnn_gate", a)])
    prog += [("att_finish", a), ("conv_gates", b), ("store_kv", b), ("gate_math", b), ("scan", b)] + gb
    prog += _spread(attention(b), oa + [("rnn_gate", b)])
    prog += [("finish", a), ("att_finish", b)] + outs(b) + [("sample_values", a), ("finish", b)]
    return prog


def _t5_bucket_np(dist):
    dist = np.maximum(dist, 0)
    max_exact = N_BUCKETS // 2
    d = np.maximum(dist, 1).astype(np.float32)
    ratio = np.log(d / np.float32(max_exact)) / np.float32(math.log(MAX_DISTANCE / max_exact))
    large = max_exact + (ratio * np.float32(N_BUCKETS - max_exact)).astype(np.int32)
    large = np.minimum(large, N_BUCKETS - 1)
    return np.where(dist < max_exact, dist, large).astype(np.int32)


def _folded_bucket_table():
    i = np.arange(WINDOW)[:, None]
    j = np.arange(WINDOW)[None, :]
    dist = np.where(j <= i, i - j, i + WINDOW - j)
    return _t5_bucket_np(dist)


def _silu(x):
    h = 0.5 * x
    return h + h * jnp.tanh(h)


def _softplus(x):
    return jnp.maximum(x, 0.0) + jnp.log1p(jnp.exp(-jnp.abs(x)))


def _rms_scale(x):
    return lax.rsqrt(jnp.mean(x * x, axis=-1, keepdims=True) + EPS)


def _dot(a, b):
    return jnp.dot(a, b, preferred_element_type=F32)


def _dot_nt(a, b):
    return lax.dot_general(a, b, (((1,), (1,)), ((), ())), preferred_element_type=F32)


def _rglru_gate_dots(xh, wg_ref):
    xcb = xh.astype(BF16)
    zs = [_dot(xcb[:, g * GATE_GROUP:(g + 1) * GATE_GROUP], wg_ref[g])
          for g in range(D_RNN // GATE_GROUP)]
    za = jnp.concatenate([z[:, :GATE_GROUP] for z in zs], axis=1)
    zx = jnp.concatenate([z[:, GATE_GROUP:] for z in zs], axis=1)
    return za, zx


def _rglru_gate_math(xh, za, zx, rp):
    ix = (1.0 + jnp.tanh(zx + rp[ROW_GATE_X_B:ROW_GATE_X_B + 1])) * xh
    rate = (-0.5 * LRU_C * LOG2E) * _softplus(-rp[ROW_LAMBDA:ROW_LAMBDA + 1])
    a = jnp.exp2((1.0 + jnp.tanh(za + rp[ROW_GATE_A_B:ROW_GATE_A_B + 1])) * rate)
    u = 1.0 - a * a
    bx = (u * lax.rsqrt(jnp.maximum(u, TINY))) * ix
    return a, bx


def _prepare_params(step, raw, rp_ref, win_ref, wg_ref, wout_ref):
    cw_raw, cb_raw, ba_raw, bx_raw, lam_raw, win_raw, wga_raw, wgx_raw, wout_raw = raw
    rows = pl.ds(pl.multiple_of(step * PREP_ROWS, PREP_ROWS), PREP_ROWS)
    for c0 in range(0, D_IN, GATE_GROUP):
        w = win_raw[:, c0:c0 + GATE_GROUP]
        if C_Q <= c0 < C_Q + D_ATT:
            w = w * HEAD_DIM ** -0.5
        win_ref[rows, c0:c0 + GATE_GROUP] = w.astype(BF16)
    wout_ref[rows, :] = wout_raw[...].astype(BF16)

    @pl.when(step == 0)
    def _():
        rp_ref[0:CONV_W, :] = 0.5 * cw_raw[...]
        rp_ref[ROW_CONV_B:ROW_CONV_B + 1, :] = 0.5 * cb_raw[...]
        rp_ref[ROW_GATE_A_B:ROW_GATE_A_B + 1, :] = 0.5 * ba_raw[...]
        rp_ref[ROW_GATE_X_B:ROW_GATE_X_B + 1, :] = 0.5 * bx_raw[...]
        rp_ref[ROW_LAMBDA:ROW_LAMBDA + 1, :] = lam_raw[...]
        per = GATE_GROUP // RNN_BLOCK
        zero = jnp.zeros((RNN_BLOCK, RNN_BLOCK), F32)
        for g in range(D_RNN // GATE_GROUP):
            for half, w_raw in enumerate((wga_raw, wgx_raw)):
                rows_g = [jnp.concatenate([w_raw[g * per + m] if k == m else zero for k in range(per)], axis=1)
                          for m in range(per)]
                wg_ref[g, :, half * GATE_GROUP:(half + 1) * GATE_GROUP] = jnp.concatenate(rows_g, axis=0).astype(BF16)


def _build_bias_table(bkt, rb_ref, head):
    tbl = jnp.zeros(bkt.shape, F32)
    for k in range(N_BUCKETS):
        tbl = jnp.where(bkt == k, rb_ref[head, k], tbl)
    return tbl


_SampleRefs = collections.namedtuple(
    "_SampleRefs", "x bktrow sc h0 kc vc y conv rnn ko vo qz o knew vnew rnnout gatt bias sink")


def _layer_kernel(rb_ref, sinks_ref, x_ref, bkt_ref, gpre_ref, gpost_ref,
                  cw_raw, cb_raw, ba_raw, bx_raw, lam_raw, win_raw, wga_raw, wgx_raw, wout_raw,
                  xs_ref, bktrow_ref, sc_ref, h0_ref, kc_ref, vc_ref,
                  y_ref, conv_ref, rnn_ref, kw_ref, vw_ref, ys_ref, convs_ref, rnns_ref, ko_ref, vo_ref,
                  rp_ref, win_ref, wg_ref, wout_ref, *scratch, tb, nseq, bb, nprep):
    step = pl.program_id(0)

    @pl.when(step < nprep)
    def _():
        _prepare_params(step, (cw_raw, cb_raw, ba_raw, bx_raw, lam_raw, win_raw, wga_raw, wgx_raw, wout_raw),
                        rp_ref, win_ref, wg_ref, wout_ref)

        @pl.when(step == 0)
        def _():
            bias_ref = scratch[0]
            bkt = bkt_ref[...]
            for h in range(N_HEADS):
                bias_ref[h] = _build_bias_table(bkt, rb_ref, h)

    @pl.when(step >= nprep)
    def _():
        _layer_step(step - nprep, pl.num_programs(0) - nprep - 1,
                    rb_ref, sinks_ref, x_ref, bkt_ref, gpre_ref, gpost_ref, rp_ref, win_ref, wg_ref, wout_ref,
                    xs_ref, bktrow_ref, sc_ref, h0_ref, kc_ref, vc_ref,
                    y_ref, conv_ref, rnn_ref, kw_ref, vw_ref, ys_ref, convs_ref, rnns_ref, ko_ref, vo_ref,
                    *scratch, tb=tb, nseq=nseq, bb=bb)


def _layer_step(t, last_t, rb_ref, sinks_ref, x_ref, bkt_ref, gpre_ref, gpost_ref, rp_ref, win_ref, wg_ref, wout_ref,
                xs_ref, bktrow_ref, sc_ref, h0_ref, kc_ref, vc_ref,
                y_ref, conv_ref, rnn_ref, kw_ref, vw_ref, ys_ref, convs_ref, rnns_ref, ko_ref, vo_ref,
                bias_ref, xpad_ref, hc_ref, kd_ref, vt_ref, att_ref, xr_ref, hn_ref,
                qz_ref, o_ref, knew_ref, vnew_ref, rnnout_ref, gatt_ref, sbias_ref, ssink_ref, *, tb, nseq, bb):
    nq = tb // WINDOW
    sr = _SampleRefs(xs_ref, bktrow_ref, sc_ref, h0_ref, kc_ref, vc_ref, ys_ref, convs_ref, rnns_ref, ko_ref, vo_ref,
                     qz_ref, o_ref, knew_ref, vnew_ref, rnnout_ref, gatt_ref, sbias_ref, ssink_ref)

    @pl.when(t == 0)
    def _():
        xpad_ref[...] = jnp.zeros((nseq, (CONV_W - 1) * SUBLANES, D_RNN), F32)
        hc_ref[...] = jnp.zeros((nseq, 1, D_RNN), F32)
        kd_ref[:, :, 0:WINDOW, :] = jnp.zeros((nseq, N_KV_HEADS, WINDOW, LANES), BF16)
        vt_ref[:, :, 0:HEAD_DIM, 0:WINDOW] = jnp.zeros((nseq, N_KV_HEADS, HEAD_DIM, WINDOW), BF16)
        vt_ref[:, :, HEAD_DIM:, :] = jnp.ones((nseq, N_KV_HEADS, ONES_ROWS, tb + WINDOW), BF16)
        _sample_setup(sr, rb_ref, sinks_ref, gpre_ref, rp_ref, win_ref, wg_ref)

    rp = rp_ref[...]
    sub8 = lax.broadcasted_iota(jnp.int32, (SUBLANES, D_RNN), 0)
    lo = lax.broadcasted_iota(jnp.int32, (tb, LANES), 1) < HEAD_DIM
    key_idx = lax.broadcasted_iota(jnp.int32, (WINDOW, WINDOW), 0)
    qry_idx = lax.broadcasted_iota(jnp.int32, (WINDOW, WINDOW), 1)
    own = key_idx <= qry_idx
    lane_lo = qry_idx < HEAD_DIM
    has_prev0 = (jnp.zeros((WINDOW, WINDOW), jnp.int32) + t) > 0
    zero_b = jnp.zeros((WINDOW, WINDOW), BF16)
    pair_lo = lax.broadcasted_iota(jnp.int32, (1, 2 * WINDOW), 1) < WINDOW
    pairs = [(qi, c) for qi in range(nq) for c in range(D_ATT // LANES)]
    gate_cols = [C_GRNN, C_GRNN + GATE_GROUP, C_GATT, C_GATT + GATE_GROUP]

    def norm(s, st):
        parts = []
        for r in range(0, tb, WINDOW):
            x = x_ref[s, r:r + WINDOW, :]
            parts.append((x * _rms_scale(x) * gpre_ref[...]).astype(BF16))
        st["xn"] = jnp.concatenate(parts, axis=0)

    def project(s, st, name):
        c0, width = dict(x_rnn=(C_XRNN, D_RNN), kv=(C_KV, 2 * KV_DIM), q=(C_Q, D_ATT))[name]
        st[name] = _dot(st["xn"], win_ref[:, c0:c0 + width])

    clen = tb // SUBLANES
    pitch = clen + SUBLANES
    ncol = D_RNN // LANES
    ntail = CONV_W - 1

    def conv_gates(s, st):
        x_rnn = st["x_rnn"]
        for c in range(ncol):
            for k in range(SUBLANES):
                xr_ref[s, c, k * pitch:k * pitch + clen, :] = x_rnn[k * clen:(k + 1) * clen, c * LANES:(c + 1) * LANES]
        xp = jnp.concatenate(
            [jnp.concatenate([xr_ref[s, c, pl.ds(v, SUBLANES, stride=pitch), :] for v in range(clen)], axis=0)
             for c in range(ncol)], axis=1)
        tail = xp[(clen - ntail) * SUBLANES:, :]
        prev_tail = xpad_ref[s]
        heads = [jnp.where(sub8 == 0,
                           pltpu.roll(prev_tail[j * SUBLANES:(j + 1) * SUBLANES], 1, axis=0),
                           pltpu.roll(tail[j * SUBLANES:(j + 1) * SUBLANES], 1, axis=0)) for j in range(ntail)]
        xh = rp[ROW_CONV_B:ROW_CONV_B + 1]
        for tap in range(ntail):
            shift = ntail - tap
            xh = xh + rp[tap:tap + 1] * jnp.concatenate(heads[ntail - shift:] + [xp[:(clen - shift) * SUBLANES]], axis=0)
        xh = xh + rp[ntail:ntail + 1] * xp
        xpad_ref[s] = tail
        for j in range(ntail):
            conv_ref[j, s:s + 1, :] = tail[j * SUBLANES + SUBLANES - 1:(j + 1) * SUBLANES]
        st["xh"] = xh
        st["za"], st["zx"] = _rglru_gate_dots(xh, wg_ref)

    def gate_math(s, st):
        st["a"], st["bx"] = _rglru_gate_math(st.pop("xh"), st.pop("za"), st.pop("zx"), rp)

    def store_kv(s, st):
        st["qb"] = st.pop("q").astype(BF16)
        k01 = st["kv"][:, :KV_DIM]
        v01 = st["kv"][:, KV_DIM:]
        k10 = pltpu.roll(k01, HEAD_DIM, axis=1)
        kd_ref[s, 0, WINDOW:WINDOW + tb, :] = jnp.where(lo, k01, k10).astype(BF16)
        kd_ref[s, 1, WINDOW:WINDOW + tb, :] = jnp.where(lo, k10, k01).astype(BF16)
        v_t = v01.T.astype(BF16)
        vt_ref[s, 0, 0:HEAD_DIM, WINDOW:WINDOW + tb] = v_t[0:HEAD_DIM]
        vt_ref[s, 1, 0:HEAD_DIM, WINDOW:WINDOW + tb] = v_t[HEAD_DIM:]
        kw_ref[s] = k01[tb - WINDOW:, :].T
        vw_ref[s] = v01[tb - WINDOW:, :].T

    def gate_piece(s, st, i):
        st[("gate", i)] = _dot(st["xn"], win_ref[:, gate_cols[i]:gate_cols[i] + GATE_GROUP])

    def scores(s, st, qi, c):
        rows = slice(qi * WINDOW, (qi + 1) * WINDOW)
        win = slice(qi * WINDOW, qi * WINDOW + 2 * WINDOW)
        kvh = (2 * c) // GROUP
        qc = st["qb"][rows, c * LANES:(c + 1) * LANES]
        qpair = jnp.concatenate([jnp.where(lane_lo, qc, zero_b), jnp.where(lane_lo, zero_b, qc)], axis=0)
        sc = _dot_nt(kd_ref[s, kvh, win, :], qpair)
        p_prev, p_own, ms = [], [], []
        for half in range(2):
            head = 2 * c + half
            cols = slice(half * WINDOW, (half + 1) * WINDOW)
            sf = jnp.where(own, sc[WINDOW:, cols], sc[:WINDOW, cols]) + bias_ref[head]
            if qi == 0:
                sf = jnp.where(own | has_prev0, sf, NEG_INF)
            m = jnp.max(sf, axis=0, keepdims=True)
            e = jnp.exp(sf - m).astype(BF16)
            p_prev.append(jnp.where(own, zero_b, e))
            p_own.append(jnp.where(own, e, zero_b))
            ms.append(m)
        p = jnp.concatenate([jnp.concatenate(p_prev, axis=1), jnp.concatenate(p_own, axis=1)], axis=0)
        return p, jnp.concatenate(ms, axis=1)

    def values(s, st, qi, c, p, m):
        rows = slice(qi * WINDOW, (qi + 1) * WINDOW)
        win = slice(qi * WINDOW, qi * WINDOW + 2 * WINDOW)
        kvh = (2 * c) // GROUP
        oa = _dot(vt_ref[s, kvh, :, win], p)
        sink = jnp.where(pair_lo, sinks_ref[2 * c], sinks_ref[2 * c + 1])
        den = oa[HEAD_DIM:HEAD_DIM + 1, :] + jnp.exp(sink - m)
        o = oa[0:HEAD_DIM, :] * (1.0 / den)
        o_t = jnp.concatenate([o[:, :WINDOW], o[:, WINDOW:]], axis=0)
        att_ref[s, rows, c * LANES:(c + 1) * LANES] = o_t.T

    def score(s, st, n):
        st[("p", n)] = scores(s, st, *pairs[n])

    def value(s, st, n):
        values(s, st, *pairs[n], *st.pop(("p", n)))

    def att_finish(s, st):
        kd_ref[s, :, 0:WINDOW, :] = kd_ref[s, :, tb:tb + WINDOW, :]
        vt_ref[s, :, :, 0:WINDOW] = vt_ref[s, :, :, tb:tb + WINDOW]
        g_att = jnp.concatenate([st[("gate", 2)], st[("gate", 3)]], axis=1)
        st["att_out"] = (att_ref[s] * _silu(g_att)).astype(BF16)

    def scan(s, st):
        a3 = st["a"].reshape(clen, SUBLANES, D_RNN)
        b3 = st["bx"].reshape(clen, SUBLANES, D_RNN)
        hl, pr = [b3[0]], [a3[0]]
        for v in range(1, clen):
            hl.append(a3[v] * hl[-1] + b3[v])
            pr.append(a3[v] * pr[-1])
        h_end, p_end = hl[-1], pr[-1]
        h_in = jnp.broadcast_to(hc_ref[s], (SUBLANES, D_RNN))
        entry = h_in
        for _ in range(SUBLANES - 1):
            entry = jnp.where(sub8 == 0, h_in, pltpu.roll(h_end + p_end * entry, 1, axis=0))
        carry = (h_end + p_end * entry)[SUBLANES - 1:SUBLANES, :]
        hc_ref[s] = carry
        rnn_ref[0, s:s + 1, :] = carry
        for v in range(clen):
            hv = hl[v] + pr[v] * entry
            for c in range(ncol):
                hn_ref[s, c, pl.ds(v, SUBLANES, stride=pitch), :] = hv[:, c * LANES:(c + 1) * LANES]
        h = jnp.concatenate(
            [jnp.concatenate([hn_ref[s, c, k * pitch:k * pitch + clen, :] for k in range(SUBLANES)], axis=0)
             for c in range(ncol)], axis=1)
        st["h"] = h

    def rnn_gate(s, st):
        g_rnn = jnp.concatenate([st[("gate", 0)], st[("gate", 1)]], axis=1)
        st["rnn_out"] = (st.pop("h") * _silu(g_rnn)).astype(BF16)

    def out_piece(s, st, j):
        cols = slice(j * GATE_GROUP, (j + 1) * GATE_GROUP)
        st[("out", j)] = _dot(st["rnn_out"], wout_ref[0:D_RNN, cols]) + _dot(st["att_out"], wout_ref[D_RNN:, cols])

    def finish(s, st):
        for r in range(0, tb, WINDOW):
            rows = slice(r, r + WINDOW)
            out = jnp.concatenate([st[("out", j)][rows] for j in range(D_MODEL // GATE_GROUP)], axis=1)
            y_ref[s, rows, :] = x_ref[s, rows, :] + out * _rms_scale(out) * gpost_ref[...]

    def sample_scores(s, st):
        _sample_scores(sr, t, bb, st)

    def sample_values(s, st):
        _sample_values(sr, t, bb, st)

    tasks = dict(sample_scores=sample_scores, sample_values=sample_values, norm=norm, project=project,
                 conv_gates=conv_gates, gate_math=gate_math, store_kv=store_kv, gate_piece=gate_piece,
                 score=score, value=value, att_finish=att_finish, scan=scan, rnn_gate=rnn_gate, out_piece=out_piece, finish=finish)
    sts = [dict() for _ in range(nseq)]
    for name, s, *arg in _step_program(len(pairs)):
        tasks[name](s, sts[s], *arg)

    @pl.when(t == last_t)
    def _():
        _sample_finish(sr, gpost_ref, wout_ref)


def _layer_call(x, rb, sinks, bkt, gpre, gpost, raw, xs, bktrow, sc, h0, kc_t, vc_t):
    bsz, seq, _ = x.shape
    tb = PROMPT_BLOCK
    nt = seq // tb
    nprep = D_MODEL // PREP_ROWS
    nb = xs.shape[0]
    bb = nb // nt
    layer_t = lambda t: jnp.maximum(t - nprep, 0)
    prep_t = lambda t: jnp.minimum(t, nprep - 1)
    win_t_spec = pl.BlockSpec((bb, KV_DIM, WINDOW), lambda t: (layer_t(t), 0, 0))
    x_spec = pl.BlockSpec((bsz, tb, D_MODEL), lambda t: (0, layer_t(t), 0))
    const = lambda *shape: pl.BlockSpec(shape, lambda t: (0,) * len(shape), pipeline_mode=pl.Buffered(1))
    whole = lambda *shape: pl.BlockSpec(shape, lambda t: (0,) * len(shape))
    smem = pl.BlockSpec(memory_space=pltpu.SMEM)
    out_shapes = (
        jax.ShapeDtypeStruct((bsz, seq, D_MODEL), F32),
        jax.ShapeDtypeStruct((CONV_W - 1, bsz, D_RNN), F32),
        jax.ShapeDtypeStruct((1, bsz, D_RNN), F32),
        jax.ShapeDtypeStruct((bsz, KV_DIM, WINDOW), F32),
        jax.ShapeDtypeStruct((bsz, KV_DIM, WINDOW), F32),
        jax.ShapeDtypeStruct((nb, 1, D_MODEL), F32),
        jax.ShapeDtypeStruct((CONV_W - 1, nb, D_RNN), F32),
        jax.ShapeDtypeStruct((nb, D_RNN), F32),
        jax.ShapeDtypeStruct((nb, KV_DIM, WINDOW), F32),
        jax.ShapeDtypeStruct((nb, KV_DIM, WINDOW), F32),
    )
    return pl.pallas_call(
        functools.partial(_layer_kernel, tb=tb, nseq=bsz, bb=bb, nprep=nprep),
        grid=(nprep + nt,),
        in_specs=[
            smem, smem,
            x_spec,
            const(WINDOW, WINDOW),
            const(1, D_MODEL), const(1, D_MODEL),
            const(CONV_W, D_RNN), const(1, D_RNN), const(1, D_RNN), const(1, D_RNN), const(1, D_RNN),
            pl.BlockSpec((PREP_ROWS, D_IN), lambda t: (prep_t(t), 0)),
            const(N_RNN_BLOCKS, RNN_BLOCK, RNN_BLOCK), const(N_RNN_BLOCKS, RNN_BLOCK, RNN_BLOCK),
            pl.BlockSpec((PREP_ROWS, D_MODEL), lambda t: (prep_t(t), 0)),
            const(nb, 1, D_MODEL), const(1, LANES), const(CONV_W - 1, nb, D_RNN), const(nb, D_RNN),
            win_t_spec, win_t_spec,
        ],
        out_specs=(
            x_spec,
            whole(CONV_W - 1, bsz, D_RNN),
            whole(1, bsz, D_RNN),
            whole(bsz, KV_DIM, WINDOW),
            whole(bsz, KV_DIM, WINDOW),
            whole(nb, 1, D_MODEL), whole(CONV_W - 1, nb, D_RNN), whole(nb, D_RNN),
            win_t_spec, win_t_spec,
        ),
        out_shape=out_shapes,
        scratch_shapes=[
            pltpu.VMEM((SUBLANES, D_RNN), F32),
            pltpu.VMEM((D_MODEL, D_IN), BF16),
            pltpu.VMEM((D_RNN // GATE_GROUP, GATE_GROUP, 2 * GATE_GROUP), BF16),
            pltpu.VMEM((D_MODEL, D_MODEL), BF16),
            pltpu.VMEM((N_HEADS, WINDOW, WINDOW), F32),
            pltpu.VMEM((bsz, (CONV_W - 1) * SUBLANES, D_RNN), F32),
            pltpu.VMEM((bsz, 1, D_RNN), F32),
            pltpu.VMEM((bsz, N_KV_HEADS, tb + WINDOW, LANES), BF16),
            pltpu.VMEM((bsz, N_KV_HEADS, HEAD_DIM + ONES_ROWS, tb + WINDOW), BF16),
            pltpu.VMEM((bsz, tb, D_ATT), F32),
            pltpu.VMEM((bsz, D_RNN // LANES, tb + SUBLANES * SUBLANES, LANES), F32),
            pltpu.VMEM((bsz, D_RNN // LANES, tb + SUBLANES * SUBLANES, LANES), F32),
            pltpu.VMEM((nb * N_HEADS, LANES), F32),
            pltpu.VMEM((nb * N_HEADS, LANES), F32),
            pltpu.VMEM((nb, KV_DIM), F32),
            pltpu.VMEM((nb, KV_DIM), F32),
            pltpu.VMEM((nb, D_RNN), F32),
            pltpu.VMEM((nb, D_ATT), F32),
            pltpu.VMEM((SUBLANES, LANES), F32),
            pltpu.VMEM((SUBLANES, LANES), F32),
        ],
        compiler_params=pltpu.CompilerParams(
            dimension_semantics=("arbitrary",),
            vmem_limit_bytes=VMEM_LIMIT),
        name="layer_step",
    )(rb, sinks, x, bkt, gpre, gpost, *raw, xs, bktrow, sc, h0, kc_t, vc_t)


def _sample_setup(sr, rb_ref, sinks_ref, gpre_ref, rp_ref, win_ref, wg_ref):
    nb = sr.x.shape[0]
    sub = lax.broadcasted_iota(jnp.int32, (SUBLANES, LANES), 0)
    lane_lo = lax.broadcasted_iota(jnp.int32, (nb, LANES), 1) < HEAD_DIM
    bkt = jnp.broadcast_to(sr.bktrow[...], (SUBLANES, LANES))
    bias = jnp.zeros((SUBLANES, LANES), F32)
    sinkm = jnp.zeros((SUBLANES, LANES), F32)
    for h in range(N_HEADS):
        bias = jnp.where(sub == h, _build_bias_table(bkt, rb_ref, h), bias)
        sinkm = jnp.where(sub == h, sinks_ref[h], sinkm)
    sr.bias[...] = bias
    sr.sink[...] = sinkm

    x = sr.x[:, 0, :]
    xn = (x * _rms_scale(x) * gpre_ref[...]).astype(BF16)
    rp = rp_ref[...]
    x_rnn = _dot(xn, win_ref[:, C_XRNN:C_XRNN + D_RNN])
    xh = rp[ROW_CONV_B:ROW_CONV_B + 1]
    for tap in range(CONV_W - 1):
        xh = xh + rp[tap:tap + 1] * sr.sc[tap]
    xh = xh + rp[CONV_W - 1:CONV_W] * x_rnn
    sr.conv[0] = sr.sc[1]
    sr.conv[1] = sr.sc[2]
    sr.conv[2] = x_rnn
    a, bx = _rglru_gate_math(xh, *_rglru_gate_dots(xh, wg_ref), rp)
    h = a * sr.h0[...] + bx
    sr.rnn[...] = h
    g_rnn = _dot(xn, win_ref[:, C_GRNN:C_GRNN + D_RNN])
    sr.rnnout[...] = h * _silu(g_rnn)
    sr.gatt[...] = _dot(xn, win_ref[:, C_GATT:C_GATT + D_ATT])
    kv = _dot(xn, win_ref[:, C_KV:C_KV + 2 * KV_DIM])
    sr.knew[...] = kv[:, :KV_DIM]
    sr.vnew[...] = kv[:, KV_DIM:]
    q = _dot(xn, win_ref[:, C_Q:C_Q + D_ATT])
    for head in range(N_HEADS):
        c = head // 2
        kvh = head // GROUP
        qc = q[:, c * LANES:(c + 1) * LANES]
        if (head % 2) != kvh:
            qc = pltpu.roll(qc, HEAD_DIM, axis=1)
        qz = jnp.where(lane_lo, qc, 0.0) if kvh == 0 else jnp.where(lane_lo, 0.0, qc)
        sr.qz[pl.ds(head, nb, stride=N_HEADS), :] = qz


def _sample_scores(sr, step, bb, st):
    newest = lax.broadcasted_iota(jnp.int32, (WINDOW, LANES), 0) == WINDOW - 1
    bias = sr.bias[...]
    st["rows"] = [pl.multiple_of((step * bb + bi) * N_HEADS, N_HEADS) for bi in range(bb)]
    st["scores"] = []
    for bi in range(bb):
        newk = jnp.where(newest, sr.knew[pl.ds(step * bb + bi, 1), :], pltpu.roll(sr.kc[bi].T, WINDOW - 1, axis=0))
        sr.ko[bi] = newk.T
        qz = sr.qz[pl.ds(st["rows"][bi], N_HEADS), :].astype(BF16)
        st["scores"].append(_dot_nt(qz, newk.astype(BF16)) + bias)


def _sample_values(sr, step, bb, st):
    newest = lax.broadcasted_iota(jnp.int32, (WINDOW, LANES), 0) == WINDOW - 1
    sinkm = sr.sink[...]
    for bi, s in enumerate(st.pop("scores")):
        m = jnp.maximum(jnp.max(s, axis=1, keepdims=True), sinkm)
        e = jnp.exp(s - m)
        den = jnp.sum(e, axis=1, keepdims=True) + jnp.exp(sinkm - m)
        p = (e * (1.0 / den)).astype(BF16)
        newv = jnp.where(newest, sr.vnew[pl.ds(step * bb + bi, 1), :], pltpu.roll(sr.vc[bi].T, WINDOW - 1, axis=0))
        sr.vo[bi] = newv.T
        sr.o[pl.ds(st["rows"][bi], N_HEADS), :] = _dot(p, newv.astype(BF16))


def _sample_finish(sr, gpost_ref, wout_ref):
    nb = sr.x.shape[0]
    lane_lo = lax.broadcasted_iota(jnp.int32, (nb, LANES), 1) < HEAD_DIM
    cols = []
    for c in range(D_ATT // LANES):
        kvh = (2 * c) // GROUP
        halves = []
        for half in range(2):
            oh = sr.o[pl.ds(2 * c + half, nb, stride=N_HEADS), :]
            if half != kvh:
                oh = pltpu.roll(oh, HEAD_DIM, axis=1)
            halves.append(oh)
        cols.append(jnp.where(lane_lo, halves[0], halves[1]))
    att = jnp.concatenate(cols, axis=1)
    att_out = (att * _silu(sr.gatt[...])).astype(BF16)
    rnn_out = sr.rnnout[...].astype(BF16)
    out = _dot(rnn_out, wout_ref[0:D_RNN, :]) + _dot(att_out, wout_ref[D_RNN:, :])
    sr.y[:, 0, :] = sr.x[:, 0, :] + out * _rms_scale(out) * gpost_ref[...]


def kernel(x_prompt, x_sample, state_conv, state_rnn, cache_k_win, cache_v_win, norm_pre, norm_post,
           w_in, conv_w, conv_b, w_gate_a, b_gate_a, w_gate_x, b_gate_x, lru_lambda, attn_sinks,
           rel_bias, w_out):
    assert w_in.shape[0] == 1, "single-layer trunk"
    bsz, seq, _ = x_prompt.shape
    nb = x_sample.shape[0]
    wb = cache_k_win.shape[2]
    assert wb == WINDOW and x_sample.shape[1] == 1 and seq % PROMPT_BLOCK == 0 and nb % (seq // PROMPT_BLOCK) == 0
    assert bsz == PROMPT_SEQS, "the step program is written for this many prompt sequences"

    raw = (conv_w[0], conv_b, b_gate_a, b_gate_x, lru_lambda, w_in[0], w_gate_a[0], w_gate_x[0], w_out[0])
    gpre = norm_pre.reshape(1, D_MODEL)
    gpost = norm_post.reshape(1, D_MODEL)
    sinks = attn_sinks[0]
    bkt_np = _folded_bucket_table()
    bkt = jnp.asarray(np.ascontiguousarray(bkt_np.T))
    bktrow = jnp.asarray(bkt_np[WINDOW - 1:WINDOW, :])

    feature_major = lambda z: jnp.transpose(z[0], (0, 2, 3, 1)).reshape(nb, KV_DIM, wb)
    y_p, conv_p, rnn_p, kw_p, vw_p, y_s, conv_s, rnn_s, kw_s, vw_s = _layer_call(
        x_prompt, rel_bias.T, sinks, bkt, gpre, gpost, raw,
        x_sample, bktrow, jnp.transpose(state_conv[0], (1, 0, 2)), state_rnn[0],
        feature_major(cache_k_win), feature_major(cache_v_win))

    kv5 = lambda z, n: jnp.transpose(z.reshape(n, N_KV_HEADS, HEAD_DIM, wb), (0, 3, 1, 2))[None]
    return (y_p, y_s,
            jnp.transpose(conv_p, (1, 0, 2))[None], rnn_p, kv5(kw_p, bsz), kv5(vw_p, bsz),
            jnp.transpose(conv_s, (1, 0, 2))[None], rnn_s[None], kv5(kw_s, nb), kv5(vw_s, nb))
```

```python
import collections
import functools
import math

import numpy as np
import jax
import jax.numpy as jnp
from jax import lax
from jax.experimental import pallas as pl
from jax.experimental.pallas import tpu as pltpu

D_MODEL = 1024
D_RNN = 512
D_ATT = 512
HEAD_DIM = 64
N_HEADS = 8
N_KV_HEADS = 2
GROUP = N_HEADS // N_KV_HEADS
KV_DIM = N_KV_HEADS * HEAD_DIM
N_RNN_BLOCKS = 8
RNN_BLOCK = D_RNN // N_RNN_BLOCKS
CONV_W = 4
LRU_C = 8.0
WINDOW = 128
N_BUCKETS = 32
MAX_DISTANCE = 128
EPS = 1e-6
NEG_INF = -1e30
D_IN = 2 * D_RNN + 2 * D_ATT + 2 * KV_DIM

C_XRNN = 0
C_GRNN = D_RNN
C_Q = 2 * D_RNN
C_KV = 2 * D_RNN + D_ATT
C_GATT = 2 * D_RNN + D_ATT + 2 * KV_DIM

SUBLANES = 8
LANES = 128
GATE_GROUP = 256
ONES_ROWS = 16
SCORE_LOOKAHEAD = 8
PROMPT_SEQS = 2
PROMPT_BLOCK = 512
PREP_ROWS = 256
V7X_VMEM_BYTES = 64 * 1024 * 1024
VMEM_LIMIT = V7X_VMEM_BYTES - 8 * 1024 * 1024

TINY = 1e-37
ROW_CONV_B = CONV_W
ROW_GATE_A_B = CONV_W + 1
ROW_GATE_X_B = CONV_W + 2
ROW_LAMBDA = CONV_W + 3
LOG2E = 1.4426950408889634
F32 = jnp.float32
BF16 = jnp.bfloat16


def _spread(groups, bulk):
    out, done = [], 0
    for i, group in enumerate(groups):
        out += group
        want = (i + 1) * len(bulk) // len(groups)
        out += bulk[done:want]
        done = want
    return out


def _step_program(npairs):
    a, b = range(PROMPT_SEQS)
    n_out = D_MODEL // GATE_GROUP

    def attention(s):
        groups = [[("score", s, n) for n in range(SCORE_LOOKAHEAD)]]
        for n in range(npairs):
            ahead = [("score", s, n + SCORE_LOOKAHEAD)] if n + SCORE_LOOKAHEAD < npairs else []
            groups.append(ahead + [("value", s, n)])
        return groups

    project = lambda s: [("project", s, name) for name in ("x_rnn", "kv", "q")]
    gates = lambda s: [("gate_piece", s, i) for i in range((D_RNN + D_ATT) // GATE_GROUP)]
    outs = lambda s: [("out_piece", s, j) for j in range(n_out)]
    pa, pb = project(a), project(b)
    ga, gb, oa = gates(a), gates(b), outs(a)
    prog = [("sample_scores", a), ("norm", a), pa[0], ("norm", b), pa[1], pa[2], ("store_kv", a)]
    prog += [("conv_gates", a), ("gate_math", a), ("scan", a)] + pb
    prog += _spread(attention(a), ga + [("rnn_gate", a)])
    prog += [("att_finish", a), ("conv_gates", b), ("store_kv", b), ("gate_math", b), ("scan", b)] + gb
    prog += _spread(attention(b), oa + [("rnn_gate", b)])
    prog += [("finish", a), ("att_finish", b)] + outs(b) + [("sample_values", a), ("finish", b)]
    return prog


def _t5_bucket_np(dist):
    dist = np.maximum(dist, 0)
    max_exact = N_BUCKETS // 2
    d = np.maximum(dist, 1).astype(np.float32)
    ratio = np.log(d / np.float32(max_exact)) / np.float32(math.log(MAX_DISTANCE / max_exact))
    large = max_exact + (ratio * np.float32(N_BUCKETS - max_exact)).astype(np.int32)
    large = np.minimum(large, N_BUCKETS - 1)
    return np.where(dist < max_exact, dist, large).astype(np.int32)


def _folded_bucket_table():
    i = np.arange(WINDOW)[:, None]
    j = np.arange(WINDOW)[None, :]
    dist = np.where(j <= i, i - j, i + WINDOW - j)
    return _t5_bucket_np(dist)


def _silu(x):
    h = 0.5 * x
    return h + h * jnp.tanh(h)


def _softplus(x):
    return jnp.maximum(x, 0.0) + jnp.log1p(jnp.exp(-jnp.abs(x)))


def _rms_scale(x):
    return lax.rsqrt(jnp.mean(x * x, axis=-1, keepdims=True) + EPS)


def _dot(a, b):
    return jnp.dot(a, b, preferred_element_type=F32)


def _dot_nt(a, b):
    return lax.dot_general(a, b, (((1,), (1,)), ((), ())), preferred_element_type=F32)


def _rglru_gate_dots(xh, wg_ref):
    xcb = xh.astype(BF16)
    zs = [_dot(xcb[:, g * GATE_GROUP:(g + 1) * GATE_GROUP], wg_ref[g])
          for g in range(D_RNN // GATE_GROUP)]
    za = jnp.concatenate([z[:, :GATE_GROUP] for z in zs], axis=1)
    zx = jnp.concatenate([z[:, GATE_GROUP:] for z in zs], axis=1)
    return za, zx


def _rglru_gate_math(xh, za, zx, rp):
    ix = (1.0 + jnp.tanh(zx + rp[ROW_GATE_X_B:ROW_GATE_X_B + 1])) * xh
    rate = (-0.5 * LRU_C * LOG2E) * _softplus(-rp[ROW_LAMBDA:ROW_LAMBDA + 1])
    a = jnp.exp2((1.0 + jnp.tanh(za + rp[ROW_GATE_A_B:ROW_GATE_A_B + 1])) * rate)
    u = 1.0 - a * a
    bx = (u * lax.rsqrt(jnp.maximum(u, TINY))) * ix
    return a, bx


def _prepare_params(step, raw, rp_ref, win_ref, wg_ref, wout_ref):
    cw_raw, cb_raw, ba_raw, bx_raw, lam_raw, win_raw, wga_raw, wgx_raw, wout_raw = raw
    rows = pl.ds(pl.multiple_of(step * PREP_ROWS, PREP_ROWS), PREP_ROWS)
    for c0 in range(0, D_IN, GATE_GROUP):
        w = win_raw[:, c0:c0 + GATE_GROUP]
        if C_Q <= c0 < C_Q + D_ATT:
            w = w * HEAD_DIM ** -0.5
        win_ref[rows, c0:c0 + GATE_GROUP] = w.astype(BF16)
    wout_ref[rows, :] = wout_raw[...].astype(BF16)

    @pl.when(step == 0)
    def _():
        rp_ref[0:CONV_W, :] = 0.5 * cw_raw[...]
        rp_ref[ROW_CONV_B:ROW_CONV_B + 1, :] = 0.5 * cb_raw[...]
        rp_ref[ROW_GATE_A_B:ROW_GATE_A_B + 1, :] = 0.5 * ba_raw[...]
        rp_ref[ROW_GATE_X_B:ROW_GATE_X_B + 1, :] = 0.5 * bx_raw[...]
        rp_ref[ROW_LAMBDA:ROW_LAMBDA + 1, :] = lam_raw[...]
        per = GATE_GROUP // RNN_BLOCK
        zero = jnp.zeros((RNN_BLOCK, RNN_BLOCK), F32)
        for g in range(D_RNN // GATE_GROUP):
            for half, w_raw in enumerate((wga_raw, wgx_raw)):
                rows_g = [jnp.concatenate([w_raw[g * per + m] if k == m else zero for k in range(per)], axis=1)
                          for m in range(per)]
                wg_ref[g, :, half * GATE_GROUP:(half + 1) * GATE_GROUP] = jnp.concatenate(rows_g, axis=0).astype(BF16)


def _build_bias_table(bkt, rb_ref, head):
    tbl = jnp.zeros(bkt.shape, F32)
    for k in range(N_BUCKETS):
        tbl = jnp.where(bkt == k, rb_ref[head, k], tbl)
    return tbl


_SampleRefs = collections.namedtuple(
    "_SampleRefs", "x bktrow sc h0 kc vc y conv rnn ko vo qz o knew vnew rnnout gatt bias sink")


def _layer_kernel(rb_ref, sinks_ref, x_ref, bkt_ref, gpre_ref, gpost_ref,
                  cw_raw, cb_raw, ba_raw, bx_raw, lam_raw, win_raw, wga_raw, wgx_raw, wout_raw,
                  xs_ref, bktrow_ref, sc_ref, h0_ref, kc_ref, vc_ref,
                  y_ref, conv_ref, rnn_ref, kw_ref, vw_ref, ys_ref, convs_ref, rnns_ref, ko_ref, vo_ref,
                  rp_ref, win_ref, wg_ref, wout_ref, *scratch, tb, nseq, bb, nprep):
    step = pl.program_id(0)

    @pl.when(step < nprep)
    def _():
        _prepare_params(step, (cw_raw, cb_raw, ba_raw, bx_raw, lam_raw, win_raw, wga_raw, wgx_raw, wout_raw),
                        rp_ref, win_ref, wg_ref, wout_ref)

        @pl.when(step == 0)
        def _():
            bias_ref = scratch[0]
            bkt = bkt_ref[...]
            for h in range(N_HEADS):
                bias_ref[h] = _build_bias_table(bkt, rb_ref, h)

    @pl.when(step >= nprep)
    def _():
        _layer_step(step - nprep, pl.num_programs(0) - nprep - 1,
                    rb_ref, sinks_ref, x_ref, bkt_ref, gpre_ref, gpost_ref, rp_ref, win_ref, wg_ref, wout_ref,
                    xs_ref, bktrow_ref, sc_ref, h0_ref, kc_ref, vc_ref,
                    y_ref, conv_ref, rnn_ref, kw_ref, vw_ref, ys_ref, convs_ref, rnns_ref, ko_ref, vo_ref,
                    *scratch, tb=tb, nseq=nseq, bb=bb)


def _layer_step(t, last_t, rb_ref, sinks_ref, x_ref, bkt_ref, gpre_ref, gpost_ref, rp_ref, win_ref, wg_ref, wout_ref,
                xs_ref, bktrow_ref, sc_ref, h0_ref, kc_ref, vc_ref,
                y_ref, conv_ref, rnn_ref, kw_ref, vw_ref, ys_ref, convs_ref, rnns_ref, ko_ref, vo_ref,
                bias_ref, xpad_ref, hc_ref, kd_ref, vt_ref, att_ref, xr_ref, hn_ref,
                qz_ref, o_ref, knew_ref, vnew_ref, rnnout_ref, gatt_ref, sbias_ref, ssink_ref, *, tb, nseq, bb):
    nq = tb // WINDOW
    sr = _SampleRefs(xs_ref, bktrow_ref, sc_ref, h0_ref, kc_ref, vc_ref, ys_ref, convs_ref, rnns_ref, ko_ref, vo_ref,
                     qz_ref, o_ref, knew_ref, vnew_ref, rnnout_ref, gatt_ref, sbias_ref, ssink_ref)

    @pl.when(t == 0)
    def _():
        xpad_ref[...] = jnp.zeros((nseq, (CONV_W - 1) * SUBLANES, D_RNN), F32)
        hc_ref[...] = jnp.zeros((nseq, 1, D_RNN), F32)
        kd_ref[:, :, 0:WINDOW, :] = jnp.zeros((nseq, N_KV_HEADS, WINDOW, LANES), BF16)
        vt_ref[:, :, 0:HEAD_DIM, 0:WINDOW] = jnp.zeros((nseq, N_KV_HEADS, HEAD_DIM, WINDOW), BF16)
        vt_ref[:, :, HEAD_DIM:, :] = jnp.ones((nseq, N_KV_HEADS, ONES_ROWS, tb + WINDOW), BF16)
        _sample_setup(sr, rb_ref, sinks_ref, gpre_ref, rp_ref, win_ref, wg_ref)

    sub8 = lax.broadcasted_iota(jnp.int32, (SUBLANES, D_RNN), 0)
    lo = lax.broadcasted_iota(jnp.int32, (tb, LANES), 1) < HEAD_DIM
    key_idx = lax.broadcasted_iota(jnp.int32, (WINDOW, WINDOW), 0)
    qry_idx = lax.broadcasted_iota(jnp.int32, (WINDOW, WINDOW), 1)
    own = key_idx <= qry_idx
    lane_lo = qry_idx < HEAD_DIM
    has_prev0 = (jnp.zeros((WINDOW, WINDOW), jnp.int32) + t) > 0
    zero_b = jnp.zeros((WINDOW, WINDOW), BF16)
    pair_lo = lax.broadcasted_iota(jnp.int32, (1, 2 * WINDOW), 1) < WINDOW
    pairs = [(qi, c) for qi in range(nq) for c in range(D_ATT // LANES)]
    gate_cols = [C_GRNN, C_GRNN + GATE_GROUP, C_GATT, C_GATT + GATE_GROUP]

    def norm(s, st):
        x = x_ref[s]
        st["xn"] = (x * _rms_scale(x) * gpre_ref[...]).astype(BF16)

    def project(s, st, name):
        c0, width = dict(x_rnn=(C_XRNN, D_RNN), kv=(C_KV, 2 * KV_DIM), q=(C_Q, D_ATT))[name]
        st[name] = _dot(st["xn"], win_ref[:, c0:c0 + width])

    clen = tb // SUBLANES
    pitch = clen + SUBLANES
    ncol = D_RNN // LANES
    ntail = CONV_W - 1

    def conv_gates(s, st):
        x_rnn = st["x_rnn"]
        for c in range(ncol):
            for k in range(SUBLANES):
                xr_ref[s, c, k * pitch:k * pitch + clen, :] = x_rnn[k * clen:(k + 1) * clen, c * LANES:(c + 1) * LANES]
        xp = jnp.concatenate(
            [jnp.concatenate([xr_ref[s, c, pl.ds(v, SUBLANES, stride=pitch), :] for v in range(clen)], axis=0)
             for c in range(ncol)], axis=1)
        tail = xp[(clen - ntail) * SUBLANES:, :]
        prev_tail = xpad_ref[s]
        heads = [jnp.where(sub8 == 0,
                           pltpu.roll(prev_tail[j * SUBLANES:(j + 1) * SUBLANES], 1, axis=0),
                           pltpu.roll(tail[j * SUBLANES:(j + 1) * SUBLANES], 1, axis=0)) for j in range(ntail)]
        rp = rp_ref[...]
        xh = rp[ROW_CONV_B:ROW_CONV_B + 1]
        for tap in range(ntail):
            shift = ntail - tap
            xh = xh + rp[tap:tap + 1] * jnp.concatenate(heads[ntail - shift:] + [xp[:(clen - shift) * SUBLANES]], axis=0)
        xh = xh + rp[ntail:ntail + 1] * xp
        xpad_ref[s] = tail
        for j in range(ntail):
            conv_ref[j, s:s + 1, :] = tail[j * SUBLANES + SUBLANES - 1:(j + 1) * SUBLANES]
        st["xh"] = xh
        st["za"], st["zx"] = _rglru_gate_dots(xh, wg_ref)

    def gate_math(s, st):
        st["a"], st["bx"] = _rglru_gate_math(st.pop("xh"), st.pop("za"), st.pop("zx"), rp_ref[...])

    def store_kv(s, st):
        st["qb"] = st.pop("q").astype(BF16)
        k01 = st["kv"][:, :KV_DIM]
        v01 = st["kv"][:, KV_DIM:]
        k10 = pltpu.roll(k01, HEAD_DIM, axis=1)
        kd_ref[s, 0, WINDOW:WINDOW + tb, :] = jnp.where(lo, k01, k10).astype(BF16)
        kd_ref[s, 1, WINDOW:WINDOW + tb, :] = jnp.where(lo, k10, k01).astype(BF16)
        v_t = v01.T.astype(BF16)
        vt_ref[s, 0, 0:HEAD_DIM, WINDOW:WINDOW + tb] = v_t[0:HEAD_DIM]
        vt_ref[s, 1, 0:HEAD_DIM, WINDOW:WINDOW + tb] = v_t[HEAD_DIM:]
        kw_ref[s] = k01[tb - WINDOW:, :].T
        vw_ref[s] = v01[tb - WINDOW:, :].T

    def gate_piece(s, st, i):
        st[("gate", i)] = _dot(st["xn"], win_ref[:, gate_cols[i]:gate_cols[i] + GATE_GROUP])

    def scores(s, st, qi, c):
        rows = slice(qi * WINDOW, (qi + 1) * WINDOW)
        win = slice(qi * WINDOW, qi * WINDOW + 2 * WINDOW)
        kvh = (2 * c) // GROUP
        qc = st["qb"][rows, c * LANES:(c + 1) * LANES]
        qpair = jnp.concatenate([jnp.where(lane_lo, qc, zero_b), jnp.where(lane_lo, zero_b, qc)], axis=0)
        sc = _dot_nt(kd_ref[s, kvh, win, :], qpair)
        p_prev, p_own, ms = [], [], []
        for half in range(2):
            head = 2 * c + half
            cols = slice(half * WINDOW, (half + 1) * WINDOW)
            sf = jnp.where(own, sc[WINDOW:, cols], sc[:WINDOW, cols]) + bias_ref[head]
            if qi == 0:
                sf = jnp.where(own | has_prev0, sf, NEG_INF)
            m = jnp.max(sf, axis=0, keepdims=True)
            e = jnp.exp(sf - m).astype(BF16)
            p_prev.append(jnp.where(own, zero_b, e))
            p_own.append(jnp.where(own, e, zero_b))
            ms.append(m)
        p = jnp.concatenate([jnp.concatenate(p_prev, axis=1), jnp.concatenate(p_own, axis=1)], axis=0)
        return p, jnp.concatenate(ms, axis=1)

    def values(s, st, qi, c, p, m):
        rows = slice(qi * WINDOW, (qi + 1) * WINDOW)
        win = slice(qi * WINDOW, qi * WINDOW + 2 * WINDOW)
        kvh = (2 * c) // GROUP
        oa = _dot(vt_ref[s, kvh, :, win], p)
        sink = jnp.where(pair_lo, sinks_ref[2 * c], sinks_ref[2 * c + 1])
        den = oa[HEAD_DIM:HEAD_DIM + 1, :] + jnp.exp(sink - m)
        o = oa[0:HEAD_DIM, :] * (1.0 / den)
        o_t = jnp.concatenate([o[:, :WINDOW], o[:, WINDOW:]], axis=0)
        att_ref[s, rows, c * LANES:(c + 1) * LANES] = o_t.T

    def score(s, st, n):
        st[("p", n)] = scores(s, st, *pairs[n])

    def value(s, st, n):
        values(s, st, *pairs[n], *st.pop(("p", n)))

    def att_finish(s, st):
        kd_ref[s, :, 0:WINDOW, :] = kd_ref[s, :, tb:tb + WINDOW, :]
        vt_ref[s, :, :, 0:WINDOW] = vt_ref[s, :, :, tb:tb + WINDOW]
        g_att = jnp.concatenate([st[("gate", 2)], st[("gate", 3)]], axis=1)
        st["att_out"] = (att_ref[s] * _silu(g_att)).astype(BF16)

    def scan(s, st):
        a3 = st["a"].reshape(clen, SUBLANES, D_RNN)
        b3 = st["bx"].reshape(clen, SUBLANES, D_RNN)
        hl, pr = [b3[0]], [a3[0]]
        for v in range(1, clen):
            hl.append(a3[v] * hl[-1] + b3[v])
            pr.append(a3[v] * pr[-1])
        h_end, p_end = hl[-1], pr[-1]
        h_in = jnp.broadcast_to(hc_ref[s], (SUBLANES, D_RNN))
        entry = h_in
        for _ in range(SUBLANES - 1):
            entry = jnp.where(sub8 == 0, h_in, pltpu.roll(h_end + p_end * entry, 1, axis=0))
        carry = (h_end + p_end * entry)[SUBLANES - 1:SUBLANES, :]
        hc_ref[s] = carry
        rnn_ref[0, s:s + 1, :] = carry
        for v in range(clen):
            hv = hl[v] + pr[v] * entry
            for c in range(ncol):
                hn_ref[s, c, pl.ds(v, SUBLANES, stride=pitch), :] = hv[:, c * LANES:(c + 1) * LANES]
        h = jnp.concatenate(
            [jnp.concatenate([hn_ref[s, c, k * pitch:k * pitch + clen, :] for k in range(SUBLANES)], axis=0)
             for c in range(ncol)], axis=1)
        st["h"] = h

    def rnn_gate(s, st):
        g_rnn = jnp.concatenate([st[("gate", 0)], st[("gate", 1)]], axis=1)
        st["rnn_out"] = (st.pop("h") * _silu(g_rnn)).astype(BF16)

    def out_piece(s, st, j):
        cols = slice(j * GATE_GROUP, (j + 1) * GATE_GROUP)
        st[("out", j)] = _dot(st["rnn_out"], wout_ref[0:D_RNN, cols]) + _dot(st["att_out"], wout_ref[D_RNN:, cols])

    def finish(s, st):
        out = jnp.concatenate([st[("out", j)] for j in range(D_MODEL // GATE_GROUP)], axis=1)
        y_ref[s] = x_ref[s] + out * _rms_scale(out) * gpost_ref[...]

    def sample_scores(s, st):
        _sample_scores(sr, t, bb, st)

    def sample_values(s, st):
        _sample_values(sr, t, bb, st)

    tasks = dict(sample_scores=sample_scores, sample_values=sample_values, norm=norm, project=project,
                 conv_gates=conv_gates, gate_math=gate_math, store_kv=store_kv, gate_piece=gate_piece,
                 score=score, value=value, att_finish=att_finish, scan=scan, rnn_gate=rnn_gate, out_piece=out_piece, finish=finish)
    sts = [dict() for _ in range(nseq)]
    for name, s, *arg in _step_program(len(pairs)):
        tasks[name](s, sts[s], *arg)

    @pl.when(t == last_t)
    def _():
        _sample_finish(sr, gpost_ref, wout_ref)


def _layer_call(x, rb, sinks, bkt, gpre, gpost, raw, xs, bktrow, sc, h0, kc_t, vc_t):
    bsz, seq, _ = x.shape
    tb = PROMPT_BLOCK
    nt = seq // tb
    nprep = D_MODEL // PREP_ROWS
    nb = xs.shape[0]
    bb = nb // nt
    layer_t = lambda t: jnp.maximum(t - nprep, 0)
    prep_t = lambda t: jnp.minimum(t, nprep - 1)
    win_t_spec = pl.BlockSpec((bb, KV_DIM, WINDOW), lambda t: (layer_t(t), 0, 0))
    x_spec = pl.BlockSpec((bsz, tb, D_MODEL), lambda t: (0, layer_t(t), 0))
    const = lambda *shape: pl.BlockSpec(shape, lambda t: (0,) * len(shape), pipeline_mode=pl.Buffered(1))
    whole = lambda *shape: pl.BlockSpec(shape, lambda t: (0,) * len(shape))
    smem = pl.BlockSpec(memory_space=pltpu.SMEM)
    out_shapes = (
        jax.ShapeDtypeStruct((bsz, seq, D_MODEL), F32),
        jax.ShapeDtypeStruct((CONV_W - 1, bsz, D_RNN), F32),
        jax.ShapeDtypeStruct((1, bsz, D_RNN), F32),
        jax.ShapeDtypeStruct((bsz, KV_DIM, WINDOW), F32),
        jax.ShapeDtypeStruct((bsz, KV_DIM, WINDOW), F32),
        jax.ShapeDtypeStruct((nb, 1, D_MODEL), F32),
        jax.ShapeDtypeStruct((CONV_W - 1, nb, D_RNN), F32),
        jax.ShapeDtypeStruct((nb, D_RNN), F32),
        jax.ShapeDtypeStruct((nb, KV_DIM, WINDOW), F32),
        jax.ShapeDtypeStruct((nb, KV_DIM, WINDOW), F32),
    )
    return pl.pallas_call(
        functools.partial(_layer_kernel, tb=tb, nseq=bsz, bb=bb, nprep=nprep),
        grid=(nprep + nt,),
        in_specs=[
            smem, smem,
            x_spec,
            const(WINDOW, WINDOW),
            const(1, D_MODEL), const(1, D_MODEL),
            const(CONV_W, D_RNN), const(1, D_RNN), const(1, D_RNN), const(1, D_RNN), const(1, D_RNN),
            pl.BlockSpec((PREP_ROWS, D_IN), lambda t: (prep_t(t), 0)),
            const(N_RNN_BLOCKS, RNN_BLOCK, RNN_BLOCK), const(N_RNN_BLOCKS, RNN_BLOCK, RNN_BLOCK),
            pl.BlockSpec((PREP_ROWS, D_MODEL), lambda t: (prep_t(t), 0)),
            const(nb, 1, D_MODEL), const(1, LANES), const(CONV_W - 1, nb, D_RNN), const(nb, D_RNN),
            win_t_spec, win_t_spec,
        ],
        out_specs=(
            x_spec,
            whole(CONV_W - 1, bsz, D_RNN),
            whole(1, bsz, D_RNN),
            whole(bsz, KV_DIM, WINDOW),
            whole(bsz, KV_DIM, WINDOW),
            whole(nb, 1, D_MODEL), whole(CONV_W - 1, nb, D_RNN), whole(nb, D_RNN),
            win_t_spec, win_t_spec,
        ),
        out_shape=out_shapes,
        scratch_shapes=[
            pltpu.VMEM((SUBLANES, D_RNN), F32),
            pltpu.VMEM((D_MODEL, D_IN), BF16),
            pltpu.VMEM((D_RNN // GATE_GROUP, GATE_GROUP, 2 * GATE_GROUP), BF16),
            pltpu.VMEM((D_MODEL, D_MODEL), BF16),
            pltpu.VMEM((N_HEADS, WINDOW, WINDOW), F32),
            pltpu.VMEM((bsz, (CONV_W - 1) * SUBLANES, D_RNN), F32),
            pltpu.VMEM((bsz, 1, D_RNN), F32),
            pltpu.VMEM((bsz, N_KV_HEADS, tb + WINDOW, LANES), BF16),
            pltpu.VMEM((bsz, N_KV_HEADS, HEAD_DIM + ONES_ROWS, tb + WINDOW), BF16),
            pltpu.VMEM((bsz, tb, D_ATT), F32),
            pltpu.VMEM((bsz, D_RNN // LANES, tb + SUBLANES * SUBLANES, LANES), F32),
            pltpu.VMEM((bsz, D_RNN // LANES, tb + SUBLANES * SUBLANES, LANES), F32),
            pltpu.VMEM((nb * N_HEADS, LANES), F32),
            pltpu.VMEM((nb * N_HEADS, LANES), F32),
            pltpu.VMEM((nb, KV_DIM), F32),
            pltpu.VMEM((nb, KV_DIM), F32),
            pltpu.VMEM((nb, D_RNN), F32),
            pltpu.VMEM((nb, D_ATT), F32),
            pltpu.VMEM((SUBLANES, LANES), F32),
            pltpu.VMEM((SUBLANES, LANES), F32),
        ],
        compiler_params=pltpu.CompilerParams(
            dimension_semantics=("arbitrary",),
            vmem_limit_bytes=VMEM_LIMIT),
        name="layer_step",
    )(rb, sinks, x, bkt, gpre, gpost, *raw, xs, bktrow, sc, h0, kc_t, vc_t)


def _sample_setup(sr, rb_ref, sinks_ref, gpre_ref, rp_ref, win_ref, wg_ref):
    nb = sr.x.shape[0]
    sub = lax.broadcasted_iota(jnp.int32, (SUBLANES, LANES), 0)
    lane_lo = lax.broadcasted_iota(jnp.int32, (nb, LANES), 1) < HEAD_DIM
    bkt = jnp.broadcast_to(sr.bktrow[...], (SUBLANES, LANES))
    bias = jnp.zeros((SUBLANES, LANES), F32)
    sinkm = jnp.zeros((SUBLANES, LANES), F32)
    for h in range(N_HEADS):
        bias = jnp.where(sub == h, _build_bias_table(bkt, rb_ref, h), bias)
        sinkm = jnp.where(sub == h, sinks_ref[h], sinkm)
    sr.bias[...] = bias
    sr.sink[...] = sinkm

    x = sr.x[:, 0, :]
    xn = (x * _rms_scale(x) * gpre_ref[...]).astype(BF16)
    rp = rp_ref[...]
    x_rnn = _dot(xn, win_ref[:, C_XRNN:C_XRNN + D_RNN])
    xh = rp[ROW_CONV_B:ROW_CONV_B + 1]
    for tap in range(CONV_W - 1):
        xh = xh + rp[tap:tap + 1] * sr.sc[tap]
    xh = xh + rp[CONV_W - 1:CONV_W] * x_rnn
    sr.conv[0] = sr.sc[1]
    sr.conv[1] = sr.sc[2]
    sr.conv[2] = x_rnn
    a, bx = _rglru_gate_math(xh, *_rglru_gate_dots(xh, wg_ref), rp)
    h = a * sr.h0[...] + bx
    sr.rnn[...] = h
    g_rnn = _dot(xn, win_ref[:, C_GRNN:C_GRNN + D_RNN])
    sr.rnnout[...] = h * _silu(g_rnn)
    sr.gatt[...] = _dot(xn, win_ref[:, C_GATT:C_GATT + D_ATT])
    kv = _dot(xn, win_ref[:, C_KV:C_KV + 2 * KV_DIM])
    sr.knew[...] = kv[:, :KV_DIM]
    sr.vnew[...] = kv[:, KV_DIM:]
    q = _dot(xn, win_ref[:, C_Q:C_Q + D_ATT])
    for head in range(N_HEADS):
        c = head // 2
        kvh = head // GROUP
        qc = q[:, c * LANES:(c + 1) * LANES]
        if (head % 2) != kvh:
            qc = pltpu.roll(qc, HEAD_DIM, axis=1)
        qz = jnp.where(lane_lo, qc, 0.0) if kvh == 0 else jnp.where(lane_lo, 0.0, qc)
        sr.qz[pl.ds(head, nb, stride=N_HEADS), :] = qz


def _sample_scores(sr, step, bb, st):
    newest = lax.broadcasted_iota(jnp.int32, (WINDOW, LANES), 0) == WINDOW - 1
    bias = sr.bias[...]
    st["rows"] = [pl.multiple_of((step * bb + bi) * N_HEADS, N_HEADS) for bi in range(bb)]
    st["scores"] = []
    for bi in range(bb):
        newk = jnp.where(newest, sr.knew[pl.ds(step * bb + bi, 1), :], pltpu.roll(sr.kc[bi].T, WINDOW - 1, axis=0))
        sr.ko[bi] = newk.T
        qz = sr.qz[pl.ds(st["rows"][bi], N_HEADS), :].astype(BF16)
        st["scores"].append(_dot_nt(qz, newk.astype(BF16)) + bias)


def _sample_values(sr, step, bb, st):
    newest = lax.broadcasted_iota(jnp.int32, (WINDOW, LANES), 0) == WINDOW - 1
    sinkm = sr.sink[...]
    for bi, s in enumerate(st.pop("scores")):
        m = jnp.maximum(jnp.max(s, axis=1, keepdims=True), sinkm)
        e = jnp.exp(s - m)
        den = jnp.sum(e, axis=1, keepdims=True) + jnp.exp(sinkm - m)
        p = (e * (1.0 / den)).astype(BF16)
        newv = jnp.where(newest, sr.vnew[pl.ds(step * bb + bi, 1), :], pltpu.roll(sr.vc[bi].T, WINDOW - 1, axis=0))
        sr.vo[bi] = newv.T
        sr.o[pl.ds(st["rows"][bi], N_HEADS), :] = _dot(p, newv.astype(BF16))


def _sample_finish(sr, gpost_ref, wout_ref):
    nb = sr.x.shape[0]
    lane_lo = lax.broadcasted_iota(jnp.int32, (nb, LANES), 1) < HEAD_DIM
    cols = []
    for c in range(D_ATT // LANES):
        kvh = (2 * c) // GROUP
        halves = []
        for half in range(2):
            oh = sr.o[pl.ds(2 * c + half, nb, stride=N_HEADS), :]
            if half != kvh:
                oh = pltpu.roll(oh, HEAD_DIM, axis=1)
            halves.append(oh)
        cols.append(jnp.where(lane_lo, halves[0], halves[1]))
    att = jnp.concatenate(cols, axis=1)
    att_out = (att * _silu(sr.gatt[...])).astype(BF16)
    rnn_out = sr.rnnout[...].astype(BF16)
    out = _dot(rnn_out, wout_ref[0:D_RNN, :]) + _dot(att_out, wout_ref[D_RNN:, :])
    sr.y[:, 0, :] = sr.x[:, 0, :] + out * _rms_scale(out) * gpost_ref[...]


def kernel(x_prompt, x_sample, state_conv, state_rnn, cache_k_win, cache_v_win, norm_pre, norm_post,
           w_in, conv_w, conv_b, w_gate_a, b_gate_a, w_gate_x, b_gate_x, lru_lambda, attn_sinks,
           rel_bias, w_out):
    assert w_in.shape[0] == 1, "single-layer trunk"
    bsz, seq, _ = x_prompt.shape
    nb = x_sample.shape[0]
    wb = cache_k_win.shape[2]
    assert wb == WINDOW and x_sample.shape[1] == 1 and seq % PROMPT_BLOCK == 0 and nb % (seq // PROMPT_BLOCK) == 0
    assert bsz == PROMPT_SEQS, "the step program is written for this many prompt sequences"

    raw = (conv_w[0], conv_b, b_gate_a, b_gate_x, lru_lambda, w_in[0], w_gate_a[0], w_gate_x[0], w_out[0])
    gpre = norm_pre.reshape(1, D_MODEL)
    gpost = norm_post.reshape(1, D_MODEL)
    sinks = attn_sinks[0]
    bkt_np = _folded_bucket_table()
    bkt = jnp.asarray(np.ascontiguousarray(bkt_np.T))
    bktrow = jnp.asarray(bkt_np[WINDOW - 1:WINDOW, :])

    feature_major = lambda z: jnp.transpose(z[0], (0, 2, 3, 1)).reshape(nb, KV_DIM, wb)
    y_p, conv_p, rnn_p, kw_p, vw_p, y_s, conv_s, rnn_s, kw_s, vw_s = _layer_call(
        x_prompt, rel_bias.T, sinks, bkt, gpre, gpost, raw,
        x_sample, bktrow, jnp.transpose(state_conv[0], (1, 0, 2)), state_rnn[0],
        feature_major(cache_k_win), feature_major(cache_v_win))

    kv5 = lambda z, n: jnp.transpose(z.reshape(n, N_KV_HEADS, HEAD_DIM, wb), (0, 3, 1, 2))[None]
    return (y_p, y_s,
            jnp.transpose(conv_p, (1, 0, 2))[None], rnn_p, kv5(kw_p, bsz), kv5(vw_p, bsz),
            jnp.transpose(conv_s, (1, 0, 2))[None], rnn_s[None], kv5(kw_s, nb), kv5(vw_s, nb))
```
